```python
import math
import jax, jax.numpy as jnp
from jax import lax
import numpy as np

D_MODEL = 2048
BATCH = 4
SEQ = 4096
DEPTH = 2

N_EVEN = (DEPTH + 1) // 2
N_ODD = DEPTH // 2

A_HEADS = 8
A_DK = 128
A_DV = 128
B_HEADS = 4
B_DK = 128
B_DV = 256
B_GATE_RANK = 16
B_GATE_TAU = 16.0
C_HEADS = 8
C_DH = 128
C_DLAT = 256
IDX_HEADS = 16
IDX_DIM = 64
TOPK_MAX = 256
Q_BLOCK = 128
D_HEADS = 4
D_DK = 128
D_DV = 256
CONV_K = 4
CHUNK = 64
REL_BUCKETS = 32
REL_MAX_DIST = 128
N_EXPERTS = 16
N_GROUPS = 4
EXPERTS_PER_GROUP = N_EXPERTS // N_GROUPS
TOP_K = 2
D_EXPERT = 512
DN_ALPHA = (2 * DEPTH) ** 0.25
DN_BETA = (8 * DEPTH) ** -0.25
EPS = 1e-5

AB_SIZES = (A_HEADS * A_DK, A_HEADS * A_DK, A_HEADS * A_DV, A_HEADS * A_DV,
            B_HEADS * B_DK, B_HEADS * B_DK, B_HEADS * B_DV, B_HEADS * B_DV, B_GATE_RANK)
CD_SIZES = (C_HEADS * C_DH, C_DLAT, IDX_HEADS * IDX_DIM, IDX_DIM, IDX_HEADS,
            D_HEADS * D_DK, D_HEADS * D_DK, D_HEADS * D_DV, D_HEADS, D_HEADS, D_HEADS * D_DV)
P_AB = sum(AB_SIZES)
P_CD = sum(CD_SIZES)
MIX_AB = A_HEADS * A_DV + B_HEADS * B_DV
MIX_CD = C_HEADS * C_DH + D_HEADS * D_DV

kernel_name = "hybrid_hgrn2_gla_dsa_mlstm_moe_deepnorm"

F32 = jnp.float32


def split_cols(h, sizes):
    return jnp.split(h, [int(i) for i in np.cumsum(sizes)[:-1]], axis=-1)


def layer_norm(x, g, b):
    xf = x.astype(F32)
    mu = jnp.mean(xf, -1, keepdims=True)
    var = jnp.mean(jnp.square(xf - mu), -1, keepdims=True)
    return ((xf - mu) * lax.rsqrt(var + EPS) * g + b).astype(x.dtype)


def rms_norm(x, g):
    xf = x.astype(F32)
    return xf * lax.rsqrt(jnp.mean(jnp.square(xf), -1, keepdims=True) + EPS) * g


def to_chunks(t):
    b, s, h, d = t.shape
    return t.astype(F32).reshape(b, s // CHUNK, CHUNK, h, d).transpose(1, 0, 3, 2, 4)


def from_chunks(t):
    n, b, h, l, d = t.shape
    return t.transpose(1, 0, 3, 2, 4).reshape(b, n * l, h, d)


def gate_to_chunks(t):
    b, s, h = t.shape
    return t.astype(F32).reshape(b, s // CHUNK, CHUNK, h).transpose(1, 0, 3, 2)


def gated_linear_recurrence(q, k, v, log_a):
    b_, _, h_, dk = q.shape
    dv = v.shape[-1]
    causal = jnp.tril(jnp.ones((CHUNK, CHUNK), bool))[:, :, None]

    def step(state, inp):
        qc, kc, vc, la = inp
        bcum = jnp.cumsum(la, axis=2)
        o_inter = jnp.einsum('bhtd,bhde->bhte', qc * jnp.exp(bcum), state)
        diff = bcum[:, :, :, None, :] - bcum[:, :, None, :, :]
        dec = jnp.where(causal, jnp.exp(jnp.where(causal, diff, 0.0)), 0.0)
        attn = jnp.einsum('bhtd,bhsd,bhtsd->bhts', qc, kc, dec)
        o = o_inter + jnp.einsum('bhts,bhse->bhte', attn, vc)
        b_last = bcum[:, :, -1:, :]
        state = (jnp.exp(b_last)[:, :, 0, :, None] * state
                 + jnp.einsum('bhsd,bhse->bhde', kc * jnp.exp(b_last - bcum), vc))
        return state, o

    init = jnp.zeros((b_, h_, dk, dv), F32)
    _, oc = lax.scan(step, init, (to_chunks(q), to_chunks(k), to_chunks(v), to_chunks(log_a)))
    return from_chunks(oc)


def mlstm_chunkwise(q, k, v, ig, lf):
    b_, _, h_, dk = q.shape
    dv = v.shape[-1]
    causal = jnp.tril(jnp.ones((CHUNK, CHUNK), bool))

    def step(carry, inp):
        cmat, nvec, m = carry
        qc, kc, vc, igc, lfc = inp
        bcum = jnp.cumsum(lfc, -1)
        log_w = jnp.where(causal, bcum[..., :, None] - bcum[..., None, :] + igc[..., None, :], -jnp.inf)
        log_inter = bcum + m[..., None]
        m_t = jnp.maximum(log_w.max(-1), log_inter)
        s = jnp.einsum('bhtd,bhsd->bhts', qc, kc) * jnp.exp(log_w - m_t[..., None])
        w_inter = jnp.exp(log_inter - m_t)
        num = (jnp.einsum('bhts,bhse->bhte', s, vc)
               + w_inter[..., None] * jnp.einsum('bhtd,bhde->bhte', qc, cmat))
        qn = s.sum(-1) + w_inter * jnp.einsum('bhtd,bhd->bht', qc, nvec)
        h = num / jnp.maximum(jnp.abs(qn), jnp.exp(-m_t))[..., None]
        b_last = bcum[..., -1]
        log_u = b_last[..., None] - bcum + igc
        m_new = jnp.maximum(b_last + m, log_u.max(-1))
        u = jnp.exp(log_u - m_new[..., None])
        decay = jnp.exp(b_last + m - m_new)
        cmat = decay[..., None, None] * cmat + jnp.einsum('bhs,bhsd,bhse->bhde', u, kc, vc)
        nvec = decay[..., None] * nvec + jnp.einsum('bhs,bhsd->bhd', u, kc)
        return (cmat, nvec, m_new), h

    init = (jnp.zeros((b_, h_, dk, dv), F32), jnp.zeros((b_, h_, dk), F32), jnp.zeros((b_, h_), F32))
    _, hc = lax.scan(step, init, (to_chunks(q), to_chunks(k), to_chunks(v),
                                  gate_to_chunks(ig), gate_to_chunks(lf)))
    return from_chunks(hc)


def t5_bucket(rel):
    n = jnp.maximum(rel, 0)
    max_exact = REL_BUCKETS // 2
    large = max_exact + (jnp.log(jnp.maximum(n, 1).astype(F32) / max_exact)
                         / math.log(REL_MAX_DIST / max_exact)
                         * (REL_BUCKETS - max_exact)).astype(jnp.int32)
    large = jnp.minimum(large, REL_BUCKETS - 1)
    return jnp.where(n < max_exact, n, large)


def dsa_attention(q_lat, ckv, iq, ik, iw, rel_table):
    b_, s_, h_, _ = q_lat.shape
    k_sel = min(TOPK_MAX, s_ // 4)
    s_pos = jnp.arange(s_)

    def block(j):
        start = j * Q_BLOCK
        ql = lax.dynamic_slice_in_dim(q_lat, start, Q_BLOCK, axis=1)
        qi = lax.dynamic_slice_in_dim(iq, start, Q_BLOCK, axis=1)
        wi = lax.dynamic_slice_in_dim(iw, start, Q_BLOCK, axis=1)
        t = start + jnp.arange(Q_BLOCK)
        score = jnp.einsum('bqhs,bqh->bqs', jax.nn.relu(jnp.einsum('bqhd,bsd->bqhs', qi, ik)), wi)
        score = jnp.where(s_pos[None, None, :] <= t[None, :, None], score.astype(F32), -jnp.inf)
        _, idx = lax.top_k(score, k_sel)
        valid = idx <= t[None, :, None]
        kv = jax.vmap(lambda c, i: c[i])(ckv, idx)
        bias = rel_table[t5_bucket(t[None, :, None] - idx)]
        logits = jnp.einsum('bqhc,bqkc->bhqk', ql, kv).astype(F32) + bias.transpose(0, 3, 1, 2).astype(F32)
        logits = jnp.where(valid[:, None], logits, -jnp.inf)
        p = jax.nn.softmax(logits, axis=-1).astype(kv.dtype)
        return jnp.einsum('bhqk,bqkc->bqhc', p, kv)

    out = lax.map(block, jnp.arange(s_ // Q_BLOCK))
    return out.transpose(1, 0, 2, 3, 4).reshape(b_, s_, h_, -1)


def causal_depthwise_conv(x, w):
    kc, cc = w.shape
    return lax.conv_general_dilated(x, w[:, None, :].astype(x.dtype), window_strides=(1,),
                                    padding=[(kc - 1, 0)], dimension_numbers=('NWC', 'WIO', 'NWC'),
                                    feature_group_count=cc)


def mix_ab(x, w_in, lb_logits, li, a_norm_g, gla_wa2, gla_ba2, b_norm_g):
    b_, s_, _ = x.shape
    h = x @ w_in
    aq, af, ai, ag, bq, bk, bv, bg, ba = split_cols(h, AB_SIZES)
    lb = jnp.cumsum(jax.nn.softmax(lb_logits.astype(F32), axis=0), axis=0)[li]
    f = lb + (1.0 - lb) * jax.nn.sigmoid(af.astype(F32))
    hs = lambda t, n: t.reshape(b_, s_, n, -1)
    oa = gated_linear_recurrence(hs(jax.nn.silu(aq), A_HEADS), hs(1.0 - f, A_HEADS),
                                 hs(ai, A_HEADS), hs(jnp.log(f), A_HEADS))
    oa = rms_norm(oa, a_norm_g) * jax.nn.silu(hs(ag, A_HEADS).astype(F32))
    log_alpha = jax.nn.log_sigmoid((ba @ gla_wa2 + gla_ba2).astype(F32)) / B_GATE_TAU
    ob = gated_linear_recurrence(hs(bq * B_DK ** -0.5, B_HEADS), hs(bk, B_HEADS),
                                 hs(bv, B_HEADS), hs(log_alpha, B_HEADS))
    ob = rms_norm(ob, b_norm_g) * jax.nn.silu(hs(bg, B_HEADS).astype(F32))
    return jnp.concatenate([oa.reshape(b_, s_, -1), ob.reshape(b_, s_, -1)], -1).astype(x.dtype)


def mix_cd(x, w_in, rel_table, ckv_g, w_uk, w_uv, conv_w, gate_b, d_norm_g):
    b_, s_, _ = x.shape
    h = x @ w_in
    cq, ckv, iq, ik, iw, dq, dk, dv, di, df, dog = split_cols(h, CD_SIZES)
    ckv = rms_norm(ckv, ckv_g).astype(x.dtype)
    q_lat = jnp.einsum('bshd,chd->bshc', cq.reshape(b_, s_, C_HEADS, C_DH), w_uk) * C_DH ** -0.5
    o_lat = dsa_attention(q_lat, ckv, iq.reshape(b_, s_, IDX_HEADS, IDX_DIM) * IDX_DIM ** -0.5,
                          ik, iw * IDX_HEADS ** -0.5, rel_table)
    oc = jnp.einsum('bshc,chd->bshd', o_lat, w_uv).astype(F32)
    qk = jax.nn.silu(causal_depthwise_conv(jnp.concatenate([dq, dk], -1), conv_w))
    dq, dk = jnp.split(qk, 2, axis=-1)
    hs = lambda t, n: t.reshape(b_, s_, n, -1)
    ig = di + gate_b[0]
    lf = jax.nn.log_sigmoid((df + gate_b[1]).astype(F32))
    od = mlstm_chunkwise(hs(dq, D_HEADS), hs(dk * D_DK ** -0.5, D_HEADS), hs(dv, D_HEADS), ig, lf)
    od = rms_norm(od, d_norm_g) * jax.nn.sigmoid(hs(dog, D_HEADS).astype(F32))
    return jnp.concatenate([oc.reshape(b_, s_, -1), od.reshape(b_, s_, -1)], -1).astype(x.dtype)


def moe(x, w_router, b_router, w1, w3, w2):
    b_, s_, d_ = x.shape
    xt = x.reshape(-1, d_)
    aff = jax.nn.sigmoid((xt @ w_router).astype(F32))
    sel = aff + b_router.astype(F32)
    grp_score = lax.top_k(sel.reshape(-1, N_GROUPS, EXPERTS_PER_GROUP), TOP_K)[0].sum(-1)
    g_best = jnp.argmax(grp_score, -1)
    in_grp = (jnp.arange(N_EXPERTS) // EXPERTS_PER_GROUP)[None, :] == g_best[:, None]
    _, top_idx = lax.top_k(jnp.where(in_grp, sel, -jnp.inf), TOP_K)
    top_aff = jnp.take_along_axis(aff, top_idx, -1)
    top_w = top_aff / jnp.sum(top_aff, -1, keepdims=True)
    gates = jnp.sum(jax.nn.one_hot(top_idx, N_EXPERTS, dtype=F32) * top_w[..., None], axis=1)
    hid = jax.nn.silu(jnp.einsum('td,edf->tef', xt, w1)) * jnp.einsum('td,edf->tef', xt, w3)
    y = jnp.einsum('tef,efd->td', hid * gates[..., None].astype(hid.dtype), w2)
    return y.reshape(b_, s_, d_).astype(x.dtype)


def setup_inputs(seed: int = 0) -> dict:
    key = jax.random.key(seed)
    ks = jax.random.split(key, 26)
    nrm = lambda k, shape, scale: jax.random.normal(k, shape, F32) * scale
    forget_b = jnp.broadcast_to(jnp.linspace(3.0, 6.0, D_HEADS, dtype=F32), (N_ODD, D_HEADS)) + nrm(ks[14], (N_ODD, D_HEADS), 0.1)
    return {
        "x": nrm(ks[0], (BATCH, SEQ, D_MODEL), 1.0),
        "w_in_ab": nrm(ks[1], (N_EVEN, D_MODEL, P_AB), D_MODEL ** -0.5),
        "w_out_ab": nrm(ks[2], (N_EVEN, MIX_AB, D_MODEL), DN_BETA * MIX_AB ** -0.5),
        "hgrn_lb_logits": nrm(ks[3], (N_EVEN + 1, A_HEADS * A_DK), 0.5),
        "a_norm_g": 1.0 + nrm(ks[4], (N_EVEN, A_DV), 0.02),
        "gla_wa2": nrm(ks[5], (N_EVEN, B_GATE_RANK, B_HEADS * B_DK), B_GATE_RANK ** -0.5),
        "gla_ba2": nrm(ks[6], (N_EVEN, B_HEADS * B_DK), 0.1),
        "b_norm_g": 1.0 + nrm(ks[7], (N_EVEN, B_DV), 0.02),
        "w_in_cd": nrm(ks[8], (N_ODD, D_MODEL, P_CD), D_MODEL ** -0.5),
        "w_out_cd": nrm(ks[9], (N_ODD, MIX_CD, D_MODEL), DN_BETA * MIX_CD ** -0.5),
        "ckv_norm_g": 1.0 + nrm(ks[10], (N_ODD, C_DLAT), 0.02),
        "w_uk": nrm(ks[11], (N_ODD, C_DLAT, C_HEADS, C_DH), C_DLAT ** -0.5),
        "w_uv": nrm(ks[12], (N_ODD, C_DLAT, C_HEADS, C_DH), C_DLAT ** -0.5),
        "mlstm_conv_w": nrm(ks[13], (N_ODD, CONV_K, 2 * D_HEADS * D_DK), CONV_K ** -0.5),
        "mlstm_gate_b": jnp.stack([nrm(ks[15], (N_ODD, D_HEADS), 0.1), forget_b], axis=1),
        "d_norm_g": 1.0 + nrm(ks[16], (N_ODD, D_DV), 0.02),
        "rel_table": nrm(ks[17], (REL_BUCKETS, C_HEADS), 0.5),
        "w_router": nrm(ks[18], (D_MODEL, N_EXPERTS), D_MODEL ** -0.5),
        "b_router": nrm(ks[19], (N_EXPERTS,), 0.01),
        "moe_w1": nrm(ks[20], (DEPTH, N_EXPERTS, D_MODEL, D_EXPERT), D_MODEL ** -0.5),
        "moe_w3": nrm(ks[21], (DEPTH, N_EXPERTS, D_MODEL, D_EXPERT), D_MODEL ** -0.5),
        "moe_w2": nrm(ks[22], (DEPTH, N_EXPERTS, D_EXPERT, D_MODEL), DN_BETA * D_EXPERT ** -0.5),
        "ln_g": 1.0 + nrm(ks[23], (DEPTH, 2, D_MODEL), 0.02),
        "ln_b": nrm(ks[24], (DEPTH, 2, D_MODEL), 0.02),
    }


def reference(x, w_in_ab, w_out_ab, hgrn_lb_logits, a_norm_g, gla_wa2, gla_ba2, b_norm_g,
              w_in_cd, w_out_cd, ckv_norm_g, w_uk, w_uv, mlstm_conv_w, mlstm_gate_b, d_norm_g,
              rel_table, w_router, b_router, moe_w1, moe_w3, moe_w2, ln_g, ln_b):
    for layer in range(DEPTH):
        li = layer // 2
        if layer % 2 == 0:
            mixed = mix_ab(x, w_in_ab[li], hgrn_lb_logits, li, a_norm_g[li],
                           gla_wa2[li], gla_ba2[li], b_norm_g[li]) @ w_out_ab[li]
        else:
            mixed = mix_cd(x, w_in_cd[li], rel_table, ckv_norm_g[li], w_uk[li], w_uv[li],
                           mlstm_conv_w[li], mlstm_gate_b[li], d_norm_g[li]) @ w_out_cd[li]
        x = layer_norm(DN_ALPHA * x + mixed, ln_g[layer, 0], ln_b[layer, 0])
        ffn = moe(x, w_router, b_router, moe_w1[layer], moe_w3[layer], moe_w2[layer])
        x = layer_norm(DN_ALPHA * x + ffn, ln_g[layer, 1], ln_b[layer, 1])
    return x
```

```python
import functools
import math

import numpy as np
import jax
import jax.numpy as jnp
from jax import lax
from jax.experimental import pallas as pl
from jax.experimental.pallas import tpu as pltpu

F32 = jnp.float32
BF16 = jnp.bfloat16
I32 = jnp.int32

D_MODEL = 2048
DEPTH = 2
A_HEADS, A_DK, A_DV = 8, 128, 128
B_HEADS, B_DK, B_DV = 4, 128, 256
B_GATE_RANK, B_GATE_TAU = 16, 16.0
C_HEADS, C_DH, C_DLAT = 8, 128, 256
IDX_HEADS, IDX_DIM = 16, 64
TOPK_MAX = 256
D_HEADS, D_DK, D_DV = 4, 128, 256
CONV_K = 4
REL_BUCKETS, REL_MAX_DIST = 32, 128
N_EXPERTS, N_GROUPS, TOP_K, D_EXPERT = 16, 4, 2, 512
EXPERTS_PER_GROUP = N_EXPERTS // N_GROUPS
DN_ALPHA = (2 * DEPTH) ** 0.25
EPS = 1e-5

LANES = 128
VMEM_LIMIT = 56 * 1024 * 1024

CHUNK = 64
N_LEVELS = 6
TIME_BLOCK = 256
QB = 128
NEG = -1e30
INT_MIN = -2 ** 31


def _cparams(*sem):
    return pltpu.CompilerParams(dimension_semantics=sem, vmem_limit_bytes=VMEM_LIMIT)


def _dot(a, b):
    return jnp.dot(a, b, preferred_element_type=F32)


def _dot_nt(a, b):
    return lax.dot_general(a, b, (((1,), (1,)), ((), ())), preferred_element_type=F32)


def _dot_tn(a, b):
    return lax.dot_general(a, b, (((0,), (0,)), ((), ())), preferred_element_type=F32)


def _split3(a):
    hi = a.astype(BF16)
    r1 = a - hi.astype(F32)
    mid = r1.astype(BF16)
    lo = (r1 - mid.astype(F32)).astype(BF16)
    return hi, mid, lo


def _dot01(m01, a):
    hi, mid, lo = _split3(a)
    return _dot(m01, hi) + _dot(m01, mid) + _dot(m01, lo)


def _sigmoid(x):
    return 1.0 / (1.0 + jnp.exp(-x))


def _silu(x):
    return x * _sigmoid(x)


def _log_sigmoid(x):
    return jnp.minimum(x, 0.0) - jnp.log(1.0 + jnp.exp(-jnp.abs(x)))


def _proj_kernel(x_ref, w_ref, o_ref, xb_ref):
    @pl.when(pl.program_id(1) == 0)
    def _():
        xb_ref[...] = x_ref[...].astype(BF16)

    o_ref[...] = _dot(xb_ref[...], w_ref[...])


def _project(x, w, tm, tn):
    m, k = x.shape
    n = w.shape[1]
    return pl.pallas_call(
        _proj_kernel,
        out_shape=jax.ShapeDtypeStruct((m, n), F32),
        grid=(m // tm, n // tn),
        in_specs=[pl.BlockSpec((tm, k), lambda i, j: (i, 0)),
                  pl.BlockSpec((k, tn), lambda i, j: (0, j))],
        out_specs=pl.BlockSpec((tm, tn), lambda i, j: (i, j)),
        scratch_shapes=[pltpu.VMEM((tm, k), BF16)],
        compiler_params=_cparams("parallel", "arbitrary"),
        name="in_proj",
    )(x, w)


def _layer_norm_rows(z, g, b):
    mu = jnp.mean(z, axis=-1, keepdims=True)
    zc = z - mu
    var = jnp.mean(zc * zc, axis=-1, keepdims=True)
    return zc * lax.rsqrt(var + EPS) * g + b


def _outproj_ln_kernel(x_ref, ma_ref, mb_ref, wa_ref, wb_ref, g_ref, b_ref, o_ref):
    mixed = _dot(ma_ref[...].astype(BF16), wa_ref[...]) + _dot(mb_ref[...].astype(BF16), wb_ref[...])
    o_ref[...] = _layer_norm_rows(DN_ALPHA * x_ref[...] + mixed, g_ref[...], b_ref[...])


def _outproj_ln(x, mix_a, mix_b, w_a, w_b, g, b, tm=256):
    m, d = x.shape
    ka, kb = mix_a.shape[1], mix_b.shape[1]
    row = lambda i: (i, 0)
    fixed = lambda i: (0, 0)
    return pl.pallas_call(
        _outproj_ln_kernel,
        out_shape=jax.ShapeDtypeStruct((m, d), F32),
        grid=(m // tm,),
        in_specs=[pl.BlockSpec((tm, d), row), pl.BlockSpec((tm, ka), row), pl.BlockSpec((tm, kb), row),
                  pl.BlockSpec((ka, d), fixed), pl.BlockSpec((kb, d), fixed),
                  pl.BlockSpec((1, d), fixed), pl.BlockSpec((1, d), fixed)],
        out_specs=pl.BlockSpec((tm, d), row),
        compiler_params=_cparams("parallel"),
        name="out_proj_ln",
    )(x, mix_a, mix_b, w_a, w_b, g.reshape(1, d), b.reshape(1, d))


def _chunk_constants():
    t = np.arange(CHUNK)
    tri = (t[:, None] >= t[None, :]).astype(np.float32)
    cum, pair, odd = [tri], [], []
    for lev in range(1, N_LEVELS + 1):
        c = CHUNK >> lev
        mid = (t // (2 * c)) * 2 * c + c - 1
        cum.append(tri - tri[mid])
        pair.append((t[:, None] // (2 * c) == t[None, :] // (2 * c)).astype(np.float32))
        odd.append(np.broadcast_to((((t // c) & 1) == 1).astype(np.float32)[:, None], (CHUNK, LANES)))
    pair.append(np.eye(CHUNK, dtype=np.float32))
    return (jnp.asarray(np.concatenate(cum, 0), BF16), jnp.asarray(np.stack(pair), F32),
            jnp.asarray(np.stack(odd), F32))


def _glr_chunk(q, k, v, la, st_ref, cum_ref, pair_ref, odd_ref):
    d = _dot01(cum_ref[...], la)
    bcum = d[0:CHUNK]
    attn = pair_ref[N_LEVELS] * _dot_nt(q.astype(BF16), k.astype(BF16))
    for lev in range(1, N_LEVELS + 1):
        e = jnp.exp(-jnp.abs(d[lev * CHUNK:(lev + 1) * CHUNK]))
        eq = e * odd_ref[lev - 1]
        ql = (q * eq).astype(BF16)
        kl = (k * (e - eq)).astype(BF16)
        attn = attn + pair_ref[lev - 1] * _dot_nt(ql, kl)
    st = st_ref[...]
    o = _dot_nt((q * jnp.exp(bcum)).astype(BF16), st.astype(BF16)) + _dot(attn.astype(BF16), v.astype(BF16))
    b_last = bcum[CHUNK - 1:CHUNK]
    kdec = (k * jnp.exp(b_last - bcum)).astype(BF16)
    st_ref[...] = st * jnp.exp(b_last) + _dot_tn(v.astype(BF16), kdec)
    return o


def _rms_gate(o, g, gate):
    ms = jnp.mean(o * o, axis=-1, keepdims=True)
    return o * lax.rsqrt(ms + EPS) * g * gate


def _hgrn2_kernel(q_ref, f_ref, i_ref, g_ref, lb_ref, ng_ref, cum_ref, pair_ref, odd_ref, o_ref, st_ref):
    @pl.when(pl.program_id(2) == 0)
    def _():
        st_ref[...] = jnp.zeros_like(st_ref)

    lb = lb_ref[...]

    def body(c, carry):
        rows = pl.ds(pl.multiple_of(c * CHUNK, CHUNK), CHUNK)
        f = lb + (1.0 - lb) * _sigmoid(f_ref[rows, :])
        o = _glr_chunk(_silu(q_ref[rows, :]), 1.0 - f, i_ref[rows, :], jnp.log(f),
                       st_ref, cum_ref, pair_ref, odd_ref)
        o_ref[rows, :] = _rms_gate(o, ng_ref[...], _silu(g_ref[rows, :]))
        return carry

    lax.fori_loop(0, q_ref.shape[0] // CHUNK, body, 0)


def _gla_kernel(q_ref, k_ref, v_ref, g_ref, r_ref, wa_ref, ba_ref, ng_ref, cum_ref, pair_ref, odd_ref,
                o_ref, st_ref):
    @pl.when(pl.program_id(2) == 0)
    def _():
        st_ref[...] = jnp.zeros_like(st_ref)

    def body(c, carry):
        rows = pl.ds(pl.multiple_of(c * CHUNK, CHUNK), CHUNK)
        pre = _dot(r_ref[rows, :].astype(BF16), wa_ref[...]) + ba_ref[...]
        la = _log_sigmoid(pre) * (1.0 / B_GATE_TAU)
        o = _glr_chunk(q_ref[rows, :] * (B_DK ** -0.5), k_ref[rows, :], v_ref[rows, :], la,
                       st_ref, cum_ref, pair_ref, odd_ref)
        o_ref[rows, :] = _rms_gate(o, ng_ref[...], _silu(g_ref[rows, :]))
        return carry

    lax.fori_loop(0, q_ref.shape[0] // CHUNK, body, 0)


def _const_spec(arr):
    nd = arr.ndim
    return pl.BlockSpec(arr.shape, lambda *_: (0,) * nd)


def _hgrn2(h, lb, norm_g, batch, seq):
    tb = min(TIME_BLOCK, seq)
    nt = seq // tb
    consts = _chunk_constants()
    col = lambda off: pl.BlockSpec((tb, A_DK), lambda b, hh, t, off=off: (b * nt + t, off + hh))
    return pl.pallas_call(
        _hgrn2_kernel,
        out_shape=jax.ShapeDtypeStruct((batch * seq, A_HEADS * A_DV), F32),
        grid=(batch, A_HEADS, nt),
        in_specs=[col(0), col(A_HEADS), col(2 * A_HEADS), col(3 * A_HEADS),
                  pl.BlockSpec((1, A_DK), lambda b, hh, t: (0, hh)),
                  pl.BlockSpec((1, A_DV), lambda b, hh, t: (0, 0))] + [_const_spec(c) for c in consts],
        out_specs=pl.BlockSpec((tb, A_DV), lambda b, hh, t: (b * nt + t, hh)),
        scratch_shapes=[pltpu.VMEM((A_DV, A_DK), F32)],
        compiler_params=_cparams("parallel", "parallel", "arbitrary"),
        name="hgrn2",
    )(h, h, h, h, lb.reshape(1, -1), norm_g.reshape(1, -1), *consts)


def _gla(h, wa2p, ba2, norm_g, batch, seq, q_off, k_off, v_off, g_off, r_off):
    tb = min(TIME_BLOCK, seq)
    nt = seq // tb
    consts = _chunk_constants()
    colk = lambda off: pl.BlockSpec((tb, B_DK), lambda b, hh, t, off=off: (b * nt + t, off + hh))
    colv = lambda off: pl.BlockSpec((tb, B_DV), lambda b, hh, t, off=off: (b * nt + t, off + hh))
    return pl.pallas_call(
        _gla_kernel,
        out_shape=jax.ShapeDtypeStruct((batch * seq, B_HEADS * B_DV), F32),
        grid=(batch, B_HEADS, nt),
        in_specs=[colk(q_off), colk(k_off), colv(v_off), colv(g_off),
                  pl.BlockSpec((tb, LANES), lambda b, hh, t: (b * nt + t, r_off)),
                  pl.BlockSpec((LANES, B_DK), lambda b, hh, t: (0, hh)),
                  pl.BlockSpec((1, B_DK), lambda b, hh, t: (0, hh)),
                  pl.BlockSpec((1, B_DV), lambda b, hh, t: (0, 0))] + [_const_spec(c) for c in consts],
        out_specs=pl.BlockSpec((tb, B_DV), lambda b, hh, t: (b * nt + t, hh)),
        scratch_shapes=[pltpu.VMEM((B_DV, B_DK), F32)],
        compiler_params=_cparams("parallel", "parallel", "arbitrary"),
        name="gla",
    )(h, h, h, h, h, wa2p, ba2.reshape(1, -1), norm_g.reshape(1, -1), *consts)


AB_MAIN = 4 * A_HEADS * A_DK + 2 * B_HEADS * B_DK + 2 * B_HEADS * B_DV
AB_PAD = 7680


def _mix_ab(x2d, w_in, lb_logits, li, a_norm_g, wa2, ba2, b_norm_g, batch, seq):
    d = x2d.shape[1]
    w = jnp.concatenate([w_in, jnp.zeros((d, AB_PAD - w_in.shape[1]), F32)], axis=1).astype(BF16)
    h = _project(x2d, w, min(1024, x2d.shape[0]), 512)
    lb = jnp.cumsum(jax.nn.softmax(lb_logits.astype(F32), axis=0), axis=0)[li]
    oa = _hgrn2(h, lb, a_norm_g, batch, seq)
    wa2p = jnp.concatenate([wa2, jnp.zeros((LANES - B_GATE_RANK, wa2.shape[1]), F32)], axis=0).astype(BF16)
    ob = _gla(h, wa2p, ba2, b_norm_g, batch, seq,
              q_off=4 * A_HEADS, k_off=4 * A_HEADS + B_HEADS, v_off=(4 * A_HEADS + 2 * B_HEADS) // 2,
              g_off=(4 * A_HEADS + 2 * B_HEADS) // 2 + B_HEADS, r_off=AB_MAIN // LANES)
    return oa, ob


CONV_HALO = 8


def _causal_conv(x_ref, w_ref, buf_ref, tail_ref):
    tb = x_ref.shape[0]
    x = x_ref[...]
    buf_ref[0:CONV_HALO, :] = tail_ref[...]
    buf_ref[CONV_HALO:CONV_HALO + tb, :] = x
    tail_ref[...] = x[tb - CONV_HALO:tb]
    y = w_ref[CONV_K - 1:CONV_K, :] * x
    for j in range(CONV_K - 1):
        y = y + w_ref[j:j + 1, :] * buf_ref[pl.ds(CONV_HALO - (CONV_K - 1) + j, tb), :]
    return y


def _row_to_col(row, eye):
    return jnp.sum(jnp.where(eye, row, 0.0), axis=1, keepdims=True)


def _mlstm_kernel(q_ref, k_ref, v_ref, og_ref, wq_ref, wk_ref, ig_ref, fg_ref, gb_ref, ng_ref, tri_ref,
                  o_ref, ct_ref, n_ref, m_ref, qt_ref, kt_ref, qs_ref, ks_ref, buf_ref):
    @pl.when(pl.program_id(2) == 0)
    def _():
        ct_ref[...] = jnp.zeros_like(ct_ref)
        n_ref[...] = jnp.zeros_like(n_ref)
        m_ref[...] = jnp.zeros_like(m_ref)
        qt_ref[...] = jnp.zeros_like(qt_ref)
        kt_ref[...] = jnp.zeros_like(kt_ref)

    qs_ref[...] = _silu(_causal_conv(q_ref, wq_ref, buf_ref, qt_ref))
    ks_ref[...] = _silu(_causal_conv(k_ref, wk_ref, buf_ref, kt_ref)) * (D_DK ** -0.5)

    r_i = lax.broadcasted_iota(I32, (CHUNK, CHUNK), 0)
    c_i = lax.broadcasted_iota(I32, (CHUNK, CHUNK), 1)
    eye = r_i == c_i
    causal = r_i >= c_i

    def body(c, carry):
        rows = pl.ds(pl.multiple_of(c * CHUNK, CHUNK), CHUNK)
        q = qs_ref[rows, :]
        k = ks_ref[rows, :]
        v = v_ref[rows, :].astype(BF16)
        qb = q.astype(BF16)
        ig_row = ig_ref[pl.ds(c, 1), :] + gb_ref[0]
        lf_row = _log_sigmoid(fg_ref[pl.ds(c, 1), :] + gb_ref[1])
        hi, mid, lo = _split3(lf_row)
        tri = tri_ref[...]
        bcum_row = _dot(hi, tri) + _dot(mid, tri) + _dot(lo, tri)
        bcum_col = _row_to_col(bcum_row, eye)
        ig_col = _row_to_col(ig_row, eye)
        m_prev = m_ref[:, 0:1]
        log_w = jnp.where(causal, bcum_col - bcum_row + ig_row, NEG)
        log_inter = bcum_col + m_prev
        m_t = jnp.maximum(jnp.max(log_w, axis=1, keepdims=True), log_inter)
        s = _dot_nt(qb, k.astype(BF16)) * jnp.exp(log_w - m_t)
        w_inter = jnp.exp(log_inter - m_t)
        num = _dot(s.astype(BF16), v) + w_inter * _dot_nt(qb, ct_ref[...].astype(BF16))
        qn = jnp.sum(s, axis=1, keepdims=True) + w_inter * jnp.sum(q * n_ref[...], axis=1, keepdims=True)
        h = num / jnp.maximum(jnp.abs(qn), jnp.exp(-m_t))
        o_ref[rows, :] = _rms_gate(h, ng_ref[...], _sigmoid(og_ref[rows, :]))
        b_last = bcum_row[:, CHUNK - 1:CHUNK]
        log_u = b_last - bcum_col + ig_col
        m_new = jnp.maximum(b_last + m_prev, jnp.max(log_u, axis=0, keepdims=True))
        decay = jnp.exp(b_last + m_prev - m_new)
        ku = k * jnp.exp(log_u - m_new)
        ct_ref[...] = decay * ct_ref[...] + _dot_tn(v, ku.astype(BF16))
        n_ref[...] = decay * n_ref[...] + jnp.sum(ku, axis=0, keepdims=True)
        m_ref[...] = jnp.broadcast_to(m_new, m_ref.shape)
        return carry

    lax.fori_loop(0, q_ref.shape[0] // CHUNK, body, 0)


def _mlstm(h, conv_w, ig_rows, fg_rows, gate_b, norm_g, batch, seq, q_off, k_off, v_off, g_off):
    tb = min(TIME_BLOCK, seq)
    nt = seq // tb
    nc = tb // CHUNK
    t = np.arange(CHUNK)
    tri = jnp.asarray(t[:, None] <= t[None, :], BF16)
    gb = jnp.broadcast_to(gate_b.reshape(2, D_HEADS, 1, 1), (2, D_HEADS, 1, CHUNK)).astype(F32)
    colk = lambda off: pl.BlockSpec((tb, D_DK), lambda b, hh, t, off=off: (b * nt + t, off + hh))
    colv = lambda off: pl.BlockSpec((tb, D_DV), lambda b, hh, t, off=off: (b * nt + t, off + hh))
    gate = pl.BlockSpec((None, None, None, nc, CHUNK), lambda b, hh, t: (b, hh, t, 0, 0))
    return pl.pallas_call(
        _mlstm_kernel,
        out_shape=jax.ShapeDtypeStruct((batch * seq, D_HEADS * D_DV), F32),
        grid=(batch, D_HEADS, nt),
        in_specs=[colk(q_off), colk(k_off), colv(v_off), colv(g_off),
                  pl.BlockSpec((CONV_K, D_DK), lambda b, hh, t: (0, hh)),
                  pl.BlockSpec((CONV_K, D_DK), lambda b, hh, t: (0, D_HEADS + hh)),
                  gate, gate,
                  pl.BlockSpec((2, None, 1, CHUNK), lambda b, hh, t: (0, hh, 0, 0)),
                  pl.BlockSpec((1, D_DV), lambda b, hh, t: (0, 0)),
                  pl.BlockSpec((CHUNK, CHUNK), lambda b, hh, t: (0, 0))],
        out_specs=pl.BlockSpec((tb, D_DV), lambda b, hh, t: (b * nt + t, hh)),
        scratch_shapes=[pltpu.VMEM((D_DV, D_DK), F32), pltpu.VMEM((1, D_DK), F32), pltpu.VMEM((1, LANES), F32),
                        pltpu.VMEM((CONV_HALO, D_DK), F32), pltpu.VMEM((CONV_HALO, D_DK), F32),
                        pltpu.VMEM((tb, D_DK), F32), pltpu.VMEM((tb, D_DK), F32),
                        pltpu.VMEM((tb + CONV_HALO, D_DK), F32)],
        compiler_params=_cparams("parallel", "parallel", "arbitrary"),
        name="mlstm",
    )(h, h, h, h, conv_w, conv_w, ig_rows, fg_rows, gb, norm_g.reshape(1, -1), tri)


CD_CQ, CD_IQ, CD_CKV, CD_DQ, CD_DK, CD_DV, CD_DOG, CD_TAIL = 0, 1024, 2048, 2304, 2816, 3328, 4352, 5376
CD_PAD = 5632
TAIL_IK, TAIL_IW, TAIL_DI, TAIL_DF = 0, 64, 80, 84


def _pack_cd(w_in):
    cq, ckv, iq, ik, iw, dq, dk, dv, di, df, dog = jnp.split(
        w_in, [int(i) for i in np.cumsum(
            (C_HEADS * C_DH, C_DLAT, IDX_HEADS * IDX_DIM, IDX_DIM, IDX_HEADS, D_HEADS * D_DK, D_HEADS * D_DK,
             D_HEADS * D_DV, D_HEADS, D_HEADS))], axis=1)
    used = CD_TAIL + IDX_DIM + IDX_HEADS + 2 * D_HEADS
    pad = jnp.zeros((w_in.shape[0], CD_PAD - used), F32)
    return jnp.concatenate([cq, iq, ckv, dq, dk, dv, dog, ik, iw, di, df, pad], axis=1).astype(BF16)


def _dsa_prep_kernel(ckv_ref, tail_ref, g_ref, ckv_o, ckvt_o, klo_o, khi_o):
    c = ckv_ref[...]
    cn = c * lax.rsqrt(jnp.mean(c * c, axis=-1, keepdims=True) + EPS) * g_ref[...]
    ckv_o[...] = cn.astype(BF16)
    ckvt_o[...] = cn.T.astype(BF16)
    tail = tail_ref[...]
    lane = lax.broadcasted_iota(I32, tail.shape, 1)
    klo_o[...] = jnp.where(lane < IDX_DIM, tail, 0.0).astype(BF16)
    khi_o[...] = jnp.where(lane >= IDX_DIM, pltpu.roll(tail, IDX_DIM, axis=1), 0.0).astype(BF16)


def _dsa_prep(h, ckv_g, n_rows):
    nb = n_rows // QB
    return pl.pallas_call(
        _dsa_prep_kernel,
        out_shape=(jax.ShapeDtypeStruct((nb, QB, C_DLAT), BF16), jax.ShapeDtypeStruct((nb, C_DLAT, QB), BF16),
                   jax.ShapeDtypeStruct((nb, QB, LANES), BF16), jax.ShapeDtypeStruct((nb, QB, LANES), BF16)),
        grid=(nb,),
        in_specs=[pl.BlockSpec((QB, C_DLAT), lambda i: (i, CD_CKV // C_DLAT)),
                  pl.BlockSpec((QB, LANES), lambda i: (i, CD_TAIL // LANES)),
                  pl.BlockSpec((1, C_DLAT), lambda i: (0, 0))],
        out_specs=(pl.BlockSpec((None, QB, C_DLAT), lambda i: (i, 0, 0)),
                   pl.BlockSpec((None, C_DLAT, QB), lambda i: (i, 0, 0)),
                   pl.BlockSpec((None, QB, LANES), lambda i: (i, 0, 0)),
                   pl.BlockSpec((None, QB, LANES), lambda i: (i, 0, 0))),
        compiler_params=_cparams("parallel"),
        name="dsa_prep",
    )(h, h, ckv_g.reshape(1, -1))


def _sortable_key(x):
    b = lax.bitcast_convert_type(x, I32)
    key = b ^ ((b >> 31) & 0x7FFFFFFF)
    return jnp.where(key == -1, 0, key)


def _indexer_kernel(iq_ref, tail_ref, klo_ref, khi_ref, tri_ref, mask_ref, key_ref, *, k_sel):
    j = pl.program_id(1)
    nk = key_ref.shape[0]
    w_t = tail_ref[...].T
    pairs = [iq_ref[:, LANES * p:LANES * (p + 1)].astype(BF16) for p in range(IDX_HEADS // 2)]

    def score_chunk(kc, carry):
        klo = klo_ref[kc]
        khi = khi_ref[kc]
        acc = jnp.zeros((QB, QB), F32)
        for p in range(IDX_HEADS // 2):
            r0 = TAIL_IW + 2 * p
            acc = acc + jnp.maximum(_dot_nt(klo, pairs[p]), 0.0) * w_t[r0:r0 + 1, :]
            acc = acc + jnp.maximum(_dot_nt(khi, pairs[p]), 0.0) * w_t[r0 + 1:r0 + 2, :]
        key_ref[kc] = _sortable_key(acc)
        return carry

    lax.fori_loop(0, j + 1, score_chunk, 0)
    s_loc = lax.broadcasted_iota(I32, (QB, QB), 0)
    t_loc = lax.broadcasted_iota(I32, (QB, QB), 1)
    key_ref[j] = jnp.where(s_loc > t_loc, INT_MIN, key_ref[j])

    t_abs = j * QB + lax.broadcasted_iota(I32, (1, QB), 1)
    k_row = jnp.minimum(k_sel, t_abs + 1)

    def count(pred):
        def add(kc, acc):
            return acc + jnp.where(pred(key_ref[kc]), 1, 0)
        return jnp.sum(lax.fori_loop(0, j + 1, add, jnp.zeros((QB, QB), I32)), axis=0, keepdims=True)

    def unfinished(state):
        lo, hi = state
        return jnp.max(jnp.where(lo < hi, 1, 0)) > 0

    def halve(state):
        lo, hi = state
        mid = (lo >> 1) + (hi >> 1) + (((lo & 1) + (hi & 1) + 1) >> 1)
        cnt = count(lambda k: k >= mid)
        enough = cnt >= k_row
        lo_n = jnp.where(enough, mid, lo)
        hi_n = jnp.where(cnt == k_row, mid, jnp.where(enough, hi, mid - 1))
        return lo_n, hi_n

    lo0 = jnp.full((1, QB), INT_MIN + 1, I32)
    hi0 = jnp.full((1, QB), 2 ** 31 - 1, I32)
    tau, _ = lax.while_loop(unfinished, halve, (lo0, hi0))
    n_ge = count(lambda k: k >= tau)
    has_tie = jnp.max(jnp.where(n_ge != k_row, 1, 0)) > 0

    @pl.when(jnp.logical_not(has_tie))
    def _():
        def put(kc, carry):
            mask_ref[kc] = jnp.where(key_ref[kc] >= tau, 1.0, 0.0).astype(BF16)
            return carry
        lax.fori_loop(0, j + 1, put, 0)

    @pl.when(has_tie)
    def _():
        need = (k_row - count(lambda k: k > tau)).astype(F32)

        def put(kc, seen):
            k = key_ref[kc]
            eq = jnp.where(k == tau, 1.0, 0.0)
            before = _dot(tri_ref[...], eq.astype(BF16)) + seen
            take = jnp.where(k > tau, 1.0, jnp.where(before < need, eq, 0.0))
            mask_ref[kc] = take.astype(BF16)
            return seen + jnp.sum(eq, axis=0, keepdims=True)
        lax.fori_loop(0, j + 1, put, jnp.zeros((1, QB), F32))

    def clear(kc, carry):
        mask_ref[kc] = jnp.zeros((QB, QB), BF16)
        return carry
    lax.fori_loop(j + 1, nk, clear, 0)


def _indexer(h, klo, khi, batch, seq):
    nq = seq // QB
    k_sel = min(TOPK_MAX, seq // 4)
    r = np.arange(QB)
    tri = jnp.asarray(r[None, :] < r[:, None], BF16)
    keyblk = pl.BlockSpec((None, nq, QB, LANES), lambda b, j: (b, 0, 0, 0))
    return pl.pallas_call(
        functools.partial(_indexer_kernel, k_sel=k_sel),
        out_shape=jax.ShapeDtypeStruct((batch, nq, QB, seq), BF16),
        grid=(batch, nq),
        in_specs=[pl.BlockSpec((QB, IDX_HEADS * IDX_DIM), lambda b, j: (b * nq + j, CD_IQ // (IDX_HEADS * IDX_DIM))),
                  pl.BlockSpec((QB, LANES), lambda b, j: (b * nq + j, CD_TAIL // LANES)),
                  keyblk, keyblk, pl.BlockSpec((QB, QB), lambda b, j: (0, 0))],
        out_specs=pl.BlockSpec((None, nq, QB, QB), lambda b, j: (b, 0, 0, j)),
        scratch_shapes=[pltpu.VMEM((nq, QB, QB), I32)],
        compiler_params=_cparams("parallel", "parallel"),
        name="dsa_indexer",
    )(h, h, klo.reshape(batch, nq, QB, LANES), khi.reshape(batch, nq, QB, LANES), tri)


def _dsa_attn_kernel(cq_ref, mask_ref, ckv_ref, ckvt_ref, wuk_ref, wuvt_ref, bias_ref, o_ref,
                     qt_ref, m_ref, l_ref, acc_ref, ot_ref):
    j = pl.program_id(1)
    for hh in range(C_HEADS):
        q_h = cq_ref[:, C_DH * hh:C_DH * (hh + 1)].astype(BF16)
        qt_ref[hh] = (_dot_nt(wuk_ref[hh], q_h) * (C_DH ** -0.5)).astype(BF16)
    m_ref[...] = jnp.full(m_ref.shape, NEG, F32)
    l_ref[...] = jnp.zeros_like(l_ref)
    acc_ref[...] = jnp.zeros_like(acc_ref)

    def body(kc, carry):
        ck = ckv_ref[kc]
        ckt = ckvt_ref[kc]
        sel = mask_ref[kc].astype(F32) > 0.5
        which = jnp.minimum(j - kc, 2)
        for hh in range(C_HEADS):
            logit = jnp.where(sel, _dot(ck, qt_ref[hh]) + bias_ref[hh, which], NEG)
            m_old = m_ref[hh]
            m_new = jnp.maximum(m_old, jnp.max(logit, axis=0, keepdims=True))
            alpha = jnp.exp(m_old - m_new)
            p = jnp.where(sel, jnp.exp(logit - m_new), 0.0)
            l_ref[hh] = alpha * l_ref[hh] + jnp.sum(p, axis=0, keepdims=True)
            acc_ref[hh] = alpha * acc_ref[hh] + _dot(ckt, p.astype(BF16))
            m_ref[hh] = m_new
        return carry

    lax.fori_loop(0, j + 1, body, 0)
    for hh in range(C_HEADS):
        o_lat = (acc_ref[hh] * (1.0 / l_ref[hh])).astype(BF16)
        ot_ref[C_DH * hh:C_DH * (hh + 1), :] = _dot(wuvt_ref[hh], o_lat)
    o_ref[...] = ot_ref[...].T


def _rel_bias_tiles(rel_table):
    s = np.arange(QB)[:, None]
    t = np.arange(QB)[None, :]
    dist = np.stack([np.maximum(t - s, 0), QB + t - s, np.full((QB, QB), 2 * QB)]).astype(np.int32)
    n = jnp.asarray(dist)
    max_exact = REL_BUCKETS // 2
    large = max_exact + (jnp.log(jnp.maximum(n, 1).astype(F32) / max_exact)
                         / math.log(REL_MAX_DIST / max_exact) * (REL_BUCKETS - max_exact)).astype(I32)
    bucket = jnp.where(n < max_exact, n, jnp.minimum(large, REL_BUCKETS - 1))
    return rel_table.astype(F32)[bucket].transpose(3, 0, 1, 2)


def _dsa_attention(h, mask, ckv, ckvt, w_uk, w_uv, rel_table, batch, seq):
    nq = seq // QB
    wuk = w_uk.transpose(1, 0, 2).astype(BF16)
    wuvt = w_uv.transpose(1, 2, 0).astype(BF16)
    bias = _rel_bias_tiles(rel_table)
    return pl.pallas_call(
        _dsa_attn_kernel,
        out_shape=jax.ShapeDtypeStruct((batch * seq, C_HEADS * C_DH), F32),
        grid=(batch, nq),
        in_specs=[pl.BlockSpec((QB, C_HEADS * C_DH), lambda b, j: (b * nq + j, CD_CQ // (C_HEADS * C_DH))),
                  pl.BlockSpec((None, nq, QB, QB), lambda b, j: (b, 0, 0, j)),
                  pl.BlockSpec((None, nq, QB, C_DLAT), lambda b, j: (b, 0, 0, 0)),
                  pl.BlockSpec((None, nq, C_DLAT, QB), lambda b, j: (b, 0, 0, 0)),
                  _const_spec(wuk), _const_spec(wuvt), _const_spec(bias)],
        out_specs=pl.BlockSpec((QB, C_HEADS * C_DH), lambda b, j: (b * nq + j, 0)),
        scratch_shapes=[pltpu.VMEM((C_HEADS, C_DLAT, QB), BF16), pltpu.VMEM((C_HEADS, 1, QB), F32),
                        pltpu.VMEM((C_HEADS, 1, QB), F32), pltpu.VMEM((C_HEADS, C_DLAT, QB), F32),
                        pltpu.VMEM((C_HEADS * C_DH, QB), F32)],
        compiler_params=_cparams("parallel", "parallel"),
        name="dsa_attention",
    )(h, mask, ckv.reshape(batch, nq, QB, C_DLAT), ckvt.reshape(batch, nq, C_DLAT, QB), wuk, wuvt, bias)


ROUTER_BLOCK = 512
MOE_TILE = 256


def _router_kernel(x_ref, wh_ref, wl_ref, b_ref, idx_ref, wgt_ref):
    x = x_ref[...]
    xh = x.astype(BF16)
    xl = (x - xh.astype(F32)).astype(BF16)
    logit = _dot_nt(wh_ref[...], xh) + _dot_nt(wl_ref[...], xh) + _dot_nt(wh_ref[...], xl)
    aff = _sigmoid(logit)
    sel = aff + b_ref[...]
    s_rows = [sel[e:e + 1] for e in range(N_EXPERTS)]
    a_rows = [aff[e:e + 1] for e in range(N_EXPERTS)]
    n = EXPERTS_PER_GROUP

    g_best = jnp.zeros(s_rows[0].shape, I32)
    best = None
    for g in range(N_GROUPS):
        v = s_rows[g * n:(g + 1) * n]
        top2 = None
        for a in range(n):
            for b in range(a + 1, n):
                pair = v[a] + v[b]
                top2 = pair if top2 is None else jnp.maximum(top2, pair)
        if best is None:
            best = top2
        else:
            upd = top2 > best
            g_best = jnp.where(upd, g, g_best)
            best = jnp.where(upd, top2, best)

    sv, av = [], []
    for i in range(n):
        s_i, a_i = s_rows[i], a_rows[i]
        for g in range(1, N_GROUPS):
            pick = g_best == g
            s_i = jnp.where(pick, s_rows[g * n + i], s_i)
            a_i = jnp.where(pick, a_rows[g * n + i], a_i)
        sv.append(s_i)
        av.append(a_i)

    i1, s1, a1 = jnp.zeros_like(g_best), sv[0], av[0]
    for i in range(1, n):
        upd = sv[i] > s1
        i1 = jnp.where(upd, i, i1)
        s1 = jnp.where(upd, sv[i], s1)
        a1 = jnp.where(upd, av[i], a1)
    i2 = jnp.zeros_like(g_best)
    s2 = jnp.full(s1.shape, -jnp.inf, F32)
    a2 = jnp.zeros_like(a1)
    for i in range(n):
        cand = jnp.where(i1 == i, -jnp.inf, sv[i])
        upd = cand > s2
        i2 = jnp.where(upd, i, i2)
        s2 = jnp.where(upd, cand, s2)
        a2 = jnp.where(upd, av[i], a2)

    tot = a1 + a2
    idx_ref[0:1, :] = g_best * n + i1
    idx_ref[1:2, :] = g_best * n + i2
    wgt_ref[0:1, :] = a1 / tot
    wgt_ref[1:2, :] = a2 / tot


def _router(x2d, w_router, b_router):
    t, d = x2d.shape
    tb = min(ROUTER_BLOCK, t)
    wt = w_router.T.astype(F32)
    wh = wt.astype(BF16)
    wl = (wt - wh.astype(F32)).astype(BF16)
    return pl.pallas_call(
        _router_kernel,
        out_shape=(jax.ShapeDtypeStruct((TOP_K, t), I32), jax.ShapeDtypeStruct((TOP_K, t), F32)),
        grid=(t // tb,),
        in_specs=[pl.BlockSpec((tb, d), lambda i: (i, 0)),
                  pl.BlockSpec((N_EXPERTS, d), lambda i: (0, 0)), pl.BlockSpec((N_EXPERTS, d), lambda i: (0, 0)),
                  pl.BlockSpec((N_EXPERTS, 1), lambda i: (0, 0))],
        out_specs=(pl.BlockSpec((TOP_K, tb), lambda i: (0, i)), pl.BlockSpec((TOP_K, tb), lambda i: (0, i))),
        compiler_params=_cparams("parallel"),
        name="moe_router",
    )(x2d, wh, wl, b_router.reshape(-1, 1).astype(F32))


def _route_tables(idx, wgt, tm, n_tiles):
    e = idx.reshape(-1)
    n_pairs = e.shape[0]
    onehot = (e[:, None] == jnp.arange(N_EXPERTS, dtype=I32)[None, :]).astype(I32)
    rank = jnp.sum((jnp.cumsum(onehot, axis=0) - onehot) * onehot, axis=1)
    tiles_e = (jnp.sum(onehot, axis=0) + tm - 1) // tm
    tile_end = jnp.cumsum(tiles_e)
    pos = (tile_end - tiles_e)[e] * tm + rank
    code = jnp.full((n_tiles * tm,), -1, I32).at[pos].set(jnp.arange(n_pairs, dtype=I32))
    is_pad = (code < 0).astype(I32)
    code = jnp.where(code >= 0, code, -jnp.cumsum(is_pad))
    gate = jnp.zeros((n_tiles * tm,), F32).at[pos].set(wgt.reshape(-1))
    tile_expert = jnp.minimum(jnp.searchsorted(tile_end, jnp.arange(n_tiles, dtype=I32), side="right"),
                              N_EXPERTS - 1).astype(I32)
    return code, tile_expert, gate.reshape(-1, 1)


def _moe_kernel(code_ref, texp_ref, x_hbm, gate_ref, w1_ref, w3_ref, w2_ref, y_hbm,
                xbuf, obuf, w1b, w3b, w2b, sem_in, sem_out, *, tm, n_tok):
    i = pl.program_id(0)
    n = pl.num_programs(0)
    slot = i % 2

    def start_gather(tile, s):
        def issue(r, carry):
            code = code_ref[tile * tm + r]
            tok = jnp.where(code >= n_tok, code - n_tok, jnp.maximum(code, 0))
            pltpu.make_async_copy(x_hbm.at[pl.ds(tok, 1)], xbuf.at[s, pl.ds(r, 1)], sem_in.at[s]).start()
            return carry
        lax.fori_loop(0, tm, issue, 0)

    def start_scatter(tile, s):
        def issue(r, carry):
            code = code_ref[tile * tm + r]
            dst = jnp.where(code >= 0, code, 2 * n_tok - 1 - code)
            pltpu.make_async_copy(obuf.at[s, pl.ds(r, 1)], y_hbm.at[pl.ds(dst, 1)], sem_out.at[s]).start()
            return carry
        lax.fori_loop(0, tm, issue, 0)

    def wait_gather(s):
        pltpu.make_async_copy(x_hbm.at[pl.ds(0, tm)], xbuf.at[s], sem_in.at[s]).wait()

    def wait_scatter(s):
        pltpu.make_async_copy(obuf.at[s], y_hbm.at[pl.ds(0, tm)], sem_out.at[s]).wait()

    @pl.when(i == 0)
    def _():
        start_gather(0, 0)

    @pl.when(i + 1 < n)
    def _():
        start_gather(i + 1, 1 - slot)

    @pl.when(jnp.logical_or(i == 0, texp_ref[i] != texp_ref[jnp.maximum(i - 1, 0)]))
    def _():
        w1b[...] = w1_ref[...].astype(BF16)
        w3b[...] = w3_ref[...].astype(BF16)
        w2b[...] = w2_ref[...].astype(BF16)

    wait_gather(slot)
    x = xbuf[slot].astype(BF16)
    hid = _silu(_dot(x, w1b[...])) * _dot(x, w3b[...]) * gate_ref[...]
    obuf[slot] = _dot(hid.astype(BF16), w2b[...])

    @pl.when(i >= 1)
    def _():
        wait_scatter(1 - slot)

    start_scatter(i, slot)

    @pl.when(i == n - 1)
    def _():
        wait_scatter(slot)


def _moe_experts(x2d, code, tile_expert, gate, w1, w3, w2, tm):
    t, d = x2d.shape
    n_tiles = tile_expert.shape[0]
    f = w1.shape[-1]
    grid_spec = pltpu.PrefetchScalarGridSpec(
        num_scalar_prefetch=2,
        grid=(n_tiles,),
        in_specs=[pl.BlockSpec(memory_space=pl.ANY),
                  pl.BlockSpec((tm, 1), lambda i, code, texp: (i, 0)),
                  pl.BlockSpec((None, d, f), lambda i, code, texp: (texp[i], 0, 0)),
                  pl.BlockSpec((None, d, f), lambda i, code, texp: (texp[i], 0, 0)),
                  pl.BlockSpec((None, f, d), lambda i, code, texp: (texp[i], 0, 0))],
        out_specs=pl.BlockSpec(memory_space=pl.ANY),
        scratch_shapes=[pltpu.VMEM((2, tm, d), F32), pltpu.VMEM((2, tm, d), F32),
                        pltpu.VMEM((d, f), BF16), pltpu.VMEM((d, f), BF16), pltpu.VMEM((f, d), BF16),
                        pltpu.SemaphoreType.DMA((2,)), pltpu.SemaphoreType.DMA((2,))])
    return pl.pallas_call(
        functools.partial(_moe_kernel, tm=tm, n_tok=t),
        out_shape=jax.ShapeDtypeStruct((n_tiles * tm, d), F32),
        grid_spec=grid_spec,
        compiler_params=_cparams("arbitrary"),
        name="moe_experts",
    )(code, tile_expert, x2d, gate, w1, w3, w2)


def _combine_ln_kernel(x_ref, y0_ref, y1_ref, g_ref, b_ref, o_ref):
    o_ref[...] = _layer_norm_rows(DN_ALPHA * x_ref[...] + y0_ref[...] + y1_ref[...], g_ref[...], b_ref[...])


def _combine_ln(x2d, y2, g, b, tm=256):
    t, d = x2d.shape
    nb = t // tm
    return pl.pallas_call(
        _combine_ln_kernel,
        out_shape=jax.ShapeDtypeStruct((t, d), F32),
        grid=(nb,),
        in_specs=[pl.BlockSpec((tm, d), lambda i: (i, 0)), pl.BlockSpec((tm, d), lambda i: (i, 0)),
                  pl.BlockSpec((tm, d), lambda i: (i + nb, 0)),
                  pl.BlockSpec((1, d), lambda i: (0, 0)), pl.BlockSpec((1, d), lambda i: (0, 0))],
        out_specs=pl.BlockSpec((tm, d), lambda i: (i, 0)),
        compiler_params=_cparams("parallel"),
        name="moe_combine_ln",
    )(x2d, y2, y2, g.reshape(1, d), b.reshape(1, d))


def _moe_ln(x2d, w_router, b_router, w1, w3, w2, g, b):
    t = x2d.shape[0]
    tm = min(MOE_TILE, t)
    idx, wgt = _router(x2d, w_router, b_router)
    n_tiles = TOP_K * t // tm + N_EXPERTS
    code, tile_expert, gate = _route_tables(idx, wgt, tm, n_tiles)
    y2 = _moe_experts(x2d, code, tile_expert, gate, w1, w3, w2, tm)
    return _combine_ln(x2d, y2, g, b, tm)


def _gate_rows(col, batch, seq):
    tb = min(TIME_BLOCK, seq)
    return col.reshape(batch, seq, D_HEADS).transpose(0, 2, 1).reshape(batch, D_HEADS, seq // tb, tb // CHUNK, CHUNK)


def _mix_cd(x2d, w_in, rel_table, ckv_g, w_uk, w_uv, conv_w, gate_b, d_norm_g, batch, seq):
    t = x2d.shape[0]
    h = _project(x2d, _pack_cd(w_in), min(1024, t), 512)
    ckv, ckvt, klo, khi = _dsa_prep(h, ckv_g, t)
    mask = _indexer(h, klo, khi, batch, seq)
    oc = _dsa_attention(h, mask, ckv, ckvt, w_uk, w_uv, rel_table, batch, seq)
    tail = h[:, CD_TAIL:CD_TAIL + LANES]
    ig_rows = _gate_rows(tail[:, TAIL_DI:TAIL_DI + D_HEADS], batch, seq)
    fg_rows = _gate_rows(tail[:, TAIL_DF:TAIL_DF + D_HEADS], batch, seq)
    od = _mlstm(h, conv_w, ig_rows, fg_rows, gate_b, d_norm_g, batch, seq,
                q_off=CD_DQ // D_DK, k_off=CD_DK // D_DK, v_off=CD_DV // D_DV, g_off=CD_DOG // D_DV)
    return oc, od


def kernel(x, w_in_ab, w_out_ab, hgrn_lb_logits, a_norm_g, gla_wa2, gla_ba2, b_norm_g, w_in_cd, w_out_cd,
           ckv_norm_g, w_uk, w_uv, mlstm_conv_w, mlstm_gate_b, d_norm_g, rel_table, w_router, b_router,
           moe_w1, moe_w3, moe_w2, ln_g, ln_b):
    batch, seq, d = x.shape
    x2d = x.reshape(batch * seq, d)
    for layer in range(DEPTH):
        li = layer // 2
        if layer % 2 == 0:
            mix_a, mix_b = _mix_ab(x2d, w_in_ab[li], hgrn_lb_logits, li, a_norm_g[li], gla_wa2[li], gla_ba2[li],
                                   b_norm_g[li], batch, seq)
            w_out = w_out_ab[li]
        else:
            mix_a, mix_b = _mix_cd(x2d, w_in_cd[li], rel_table, ckv_norm_g[li], w_uk[li], w_uv[li],
                                   mlstm_conv_w[li], mlstm_gate_b[li], d_norm_g[li], batch, seq)
            w_out = w_out_cd[li]
        ka = mix_a.shape[1]
        x2d = _outproj_ln(x2d, mix_a, mix_b, w_out[:ka].astype(BF16), w_out[ka:].astype(BF16),
                          ln_g[layer, 0], ln_b[layer, 0])
        x2d = _moe_ln(x2d, w_router, b_router, moe_w1[layer], moe_w3[layer], moe_w2[layer],
                      ln_g[layer, 1], ln_b[layer, 1])
    return x2d.reshape(batch, seq, d)
```

```python
import functools
import math

import numpy as np
import jax
import jax.numpy as jnp
from jax import lax
from jax.experimental import pallas as pl
from jax.experimental.pallas import tpu as pltpu

F32 = jnp.float32
BF16 = jnp.bfloat16
I32 = jnp.int32

D_MODEL = 2048
DEPTH = 2
A_HEADS, A_DK, A_DV = 8, 128, 128
B_HEADS, B_DK, B_DV = 4, 128, 256
B_GATE_RANK, B_GATE_TAU = 16, 16.0
C_HEADS, C_DH, C_DLAT = 8, 128, 256
IDX_HEADS, IDX_DIM = 16, 64
TOPK_MAX = 256
D_HEADS, D_DK, D_DV = 4, 128, 256
CONV_K = 4
REL_BUCKETS, REL_MAX_DIST = 32, 128
N_EXPERTS, N_GROUPS, TOP_K, D_EXPERT = 16, 4, 2, 512
EXPERTS_PER_GROUP = N_EXPERTS // N_GROUPS
DN_ALPHA = (2 * DEPTH) ** 0.25
EPS = 1e-5

LANES = 128
VMEM_LIMIT = 56 * 1024 * 1024

CHUNK = 64
N_LEVELS = 6
TIME_BLOCK = 256
QB = 128
NEG = -1e30
INT_MIN = -2 ** 31


def _cparams(*sem):
    return pltpu.CompilerParams(dimension_semantics=sem, vmem_limit_bytes=VMEM_LIMIT)


def _dot(a, b):
    return jnp.dot(a, b, preferred_element_type=F32)


def _dot_nt(a, b):
    return lax.dot_general(a, b, (((1,), (1,)), ((), ())), preferred_element_type=F32)


def _dot_tn(a, b):
    return lax.dot_general(a, b, (((0,), (0,)), ((), ())), preferred_element_type=F32)


def _split3(a):
    hi = a.astype(BF16)
    r1 = a - hi.astype(F32)
    mid = r1.astype(BF16)
    lo = (r1 - mid.astype(F32)).astype(BF16)
    return hi, mid, lo


def _dot01(m01, a):
    hi, mid, lo = _split3(a)
    return _dot(m01, hi) + _dot(m01, mid) + _dot(m01, lo)


def _sigmoid(x):
    return 1.0 / (1.0 + jnp.exp(-x))


def _silu(x):
    return x * _sigmoid(x)


def _log_sigmoid(x):
    return jnp.minimum(x, 0.0) - jnp.log(1.0 + jnp.exp(-jnp.abs(x)))


def _proj_kernel(x_ref, w_ref, o_ref, xb_ref):
    @pl.when(pl.program_id(1) == 0)
    def _():
        xb_ref[...] = x_ref[...].astype(BF16)

    o_ref[...] = _dot(xb_ref[...], w_ref[...])


def _project(x, w, tm, tn):
    m, k = x.shape
    n = w.shape[1]
    return pl.pallas_call(
        _proj_kernel,
        out_shape=jax.ShapeDtypeStruct((m, n), F32),
        grid=(m // tm, n // tn),
        in_specs=[pl.BlockSpec((tm, k), lambda i, j: (i, 0)),
                  pl.BlockSpec((k, tn), lambda i, j: (0, j))],
        out_specs=pl.BlockSpec((tm, tn), lambda i, j: (i, j)),
        scratch_shapes=[pltpu.VMEM((tm, k), BF16)],
        compiler_params=_cparams("parallel", "arbitrary"),
        name="in_proj",
    )(x, w)


def _layer_norm_rows(z, g, b):
    mu = jnp.mean(z, axis=-1, keepdims=True)
    zc = z - mu
    var = jnp.mean(zc * zc, axis=-1, keepdims=True)
    return zc * lax.rsqrt(var + EPS) * g + b


def _outproj_ln_kernel(x_ref, ma_ref, mb_ref, wa_ref, wb_ref, g_ref, b_ref, o_ref):
    mixed = _dot(ma_ref[...].astype(BF16), wa_ref[...]) + _dot(mb_ref[...].astype(BF16), wb_ref[...])
    o_ref[...] = _layer_norm_rows(DN_ALPHA * x_ref[...] + mixed, g_ref[...], b_ref[...])


def _outproj_ln(x, mix_a, mix_b, w_a, w_b, g, b, tm=256):
    m, d = x.shape
    ka, kb = mix_a.shape[1], mix_b.shape[1]
    row = lambda i: (i, 0)
    fixed = lambda i: (0, 0)
    return pl.pallas_call(
        _outproj_ln_kernel,
        out_shape=jax.ShapeDtypeStruct((m, d), F32),
        grid=(m // tm,),
        in_specs=[pl.BlockSpec((tm, d), row), pl.BlockSpec((tm, ka), row), pl.BlockSpec((tm, kb), row),
                  pl.BlockSpec((ka, d), fixed), pl.BlockSpec((kb, d), fixed),
                  pl.BlockSpec((1, d), fixed), pl.BlockSpec((1, d), fixed)],
        out_specs=pl.BlockSpec((tm, d), row),
        compiler_params=_cparams("parallel"),
        name="out_proj_ln",
    )(x, mix_a, mix_b, w_a, w_b, g.reshape(1, d), b.reshape(1, d))


def _chunk_constants():
    t = np.arange(CHUNK)
    tri = (t[:, None] >= t[None, :]).astype(np.float32)
    cum, pair, odd = [tri], [], []
    for lev in range(1, N_LEVELS + 1):
        c = CHUNK >> lev
        mid = (t // (2 * c)) * 2 * c + c - 1
        cum.append(tri - tri[mid])
        pair.append((t[:, None] // (2 * c) == t[None, :] // (2 * c)).astype(np.float32))
        odd.append(np.broadcast_to((((t // c) & 1) == 1).astype(np.float32)[:, None], (CHUNK, LANES)))
    pair.append(np.eye(CHUNK, dtype=np.float32))
    return (jnp.asarray(np.concatenate(cum, 0), BF16), jnp.asarray(np.stack(pair), F32),
            jnp.asarray(np.stack(odd), F32))


def _glr_chunk(q, k, v, la, st_ref, cum_ref, pair_ref, odd_ref):
    d = _dot01(cum_ref[...], la)
    bcum = d[0:CHUNK]
    attn = pair_ref[N_LEVELS] * _dot_nt(q.astype(BF16), k.astype(BF16))
    for lev in range(1, N_LEVELS + 1):
        e = jnp.exp(-jnp.abs(d[lev * CHUNK:(lev + 1) * CHUNK]))
        eq = e * odd_ref[lev - 1]
        ql = (q * eq).astype(BF16)
        kl = (k * (e - eq)).astype(BF16)
        attn = attn + pair_ref[lev - 1] * _dot_nt(ql, kl)
    st = st_ref[...]
    o = _dot_nt((q * jnp.exp(bcum)).astype(BF16), st.astype(BF16)) + _dot(attn.astype(BF16), v.astype(BF16))
    b_last = bcum[CHUNK - 1:CHUNK]
    kdec = (k * jnp.exp(b_last - bcum)).astype(BF16)
    st_ref[...] = st * jnp.exp(b_last) + _dot_tn(v.astype(BF16), kdec)
    return o


def _rms_gate(o, g, gate):
    ms = jnp.mean(o * o, axis=-1, keepdims=True)
    return o * lax.rsqrt(ms + EPS) * g * gate


HEAD_GROUP = 4


def _hgrn2_kernel(q_ref, f_ref, i_ref, g_ref, lb_ref, ng_ref, cum_ref, pair_ref, odd_ref, o_ref, st_ref):
    @pl.when(pl.program_id(2) == 0)
    def _():
        st_ref[...] = jnp.zeros_like(st_ref)

    def body(c, carry):
        rows = pl.ds(pl.multiple_of(c * CHUNK, CHUNK), CHUNK)
        for hh in range(HEAD_GROUP):
            ck = slice(A_DK * hh, A_DK * (hh + 1))
            cv = slice(A_DV * hh, A_DV * (hh + 1))
            lb = lb_ref[:, ck]
            f = lb + (1.0 - lb) * _sigmoid(f_ref[rows, ck])
            o = _glr_chunk(_silu(q_ref[rows, ck]), 1.0 - f, i_ref[rows, cv], jnp.log(f),
                           st_ref.at[hh], cum_ref, pair_ref, odd_ref)
            o_ref[rows, cv] = _rms_gate(o, ng_ref[...], _silu(g_ref[rows, cv]))
        return carry

    lax.fori_loop(0, q_ref.shape[0] // CHUNK, body, 0)


def _gla_kernel(q_ref, k_ref, v_ref, g_ref, r_ref, wa_ref, ba_ref, ng_ref, cum_ref, pair_ref, odd_ref,
                o_ref, st_ref):
    @pl.when(pl.program_id(2) == 0)
    def _():
        st_ref[...] = jnp.zeros_like(st_ref)

    def body(c, carry):
        rows = pl.ds(pl.multiple_of(c * CHUNK, CHUNK), CHUNK)
        pre = _dot(r_ref[rows, :].astype(BF16), wa_ref[...]) + ba_ref[...]
        la = _log_sigmoid(pre) * (1.0 / B_GATE_TAU)
        for hh in range(HEAD_GROUP):
            ck = slice(B_DK * hh, B_DK * (hh + 1))
            cv = slice(B_DV * hh, B_DV * (hh + 1))
            o = _glr_chunk(q_ref[rows, ck] * (B_DK ** -0.5), k_ref[rows, ck], v_ref[rows, cv], la[:, ck],
                           st_ref.at[hh], cum_ref, pair_ref, odd_ref)
            o_ref[rows, cv] = _rms_gate(o, ng_ref[...], _silu(g_ref[rows, cv]))
        return carry

    lax.fori_loop(0, q_ref.shape[0] // CHUNK, body, 0)


def _const_spec(arr):
    nd = arr.ndim
    return pl.BlockSpec(arr.shape, lambda *_: (0,) * nd)


def _hgrn2(h, lb, norm_g, batch, seq):
    tb = min(TIME_BLOCK, seq)
    nt = seq // tb
    ng = A_HEADS // HEAD_GROUP
    wk, wv = HEAD_GROUP * A_DK, HEAD_GROUP * A_DV
    consts = _chunk_constants()
    col = lambda seg, w: pl.BlockSpec((tb, w), lambda b, g, t, seg=seg: (b * nt + t, seg * ng + g))
    return pl.pallas_call(
        _hgrn2_kernel,
        out_shape=jax.ShapeDtypeStruct((batch * seq, A_HEADS * A_DV), F32),
        grid=(batch, ng, nt),
        in_specs=[col(0, wk), col(1, wk), col(2, wv), col(3, wv),
                  pl.BlockSpec((1, wk), lambda b, g, t: (0, g)),
                  pl.BlockSpec((1, A_DV), lambda b, g, t: (0, 0))] + [_const_spec(c) for c in consts],
        out_specs=pl.BlockSpec((tb, wv), lambda b, g, t: (b * nt + t, g)),
        scratch_shapes=[pltpu.VMEM((HEAD_GROUP, A_DV, A_DK), F32)],
        compiler_params=_cparams("parallel", "parallel", "arbitrary"),
        name="hgrn2",
    )(h, h, h, h, lb.reshape(1, -1), norm_g.reshape(1, -1), *consts)


def _gla(h, wa2p, ba2, norm_g, batch, seq, q_off, k_off, v_off, g_off, r_off):
    tb = min(TIME_BLOCK, seq)
    nt = seq // tb
    wk, wv = B_HEADS * B_DK, B_HEADS * B_DV
    consts = _chunk_constants()
    col = lambda off, w: pl.BlockSpec((tb, w), lambda b, g, t, off=off: (b * nt + t, off))
    fixed = lambda b, g, t: (0, 0)
    return pl.pallas_call(
        _gla_kernel,
        out_shape=jax.ShapeDtypeStruct((batch * seq, wv), F32),
        grid=(batch, 1, nt),
        in_specs=[col(q_off, wk), col(k_off, wk), col(v_off, wv), col(g_off, wv), col(r_off, LANES),
                  pl.BlockSpec((LANES, wk), fixed), pl.BlockSpec((1, wk), fixed),
                  pl.BlockSpec((1, B_DV), fixed)] + [_const_spec(c) for c in consts],
        out_specs=pl.BlockSpec((tb, wv), lambda b, g, t: (b * nt + t, 0)),
        scratch_shapes=[pltpu.VMEM((B_HEADS, B_DV, B_DK), F32)],
        compiler_params=_cparams("parallel", "parallel", "arbitrary"),
        name="gla",
    )(h, h, h, h, h, wa2p, ba2.reshape(1, -1), norm_g.reshape(1, -1), *consts)


AB_MAIN = 4 * A_HEADS * A_DK + 2 * B_HEADS * B_DK + 2 * B_HEADS * B_DV
AB_PAD = 7680


def _mix_ab(x2d, w_in, lb_logits, li, a_norm_g, wa2, ba2, b_norm_g, batch, seq):
    d = x2d.shape[1]
    w = jnp.pad(w_in.astype(BF16), ((0, 0), (0, AB_PAD - w_in.shape[1])))
    h = _project(x2d, w, min(1024, x2d.shape[0]), 512)
    lb = jnp.cumsum(jax.nn.softmax(lb_logits.astype(F32), axis=0), axis=0)[li]
    oa = _hgrn2(h, lb, a_norm_g, batch, seq)
    wa2p = jnp.concatenate([wa2, jnp.zeros((LANES - B_GATE_RANK, wa2.shape[1]), F32)], axis=0).astype(BF16)
    a_cols = 4 * A_HEADS * A_DK
    wk, wv = B_HEADS * B_DK, B_HEADS * B_DV
    ob = _gla(h, wa2p, ba2, b_norm_g, batch, seq, q_off=a_cols // wk, k_off=a_cols // wk + 1,
              v_off=(a_cols + 2 * wk) // wv, g_off=(a_cols + 2 * wk) // wv + 1, r_off=AB_MAIN // LANES)
    return oa, ob


CONV_HALO = 8


def _causal_conv(x_ref, w_ref, buf_ref, tail_ref):
    tb = x_ref.shape[0]
    x = x_ref[...]
    buf_ref[0:CONV_HALO, :] = tail_ref[...]
    buf_ref[CONV_HALO:CONV_HALO + tb, :] = x
    tail_ref[...] = x[tb - CONV_HALO:tb]
    y = w_ref[CONV_K - 1:CONV_K, :] * x
    for j in range(CONV_K - 1):
        y = y + w_ref[j:j + 1, :] * buf_ref[pl.ds(CONV_HALO - (CONV_K - 1) + j, tb), :]
    return y


def _row_to_col(row, eye):
    return jnp.sum(jnp.where(eye, row, 0.0), axis=1, keepdims=True)


def _mlstm_kernel(q_ref, k_ref, v_ref, og_ref, wq_ref, wk_ref, ig_ref, fg_ref, gb_ref, ng_ref, tri_ref,
                  o_ref, ct_ref, n_ref, m_ref, qt_ref, kt_ref, qs_ref, ks_ref, buf_ref):
    @pl.when(pl.program_id(2) == 0)
    def _():
        ct_ref[...] = jnp.zeros_like(ct_ref)
        n_ref[...] = jnp.zeros_like(n_ref)
        m_ref[...] = jnp.zeros_like(m_ref)
        qt_ref[...] = jnp.zeros_like(qt_ref)
        kt_ref[...] = jnp.zeros_like(kt_ref)

    qs_ref[...] = _silu(_causal_conv(q_ref, wq_ref, buf_ref, qt_ref))
    ks_ref[...] = _silu(_causal_conv(k_ref, wk_ref, buf_ref, kt_ref)) * (D_DK ** -0.5)

    r_i = lax.broadcasted_iota(I32, (CHUNK, CHUNK), 0)
    c_i = lax.broadcasted_iota(I32, (CHUNK, CHUNK), 1)
    eye = r_i == c_i
    causal = r_i >= c_i

    def body(c, carry):
        rows = pl.ds(pl.multiple_of(c * CHUNK, CHUNK), CHUNK)
        tri = tri_ref[...]
        for hh in range(D_HEADS):
            ck = slice(D_DK * hh, D_DK * (hh + 1))
            cv = slice(D_DV * hh, D_DV * (hh + 1))
            q = qs_ref[rows, ck]
            k = ks_ref[rows, ck]
            v = v_ref[rows, cv].astype(BF16)
            qb = q.astype(BF16)
            ig_row = ig_ref[hh, pl.ds(c, 1), :] + gb_ref[0, hh]
            lf_row = _log_sigmoid(fg_ref[hh, pl.ds(c, 1), :] + gb_ref[1, hh])
            hi, mid, lo = _split3(lf_row)
            bcum_row = _dot(hi, tri) + _dot(mid, tri) + _dot(lo, tri)
            bcum_col = _row_to_col(bcum_row, eye)
            ig_col = _row_to_col(ig_row, eye)
            m_prev = m_ref[hh, :, 0:1]
            log_w = jnp.where(causal, bcum_col - bcum_row + ig_row, NEG)
            log_inter = bcum_col + m_prev
            m_t = jnp.maximum(jnp.max(log_w, axis=1, keepdims=True), log_inter)
            s = _dot_nt(qb, k.astype(BF16)) * jnp.exp(log_w - m_t)
            w_inter = jnp.exp(log_inter - m_t)
            num = _dot(s.astype(BF16), v) + w_inter * _dot_nt(qb, ct_ref[hh].astype(BF16))
            qn = jnp.sum(s, axis=1, keepdims=True) + w_inter * jnp.sum(q * n_ref[hh], axis=1, keepdims=True)
            h = num / jnp.maximum(jnp.abs(qn), jnp.exp(-m_t))
            o_ref[rows, cv] = _rms_gate(h, ng_ref[...], _sigmoid(og_ref[rows, cv]))
            b_last = bcum_row[:, CHUNK - 1:CHUNK]
            log_u = b_last - bcum_col + ig_col
            m_new = jnp.maximum(b_last + m_prev, jnp.max(log_u, axis=0, keepdims=True))
            decay = jnp.exp(b_last + m_prev - m_new)
            ku = k * jnp.exp(log_u - m_new)
            ct_ref[hh] = decay * ct_ref[hh] + _dot_tn(v, ku.astype(BF16))
            n_ref[hh] = decay * n_ref[hh] + jnp.sum(ku, axis=0, keepdims=True)
            m_ref[hh] = jnp.broadcast_to(m_new, (1, LANES))
        return carry

    lax.fori_loop(0, q_ref.shape[0] // CHUNK, body, 0)


def _mlstm(h, conv_w, ig_rows, fg_rows, gate_b, norm_g, batch, seq, q_off, k_off, v_off, g_off):
    tb = min(TIME_BLOCK, seq)
    nt = seq // tb
    nc = tb // CHUNK
    wk, wv = D_HEADS * D_DK, D_HEADS * D_DV
    t = np.arange(CHUNK)
    tri = jnp.asarray(t[:, None] <= t[None, :], BF16)
    gb = jnp.broadcast_to(gate_b.reshape(2, D_HEADS, 1, 1), (2, D_HEADS, 1, CHUNK)).astype(F32)
    col = lambda off, w: pl.BlockSpec((tb, w), lambda b, g, t, off=off: (b * nt + t, off))
    gate = pl.BlockSpec((None, D_HEADS, None, nc, CHUNK), lambda b, g, t: (b, 0, t, 0, 0))
    fixed = lambda b, g, t: (0, 0)
    return pl.pallas_call(
        _mlstm_kernel,
        out_shape=jax.ShapeDtypeStruct((batch * seq, wv), F32),
        grid=(batch, 1, nt),
        in_specs=[col(q_off, wk), col(k_off, wk), col(v_off, wv), col(g_off, wv),
                  pl.BlockSpec((CONV_K, wk), lambda b, g, t: (0, 0)),
                  pl.BlockSpec((CONV_K, wk), lambda b, g, t: (0, 1)),
                  gate, gate,
                  pl.BlockSpec((2, D_HEADS, 1, CHUNK), lambda b, g, t: (0, 0, 0, 0)),
                  pl.BlockSpec((1, D_DV), fixed), pl.BlockSpec((CHUNK, CHUNK), fixed)],
        out_specs=pl.BlockSpec((tb, wv), lambda b, g, t: (b * nt + t, 0)),
        scratch_shapes=[pltpu.VMEM((D_HEADS, D_DV, D_DK), F32), pltpu.VMEM((D_HEADS, 1, D_DK), F32),
                        pltpu.VMEM((D_HEADS, 1, LANES), F32),
                        pltpu.VMEM((CONV_HALO, wk), F32), pltpu.VMEM((CONV_HALO, wk), F32),
                        pltpu.VMEM((tb, wk), F32), pltpu.VMEM((tb, wk), F32),
                        pltpu.VMEM((tb + CONV_HALO, wk), F32)],
        compiler_params=_cparams("parallel", "parallel", "arbitrary"),
        name="mlstm",
    )(h, h, h, h, conv_w, conv_w, ig_rows, fg_rows, gb, norm_g.reshape(1, -1), tri)


CD_CQ, CD_IQ, CD_DQ, CD_DK, CD_DV, CD_DOG, CD_CKV, CD_TAIL = 0, 1024, 2048, 2560, 3072, 4096, 5120, 5376
CD_PAD = 5632
TAIL_IK, TAIL_IW, TAIL_DI, TAIL_DF = 0, 64, 80, 84


def _pack_cd(w_in):
    cq, ckv, iq, ik, iw, dq, dk, dv, di, df, dog = jnp.split(
        w_in, [int(i) for i in np.cumsum(
            (C_HEADS * C_DH, C_DLAT, IDX_HEADS * IDX_DIM, IDX_DIM, IDX_HEADS, D_HEADS * D_DK, D_HEADS * D_DK,
             D_HEADS * D_DV, D_HEADS, D_HEADS))], axis=1)
    used = CD_TAIL + IDX_DIM + IDX_HEADS + 2 * D_HEADS
    pad = jnp.zeros((w_in.shape[0], CD_PAD - used), F32)
    return jnp.concatenate([cq, iq, dq, dk, dv, dog, ckv, ik, iw, di, df, pad], axis=1).astype(BF16)


KB = 2 * QB


def _dsa_prep_kernel(ckv_ref, tail_ref, g_ref, ckv_o, ckvt_o, kdup_o):
    c = ckv_ref[...]
    cn = c * lax.rsqrt(jnp.mean(c * c, axis=-1, keepdims=True) + EPS) * g_ref[...]
    ckv_o[...] = cn.astype(BF16)
    ckvt_o[...] = cn.T.astype(BF16)
    tail = tail_ref[...]
    lane = lax.broadcasted_iota(I32, tail.shape, 1)
    kdup_o[...] = jnp.where(lane < IDX_DIM, tail, pltpu.roll(tail, IDX_DIM, axis=1)).astype(BF16)


def _dsa_prep(h, ckv_g, n_rows):
    nb = n_rows // KB
    return pl.pallas_call(
        _dsa_prep_kernel,
        out_shape=(jax.ShapeDtypeStruct((nb, KB, C_DLAT), BF16), jax.ShapeDtypeStruct((nb, C_DLAT, KB), BF16),
                   jax.ShapeDtypeStruct((nb, KB, LANES), BF16)),
        grid=(nb,),
        in_specs=[pl.BlockSpec((KB, C_DLAT), lambda i: (i, CD_CKV // C_DLAT)),
                  pl.BlockSpec((KB, LANES), lambda i: (i, CD_TAIL // LANES)),
                  pl.BlockSpec((1, C_DLAT), lambda i: (0, 0))],
        out_specs=(pl.BlockSpec((None, KB, C_DLAT), lambda i: (i, 0, 0)),
                   pl.BlockSpec((None, C_DLAT, KB), lambda i: (i, 0, 0)),
                   pl.BlockSpec((None, KB, LANES), lambda i: (i, 0, 0))),
        compiler_params=_cparams("parallel"),
        name="dsa_prep",
    )(h, h, ckv_g.reshape(1, -1))


def _sortable_key(x):
    b = lax.bitcast_convert_type(x, I32)
    key = b ^ ((b >> 31) & 0x7FFFFFFF)
    return jnp.where(key == -1, 0, key)


BISECT_STEPS = 2


def _indexer_kernel(iq_ref, tail_ref, kdup_ref, tri_ref, mask_ref, key_ref, wst_ref, *, k_sel):
    j = pl.program_id(1)
    nk = key_ref.shape[0]
    n_live = (j * QB + QB + KB - 1) // KB
    w_t = tail_ref[...].T
    lane = lax.broadcasted_iota(I32, (QB, LANES), 1)
    for p in range(IDX_HEADS // 2):
        pair = iq_ref[:, LANES * p:LANES * (p + 1)]
        wst_ref[p, 0:QB, :] = jnp.where(lane < IDX_DIM, pair, 0.0).astype(BF16)
        wst_ref[p, QB:2 * QB, :] = jnp.where(lane >= IDX_DIM, pair, 0.0).astype(BF16)

    s_loc = lax.broadcasted_iota(I32, (KB, QB), 0)
    t_abs = j * QB + lax.broadcasted_iota(I32, (1, QB), 1)

    def score_chunk(kc, carry):
        kd = kdup_ref[kc]
        acc = jnp.zeros((KB, QB), F32)
        for p in range(IDX_HEADS // 2):
            r0 = TAIL_IW + 2 * p
            both = _dot_nt(kd, wst_ref[p])
            acc = acc + jnp.maximum(both[:, 0:QB], 0.0) * w_t[r0:r0 + 1, :]
            acc = acc + jnp.maximum(both[:, QB:2 * QB], 0.0) * w_t[r0 + 1:r0 + 2, :]
        key_ref[kc] = jnp.where(kc * KB + s_loc > t_abs, INT_MIN, _sortable_key(acc))
        return carry

    lax.fori_loop(0, n_live, score_chunk, 0)
    k_row = jnp.minimum(k_sel, t_abs + 1)

    def count(pred):
        def add(kc, acc):
            return acc + jnp.where(pred(key_ref[kc]), 1, 0)
        return jnp.sum(lax.fori_loop(0, n_live, add, jnp.zeros((KB, QB), I32)), axis=0, keepdims=True)

    def unfinished(state):
        lo, hi = state
        return jnp.max(jnp.where(lo < hi, 1, 0)) > 0

    def halve(state):
        lo, hi = state
        mid = (lo >> 1) + (hi >> 1) + (((lo & 1) + (hi & 1) + 1) >> 1)
        cnt = count(lambda k: k >= mid)
        enough = cnt >= k_row
        lo_n = jnp.where(enough, mid, lo)
        hi_n = jnp.where(cnt == k_row, mid, jnp.where(enough, hi, mid - 1))
        return lo_n, hi_n

    def halve_steps(state):
        for _ in range(BISECT_STEPS):
            state = halve(state)
        return state

    lo0 = jnp.full((1, QB), INT_MIN + 1, I32)
    hi0 = jnp.full((1, QB), 2 ** 31 - 1, I32)
    tau, _ = lax.while_loop(unfinished, halve_steps, (lo0, hi0))
    n_ge = count(lambda k: k >= tau)
    has_tie = jnp.max(jnp.where(n_ge != k_row, 1, 0)) > 0

    @pl.when(jnp.logical_not(has_tie))
    def _():
        def put(kc, carry):
            mask_ref[kc] = jnp.where(key_ref[kc] >= tau, 1.0, 0.0).astype(BF16)
            return carry
        lax.fori_loop(0, n_live, put, 0)

    @pl.when(has_tie)
    def _():
        need = (k_row - count(lambda k: k > tau)).astype(F32)

        def put(kc, seen):
            k = key_ref[kc]
            eq = jnp.where(k == tau, 1.0, 0.0)
            before = _dot(tri_ref[...], eq.astype(BF16)) + seen
            take = jnp.where(k > tau, 1.0, jnp.where(before < need, eq, 0.0))
            mask_ref[kc] = take.astype(BF16)
            return seen + jnp.sum(eq, axis=0, keepdims=True)
        lax.fori_loop(0, n_live, put, jnp.zeros((1, QB), F32))

    def clear(kc, carry):
        mask_ref[kc] = jnp.zeros((KB, QB), BF16)
        return carry
    lax.fori_loop(n_live, nk, clear, 0)


def _indexer(h, kdup, batch, seq):
    nq = seq // QB
    nk = seq // KB
    k_sel = min(TOPK_MAX, seq // 4)
    r = np.arange(KB)
    tri = jnp.asarray(r[None, :] < r[:, None], BF16)
    return pl.pallas_call(
        functools.partial(_indexer_kernel, k_sel=k_sel),
        out_shape=jax.ShapeDtypeStruct((batch, nk, KB, seq), BF16),
        grid=(batch, nq),
        in_specs=[pl.BlockSpec((QB, IDX_HEADS * IDX_DIM), lambda b, j: (b * nq + j, CD_IQ // (IDX_HEADS * IDX_DIM))),
                  pl.BlockSpec((QB, LANES), lambda b, j: (b * nq + j, CD_TAIL // LANES)),
                  pl.BlockSpec((None, nk, KB, LANES), lambda b, j: (b, 0, 0, 0)),
                  pl.BlockSpec((KB, KB), lambda b, j: (0, 0))],
        out_specs=pl.BlockSpec((None, nk, KB, QB), lambda b, j: (b, 0, 0, j)),
        scratch_shapes=[pltpu.VMEM((nk, KB, QB), I32), pltpu.VMEM((IDX_HEADS // 2, 2 * QB, LANES), BF16)],
        compiler_params=_cparams("parallel", "parallel"),
        name="dsa_indexer",
    )(h, h, kdup.reshape(batch, nk, KB, LANES), tri)


def _dsa_attn_kernel(cq_ref, mask_ref, ckv_ref, ckvt_ref, wuk_ref, wuvt_ref, bias_ref, o_ref,
                     qt_ref, m_ref, l_ref, acc_ref, ot_ref):
    j = pl.program_id(1)
    for hh in range(C_HEADS):
        q_h = cq_ref[:, C_DH * hh:C_DH * (hh + 1)].astype(BF16)
        qt_ref[hh] = (_dot_nt(wuk_ref[hh], q_h) * (C_DH ** -0.5)).astype(BF16)
    m_ref[...] = jnp.full(m_ref.shape, NEG, F32)
    l_ref[...] = jnp.zeros_like(l_ref)
    acc_ref[...] = jnp.zeros_like(acc_ref)

    n_live = (j * QB + QB + KB - 1) // KB
    odd = (j % 2) == 1

    def body(kc, carry):
        ck = ckv_ref[kc]
        ckt = ckvt_ref[kc]
        sel = mask_ref[kc].astype(F32) > 0.5
        back = n_live - 1 - kc
        which = jnp.where(back == 0, jnp.where(odd, 0, 1), jnp.where(jnp.logical_and(back == 1, jnp.logical_not(odd)), 2, 3))
        for hh in range(C_HEADS):
            logit = jnp.where(sel, _dot(ck, qt_ref[hh]) + bias_ref[hh, which], NEG)
            m_old = m_ref[hh]
            m_new = jnp.maximum(m_old, jnp.max(logit, axis=0, keepdims=True))
            alpha = jnp.exp(m_old - m_new)
            p = jnp.where(sel, jnp.exp(logit - m_new), 0.0)
            l_ref[hh] = alpha * l_ref[hh] + jnp.sum(p, axis=0, keepdims=True)
            acc_ref[hh] = alpha * acc_ref[hh] + _dot(ckt, p.astype(BF16))
            m_ref[hh] = m_new
        return carry

    lax.fori_loop(0, n_live, body, 0)
    for hh in range(C_HEADS):
        o_lat = (acc_ref[hh] * (1.0 / l_ref[hh])).astype(BF16)
        ot_ref[C_DH * hh:C_DH * (hh + 1), :] = _dot(wuvt_ref[hh], o_lat)
    o_ref[...] = ot_ref[...].T


def _rel_bias_tiles(rel_table):
    s = np.arange(QB)[:, None]
    t = np.arange(QB)[None, :]
    diag, prev, far = np.maximum(t - s, 0), QB + t - s, np.full((QB, QB), 2 * QB)
    kinds = [(prev, diag), (diag, far), (far, prev), (far, far)]
    n = jnp.asarray(np.stack([np.concatenate(k, axis=0) for k in kinds]).astype(np.int32))
    max_exact = REL_BUCKETS // 2
    large = max_exact + (jnp.log(jnp.maximum(n, 1).astype(F32) / max_exact)
                         / math.log(REL_MAX_DIST / max_exact) * (REL_BUCKETS - max_exact)).astype(I32)
    bucket = jnp.where(n < max_exact, n, jnp.minimum(large, REL_BUCKETS - 1))
    return rel_table.astype(F32)[bucket].transpose(3, 0, 1, 2)


def _dsa_attention(h, mask, ckv, ckvt, w_uk, w_uv, rel_table, batch, seq):
    nq = seq // QB
    nk = seq // KB
    wuk = w_uk.transpose(1, 0, 2).astype(BF16)
    wuvt = w_uv.transpose(1, 2, 0).astype(BF16)
    bias = _rel_bias_tiles(rel_table)
    return pl.pallas_call(
        _dsa_attn_kernel,
        out_shape=jax.ShapeDtypeStruct((batch * seq, C_HEADS * C_DH), F32),
        grid=(batch, nq),
        in_specs=[pl.BlockSpec((QB, C_HEADS * C_DH), lambda b, j: (b * nq + j, CD_CQ // (C_HEADS * C_DH))),
                  pl.BlockSpec((None, nk, KB, QB), lambda b, j: (b, 0, 0, j)),
                  pl.BlockSpec((None, nk, KB, C_DLAT), lambda b, j: (b, 0, 0, 0)),
                  pl.BlockSpec((None, nk, C_DLAT, KB), lambda b, j: (b, 0, 0, 0)),
                  _const_spec(wuk), _const_spec(wuvt), _const_spec(bias)],
        out_specs=pl.BlockSpec((QB, C_HEADS * C_DH), lambda b, j: (b * nq + j, 0)),
        scratch_shapes=[pltpu.VMEM((C_HEADS, C_DLAT, QB), BF16), pltpu.VMEM((C_HEADS, 1, QB), F32),
                        pltpu.VMEM((C_HEADS, 1, QB), F32), pltpu.VMEM((C_HEADS, C_DLAT, QB), F32),
                        pltpu.VMEM((C_HEADS * C_DH, QB), F32)],
        compiler_params=_cparams("parallel", "parallel"),
        name="dsa_attention",
    )(h, mask, ckv.reshape(batch, nk, KB, C_DLAT), ckvt.reshape(batch, nk, C_DLAT, KB), wuk, wuvt, bias)


ROUTER_BLOCK = 512
MOE_TILE = 256


def _router_kernel(x_ref, wh_ref, wl_ref, b_ref, upper_ref, idx_ref, wgt_ref, rank_ref, cnt_ref):
    x = x_ref[...]
    xh = x.astype(BF16)
    xl = (x - xh.astype(F32)).astype(BF16)
    logit = _dot_nt(wh_ref[...], xh) + _dot_nt(wl_ref[...], xh) + _dot_nt(wh_ref[...], xl)
    aff = _sigmoid(logit)
    sel = aff + b_ref[...]
    s_rows = [sel[e:e + 1] for e in range(N_EXPERTS)]
    a_rows = [aff[e:e + 1] for e in range(N_EXPERTS)]
    n = EXPERTS_PER_GROUP

    g_best = jnp.zeros(s_rows[0].shape, I32)
    best = None
    for g in range(N_GROUPS):
        v = s_rows[g * n:(g + 1) * n]
        top2 = None
        for a in range(n):
            for b in range(a + 1, n):
                pair = v[a] + v[b]
                top2 = pair if top2 is None else jnp.maximum(top2, pair)
        if best is None:
            best = top2
        else:
            upd = top2 > best
            g_best = jnp.where(upd, g, g_best)
            best = jnp.where(upd, top2, best)

    sv, av = [], []
    for i in range(n):
        s_i, a_i = s_rows[i], a_rows[i]
        for g in range(1, N_GROUPS):
            pick = g_best == g
            s_i = jnp.where(pick, s_rows[g * n + i], s_i)
            a_i = jnp.where(pick, a_rows[g * n + i], a_i)
        sv.append(s_i)
        av.append(a_i)

    i1, s1, a1 = jnp.zeros_like(g_best), sv[0], av[0]
    for i in range(1, n):
        upd = sv[i] > s1
        i1 = jnp.where(upd, i, i1)
        s1 = jnp.where(upd, sv[i], s1)
        a1 = jnp.where(upd, av[i], a1)
    i2 = jnp.zeros_like(g_best)
    s2 = jnp.full(s1.shape, -jnp.inf, F32)
    a2 = jnp.zeros_like(a1)
    for i in range(n):
        cand = jnp.where(i1 == i, -jnp.inf, sv[i])
        upd = cand > s2
        i2 = jnp.where(upd, i, i2)
        s2 = jnp.where(upd, cand, s2)
        a2 = jnp.where(upd, av[i], a2)

    tot = a1 + a2
    e1 = g_best * n + i1
    e2 = g_best * n + i2
    idx_ref[0:1, :] = e1
    idx_ref[1:2, :] = e2
    wgt_ref[0:1, :] = a1 / tot
    wgt_ref[1:2, :] = a2 / tot

    @pl.when(pl.program_id(0) == 0)
    def _():
        cnt_ref[...] = jnp.zeros_like(cnt_ref)

    e_iota = lax.broadcasted_iota(I32, sel.shape, 0)
    oh1 = jnp.where(e_iota == e1, 1.0, 0.0)
    oh2 = jnp.where(e_iota == e2, 1.0, 0.0)
    both = oh1 + oh2
    before = cnt_ref[...] + _dot(both.astype(BF16), upper_ref[...])
    rank_ref[0:1, :] = jnp.sum(oh1 * before, axis=0, keepdims=True).astype(I32)
    rank_ref[1:2, :] = jnp.sum(oh2 * before, axis=0, keepdims=True).astype(I32)
    cnt_ref[...] = cnt_ref[...] + jnp.sum(both, axis=1, keepdims=True)


def _router(x2d, w_router, b_router):
    t, d = x2d.shape
    tb = min(ROUTER_BLOCK, t)
    wt = w_router.T.astype(F32)
    wh = wt.astype(BF16)
    wl = (wt - wh.astype(F32)).astype(BF16)
    r = np.arange(tb)
    upper = jnp.asarray(r[:, None] < r[None, :], BF16)
    pair_out = pl.BlockSpec((TOP_K, tb), lambda i: (0, i))
    fixed = lambda i: (0, 0)
    return pl.pallas_call(
        _router_kernel,
        out_shape=(jax.ShapeDtypeStruct((TOP_K, t), I32), jax.ShapeDtypeStruct((TOP_K, t), F32),
                   jax.ShapeDtypeStruct((TOP_K, t), I32), jax.ShapeDtypeStruct((N_EXPERTS, 1), F32)),
        grid=(t // tb,),
        in_specs=[pl.BlockSpec((tb, d), lambda i: (i, 0)),
                  pl.BlockSpec((N_EXPERTS, d), fixed), pl.BlockSpec((N_EXPERTS, d), fixed),
                  pl.BlockSpec((N_EXPERTS, 1), fixed), pl.BlockSpec((tb, tb), fixed)],
        out_specs=(pair_out, pair_out, pair_out, pl.BlockSpec((N_EXPERTS, 1), fixed)),
        compiler_params=_cparams("arbitrary"),
        name="moe_router",
    )(x2d, wh, wl, b_router.reshape(-1, 1).astype(F32), upper)


def _route_tables(idx, rank, counts, tm, n_items):
    cnt = counts.reshape(-1).astype(I32)
    start = jnp.cumsum(cnt) - cnt
    experts = jnp.arange(N_EXPERTS, dtype=I32)
    pos = rank + jnp.sum(jnp.where(idx[..., None] == experts, start, 0), axis=-1)
    first_tile = start // tm
    n_e = jnp.where(cnt > 0, (start + cnt - 1) // tm - first_tile + 1, 0)
    item_end = jnp.cumsum(n_e)
    item = jnp.arange(n_items, dtype=I32)
    used = item < item_end[-1]
    e_i = jnp.minimum(jnp.searchsorted(item_end, item, side="right"), N_EXPERTS - 1).astype(I32)
    e_last = jnp.max(jnp.where(cnt > 0, experts, 0))
    e_i = jnp.where(used, e_i, e_last)
    tile_i = jnp.where(used, first_tile[e_i] + item - (item_end - n_e)[e_i], (TOP_K * idx.shape[1]) // tm - 1)
    lo = jnp.where(used, jnp.maximum(start[e_i], tile_i * tm) - tile_i * tm, 0)
    hi = jnp.where(used, jnp.minimum(start[e_i] + cnt[e_i], (tile_i + 1) * tm) - tile_i * tm, 0)
    return pos.astype(I32), tile_i.astype(I32), e_i, lo.astype(I32), hi.astype(I32)


DMA_UNROLL = 8


def _dispatch_kernel(pos_ref, x_hbm, xs_hbm, sem, *, tb, n_tok):
    i = pl.program_id(0)

    def issue(grp, carry):
        t0 = i * tb + grp * DMA_UNROLL
        dst = [[pos_ref[s * n_tok + t0 + u] for s in range(TOP_K)] for u in range(DMA_UNROLL)]
        for u in range(DMA_UNROLL):
            for s in range(TOP_K):
                pltpu.make_async_copy(x_hbm.at[pl.ds(t0 + u, 1)], xs_hbm.at[pl.ds(dst[u][s], 1)], sem).start()
        return carry

    lax.fori_loop(0, tb // DMA_UNROLL, issue, 0)

    def wait_block():
        pltpu.make_async_copy(x_hbm.at[pl.ds(0, TOP_K * tb)], xs_hbm.at[pl.ds(0, TOP_K * tb)], sem).wait()

    @pl.when(i >= 1)
    def _():
        wait_block()

    @pl.when(i == pl.num_programs(0) - 1)
    def _():
        wait_block()


def _dispatch(x2d, pos_flat, tb):
    t, d = x2d.shape
    grid_spec = pltpu.PrefetchScalarGridSpec(
        num_scalar_prefetch=1, grid=(t // tb,),
        in_specs=[pl.BlockSpec(memory_space=pl.ANY)],
        out_specs=pl.BlockSpec(memory_space=pl.ANY),
        scratch_shapes=[pltpu.SemaphoreType.DMA(())])
    return pl.pallas_call(
        functools.partial(_dispatch_kernel, tb=tb, n_tok=t),
        out_shape=jax.ShapeDtypeStruct((TOP_K * t, d), F32),
        grid_spec=grid_spec,
        compiler_params=_cparams("arbitrary"),
        name="moe_dispatch",
    )(pos_flat, x2d)


def _experts_kernel(tile_ref, exp_ref, lo_ref, hi_ref, x_ref, w1_ref, w3_ref, w2_ref, o_ref, w1b, w3b, w2b):
    i = pl.program_id(0)
    prev = jnp.maximum(i - 1, 0)

    @pl.when(jnp.logical_or(i == 0, exp_ref[i] != exp_ref[prev]))
    def _():
        w1b[...] = w1_ref[...].astype(BF16)
        w3b[...] = w3_ref[...].astype(BF16)
        w2b[...] = w2_ref[...].astype(BF16)

    lo = lo_ref[i]
    hi = hi_ref[i]
    first = jnp.logical_or(i == 0, tile_ref[i] != tile_ref[prev])

    @pl.when(hi > lo)
    def _():
        x = x_ref[...].astype(BF16)
        row = lax.broadcasted_iota(I32, (x.shape[0], 1), 0)
        mine = jnp.where(jnp.logical_and(row >= lo, row < hi), 1.0, 0.0)
        hid = _silu(_dot(x, w1b[...])) * _dot(x, w3b[...]) * mine
        y = _dot(hid.astype(BF16), w2b[...])

        @pl.when(first)
        def _():
            o_ref[...] = y

        @pl.when(jnp.logical_not(first))
        def _():
            o_ref[...] += y


def _experts(xs, tile_i, exp_i, lo, hi, w1, w3, w2, tm):
    n, d = xs.shape
    f = w1.shape[-1]
    by_tile = lambda i, tile, exp, lo, hi: (tile[i], 0)
    by_exp = lambda i, tile, exp, lo, hi: (exp[i], 0, 0)
    grid_spec = pltpu.PrefetchScalarGridSpec(
        num_scalar_prefetch=4, grid=(tile_i.shape[0],),
        in_specs=[pl.BlockSpec((tm, d), by_tile),
                  pl.BlockSpec((None, d, f), by_exp), pl.BlockSpec((None, d, f), by_exp),
                  pl.BlockSpec((None, f, d), by_exp)],
        out_specs=pl.BlockSpec((tm, d), by_tile),
        scratch_shapes=[pltpu.VMEM((d, f), BF16), pltpu.VMEM((d, f), BF16), pltpu.VMEM((f, d), BF16)])
    return pl.pallas_call(
        _experts_kernel,
        out_shape=jax.ShapeDtypeStruct((n, d), F32),
        grid_spec=grid_spec,
        compiler_params=_cparams("arbitrary"),
        name="moe_experts",
    )(tile_i, exp_i, lo, hi, xs, w1, w3, w2)


def _combine_ln_kernel(pos_ref, x_ref, w_ref, g_ref, b_ref, ys_hbm, o_ref, gbuf, sem, *, tb, n_tok):
    i = pl.program_id(0)
    slot = i % 2

    def start(blk, s):
        def issue(grp, carry):
            r0 = grp * DMA_UNROLL
            src = [[pos_ref[k * n_tok + blk * tb + r0 + u] for k in range(TOP_K)] for u in range(DMA_UNROLL)]
            for u in range(DMA_UNROLL):
                for k in range(TOP_K):
                    pltpu.make_async_copy(ys_hbm.at[pl.ds(src[u][k], 1)], gbuf.at[s, k, pl.ds(r0 + u, 1)],
                                          sem.at[s]).start()
            return carry
        lax.fori_loop(0, tb // DMA_UNROLL, issue, 0)

    @pl.when(i == 0)
    def _():
        start(0, 0)

    @pl.when(i + 1 < pl.num_programs(0))
    def _():
        start(i + 1, 1 - slot)

    for k in range(TOP_K):
        pltpu.make_async_copy(ys_hbm.at[pl.ds(0, tb)], gbuf.at[slot, k], sem.at[slot]).wait()
    w = w_ref[...]
    y = gbuf[slot, 0] * w[:, 0:1] + gbuf[slot, 1] * w[:, 1:2]
    o_ref[...] = _layer_norm_rows(DN_ALPHA * x_ref[...] + y, g_ref[...], b_ref[...])


def _combine_ln(x2d, ys, pos_flat, wgt_cols, g, b, tb):
    t, d = x2d.shape
    row = lambda i, pos: (i, 0)
    fixed = lambda i, pos: (0, 0)
    grid_spec = pltpu.PrefetchScalarGridSpec(
        num_scalar_prefetch=1, grid=(t // tb,),
        in_specs=[pl.BlockSpec((tb, d), row), pl.BlockSpec((tb, TOP_K), row),
                  pl.BlockSpec((1, d), fixed), pl.BlockSpec((1, d), fixed),
                  pl.BlockSpec(memory_space=pl.ANY)],
        out_specs=pl.BlockSpec((tb, d), row),
        scratch_shapes=[pltpu.VMEM((2, TOP_K, tb, d), F32), pltpu.SemaphoreType.DMA((2,))])
    return pl.pallas_call(
        functools.partial(_combine_ln_kernel, tb=tb, n_tok=t),
        out_shape=jax.ShapeDtypeStruct((t, d), F32),
        grid_spec=grid_spec,
        compiler_params=_cparams("arbitrary"),
        name="moe_combine_ln",
    )(pos_flat, x2d, wgt_cols, g.reshape(1, d), b.reshape(1, d), ys)


def _moe_ln(x2d, w_router, b_router, w1, w3, w2, g, b):
    t = x2d.shape[0]
    tm = min(MOE_TILE, t)
    idx, wgt, rank, counts = _router(x2d, w_router, b_router)
    n_items = TOP_K * t // tm + N_EXPERTS - 1
    pos, tile_i, exp_i, lo, hi = _route_tables(idx, rank, counts, tm, n_items)
    pos_flat = pos.reshape(-1)
    xs = _dispatch(x2d, pos_flat, tm)
    ys = _experts(xs, tile_i, exp_i, lo, hi, w1, w3, w2, tm)
    return _combine_ln(x2d, ys, pos_flat, wgt.T, g, b, tm)


def _gate_rows(col, batch, seq):
    tb = min(TIME_BLOCK, seq)
    return col.reshape(batch, seq, D_HEADS).transpose(0, 2, 1).reshape(batch, D_HEADS, seq // tb, tb // CHUNK, CHUNK)


def _mix_cd(x2d, w_in, rel_table, ckv_g, w_uk, w_uv, conv_w, gate_b, d_norm_g, batch, seq):
    t = x2d.shape[0]
    h = _project(x2d, _pack_cd(w_in), min(1024, t), 512)
    ckv, ckvt, kdup = _dsa_prep(h, ckv_g, t)
    mask = _indexer(h, kdup, batch, seq)
    oc = _dsa_attention(h, mask, ckv, ckvt, w_uk, w_uv, rel_table, batch, seq)
    tail = h[:, CD_TAIL:CD_TAIL + LANES]
    ig_rows = _gate_rows(tail[:, TAIL_DI:TAIL_DI + D_HEADS], batch, seq)
    fg_rows = _gate_rows(tail[:, TAIL_DF:TAIL_DF + D_HEADS], batch, seq)
    od = _mlstm(h, conv_w, ig_rows, fg_rows, gate_b, d_norm_g, batch, seq,
                q_off=CD_DQ // (D_HEADS * D_DK), k_off=CD_DK // (D_HEADS * D_DK),
                v_off=CD_DV // (D_HEADS * D_DV), g_off=CD_DOG // (D_HEADS * D_DV))
    return oc, od


def kernel(x, w_in_ab, w_out_ab, hgrn_lb_logits, a_norm_g, gla_wa2, gla_ba2, b_norm_g, w_in_cd, w_out_cd,
           ckv_norm_g, w_uk, w_uv, mlstm_conv_w, mlstm_gate_b, d_norm_g, rel_table, w_router, b_router,
           moe_w1, moe_w3, moe_w2, ln_g, ln_b):
    batch, seq, d = x.shape
    x2d = x.reshape(batch * seq, d)
    for layer in range(DEPTH):
        li = layer // 2
        if layer % 2 == 0:
            mix_a, mix_b = _mix_ab(x2d, w_in_ab[li], hgrn_lb_logits, li, a_norm_g[li], gla_wa2[li], gla_ba2[li],
                                   b_norm_g[li], batch, seq)
            w_out = w_out_ab[li]
        else:
            mix_a, mix_b = _mix_cd(x2d, w_in_cd[li], rel_table, ckv_norm_g[li], w_uk[li], w_uv[li],
                                   mlstm_conv_w[li], mlstm_gate_b[li], d_norm_g[li], batch, seq)
            w_out = w_out_cd[li]
        ka = mix_a.shape[1]
        x2d = _outproj_ln(x2d, mix_a, mix_b, w_out[:ka].astype(BF16), w_out[ka:].astype(BF16),
                          ln_g[layer, 0], ln_b[layer, 0])
        x2d = _moe_ln(x2d, w_router, b_router, moe_w1[layer], moe_w3[layer], moe_w2[layer],
                      ln_g[layer, 1], ln_b[layer, 1])
    return x2d.reshape(batch, seq, d)
```

```python
import functools
import math

import numpy as np
import jax
import jax.numpy as jnp
from jax import lax
from jax.experimental import pallas as pl
from jax.experimental.pallas import tpu as pltpu

F32 = jnp.float32
BF16 = jnp.bfloat16
I32 = jnp.int32

D_MODEL = 2048
DEPTH = 2
A_HEADS, A_DK, A_DV = 8, 128, 128
B_HEADS, B_DK, B_DV = 4, 128, 256
B_GATE_RANK, B_GATE_TAU = 16, 16.0
C_HEADS, C_DH, C_DLAT = 8, 128, 256
IDX_HEADS, IDX_DIM = 16, 64
TOPK_MAX = 256
D_HEADS, D_DK, D_DV = 4, 128, 256
CONV_K = 4
REL_BUCKETS, REL_MAX_DIST = 32, 128
N_EXPERTS, N_GROUPS, TOP_K, D_EXPERT = 16, 4, 2, 512
EXPERTS_PER_GROUP = N_EXPERTS // N_GROUPS
DN_ALPHA = (2 * DEPTH) ** 0.25
EPS = 1e-5

LANES = 128
SUBLANES = 8
VMEM_LIMIT = 56 * 1024 * 1024

CHUNK = 64
N_LEVELS = 6
TIME_BLOCK = 256
QB = 128
NEG = -1e30
INT_MIN = -2 ** 31


def _cparams(*sem):
    return pltpu.CompilerParams(dimension_semantics=sem, vmem_limit_bytes=VMEM_LIMIT)


def _dot(a, b):
    return jnp.dot(a, b, preferred_element_type=F32)


def _dot_nt(a, b):
    return lax.dot_general(a, b, (((1,), (1,)), ((), ())), preferred_element_type=F32)


def _dot_tn(a, b):
    return lax.dot_general(a, b, (((0,), (0,)), ((), ())), preferred_element_type=F32)


def _split3(a):
    hi = a.astype(BF16)
    r1 = a - hi.astype(F32)
    mid = r1.astype(BF16)
    lo = (r1 - mid.astype(F32)).astype(BF16)
    return hi, mid, lo


def _dot01(m01, a):
    hi, mid, lo = _split3(a)
    return _dot(m01, hi) + _dot(m01, mid) + _dot(m01, lo)


def _sigmoid(x):
    return 1.0 / (1.0 + jnp.exp(-x))


def _silu(x):
    return x * _sigmoid(x)


def _log_sigmoid(x):
    return jnp.minimum(x, 0.0) - jnp.log(1.0 + jnp.exp(-jnp.abs(x)))


def _proj_kernel(x_ref, w_ref, wt_ref, o_ref, xb_ref):
    j = pl.program_id(1)

    @pl.when(j == 0)
    def _():
        xb_ref[...] = x_ref[...].astype(BF16)

    @pl.when(j < pl.num_programs(1) - 1)
    def _():
        o_ref[...] = _dot(xb_ref[...], w_ref[...])

    @pl.when(j == pl.num_programs(1) - 1)
    def _():
        o_ref[...] = _dot(xb_ref[...], wt_ref[...])


def _project(x, w, w_tail, tm, tn):
    m, k = x.shape
    n_main = w.shape[1] // tn
    return pl.pallas_call(
        _proj_kernel,
        out_shape=jax.ShapeDtypeStruct((m, (n_main + 1) * tn), F32),
        grid=(m // tm, n_main + 1),
        in_specs=[pl.BlockSpec((tm, k), lambda i, j: (i, 0)),
                  pl.BlockSpec((k, tn), lambda i, j: (0, jnp.minimum(j, n_main - 1))),
                  pl.BlockSpec((k, tn), lambda i, j: (0, 0))],
        out_specs=pl.BlockSpec((tm, tn), lambda i, j: (i, j)),
        scratch_shapes=[pltpu.VMEM((tm, k), BF16)],
        compiler_params=_cparams("parallel", "arbitrary"),
        name="in_proj",
    )(x, w, w_tail)


def _layer_norm_rows(z, g, b):
    mu = jnp.mean(z, axis=-1, keepdims=True)
    zc = z - mu
    var = jnp.mean(zc * zc, axis=-1, keepdims=True)
    return zc * lax.rsqrt(var + EPS) * g + b


def _outproj_ln_kernel(x_ref, ma_ref, mb_ref, wa_ref, wb_ref, g_ref, b_ref, o_ref):
    mixed = _dot(ma_ref[...].astype(BF16), wa_ref[...]) + _dot(mb_ref[...].astype(BF16), wb_ref[...])
    o_ref[...] = _layer_norm_rows(DN_ALPHA * x_ref[...] + mixed, g_ref[...], b_ref[...])


def _outproj_ln(x, mix_a, mix_b, w_a, w_b, g, b, tm=256):
    m, d = x.shape
    ka, kb = mix_a.shape[1], mix_b.shape[1]
    row = lambda i: (i, 0)
    fixed = lambda i: (0, 0)
    return pl.pallas_call(
        _outproj_ln_kernel,
        out_shape=jax.ShapeDtypeStruct((m, d), F32),
        grid=(m // tm,),
        in_specs=[pl.BlockSpec((tm, d), row), pl.BlockSpec((tm, ka), row), pl.BlockSpec((tm, kb), row),
                  pl.BlockSpec((ka, d), fixed), pl.BlockSpec((kb, d), fixed),
                  pl.BlockSpec((1, d), fixed), pl.BlockSpec((1, d), fixed)],
        out_specs=pl.BlockSpec((tm, d), row),
        compiler_params=_cparams("parallel"),
        name="out_proj_ln",
    )(x, mix_a, mix_b, w_a, w_b, g.reshape(1, d), b.reshape(1, d))


def _chunk_constants():
    t = np.arange(CHUNK)
    tri = (t[:, None] >= t[None, :]).astype(np.float32)
    cum, pair, odd = [tri], [], []
    for lev in range(1, N_LEVELS + 1):
        c = CHUNK >> lev
        mid = (t // (2 * c)) * 2 * c + c - 1
        cum.append(tri - tri[mid])
        pair.append((t[:, None] // (2 * c) == t[None, :] // (2 * c)).astype(np.float32))
        odd.append(np.broadcast_to((((t // c) & 1) == 1).astype(np.float32)[:, None], (CHUNK, LANES)))
    pair.append(np.eye(CHUNK, dtype=np.float32))
    return (jnp.asarray(np.concatenate(cum, 0), BF16), jnp.asarray(np.stack(pair), F32),
            jnp.asarray(np.stack(odd), F32))


def _glr_chunk(q, k, v, la, st_ref, cum_ref, pair_ref, odd_ref):
    d = _dot01(cum_ref[...], la)
    bcum = d[0:CHUNK]
    attn = pair_ref[N_LEVELS] * _dot_nt(q.astype(BF16), k.astype(BF16))
    for lev in range(1, N_LEVELS + 1):
        e = jnp.exp(-jnp.abs(d[lev * CHUNK:(lev + 1) * CHUNK]))
        eq = e * odd_ref[lev - 1]
        ql = (q * eq).astype(BF16)
        kl = (k * (e - eq)).astype(BF16)
        attn = attn + pair_ref[lev - 1] * _dot_nt(ql, kl)
    st = st_ref[...]
    o = _dot_nt((q * jnp.exp(bcum)).astype(BF16), st.astype(BF16)) + _dot(attn.astype(BF16), v.astype(BF16))
    b_last = bcum[CHUNK - 1:CHUNK]
    kdec = (k * jnp.exp(b_last - bcum)).astype(BF16)
    st_ref[...] = st * jnp.exp(b_last) + _dot_tn(v.astype(BF16), kdec)
    return o


def _rms_gate(o, g, gate):
    ms = jnp.mean(o * o, axis=-1, keepdims=True)
    return o * lax.rsqrt(ms + EPS) * g * gate


HEAD_GROUP = 4


def _hgrn2_kernel(q_ref, f_ref, i_ref, g_ref, lb_ref, ng_ref, cum_ref, pair_ref, odd_ref, o_ref, st_ref):
    @pl.when(pl.program_id(2) == 0)
    def _():
        st_ref[...] = jnp.zeros_like(st_ref)

    def body(c, carry):
        rows = pl.ds(pl.multiple_of(c * CHUNK, CHUNK), CHUNK)
        for hh in range(HEAD_GROUP):
            ck = slice(A_DK * hh, A_DK * (hh + 1))
            cv = slice(A_DV * hh, A_DV * (hh + 1))
            lb = lb_ref[:, ck]
            f = lb + (1.0 - lb) * _sigmoid(f_ref[rows, ck])
            o = _glr_chunk(_silu(q_ref[rows, ck]), 1.0 - f, i_ref[rows, cv], jnp.log(f),
                           st_ref.at[hh], cum_ref, pair_ref, odd_ref)
            o_ref[rows, cv] = _rms_gate(o, ng_ref[...], _silu(g_ref[rows, cv]))
        return carry

    lax.fori_loop(0, q_ref.shape[0] // CHUNK, body, 0)


def _gla_kernel(q_ref, k_ref, v_ref, g_ref, r_ref, wa_ref, ba_ref, ng_ref, cum_ref, pair_ref, odd_ref,
                o_ref, st_ref):
    @pl.when(pl.program_id(2) == 0)
    def _():
        st_ref[...] = jnp.zeros_like(st_ref)

    def body(c, carry):
        rows = pl.ds(pl.multiple_of(c * CHUNK, CHUNK), CHUNK)
        pre = _dot(r_ref[rows, :].astype(BF16), wa_ref[...]) + ba_ref[...]
        la = _log_sigmoid(pre) * (1.0 / B_GATE_TAU)
        for hh in range(HEAD_GROUP):
            ck = slice(B_DK * hh, B_DK * (hh + 1))
            cv = slice(B_DV * hh, B_DV * (hh + 1))
            o = _glr_chunk(q_ref[rows, ck] * (B_DK ** -0.5), k_ref[rows, ck], v_ref[rows, cv], la[:, ck],
                           st_ref.at[hh], cum_ref, pair_ref, odd_ref)
            o_ref[rows, cv] = _rms_gate(o, ng_ref[...], _silu(g_ref[rows, cv]))
        return carry

    lax.fori_loop(0, q_ref.shape[0] // CHUNK, body, 0)


def _const_spec(arr):
    nd = arr.ndim
    return pl.BlockSpec(arr.shape, lambda *_: (0,) * nd)


def _hgrn2(h, lb, norm_g, batch, seq):
    tb = min(TIME_BLOCK, seq)
    nt = seq // tb
    ng = A_HEADS // HEAD_GROUP
    wk, wv = HEAD_GROUP * A_DK, HEAD_GROUP * A_DV
    consts = _chunk_constants()
    col = lambda seg, w: pl.BlockSpec((tb, w), lambda b, g, t, seg=seg: (b * nt + t, seg * ng + g))
    return pl.pallas_call(
        _hgrn2_kernel,
        out_shape=jax.ShapeDtypeStruct((batch * seq, A_HEADS * A_DV), F32),
        grid=(batch, ng, nt),
        in_specs=[col(0, wk), col(1, wk), col(2, wv), col(3, wv),
                  pl.BlockSpec((1, wk), lambda b, g, t: (0, g)),
                  pl.BlockSpec((1, A_DV), lambda b, g, t: (0, 0))] + [_const_spec(c) for c in consts],
        out_specs=pl.BlockSpec((tb, wv), lambda b, g, t: (b * nt + t, g)),
        scratch_shapes=[pltpu.VMEM((HEAD_GROUP, A_DV, A_DK), F32)],
        compiler_params=_cparams("parallel", "parallel", "arbitrary"),
        name="hgrn2",
    )(h, h, h, h, lb.reshape(1, -1), norm_g.reshape(1, -1), *consts)


def _gla(h, wa2p, ba2, norm_g, batch, seq, q_off, k_off, v_off, g_off, r_off):
    tb = min(TIME_BLOCK, seq)
    nt = seq // tb
    wk, wv = B_HEADS * B_DK, B_HEADS * B_DV
    consts = _chunk_constants()
    col = lambda off, w: pl.BlockSpec((tb, w), lambda b, g, t, off=off: (b * nt + t, off))
    fixed = lambda b, g, t: (0, 0)
    return pl.pallas_call(
        _gla_kernel,
        out_shape=jax.ShapeDtypeStruct((batch * seq, wv), F32),
        grid=(batch, 1, nt),
        in_specs=[col(q_off, wk), col(k_off, wk), col(v_off, wv), col(g_off, wv), col(r_off, LANES),
                  pl.BlockSpec((LANES, wk), fixed), pl.BlockSpec((1, wk), fixed),
                  pl.BlockSpec((1, B_DV), fixed)] + [_const_spec(c) for c in consts],
        out_specs=pl.BlockSpec((tb, wv), lambda b, g, t: (b * nt + t, 0)),
        scratch_shapes=[pltpu.VMEM((B_HEADS, B_DV, B_DK), F32)],
        compiler_params=_cparams("parallel", "parallel", "arbitrary"),
        name="gla",
    )(h, h, h, h, h, wa2p, ba2.reshape(1, -1), norm_g.reshape(1, -1), *consts)


PROJ_TILE = 512
AB_MAIN = 4 * A_HEADS * A_DK + 2 * B_HEADS * B_DK + 2 * B_HEADS * B_DV
AB_PAD = AB_MAIN + PROJ_TILE


def _mix_ab(x2d, w_in, lb_logits, li, a_norm_g, wa2, ba2, b_norm_g, batch, seq):
    d = x2d.shape[1]
    w_tail = jnp.pad(w_in[:, AB_MAIN:], ((0, 0), (0, AB_PAD - w_in.shape[1]))).astype(BF16)
    h = _project(x2d, w_in[:, :AB_MAIN].astype(BF16), w_tail, min(1024, x2d.shape[0]), PROJ_TILE)
    lb = jnp.cumsum(jax.nn.softmax(lb_logits.astype(F32), axis=0), axis=0)[li]
    oa = _hgrn2(h, lb, a_norm_g, batch, seq)
    wa2p = jnp.concatenate([wa2, jnp.zeros((LANES - B_GATE_RANK, wa2.shape[1]), F32)], axis=0).astype(BF16)
    a_cols = 4 * A_HEADS * A_DK
    wk, wv = B_HEADS * B_DK, B_HEADS * B_DV
    ob = _gla(h, wa2p, ba2, b_norm_g, batch, seq, q_off=a_cols // wk, k_off=a_cols // wk + 1,
              v_off=(a_cols + 2 * wk) // wv, g_off=(a_cols + 2 * wk) // wv + 1, r_off=AB_MAIN // LANES)
    return oa, ob


CONV_HALO = 8


def _causal_conv(x_ref, w_ref, buf_ref, tail_ref):
    tb = x_ref.shape[0]
    x = x_ref[...]
    buf_ref[0:CONV_HALO, :] = tail_ref[...]
    buf_ref[CONV_HALO:CONV_HALO + tb, :] = x
    tail_ref[...] = x[tb - CONV_HALO:tb]
    y = w_ref[CONV_K - 1:CONV_K, :] * x
    for j in range(CONV_K - 1):
        y = y + w_ref[j:j + 1, :] * buf_ref[pl.ds(CONV_HALO - (CONV_K - 1) + j, tb), :]
    return y


def _row_to_col(row, eye):
    return jnp.sum(jnp.where(eye, row, 0.0), axis=1, keepdims=True)


def _mlstm_kernel(q_ref, k_ref, v_ref, og_ref, wq_ref, wk_ref, ig_ref, fg_ref, gb_ref, ng_ref, tri_ref,
                  o_ref, ct_ref, n_ref, m_ref, qt_ref, kt_ref, qs_ref, ks_ref, buf_ref):
    @pl.when(pl.program_id(2) == 0)
    def _():
        ct_ref[...] = jnp.zeros_like(ct_ref)
        n_ref[...] = jnp.zeros_like(n_ref)
        m_ref[...] = jnp.zeros_like(m_ref)
        qt_ref[...] = jnp.zeros_like(qt_ref)
        kt_ref[...] = jnp.zeros_like(kt_ref)

    qs_ref[...] = _silu(_causal_conv(q_ref, wq_ref, buf_ref, qt_ref))
    ks_ref[...] = _silu(_causal_conv(k_ref, wk_ref, buf_ref, kt_ref)) * (D_DK ** -0.5)

    r_i = lax.broadcasted_iota(I32, (CHUNK, CHUNK), 0)
    c_i = lax.broadcasted_iota(I32, (CHUNK, CHUNK), 1)
    eye = r_i == c_i
    causal = r_i >= c_i

    def body(c, carry):
        rows = pl.ds(pl.multiple_of(c * CHUNK, CHUNK), CHUNK)
        tri = tri_ref[...]
        for hh in range(D_HEADS):
            ck = slice(D_DK * hh, D_DK * (hh + 1))
            cv = slice(D_DV * hh, D_DV * (hh + 1))
            q = qs_ref[rows, ck]
            k = ks_ref[rows, ck]
            v = v_ref[rows, cv].astype(BF16)
            qb = q.astype(BF16)
            ig_row = ig_ref[hh, pl.ds(c, 1), :] + gb_ref[0, hh]
            lf_row = _log_sigmoid(fg_ref[hh, pl.ds(c, 1), :] + gb_ref[1, hh])
            hi, mid, lo = _split3(lf_row)
            bcum_row = _dot(hi, tri) + _dot(mid, tri) + _dot(lo, tri)
            bcum_col = _row_to_col(bcum_row, eye)
            ig_col = _row_to_col(ig_row, eye)
            m_prev = m_ref[hh, :, 0:1]
            log_w = jnp.where(causal, bcum_col - bcum_row + ig_row, NEG)
            log_inter = bcum_col + m_prev
            m_t = jnp.maximum(jnp.max(log_w, axis=1, keepdims=True), log_inter)
            s = _dot_nt(qb, k.astype(BF16)) * jnp.exp(log_w - m_t)
            w_inter = jnp.exp(log_inter - m_t)
            num = _dot(s.astype(BF16), v) + w_inter * _dot_nt(qb, ct_ref[hh].astype(BF16))
            qn = jnp.sum(s, axis=1, keepdims=True) + w_inter * jnp.sum(q * n_ref[hh], axis=1, keepdims=True)
            h = num / jnp.maximum(jnp.abs(qn), jnp.exp(-m_t))
            o_ref[rows, cv] = _rms_gate(h, ng_ref[...], _sigmoid(og_ref[rows, cv]))
            b_last = bcum_row[:, CHUNK - 1:CHUNK]
            log_u = b_last - bcum_col + ig_col
            m_new = jnp.maximum(b_last + m_prev, jnp.max(log_u, axis=0, keepdims=True))
            decay = jnp.exp(b_last + m_prev - m_new)
            ku = k * jnp.exp(log_u - m_new)
            ct_ref[hh] = decay * ct_ref[hh] + _dot_tn(v, ku.astype(BF16))
            n_ref[hh] = decay * n_ref[hh] + jnp.sum(ku, axis=0, keepdims=True)
            m_ref[hh] = jnp.broadcast_to(m_new, (1, LANES))
        return carry

    lax.fori_loop(0, q_ref.shape[0] // CHUNK, body, 0)


def _mlstm(h, conv_w, ig_rows, fg_rows, gate_b, norm_g, batch, seq, q_off, k_off, v_off, g_off):
    tb = min(TIME_BLOCK, seq)
    nt = seq // tb
    nc = tb // CHUNK
    wk, wv = D_HEADS * D_DK, D_HEADS * D_DV
    t = np.arange(CHUNK)
    tri = jnp.asarray(t[:, None] <= t[None, :], BF16)
    gb = jnp.broadcast_to(gate_b.reshape(2, D_HEADS, 1, 1), (2, D_HEADS, 1, CHUNK)).astype(F32)
    col = lambda off, w: pl.BlockSpec((tb, w), lambda b, g, t, off=off: (b * nt + t, off))
    gate = pl.BlockSpec((None, D_HEADS, None, nc, CHUNK), lambda b, g, t: (b, 0, t, 0, 0))
    fixed = lambda b, g, t: (0, 0)
    return pl.pallas_call(
        _mlstm_kernel,
        out_shape=jax.ShapeDtypeStruct((batch * seq, wv), F32),
        grid=(batch, 1, nt),
        in_specs=[col(q_off, wk), col(k_off, wk), col(v_off, wv), col(g_off, wv),
                  pl.BlockSpec((CONV_K, wk), lambda b, g, t: (0, 0)),
                  pl.BlockSpec((CONV_K, wk), lambda b, g, t: (0, 1)),
                  gate, gate,
                  pl.BlockSpec((2, D_HEADS, 1, CHUNK), lambda b, g, t: (0, 0, 0, 0)),
                  pl.BlockSpec((1, D_DV), fixed), pl.BlockSpec((CHUNK, CHUNK), fixed)],
        out_specs=pl.BlockSpec((tb, wv), lambda b, g, t: (b * nt + t, 0)),
        scratch_shapes=[pltpu.VMEM((D_HEADS, D_DV, D_DK), F32), pltpu.VMEM((D_HEADS, 1, D_DK), F32),
                        pltpu.VMEM((D_HEADS, 1, LANES), F32),
                        pltpu.VMEM((CONV_HALO, wk), F32), pltpu.VMEM((CONV_HALO, wk), F32),
                        pltpu.VMEM((tb, wk), F32), pltpu.VMEM((tb, wk), F32),
                        pltpu.VMEM((tb + CONV_HALO, wk), F32)],
        compiler_params=_cparams("parallel", "parallel", "arbitrary"),
        name="mlstm",
    )(h, h, h, h, conv_w, conv_w, ig_rows, fg_rows, gb, norm_g.reshape(1, -1), tri)


CD_CQ, CD_IQ, CD_DQ, CD_DK, CD_DV, CD_DOG, CD_CKV, CD_TAIL = 0, 1024, 2048, 2560, 3072, 4096, 5120, 5376
CD_PAD = 5632
TAIL_IK, TAIL_IW, TAIL_DI, TAIL_DF = 0, 64, 80, 84


def _pack_cd(w_in):
    cq, ckv, iq, ik, iw, dq, dk, dv, di, df, dog = jnp.split(
        w_in, [int(i) for i in np.cumsum(
            (C_HEADS * C_DH, C_DLAT, IDX_HEADS * IDX_DIM, IDX_DIM, IDX_HEADS, D_HEADS * D_DK, D_HEADS * D_DK,
             D_HEADS * D_DV, D_HEADS, D_HEADS))], axis=1)
    used = CD_TAIL + IDX_DIM + IDX_HEADS + 2 * D_HEADS
    pad = jnp.zeros((w_in.shape[0], CD_PAD - used), F32)
    main = jnp.concatenate([cq, iq, dq, dk, dv, dog], axis=1).astype(BF16)
    tail = jnp.concatenate([ckv, ik, iw, di, df, pad], axis=1).astype(BF16)
    return main, tail


KB = 2 * QB


def _dsa_prep_kernel(ckv_ref, tail_ref, g_ref, ckv_o, ckvt_o, kdup_o):
    c = ckv_ref[...]
    cn = c * lax.rsqrt(jnp.mean(c * c, axis=-1, keepdims=True) + EPS) * g_ref[...]
    ckv_o[...] = cn.astype(BF16)
    ckvt_o[...] = cn.T.astype(BF16)
    tail = tail_ref[...]
    lane = lax.broadcasted_iota(I32, tail.shape, 1)
    kdup_o[...] = jnp.where(lane < IDX_DIM, tail, pltpu.roll(tail, IDX_DIM, axis=1)).astype(BF16)


def _dsa_prep(h, ckv_g, n_rows):
    nb = n_rows // KB
    return pl.pallas_call(
        _dsa_prep_kernel,
        out_shape=(jax.ShapeDtypeStruct((nb, KB, C_DLAT), BF16), jax.ShapeDtypeStruct((nb, C_DLAT, KB), BF16),
                   jax.ShapeDtypeStruct((nb, KB, LANES), BF16)),
        grid=(nb,),
        in_specs=[pl.BlockSpec((KB, C_DLAT), lambda i: (i, CD_CKV // C_DLAT)),
                  pl.BlockSpec((KB, LANES), lambda i: (i, CD_TAIL // LANES)),
                  pl.BlockSpec((1, C_DLAT), lambda i: (0, 0))],
        out_specs=(pl.BlockSpec((None, KB, C_DLAT), lambda i: (i, 0, 0)),
                   pl.BlockSpec((None, C_DLAT, KB), lambda i: (i, 0, 0)),
                   pl.BlockSpec((None, KB, LANES), lambda i: (i, 0, 0))),
        compiler_params=_cparams("parallel"),
        name="dsa_prep",
    )(h, h, ckv_g.reshape(1, -1))


def _sortable_key(x):
    b = lax.bitcast_convert_type(x, I32)
    key = b ^ ((b >> 31) & 0x7FFFFFFF)
    return jnp.where(key == -1, 0, key)


BISECT_STEPS = 4


def _indexer_kernel(iq_ref, tail_ref, kdup_ref, tri_ref, mask_ref, key_ref, wst_ref, *, k_sel):
    j = pl.program_id(1)
    nk = key_ref.shape[0]
    n_live = (j * QB + QB + KB - 1) // KB
    w_t = tail_ref[...].T
    lane = lax.broadcasted_iota(I32, (QB, LANES), 1)
    for p in range(IDX_HEADS // 2):
        pair = iq_ref[:, LANES * p:LANES * (p + 1)]
        wst_ref[p, 0:QB, :] = jnp.where(lane < IDX_DIM, pair, 0.0).astype(BF16)
        wst_ref[p, QB:2 * QB, :] = jnp.where(lane >= IDX_DIM, pair, 0.0).astype(BF16)

    s_loc = lax.broadcasted_iota(I32, (KB, QB), 0)
    t_abs = j * QB + lax.broadcasted_iota(I32, (1, QB), 1)

    def score_chunk(kc, carry):
        kd = kdup_ref[kc]
        acc = jnp.zeros((KB, QB), F32)
        for p in range(IDX_HEADS // 2):
            r0 = TAIL_IW + 2 * p
            both = _dot_nt(kd, wst_ref[p])
            acc = acc + jnp.maximum(both[:, 0:QB], 0.0) * w_t[r0:r0 + 1, :]
            acc = acc + jnp.maximum(both[:, QB:2 * QB], 0.0) * w_t[r0 + 1:r0 + 2, :]
        key_ref[kc] = jnp.where(kc * KB + s_loc > t_abs, INT_MIN, _sortable_key(acc))
        return carry

    lax.fori_loop(0, n_live, score_chunk, 0)
    k_row = jnp.minimum(k_sel, j * QB + lax.broadcasted_iota(I32, (SUBLANES, QB), 1) + 1)

    def count(pred):
        def add(kc, acc):
            hit = jnp.where(pred(key_ref[kc].reshape(KB // SUBLANES, SUBLANES, QB)), 1, 0)
            return acc + jnp.sum(hit, axis=0)
        acc = lax.fori_loop(0, n_live, add, jnp.zeros((SUBLANES, QB), I32))
        for shift in (4, 2, 1):
            acc = acc + pltpu.roll(acc, shift, axis=0)
        return acc

    def unfinished(state):
        lo, hi = state
        return jnp.max(jnp.where(lo < hi, 1, 0)) > 0

    def halve(state):
        lo, hi = state
        mid = (lo >> 1) + (hi >> 1) + (((lo & 1) + (hi & 1) + 1) >> 1)
        cnt = count(lambda k: k >= mid)
        enough = cnt >= k_row
        lo_n = jnp.where(enough, mid, lo)
        hi_n = jnp.where(cnt == k_row, mid, jnp.where(enough, hi, mid - 1))
        return lo_n, hi_n

    def halve_steps(state):
        for _ in range(BISECT_STEPS):
            state = halve(state)
        return state

    lo0 = jnp.full((SUBLANES, QB), INT_MIN + 1, I32)
    hi0 = jnp.full((SUBLANES, QB), 2 ** 31 - 1, I32)
    tau8, _ = lax.while_loop(unfinished, halve_steps, (lo0, hi0))
    n_ge = count(lambda k: k >= tau8)
    has_tie = jnp.max(jnp.where(n_ge != k_row, 1, 0)) > 0
    tau = tau8[0:1, :]

    @pl.when(jnp.logical_not(has_tie))
    def _():
        def put(kc, carry):
            mask_ref[kc] = jnp.where(key_ref[kc] >= tau, 1.0, 0.0).astype(BF16)
            return carry
        lax.fori_loop(0, n_live, put, 0)

    @pl.when(has_tie)
    def _():
        need = (k_row - count(lambda k: k > tau8)).astype(F32)[0:1, :]

        def put(kc, seen):
            k = key_ref[kc]
            eq = jnp.where(k == tau, 1.0, 0.0)
            before = _dot(tri_ref[...], eq.astype(BF16)) + seen
            take = jnp.where(k > tau, 1.0, jnp.where(before < need, eq, 0.0))
            mask_ref[kc] = take.astype(BF16)
            return seen + jnp.sum(eq, axis=0, keepdims=True)
        lax.fori_loop(0, n_live, put, jnp.zeros((1, QB), F32))

    def clear(kc, carry):
        mask_ref[kc] = jnp.zeros((KB, QB), BF16)
        return carry
    lax.fori_loop(n_live, nk, clear, 0)


def _indexer(h, kdup, batch, seq):
    nq = seq // QB
    nk = seq // KB
    k_sel = min(TOPK_MAX, seq // 4)
    r = np.arange(KB)
    tri = jnp.asarray(r[None, :] < r[:, None], BF16)
    return pl.pallas_call(
        functools.partial(_indexer_kernel, k_sel=k_sel),
        out_shape=jax.ShapeDtypeStruct((batch, nk, KB, seq), BF16),
        grid=(batch, nq),
        in_specs=[pl.BlockSpec((QB, IDX_HEADS * IDX_DIM), lambda b, j: (b * nq + j, CD_IQ // (IDX_HEADS * IDX_DIM))),
                  pl.BlockSpec((QB, LANES), lambda b, j: (b * nq + j, CD_TAIL // LANES)),
                  pl.BlockSpec((None, nk, KB, LANES), lambda b, j: (b, 0, 0, 0)),
                  pl.BlockSpec((KB, KB), lambda b, j: (0, 0))],
        out_specs=pl.BlockSpec((None, nk, KB, QB), lambda b, j: (b, 0, 0, j)),
        scratch_shapes=[pltpu.VMEM((nk, KB, QB), I32), pltpu.VMEM((IDX_HEADS // 2, 2 * QB, LANES), BF16)],
        compiler_params=_cparams("parallel", "parallel"),
        name="dsa_indexer",
    )(h, h, kdup.reshape(batch, nk, KB, LANES), tri)


def _dsa_attn_kernel(cq_ref, mask_ref, ckv_ref, ckvt_ref, wuk_ref, wuvt_ref, bias_ref, o_ref,
                     qt_ref, m_ref, l_ref, acc_ref, ot_ref):
    j = pl.program_id(1)
    for hh in range(C_HEADS):
        q_h = cq_ref[:, C_DH * hh:C_DH * (hh + 1)].astype(BF16)
        qt_ref[hh] = (_dot_nt(wuk_ref[hh], q_h) * (C_DH ** -0.5)).astype(BF16)
    m_ref[...] = jnp.full(m_ref.shape, NEG, F32)
    l_ref[...] = jnp.zeros_like(l_ref)
    acc_ref[...] = jnp.zeros_like(acc_ref)

    n_live = (j * QB + QB + KB - 1) // KB
    odd = (j % 2) == 1

    def body(kc, carry):
        ck = ckv_ref[kc]
        ckt = ckvt_ref[kc]
        sel = mask_ref[kc].astype(F32) > 0.5
        back = n_live - 1 - kc
        which = jnp.where(back == 0, jnp.where(odd, 0, 1), jnp.where(jnp.logical_and(back == 1, jnp.logical_not(odd)), 2, 3))
        for hh in range(C_HEADS):
            logit = jnp.where(sel, _dot(ck, qt_ref[hh]) + bias_ref[hh, which], NEG)
            m_old = m_ref[hh]
            m_new = jnp.maximum(m_old, jnp.max(logit, axis=0, keepdims=True))
            alpha = jnp.exp(m_old - m_new)
            p = jnp.where(sel, jnp.exp(logit - m_new), 0.0)
            l_ref[hh] = alpha * l_ref[hh] + jnp.sum(p, axis=0, keepdims=True)
            acc_ref[hh] = alpha * acc_ref[hh] + _dot(ckt, p.astype(BF16))
            m_ref[hh] = m_new
        return carry

    lax.fori_loop(0, n_live, body, 0)
    for hh in range(C_HEADS):
        o_lat = (acc_ref[hh] * (1.0 / l_ref[hh])).astype(BF16)
        ot_ref[C_DH * hh:C_DH * (hh + 1), :] = _dot(wuvt_ref[hh], o_lat)
    o_ref[...] = ot_ref[...].T


def _rel_bias_tiles(rel_table):
    s = np.arange(QB)[:, None]
    t = np.arange(QB)[None, :]
    diag, prev, far = np.maximum(t - s, 0), QB + t - s, np.full((QB, QB), 2 * QB)
    kinds = [(prev, diag), (diag, far), (far, prev), (far, far)]
    n = jnp.asarray(np.stack([np.concatenate(k, axis=0) for k in kinds]).astype(np.int32))
    max_exact = REL_BUCKETS // 2
    large = max_exact + (jnp.log(jnp.maximum(n, 1).astype(F32) / max_exact)
                         / math.log(REL_MAX_DIST / max_exact) * (REL_BUCKETS - max_exact)).astype(I32)
    bucket = jnp.where(n < max_exact, n, jnp.minimum(large, REL_BUCKETS - 1))
    onehot = (bucket[..., None] == jnp.arange(REL_BUCKETS, dtype=I32)).astype(F32)
    return jnp.einsum("kstb,bh->hkst", onehot, rel_table.astype(F32), precision=lax.Precision.HIGHEST)


def _dsa_attention(h, mask, ckv, ckvt, w_uk, w_uv, rel_table, batch, seq):
    nq = seq // QB
    nk = seq // KB
    wuk = w_uk.transpose(1, 0, 2).astype(BF16)
    wuvt = w_uv.transpose(1, 2, 0).astype(BF16)
    bias = _rel_bias_tiles(rel_table)
    return pl.pallas_call(
        _dsa_attn_kernel,
        out_shape=jax.ShapeDtypeStruct((batch * seq, C_HEADS * C_DH), F32),
        grid=(batch, nq),
        in_specs=[pl.BlockSpec((QB, C_HEADS * C_DH), lambda b, j: (b * nq + j, CD_CQ // (C_HEADS * C_DH))),
                  pl.BlockSpec((None, nk, KB, QB), lambda b, j: (b, 0, 0, j)),
                  pl.BlockSpec((None, nk, KB, C_DLAT), lambda b, j: (b, 0, 0, 0)),
                  pl.BlockSpec((None, nk, C_DLAT, KB), lambda b, j: (b, 0, 0, 0)),
                  _const_spec(wuk), _const_spec(wuvt), _const_spec(bias)],
        out_specs=pl.BlockSpec((QB, C_HEADS * C_DH), lambda b, j: (b * nq + j, 0)),
        scratch_shapes=[pltpu.VMEM((C_HEADS, C_DLAT, QB), BF16), pltpu.VMEM((C_HEADS, 1, QB), F32),
                        pltpu.VMEM((C_HEADS, 1, QB), F32), pltpu.VMEM((C_HEADS, C_DLAT, QB), F32),
                        pltpu.VMEM((C_HEADS * C_DH, QB), F32)],
        compiler_params=_cparams("parallel", "parallel"),
        name="dsa_attention",
    )(h, mask, ckv.reshape(batch, nk, KB, C_DLAT), ckvt.reshape(batch, nk, C_DLAT, KB), wuk, wuvt, bias)


ROUTER_BLOCK = 512
MOE_TILE = 256


def _router_kernel(x_ref, wh_ref, wl_ref, b_ref, upper_ref, idx_ref, wgt_ref, rank_ref, cnt_ref):
    x = x_ref[...]
    xh = x.astype(BF16)
    xl = (x - xh.astype(F32)).astype(BF16)
    logit = _dot_nt(wh_ref[...], xh) + _dot_nt(wl_ref[...], xh) + _dot_nt(wh_ref[...], xl)
    aff = _sigmoid(logit)
    sel = aff + b_ref[...]
    s_rows = [sel[e:e + 1] for e in range(N_EXPERTS)]
    a_rows = [aff[e:e + 1] for e in range(N_EXPERTS)]
    n = EXPERTS_PER_GROUP

    g_best = jnp.zeros(s_rows[0].shape, I32)
    best = None
    for g in range(N_GROUPS):
        v = s_rows[g * n:(g + 1) * n]
        top2 = None
        for a in range(n):
            for b in range(a + 1, n):
                pair = v[a] + v[b]
                top2 = pair if top2 is None else jnp.maximum(top2, pair)
        if best is None:
            best = top2
        else:
            upd = top2 > best
            g_best = jnp.where(upd, g, g_best)
            best = jnp.where(upd, top2, best)

    sv, av = [], []
    for i in range(n):
        s_i, a_i = s_rows[i], a_rows[i]
        for g in range(1, N_GROUPS):
            pick = g_best == g
            s_i = jnp.where(pick, s_rows[g * n + i], s_i)
            a_i = jnp.where(pick, a_rows[g * n + i], a_i)
        sv.append(s_i)
        av.append(a_i)

    i1, s1, a1 = jnp.zeros_like(g_best), sv[0], av[0]
    for i in range(1, n):
        upd = sv[i] > s1
        i1 = jnp.where(upd, i, i1)
        s1 = jnp.where(upd, sv[i], s1)
        a1 = jnp.where(upd, av[i], a1)
    i2 = jnp.zeros_like(g_best)
    s2 = jnp.full(s1.shape, -jnp.inf, F32)
    a2 = jnp.zeros_like(a1)
    for i in range(n):
        cand = jnp.where(i1 == i, -jnp.inf, sv[i])
        upd = cand > s2
        i2 = jnp.where(upd, i, i2)
        s2 = jnp.where(upd, cand, s2)
        a2 = jnp.where(upd, av[i], a2)

    tot = a1 + a2
    e1 = g_best * n + i1
    e2 = g_best * n + i2
    idx_ref[0:1, :] = e1
    idx_ref[1:2, :] = e2
    wgt_ref[0:1, :] = a1 / tot
    wgt_ref[1:2, :] = a2 / tot

    @pl.when(pl.program_id(0) == 0)
    def _():
        cnt_ref[...] = jnp.zeros_like(cnt_ref)

    e_iota = lax.broadcasted_iota(I32, sel.shape, 0)
    oh1 = jnp.where(e_iota == e1, 1.0, 0.0)
    oh2 = jnp.where(e_iota == e2, 1.0, 0.0)
    both = oh1 + oh2
    before = cnt_ref[...] + _dot(both.astype(BF16), upper_ref[...])
    rank_ref[0:1, :] = jnp.sum(oh1 * before, axis=0, keepdims=True).astype(I32)
    rank_ref[1:2, :] = jnp.sum(oh2 * before, axis=0, keepdims=True).astype(I32)
    cnt_ref[...] = cnt_ref[...] + jnp.sum(both, axis=1, keepdims=True)


def _router(x2d, w_router, b_router):
    t, d = x2d.shape
    tb = min(ROUTER_BLOCK, t)
    wt = w_router.T.astype(F32)
    wh = wt.astype(BF16)
    wl = (wt - wh.astype(F32)).astype(BF16)
    r = np.arange(tb)
    upper = jnp.asarray(r[:, None] < r[None, :], BF16)
    pair_out = pl.BlockSpec((TOP_K, tb), lambda i: (0, i))
    fixed = lambda i: (0, 0)
    return pl.pallas_call(
        _router_kernel,
        out_shape=(jax.ShapeDtypeStruct((TOP_K, t), I32), jax.ShapeDtypeStruct((TOP_K, t), F32),
                   jax.ShapeDtypeStruct((TOP_K, t), I32), jax.ShapeDtypeStruct((N_EXPERTS, 1), F32)),
        grid=(t // tb,),
        in_specs=[pl.BlockSpec((tb, d), lambda i: (i, 0)),
                  pl.BlockSpec((N_EXPERTS, d), fixed), pl.BlockSpec((N_EXPERTS, d), fixed),
                  pl.BlockSpec((N_EXPERTS, 1), fixed), pl.BlockSpec((tb, tb), fixed)],
        out_specs=(pair_out, pair_out, pair_out, pl.BlockSpec((N_EXPERTS, 1), fixed)),
        compiler_params=_cparams("arbitrary"),
        name="moe_router",
    )(x2d, wh, wl, b_router.reshape(-1, 1).astype(F32), upper)


def _route_tables(idx, rank, counts, tm, n_items):
    cnt = counts.reshape(-1).astype(I32)
    start = jnp.cumsum(cnt) - cnt
    experts = jnp.arange(N_EXPERTS, dtype=I32)
    pos = rank + jnp.sum(jnp.where(idx[..., None] == experts, start, 0), axis=-1)
    first_tile = start // tm
    n_e = jnp.where(cnt > 0, (start + cnt - 1) // tm - first_tile + 1, 0)
    item_end = jnp.cumsum(n_e)
    item = jnp.arange(n_items, dtype=I32)
    used = item < item_end[-1]
    e_i = jnp.minimum(jnp.searchsorted(item_end, item, side="right"), N_EXPERTS - 1).astype(I32)
    e_last = jnp.max(jnp.where(cnt > 0, experts, 0))
    e_i = jnp.where(used, e_i, e_last)
    tile_i = jnp.where(used, first_tile[e_i] + item - (item_end - n_e)[e_i], (TOP_K * idx.shape[1]) // tm - 1)
    lo = jnp.where(used, jnp.maximum(start[e_i], tile_i * tm) - tile_i * tm, 0)
    hi = jnp.where(used, jnp.minimum(start[e_i] + cnt[e_i], (tile_i + 1) * tm) - tile_i * tm, 0)
    return pos.astype(I32), tile_i.astype(I32), e_i, lo.astype(I32), hi.astype(I32)


DMA_UNROLL = 8


def _dispatch_kernel(pos_ref, x_ref, xs_hbm, stage, sem, *, tb, n_tok):
    i = pl.program_id(0)
    slot = i % 2
    stage[slot] = x_ref[...]

    def issue(grp, carry):
        r0 = grp * DMA_UNROLL
        dst = [[pos_ref[s * n_tok + i * tb + r0 + u] for s in range(TOP_K)] for u in range(DMA_UNROLL)]
        for u in range(DMA_UNROLL):
            for s in range(TOP_K):
                pltpu.make_async_copy(stage.at[slot, pl.ds(r0 + u, 1)], xs_hbm.at[pl.ds(dst[u][s], 1)],
                                      sem.at[slot]).start()
        return carry

    lax.fori_loop(0, tb // DMA_UNROLL, issue, 0)

    def wait_block(s):
        for _ in range(TOP_K):
            pltpu.make_async_copy(stage.at[s], xs_hbm.at[pl.ds(0, tb)], sem.at[s]).wait()

    @pl.when(i >= 1)
    def _():
        wait_block(1 - slot)

    @pl.when(i == pl.num_programs(0) - 1)
    def _():
        wait_block(slot)


def _dispatch(x2d, pos_flat, tb):
    t, d = x2d.shape
    grid_spec = pltpu.PrefetchScalarGridSpec(
        num_scalar_prefetch=1, grid=(t // tb,),
        in_specs=[pl.BlockSpec((tb, d), lambda i, pos: (i, 0))],
        out_specs=pl.BlockSpec(memory_space=pl.ANY),
        scratch_shapes=[pltpu.VMEM((2, tb, d), F32), pltpu.SemaphoreType.DMA((2,))])
    return pl.pallas_call(
        functools.partial(_dispatch_kernel, tb=tb, n_tok=t),
        out_shape=jax.ShapeDtypeStruct((TOP_K * t, d), F32),
        grid_spec=grid_spec,
        compiler_params=_cparams("arbitrary"),
        name="moe_dispatch",
    )(pos_flat, x2d)


def _experts_kernel(tile_ref, exp_ref, lo_ref, hi_ref, x_ref, w1_ref, w3_ref, w2_ref, o_ref, w1b, w3b, w2b):
    i = pl.program_id(0)
    prev = jnp.maximum(i - 1, 0)

    @pl.when(jnp.logical_or(i == 0, exp_ref[i] != exp_ref[prev]))
    def _():
        w1b[...] = w1_ref[...].astype(BF16)
        w3b[...] = w3_ref[...].astype(BF16)
        w2b[...] = w2_ref[...].astype(BF16)

    lo = lo_ref[i]
    hi = hi_ref[i]
    first = jnp.logical_or(i == 0, tile_ref[i] != tile_ref[prev])

    @pl.when(hi > lo)
    def _():
        x = x_ref[...].astype(BF16)
        row = lax.broadcasted_iota(I32, (x.shape[0], 1), 0)
        mine = jnp.where(jnp.logical_and(row >= lo, row < hi), 1.0, 0.0)
        hid = _silu(_dot(x, w1b[...])) * _dot(x, w3b[...]) * mine
        y = _dot(hid.astype(BF16), w2b[...])

        @pl.when(first)
        def _():
            o_ref[...] = y

        @pl.when(jnp.logical_not(first))
        def _():
            o_ref[...] += y


def _experts(xs, tile_i, exp_i, lo, hi, w1, w3, w2, tm):
    n, d = xs.shape
    f = w1.shape[-1]
    by_tile = lambda i, tile, exp, lo, hi: (tile[i], 0)
    by_exp = lambda i, tile, exp, lo, hi: (exp[i], 0, 0)
    grid_spec = pltpu.PrefetchScalarGridSpec(
        num_scalar_prefetch=4, grid=(tile_i.shape[0],),
        in_specs=[pl.BlockSpec((tm, d), by_tile),
                  pl.BlockSpec((None, d, f), by_exp), pl.BlockSpec((None, d, f), by_exp),
                  pl.BlockSpec((None, f, d), by_exp)],
        out_specs=pl.BlockSpec((tm, d), by_tile),
        scratch_shapes=[pltpu.VMEM((d, f), BF16), pltpu.VMEM((d, f), BF16), pltpu.VMEM((f, d), BF16)])
    return pl.pallas_call(
        _experts_kernel,
        out_shape=jax.ShapeDtypeStruct((n, d), F32),
        grid_spec=grid_spec,
        compiler_params=_cparams("arbitrary"),
        name="moe_experts",
    )(tile_i, exp_i, lo, hi, xs, w1, w3, w2)


def _combine_ln_kernel(pos_ref, x_ref, w_ref, g_ref, b_ref, ys_hbm, o_ref, gbuf, sem, *, tb, n_tok):
    i = pl.program_id(0)
    slot = i % 2

    def start(blk, s):
        def issue(grp, carry):
            r0 = grp * DMA_UNROLL
            src = [[pos_ref[k * n_tok + blk * tb + r0 + u] for k in range(TOP_K)] for u in range(DMA_UNROLL)]
            for u in range(DMA_UNROLL):
                for k in range(TOP_K):
                    pltpu.make_async_copy(ys_hbm.at[pl.ds(src[u][k], 1)], gbuf.at[s, k, pl.ds(r0 + u, 1)],
                                          sem.at[s]).start()
            return carry
        lax.fori_loop(0, tb // DMA_UNROLL, issue, 0)

    @pl.when(i == 0)
    def _():
        start(0, 0)

    @pl.when(i + 1 < pl.num_programs(0))
    def _():
        start(i + 1, 1 - slot)

    for k in range(TOP_K):
        pltpu.make_async_copy(ys_hbm.at[pl.ds(0, tb)], gbuf.at[slot, k], sem.at[slot]).wait()
    w = w_ref[...]
    y = gbuf[slot, 0] * w[:, 0:1] + gbuf[slot, 1] * w[:, 1:2]
    o_ref[...] = _layer_norm_rows(DN_ALPHA * x_ref[...] + y, g_ref[...], b_ref[...])


def _combine_ln(x2d, ys, pos_flat, wgt_cols, g, b, tb):
    t, d = x2d.shape
    row = lambda i, pos: (i, 0)
    fixed = lambda i, pos: (0, 0)
    grid_spec = pltpu.PrefetchScalarGridSpec(
        num_scalar_prefetch=1, grid=(t // tb,),
        in_specs=[pl.BlockSpec((tb, d), row), pl.BlockSpec((tb, TOP_K), row),
                  pl.BlockSpec((1, d), fixed), pl.BlockSpec((1, d), fixed),
                  pl.BlockSpec(memory_space=pl.ANY)],
        out_specs=pl.BlockSpec((tb, d), row),
        scratch_shapes=[pltpu.VMEM((2, TOP_K, tb, d), F32), pltpu.SemaphoreType.DMA((2,))])
    return pl.pallas_call(
        functools.partial(_combine_ln_kernel, tb=tb, n_tok=t),
        out_shape=jax.ShapeDtypeStruct((t, d), F32),
        grid_spec=grid_spec,
        compiler_params=_cparams("arbitrary"),
        name="moe_combine_ln",
    )(pos_flat, x2d, wgt_cols, g.reshape(1, d), b.reshape(1, d), ys)


def _moe_ln(x2d, w_router, b_router, w1, w3, w2, g, b):
    t = x2d.shape[0]
    tm = min(MOE_TILE, t)
    idx, wgt, rank, counts = _router(x2d, w_router, b_router)
    n_items = TOP_K * t // tm + N_EXPERTS - 1
    pos, tile_i, exp_i, lo, hi = _route_tables(idx, rank, counts, tm, n_items)
    pos_flat = pos.reshape(-1)
    xs = _dispatch(x2d, pos_flat, tm)
    ys = _experts(xs, tile_i, exp_i, lo, hi, w1, w3, w2, tm)
    return _combine_ln(x2d, ys, pos_flat, wgt.T, g, b, tm)


def _gate_rows(col, batch, seq):
    tb = min(TIME_BLOCK, seq)
    return col.reshape(batch, seq, D_HEADS).transpose(0, 2, 1).reshape(batch, D_HEADS, seq // tb, tb // CHUNK, CHUNK)


def _mix_cd(x2d, w_in, rel_table, ckv_g, w_uk, w_uv, conv_w, gate_b, d_norm_g, batch, seq):
    t = x2d.shape[0]
    h = _project(x2d, *_pack_cd(w_in), min(1024, t), PROJ_TILE)
    ckv, ckvt, kdup = _dsa_prep(h, ckv_g, t)
    mask = _indexer(h, kdup, batch, seq)
    oc = _dsa_attention(h, mask, ckv, ckvt, w_uk, w_uv, rel_table, batch, seq)
    tail = h[:, CD_TAIL:CD_TAIL + LANES]
    ig_rows = _gate_rows(tail[:, TAIL_DI:TAIL_DI + D_HEADS], batch, seq)
    fg_rows = _gate_rows(tail[:, TAIL_DF:TAIL_DF + D_HEADS], batch, seq)
    od = _mlstm(h, conv_w, ig_rows, fg_rows, gate_b, d_norm_g, batch, seq,
                q_off=CD_DQ // (D_HEADS * D_DK), k_off=CD_DK // (D_HEADS * D_DK),
                v_off=CD_DV // (D_HEADS * D_DV), g_off=CD_DOG // (D_HEADS * D_DV))
    return oc, od


def kernel(x, w_in_ab, w_out_ab, hgrn_lb_logits, a_norm_g, gla_wa2, gla_ba2, b_norm_g, w_in_cd, w_out_cd,
           ckv_norm_g, w_uk, w_uv, mlstm_conv_w, mlstm_gate_b, d_norm_g, rel_table, w_router, b_router,
           moe_w1, moe_w3, moe_w2, ln_g, ln_b):
    batch, seq, d = x.shape
    x2d = x.reshape(batch * seq, d)
    for layer in range(DEPTH):
        li = layer // 2
        if layer % 2 == 0:
            mix_a, mix_b = _mix_ab(x2d, w_in_ab[li], hgrn_lb_logits, li, a_norm_g[li], gla_wa2[li], gla_ba2[li],
                                   b_norm_g[li], batch, seq)
            w_out = w_out_ab[li]
        else:
            mix_a, mix_b = _mix_cd(x2d, w_in_cd[li], rel_table, ckv_norm_g[li], w_uk[li], w_uv[li],
                                   mlstm_conv_w[li], mlstm_gate_b[li], d_norm_g[li], batch, seq)
            w_out = w_out_cd[li]
        ka = mix_a.shape[1]
        x2d = _outproj_ln(x2d, mix_a, mix_b, w_out[:ka].astype(BF16), w_out[ka:].astype(BF16),
                          ln_g[layer, 0], ln_b[layer, 0])
        x2d = _moe_ln(x2d, w_router, b_router, moe_w1[layer], moe_w3[layer], moe_w2[layer],
                      ln_g[layer, 1], ln_b[layer, 1])
    return x2d.reshape(batch, seq, d)
```

```python
import functools
import math

import numpy as np
import jax
import jax.numpy as jnp
from jax import lax
from jax.experimental import pallas as pl
from jax.experimental.pallas import tpu as pltpu

F32 = jnp.float32
BF16 = jnp.bfloat16
I32 = jnp.int32

D_MODEL = 2048
DEPTH = 2
A_HEADS, A_DK, A_DV = 8, 128, 128
B_HEADS, B_DK, B_DV = 4, 128, 256
B_GATE_RANK, B_GATE_TAU = 16, 16.0
C_HEADS, C_DH, C_DLAT = 8, 128, 256
IDX_HEADS, IDX_DIM = 16, 64
TOPK_MAX = 256
D_HEADS, D_DK, D_DV = 4, 128, 256
CONV_K = 4
REL_BUCKETS, REL_MAX_DIST = 32, 128
N_EXPERTS, N_GROUPS, TOP_K, D_EXPERT = 16, 4, 2, 512
EXPERTS_PER_GROUP = N_EXPERTS // N_GROUPS
DN_ALPHA = (2 * DEPTH) ** 0.25
EPS = 1e-5

LANES = 128
SUBLANES = 8
VMEM_LIMIT = 56 * 1024 * 1024

CHUNK = 128
N_LEVELS = 7
TIME_BLOCK = 256
QB = 128
NEG = -1e30
INT_MIN = -2 ** 31
INT_MAX = 2 ** 31 - 1


def _cparams(*sem):
    return pltpu.CompilerParams(dimension_semantics=sem, vmem_limit_bytes=VMEM_LIMIT)


def _dot(a, b):
    return jnp.dot(a, b, preferred_element_type=F32)


def _dot_nt(a, b):
    return lax.dot_general(a, b, (((1,), (1,)), ((), ())), preferred_element_type=F32)


def _dot_tn(a, b):
    return lax.dot_general(a, b, (((0,), (0,)), ((), ())), preferred_element_type=F32)


def _split3(a):
    hi = a.astype(BF16)
    r1 = a - hi.astype(F32)
    mid = r1.astype(BF16)
    lo = (r1 - mid.astype(F32)).astype(BF16)
    return hi, mid, lo


def _dot01(m01, a):
    hi, mid, lo = _split3(a)
    return _dot(m01, hi) + _dot(m01, mid) + _dot(m01, lo)


def _sigmoid(x):
    return 1.0 / (1.0 + jnp.exp(-x))


def _silu(x):
    return x * _sigmoid(x)


def _log_sigmoid(x):
    return jnp.minimum(x, 0.0) - jnp.log(1.0 + jnp.exp(-jnp.abs(x)))


def _proj_kernel(x_ref, w_ref, wt_ref, o_ref, xb_ref):
    j = pl.program_id(1)

    @pl.when(j == 0)
    def _():
        xb_ref[...] = x_ref[...].astype(BF16)

    @pl.when(j < pl.num_programs(1) - 1)
    def _():
        o_ref[...] = _dot(xb_ref[...], w_ref[...])

    @pl.when(j == pl.num_programs(1) - 1)
    def _():
        o_ref[...] = _dot(xb_ref[...], wt_ref[...])


def _cast_kernel(w_ref, o_ref):
    o_ref[...] = w_ref[...].astype(o_ref.dtype)


def _cast_columns(w, n_cols, tn):
    k = w.shape[0]
    return pl.pallas_call(
        _cast_kernel,
        out_shape=jax.ShapeDtypeStruct((k, n_cols), BF16),
        grid=(n_cols // tn,),
        in_specs=[pl.BlockSpec((k, tn), lambda j: (0, j))],
        out_specs=pl.BlockSpec((k, tn), lambda j: (0, j)),
        compiler_params=_cparams("parallel"),
        name="weight_cast",
    )(w)


def _project(x, w, w_tail, tm, tn):
    m, k = x.shape
    n_main = w.shape[1] // tn
    return pl.pallas_call(
        _proj_kernel,
        out_shape=jax.ShapeDtypeStruct((m, (n_main + 1) * tn), F32),
        grid=(m // tm, n_main + 1),
        in_specs=[pl.BlockSpec((tm, k), lambda i, j: (i, 0)),
                  pl.BlockSpec((k, tn), lambda i, j: (0, jnp.minimum(j, n_main - 1))),
                  pl.BlockSpec((k, tn), lambda i, j: (0, 0))],
        out_specs=pl.BlockSpec((tm, tn), lambda i, j: (i, j)),
        scratch_shapes=[pltpu.VMEM((tm, k), BF16)],
        compiler_params=_cparams("parallel", "arbitrary"),
        name="in_proj",
    )(x, w, w_tail)


def _layer_norm_rows(z, g, b):
    mu = jnp.mean(z, axis=-1, keepdims=True)
    zc = z - mu
    var = jnp.mean(zc * zc, axis=-1, keepdims=True)
    return zc * lax.rsqrt(var + EPS) * g + b


def _outproj_ln_kernel(x_ref, ma_ref, mb_ref, wa_ref, wb_ref, g_ref, b_ref, o_ref):
    mixed = _dot(ma_ref[...].astype(BF16), wa_ref[...]) + _dot(mb_ref[...].astype(BF16), wb_ref[...])
    o_ref[...] = _layer_norm_rows(DN_ALPHA * x_ref[...] + mixed, g_ref[...], b_ref[...])


def _outproj_ln(x, mix_a, mix_b, w_a, w_b, g, b, tm=256):
    m, d = x.shape
    ka, kb = mix_a.shape[1], mix_b.shape[1]
    row = lambda i: (i, 0)
    fixed = lambda i: (0, 0)
    return pl.pallas_call(
        _outproj_ln_kernel,
        out_shape=jax.ShapeDtypeStruct((m, d), F32),
        grid=(m // tm,),
        in_specs=[pl.BlockSpec((tm, d), row), pl.BlockSpec((tm, ka), row), pl.BlockSpec((tm, kb), row),
                  pl.BlockSpec((ka, d), fixed), pl.BlockSpec((kb, d), fixed),
                  pl.BlockSpec((1, d), fixed), pl.BlockSpec((1, d), fixed)],
        out_specs=pl.BlockSpec((tm, d), row),
        compiler_params=_cparams("parallel"),
        name="out_proj_ln",
    )(x, mix_a, mix_b, w_a, w_b, g.reshape(1, d), b.reshape(1, d))


def _chunk_constants():
    t = np.arange(CHUNK)
    tri = (t[:, None] >= t[None, :]).astype(np.float32)
    pair, odd = [], []
    for lev in range(1, N_LEVELS + 1):
        c = CHUNK >> lev
        pair.append((t[:, None] // (2 * c) == t[None, :] // (2 * c)).astype(np.float32))
        odd.append(np.broadcast_to((((t // c) & 1) == 1).astype(np.float32)[:, None], (CHUNK, LANES)))
    pair.append(np.eye(CHUNK, dtype=np.float32))
    return jnp.asarray(tri, BF16), jnp.asarray(np.stack(pair), F32), jnp.asarray(np.stack(odd), F32)


def _level_log_decay(la, bcum, lev):
    c = CHUNK >> lev
    if 2 * c >= SUBLANES:
        mids = [jnp.broadcast_to(bcum[g * 2 * c + c - 1:g * 2 * c + c], (2 * c, bcum.shape[1]))
                for g in range(CHUNK // (2 * c))]
        return -jnp.abs(bcum - (mids[0] if len(mids) == 1 else jnp.concatenate(mids, axis=0)))
    r = lax.broadcasted_iota(I32, la.shape, 0) & (2 * c - 1)
    if c == 2:
        nxt = pltpu.roll(la, CHUNK - 1, axis=0)
        prv = pltpu.roll(la, 1, axis=0)
        return jnp.where(r == 0, nxt, jnp.where(r == 1, 0.0, jnp.where(r == 2, la, la + prv)))
    return jnp.where(r == 1, la, 0.0)


def _glr_chunk(q, k, v, la, st_ref, cum_ref, pair_ref, odd_ref):
    bcum = _dot01(cum_ref[...], la)
    attn = pair_ref[N_LEVELS] * _dot_nt(q.astype(BF16), k.astype(BF16))
    for lev in range(1, N_LEVELS + 1):
        e = jnp.exp(_level_log_decay(la, bcum, lev))
        eq = e * odd_ref[lev - 1]
        ql = (q * eq).astype(BF16)
        kl = (k * (e - eq)).astype(BF16)
        attn = attn + pair_ref[lev - 1] * _dot_nt(ql, kl)
    st = st_ref[...]
    o = _dot_nt((q * jnp.exp(bcum)).astype(BF16), st.astype(BF16)) + _dot(attn.astype(BF16), v.astype(BF16))
    b_last = bcum[CHUNK - 1:CHUNK]
    kdec = (k * jnp.exp(b_last - bcum)).astype(BF16)
    st_ref[...] = st * jnp.exp(b_last) + _dot_tn(v.astype(BF16), kdec)
    return o


def _rms_gate(o, g, gate):
    ms = jnp.mean(o * o, axis=-1, keepdims=True)
    return o * lax.rsqrt(ms + EPS) * g * gate


HEAD_GROUP = 4


def _hgrn2_kernel(q_ref, f_ref, i_ref, g_ref, lb_ref, ng_ref, cum_ref, pair_ref, odd_ref, o_ref, st_ref):
    @pl.when(pl.program_id(2) == 0)
    def _():
        st_ref[...] = jnp.zeros_like(st_ref)

    def body(c, carry):
        rows = pl.ds(pl.multiple_of(c * CHUNK, CHUNK), CHUNK)
        for hh in range(HEAD_GROUP):
            ck = slice(A_DK * hh, A_DK * (hh + 1))
            cv = slice(A_DV * hh, A_DV * (hh + 1))
            lb = lb_ref[:, ck]
            f = lb + (1.0 - lb) * _sigmoid(f_ref[rows, ck])
            o = _glr_chunk(_silu(q_ref[rows, ck]), 1.0 - f, i_ref[rows, cv], jnp.log(f),
                           st_ref.at[hh], cum_ref, pair_ref, odd_ref)
            o_ref[rows, cv] = _rms_gate(o, ng_ref[...], _silu(g_ref[rows, cv]))
        return carry

    lax.fori_loop(0, q_ref.shape[0] // CHUNK, body, 0)


def _gla_kernel(q_ref, k_ref, v_ref, g_ref, r_ref, wa_ref, ba_ref, ng_ref, cum_ref, pair_ref, odd_ref,
                o_ref, st_ref):
    @pl.when(pl.program_id(2) == 0)
    def _():
        st_ref[...] = jnp.zeros_like(st_ref)

    def body(c, carry):
        rows = pl.ds(pl.multiple_of(c * CHUNK, CHUNK), CHUNK)
        pre = _dot(r_ref[rows, :].astype(BF16), wa_ref[...]) + ba_ref[...]
        la = _log_sigmoid(pre) * (1.0 / B_GATE_TAU)
        for hh in range(HEAD_GROUP):
            ck = slice(B_DK * hh, B_DK * (hh + 1))
            cv = slice(B_DV * hh, B_DV * (hh + 1))
            o = _glr_chunk(q_ref[rows, ck] * (B_DK ** -0.5), k_ref[rows, ck], v_ref[rows, cv], la[:, ck],
                           st_ref.at[hh], cum_ref, pair_ref, odd_ref)
            o_ref[rows, cv] = _rms_gate(o, ng_ref[...], _silu(g_ref[rows, cv]))
        return carry

    lax.fori_loop(0, q_ref.shape[0] // CHUNK, body, 0)


def _const_spec(arr):
    nd = arr.ndim
    return pl.BlockSpec(arr.shape, lambda *_: (0,) * nd)


def _hgrn2(h, lb, norm_g, batch, seq):
    tb = min(TIME_BLOCK, seq)
    nt = seq // tb
    ng = A_HEADS // HEAD_GROUP
    wk, wv = HEAD_GROUP * A_DK, HEAD_GROUP * A_DV
    consts = _chunk_constants()
    col = lambda seg, w: pl.BlockSpec((tb, w), lambda b, g, t, seg=seg: (b * nt + t, seg * ng + g))
    return pl.pallas_call(
        _hgrn2_kernel,
        out_shape=jax.ShapeDtypeStruct((batch * seq, A_HEADS * A_DV), F32),
        grid=(batch, ng, nt),
        in_specs=[col(0, wk), col(1, wk), col(2, wv), col(3, wv),
                  pl.BlockSpec((1, wk), lambda b, g, t: (0, g)),
                  pl.BlockSpec((1, A_DV), lambda b, g, t: (0, 0))] + [_const_spec(c) for c in consts],
        out_specs=pl.BlockSpec((tb, wv), lambda b, g, t: (b * nt + t, g)),
        scratch_shapes=[pltpu.VMEM((HEAD_GROUP, A_DV, A_DK), F32)],
        compiler_params=_cparams("parallel", "parallel", "arbitrary"),
        name="hgrn2",
    )(h, h, h, h, lb.reshape(1, -1), norm_g.reshape(1, -1), *consts)


def _gla(h, wa2p, ba2, norm_g, batch, seq, q_off, k_off, v_off, g_off, r_off):
    tb = min(TIME_BLOCK, seq)
    nt = seq // tb
    wk, wv = B_HEADS * B_DK, B_HEADS * B_DV
    consts = _chunk_constants()
    col = lambda off, w: pl.BlockSpec((tb, w), lambda b, g, t, off=off: (b * nt + t, off))
    fixed = lambda b, g, t: (0, 0)
    return pl.pallas_call(
        _gla_kernel,
        out_shape=jax.ShapeDtypeStruct((batch * seq, wv), F32),
        grid=(batch, 1, nt),
        in_specs=[col(q_off, wk), col(k_off, wk), col(v_off, wv), col(g_off, wv), col(r_off, LANES),
                  pl.BlockSpec((LANES, wk), fixed), pl.BlockSpec((1, wk), fixed),
                  pl.BlockSpec((1, B_DV), fixed)] + [_const_spec(c) for c in consts],
        out_specs=pl.BlockSpec((tb, wv), lambda b, g, t: (b * nt + t, 0)),
        scratch_shapes=[pltpu.VMEM((B_HEADS, B_DV, B_DK), F32)],
        compiler_params=_cparams("parallel", "parallel", "arbitrary"),
        name="gla",
    )(h, h, h, h, h, wa2p, ba2.reshape(1, -1), norm_g.reshape(1, -1), *consts)


PROJ_TILE = 512
AB_MAIN = 4 * A_HEADS * A_DK + 2 * B_HEADS * B_DK + 2 * B_HEADS * B_DV
AB_PAD = AB_MAIN + PROJ_TILE


def _mix_ab(x2d, w_in, lb_logits, li, a_norm_g, wa2, ba2, b_norm_g, batch, seq):
    d = x2d.shape[1]
    w_tail = jnp.pad(w_in[:, AB_MAIN:], ((0, 0), (0, AB_PAD - w_in.shape[1]))).astype(BF16)
    h = _project(x2d, _cast_columns(w_in, AB_MAIN, PROJ_TILE), w_tail, min(1024, x2d.shape[0]), PROJ_TILE)
    lb = jnp.cumsum(jax.nn.softmax(lb_logits.astype(F32), axis=0), axis=0)[li]
    oa = _hgrn2(h, lb, a_norm_g, batch, seq)
    wa2p = jnp.concatenate([wa2, jnp.zeros((LANES - B_GATE_RANK, wa2.shape[1]), F32)], axis=0).astype(BF16)
    a_cols = 4 * A_HEADS * A_DK
    wk, wv = B_HEADS * B_DK, B_HEADS * B_DV
    ob = _gla(h, wa2p, ba2, b_norm_g, batch, seq, q_off=a_cols // wk, k_off=a_cols // wk + 1,
              v_off=(a_cols + 2 * wk) // wv, g_off=(a_cols + 2 * wk) // wv + 1, r_off=AB_MAIN // LANES)
    return oa, ob


CONV_HALO = 8


def _causal_conv(x_ref, w_ref, buf_ref, tail_ref):
    tb = x_ref.shape[0]
    x = x_ref[...]
    buf_ref[0:CONV_HALO, :] = tail_ref[...]
    buf_ref[CONV_HALO:CONV_HALO + tb, :] = x
    tail_ref[...] = x[tb - CONV_HALO:tb]
    y = w_ref[CONV_K - 1:CONV_K, :] * x
    for j in range(CONV_K - 1):
        y = y + w_ref[j:j + 1, :] * buf_ref[pl.ds(CONV_HALO - (CONV_K - 1) + j, tb), :]
    return y


def _row_to_col(row, eye):
    return jnp.sum(jnp.where(eye, row, 0.0), axis=1, keepdims=True)


def _mlstm_kernel(q_ref, k_ref, v_ref, og_ref, wq_ref, wk_ref, ig_ref, fg_ref, gb_ref, ng_ref, tri_ref,
                  o_ref, ct_ref, n_ref, m_ref, qt_ref, kt_ref, qs_ref, ks_ref, buf_ref):
    @pl.when(pl.program_id(2) == 0)
    def _():
        ct_ref[...] = jnp.zeros_like(ct_ref)
        n_ref[...] = jnp.zeros_like(n_ref)
        m_ref[...] = jnp.zeros_like(m_ref)
        qt_ref[...] = jnp.zeros_like(qt_ref)
        kt_ref[...] = jnp.zeros_like(kt_ref)

    qs_ref[...] = _silu(_causal_conv(q_ref, wq_ref, buf_ref, qt_ref))
    ks_ref[...] = _silu(_causal_conv(k_ref, wk_ref, buf_ref, kt_ref)) * (D_DK ** -0.5)

    r_i = lax.broadcasted_iota(I32, (CHUNK, CHUNK), 0)
    c_i = lax.broadcasted_iota(I32, (CHUNK, CHUNK), 1)
    eye = r_i == c_i
    causal = r_i >= c_i

    def body(c, carry):
        rows = pl.ds(pl.multiple_of(c * CHUNK, CHUNK), CHUNK)
        tri = tri_ref[...]
        for hh in range(D_HEADS):
            ck = slice(D_DK * hh, D_DK * (hh + 1))
            cv = slice(D_DV * hh, D_DV * (hh + 1))
            q = qs_ref[rows, ck]
            k = ks_ref[rows, ck]
            v = v_ref[rows, cv].astype(BF16)
            qb = q.astype(BF16)
            ig_row = ig_ref[hh, pl.ds(c, 1), :] + gb_ref[0, hh]
            lf_row = _log_sigmoid(fg_ref[hh, pl.ds(c, 1), :] + gb_ref[1, hh])
            hi, mid, lo = _split3(lf_row)
            bcum_row = _dot(hi, tri) + _dot(mid, tri) + _dot(lo, tri)
            bcum_col = _row_to_col(bcum_row, eye)
            ig_col = _row_to_col(ig_row, eye)
            m_prev = m_ref[hh, :, 0:1]
            log_w = jnp.where(causal, bcum_col - bcum_row + ig_row, NEG)
            log_inter = bcum_col + m_prev
            m_t = jnp.maximum(jnp.max(log_w, axis=1, keepdims=True), log_inter)
            s = _dot_nt(qb, k.astype(BF16)) * jnp.exp(log_w - m_t)
            w_inter = jnp.exp(log_inter - m_t)
            num = _dot(s.astype(BF16), v) + w_inter * _dot_nt(qb, ct_ref[hh].astype(BF16))
            qn = jnp.sum(s, axis=1, keepdims=True) + w_inter * jnp.sum(q * n_ref[hh], axis=1, keepdims=True)
            h = num / jnp.maximum(jnp.abs(qn), jnp.exp(-m_t))
            o_ref[rows, cv] = _rms_gate(h, ng_ref[...], _sigmoid(og_ref[rows, cv]))
            b_last = bcum_row[:, CHUNK - 1:CHUNK]
            log_u = b_last - bcum_col + ig_col
            m_new = jnp.maximum(b_last + m_prev, jnp.max(log_u, axis=0, keepdims=True))
            decay = jnp.exp(b_last + m_prev - m_new)
            ku = k * jnp.exp(log_u - m_new)
            ct_ref[hh] = decay * ct_ref[hh] + _dot_tn(v, ku.astype(BF16))
            n_ref[hh] = decay * n_ref[hh] + jnp.sum(ku, axis=0, keepdims=True)
            m_ref[hh] = jnp.broadcast_to(m_new, (1, LANES))
        return carry

    lax.fori_loop(0, q_ref.shape[0] // CHUNK, body, 0)


def _mlstm(h, conv_w, ig_rows, fg_rows, gate_b, norm_g, batch, seq, q_off, k_off, v_off, g_off):
    tb = min(TIME_BLOCK, seq)
    nt = seq // tb
    nc = tb // CHUNK
    wk, wv = D_HEADS * D_DK, D_HEADS * D_DV
    t = np.arange(CHUNK)
    tri = jnp.asarray(t[:, None] <= t[None, :], BF16)
    gb = jnp.broadcast_to(gate_b.reshape(2, D_HEADS, 1, 1), (2, D_HEADS, 1, CHUNK)).astype(F32)
    col = lambda off, w: pl.BlockSpec((tb, w), lambda b, g, t, off=off: (b * nt + t, off))
    gate = pl.BlockSpec((None, D_HEADS, None, nc, CHUNK), lambda b, g, t: (b, 0, t, 0, 0))
    fixed = lambda b, g, t: (0, 0)
    return pl.pallas_call(
        _mlstm_kernel,
        out_shape=jax.ShapeDtypeStruct((batch * seq, wv), F32),
        grid=(batch, 1, nt),
        in_specs=[col(q_off, wk), col(k_off, wk), col(v_off, wv), col(g_off, wv),
                  pl.BlockSpec((CONV_K, wk), lambda b, g, t: (0, 0)),
                  pl.BlockSpec((CONV_K, wk), lambda b, g, t: (0, 1)),
                  gate, gate,
                  pl.BlockSpec((2, D_HEADS, 1, CHUNK), lambda b, g, t: (0, 0, 0, 0)),
                  pl.BlockSpec((1, D_DV), fixed), pl.BlockSpec((CHUNK, CHUNK), fixed)],
        out_specs=pl.BlockSpec((tb, wv), lambda b, g, t: (b * nt + t, 0)),
        scratch_shapes=[pltpu.VMEM((D_HEADS, D_DV, D_DK), F32), pltpu.VMEM((D_HEADS, 1, D_DK), F32),
                        pltpu.VMEM((D_HEADS, 1, LANES), F32),
                        pltpu.VMEM((CONV_HALO, wk), F32), pltpu.VMEM((CONV_HALO, wk), F32),
                        pltpu.VMEM((tb, wk), F32), pltpu.VMEM((tb, wk), F32),
                        pltpu.VMEM((tb + CONV_HALO, wk), F32)],
        compiler_params=_cparams("parallel", "parallel", "arbitrary"),
        name="mlstm",
    )(h, h, h, h, conv_w, conv_w, ig_rows, fg_rows, gb, norm_g.reshape(1, -1), tri)


CD_CQ, CD_IQ, CD_DQ, CD_DK, CD_DV, CD_DOG, CD_CKV, CD_TAIL = 0, 1024, 2048, 2560, 3072, 4096, 5120, 5376
CD_PAD = 5632
TAIL_IK, TAIL_IW, TAIL_DI, TAIL_DF = 0, 64, 80, 84


def _pack_cd(w_in):
    cq, ckv, iq, ik, iw, dq, dk, dv, di, df, dog = jnp.split(
        w_in, [int(i) for i in np.cumsum(
            (C_HEADS * C_DH, C_DLAT, IDX_HEADS * IDX_DIM, IDX_DIM, IDX_HEADS, D_HEADS * D_DK, D_HEADS * D_DK,
             D_HEADS * D_DV, D_HEADS, D_HEADS))], axis=1)
    used = CD_TAIL + IDX_DIM + IDX_HEADS + 2 * D_HEADS
    pad = jnp.zeros((w_in.shape[0], CD_PAD - used), F32)
    main = jnp.concatenate([cq, iq, dq, dk, dv, dog], axis=1).astype(BF16)
    tail = jnp.concatenate([ckv, ik, iw, di, df, pad], axis=1).astype(BF16)
    return main, tail


KB = 2 * QB


def _dsa_prep_kernel(ckv_ref, tail_ref, g_ref, ckv_o, ckvt_o, kdup_o):
    c = ckv_ref[...]
    cn = c * lax.rsqrt(jnp.mean(c * c, axis=-1, keepdims=True) + EPS) * g_ref[...]
    ckv_o[...] = cn.astype(BF16)
    ckvt_o[...] = cn.T.astype(BF16)
    tail = tail_ref[...]
    lane = lax.broadcasted_iota(I32, tail.shape, 1)
    kdup_o[...] = jnp.where(lane < IDX_DIM, tail, pltpu.roll(tail, IDX_DIM, axis=1)).astype(BF16)


def _dsa_prep(h, ckv_g, n_rows):
    nb = n_rows // KB
    return pl.pallas_call(
        _dsa_prep_kernel,
        out_shape=(jax.ShapeDtypeStruct((nb, KB, C_DLAT), BF16), jax.ShapeDtypeStruct((nb, C_DLAT, KB), BF16),
                   jax.ShapeDtypeStruct((nb, KB, LANES), BF16)),
        grid=(nb,),
        in_specs=[pl.BlockSpec((KB, C_DLAT), lambda i: (i, CD_CKV // C_DLAT)),
                  pl.BlockSpec((KB, LANES), lambda i: (i, CD_TAIL // LANES)),
                  pl.BlockSpec((1, C_DLAT), lambda i: (0, 0))],
        out_specs=(pl.BlockSpec((None, KB, C_DLAT), lambda i: (i, 0, 0)),
                   pl.BlockSpec((None, C_DLAT, KB), lambda i: (i, 0, 0)),
                   pl.BlockSpec((None, KB, LANES), lambda i: (i, 0, 0))),
        compiler_params=_cparams("parallel"),
        name="dsa_prep",
    )(h, h, ckv_g.reshape(1, -1))


def _sortable_key(x):
    b = lax.bitcast_convert_type(x, I32)
    key = b ^ ((b >> 31) & 0x7FFFFFFF)
    return jnp.where(key == -1, 0, key)


BISECT_STEPS = 4


def _indexer_kernel(iq_ref, tail_ref, kdup_ref, tri_ref, mask_ref, key_ref, wst_ref, *, k_sel):
    j = pl.program_id(1)
    nk = key_ref.shape[0]
    n_live = (j * QB + QB + KB - 1) // KB
    w_t = tail_ref[...].T
    lane = lax.broadcasted_iota(I32, (QB, LANES), 1)
    for p in range(IDX_HEADS // 2):
        pair = iq_ref[:, LANES * p:LANES * (p + 1)]
        wst_ref[p, 0:QB, :] = jnp.where(lane < IDX_DIM, pair, 0.0).astype(BF16)
        wst_ref[p, QB:2 * QB, :] = jnp.where(lane >= IDX_DIM, pair, 0.0).astype(BF16)

    s_loc = lax.broadcasted_iota(I32, (KB, QB), 0)
    t_abs = j * QB + lax.broadcasted_iota(I32, (1, QB), 1)

    def score_chunk(kc, carry):
        kd = kdup_ref[kc]
        acc = jnp.zeros((KB, QB), F32)
        for p in range(IDX_HEADS // 2):
            r0 = TAIL_IW + 2 * p
            both = _dot_nt(kd, wst_ref[p])
            acc = acc + jnp.maximum(both[:, 0:QB], 0.0) * w_t[r0:r0 + 1, :]
            acc = acc + jnp.maximum(both[:, QB:2 * QB], 0.0) * w_t[r0 + 1:r0 + 2, :]
        key = jnp.where(kc * KB + s_loc > t_abs, INT_MIN, _sortable_key(acc))
        key_ref[kc] = key
        key = key.reshape(KB // SUBLANES, SUBLANES, QB)
        k_max, k_min = carry
        k_max = jnp.maximum(k_max, jnp.max(key, axis=0))
        k_min = jnp.minimum(k_min, jnp.min(jnp.where(key == INT_MIN, INT_MAX, key), axis=0))
        return k_max, k_min

    k_max, k_min = lax.fori_loop(0, n_live, score_chunk, (jnp.full((SUBLANES, QB), INT_MIN, I32),
                                                           jnp.full((SUBLANES, QB), INT_MAX, I32)))
    for shift in (4, 2, 1):
        k_max = jnp.maximum(k_max, pltpu.roll(k_max, shift, axis=0))
        k_min = jnp.minimum(k_min, pltpu.roll(k_min, shift, axis=0))
    k_row = jnp.minimum(k_sel, j * QB + lax.broadcasted_iota(I32, (SUBLANES, QB), 1) + 1)

    def count(pred):
        def add(kc, acc):
            hit = jnp.where(pred(key_ref[kc].reshape(KB // SUBLANES, SUBLANES, QB)), 1, 0)
            return acc + jnp.sum(hit, axis=0)
        acc = lax.fori_loop(0, n_live, add, jnp.zeros((SUBLANES, QB), I32))
        for shift in (4, 2, 1):
            acc = acc + pltpu.roll(acc, shift, axis=0)
        return acc

    def unfinished(state):
        lo, hi = state
        return jnp.max(jnp.where(lo < hi, 1, 0)) > 0

    def halve(state):
        lo, hi = state
        mid = (lo >> 1) + (hi >> 1) + (((lo & 1) + (hi & 1) + 1) >> 1)
        cnt = count(lambda k: k >= mid)
        enough = cnt >= k_row
        lo_n = jnp.where(enough, mid, lo)
        hi_n = jnp.where(cnt == k_row, mid, jnp.where(enough, hi, mid - 1))
        return lo_n, hi_n

    def halve_steps(state):
        for _ in range(BISECT_STEPS):
            state = halve(state)
        return state

    tau8, _ = lax.while_loop(unfinished, halve_steps, (k_min, k_max))
    n_ge = count(lambda k: k >= tau8)
    has_tie = jnp.max(jnp.where(n_ge != k_row, 1, 0)) > 0
    tau = tau8[0:1, :]

    @pl.when(jnp.logical_not(has_tie))
    def _():
        def put(kc, carry):
            mask_ref[kc] = jnp.where(key_ref[kc] >= tau, 1.0, 0.0).astype(BF16)
            return carry
        lax.fori_loop(0, n_live, put, 0)

    @pl.when(has_tie)
    def _():
        need = (k_row - count(lambda k: k > tau8)).astype(F32)[0:1, :]

        def put(kc, seen):
            k = key_ref[kc]
            eq = jnp.where(k == tau, 1.0, 0.0)
            before = _dot(tri_ref[...], eq.astype(BF16)) + seen
            take = jnp.where(k > tau, 1.0, jnp.where(before < need, eq, 0.0))
            mask_ref[kc] = take.astype(BF16)
            return seen + jnp.sum(eq, axis=0, keepdims=True)
        lax.fori_loop(0, n_live, put, jnp.zeros((1, QB), F32))

    def clear(kc, carry):
        mask_ref[kc] = jnp.zeros((KB, QB), BF16)
        return carry
    lax.fori_loop(n_live, nk, clear, 0)


def _indexer(h, kdup, batch, seq):
    nq = seq // QB
    nk = seq // KB
    k_sel = min(TOPK_MAX, seq // 4)
    r = np.arange(KB)
    tri = jnp.asarray(r[None, :] < r[:, None], BF16)
    return pl.pallas_call(
        functools.partial(_indexer_kernel, k_sel=k_sel),
        out_shape=jax.ShapeDtypeStruct((batch, nk, KB, seq), BF16),
        grid=(batch, nq),
        in_specs=[pl.BlockSpec((QB, IDX_HEADS * IDX_DIM), lambda b, j: (b * nq + j, CD_IQ // (IDX_HEADS * IDX_DIM))),
                  pl.BlockSpec((QB, LANES), lambda b, j: (b * nq + j, CD_TAIL // LANES)),
                  pl.BlockSpec((None, nk, KB, LANES), lambda b, j: (b, 0, 0, 0)),
                  pl.BlockSpec((KB, KB), lambda b, j: (0, 0))],
        out_specs=pl.BlockSpec((None, nk, KB, QB), lambda b, j: (b, 0, 0, j)),
        scratch_shapes=[pltpu.VMEM((nk, KB, QB), I32), pltpu.VMEM((IDX_HEADS // 2, 2 * QB, LANES), BF16)],
        compiler_params=_cparams("parallel", "parallel"),
        name="dsa_indexer",
    )(h, h, kdup.reshape(batch, nk, KB, LANES), tri)


def _dsa_attn_kernel(cq_ref, mask_ref, ckv_ref, ckvt_ref, wuk_ref, wuvt_ref, bias_ref, o_ref,
                     qt_ref, m_ref, l_ref, acc_ref, ot_ref):
    j = pl.program_id(1)
    for hh in range(C_HEADS):
        q_h = cq_ref[:, C_DH * hh:C_DH * (hh + 1)].astype(BF16)
        half = slice(QB * (hh % 2), QB * (hh % 2 + 1))
        qt_ref[hh // 2, :, half] = (_dot_nt(wuk_ref[hh], q_h) * (C_DH ** -0.5)).astype(BF16)
    m_ref[...] = jnp.full(m_ref.shape, NEG, F32)
    l_ref[...] = jnp.zeros_like(l_ref)
    acc_ref[...] = jnp.zeros_like(acc_ref)

    n_live = (j * QB + QB + KB - 1) // KB
    odd = (j % 2) == 1

    def body(kc, carry):
        ck = ckv_ref[kc]
        ckt = ckvt_ref[kc]
        drop = jnp.where(mask_ref[kc].astype(F32) > 0.5, 0.0, NEG)
        drop = jnp.concatenate([drop, drop], axis=1)
        back = n_live - 1 - kc
        which = jnp.where(back == 0, jnp.where(odd, 0, 1), jnp.where(jnp.logical_and(back == 1, jnp.logical_not(odd)), 2, 3))
        for hp in range(C_HEADS // 2):
            logit = _dot(ck, qt_ref[hp]) + (bias_ref[hp, which] + drop)
            m_old = m_ref[hp]
            m_new = jnp.maximum(m_old, jnp.max(logit, axis=0, keepdims=True))
            alpha = jnp.exp(m_old - m_new)
            p = jnp.exp(logit - m_new)
            l_ref[hp] = alpha * l_ref[hp] + jnp.sum(p, axis=0, keepdims=True)
            acc_ref[hp] = alpha * acc_ref[hp] + _dot(ckt, p.astype(BF16))
            m_ref[hp] = m_new
        return carry

    lax.fori_loop(0, n_live, body, 0)
    for hh in range(C_HEADS):
        half = slice(QB * (hh % 2), QB * (hh % 2 + 1))
        o_lat = (acc_ref[hh // 2, :, half] * (1.0 / l_ref[hh // 2, :, half])).astype(BF16)
        ot_ref[C_DH * hh:C_DH * (hh + 1), :] = _dot(wuvt_ref[hh], o_lat)
    o_ref[...] = ot_ref[...].T


def _rel_bias_tiles(rel_table):
    s = np.arange(QB)[:, None]
    t = np.arange(QB)[None, :]
    diag, prev, far = np.maximum(t - s, 0), QB + t - s, np.full((QB, QB), 2 * QB)
    kinds = [(prev, diag), (diag, far), (far, prev), (far, far)]
    n = jnp.asarray(np.stack([np.concatenate(k, axis=0) for k in kinds]).astype(np.int32))
    max_exact = REL_BUCKETS // 2
    large = max_exact + (jnp.log(jnp.maximum(n, 1).astype(F32) / max_exact)
                         / math.log(REL_MAX_DIST / max_exact) * (REL_BUCKETS - max_exact)).astype(I32)
    bucket = jnp.where(n < max_exact, n, jnp.minimum(large, REL_BUCKETS - 1))
    onehot = (bucket[..., None] == jnp.arange(REL_BUCKETS, dtype=I32)).astype(F32)
    bias = jnp.einsum("kstb,bh->hkst", onehot, rel_table.astype(F32), precision=lax.Precision.HIGHEST)
    return bias.reshape(C_HEADS // 2, 2, 4, KB, QB).transpose(0, 2, 3, 1, 4).reshape(C_HEADS // 2, 4, KB, 2 * QB)


def _dsa_attention(h, mask, ckv, ckvt, w_uk, w_uv, rel_table, batch, seq):
    nq = seq // QB
    nk = seq // KB
    wuk = w_uk.transpose(1, 0, 2).astype(BF16)
    wuvt = w_uv.transpose(1, 2, 0).astype(BF16)
    bias = _rel_bias_tiles(rel_table)
    return pl.pallas_call(
        _dsa_attn_kernel,
        out_shape=jax.ShapeDtypeStruct((batch * seq, C_HEADS * C_DH), F32),
        grid=(batch, nq),
        in_specs=[pl.BlockSpec((QB, C_HEADS * C_DH), lambda b, j: (b * nq + j, CD_CQ // (C_HEADS * C_DH))),
                  pl.BlockSpec((None, nk, KB, QB), lambda b, j: (b, 0, 0, j)),
                  pl.BlockSpec((None, nk, KB, C_DLAT), lambda b, j: (b, 0, 0, 0)),
                  pl.BlockSpec((None, nk, C_DLAT, KB), lambda b, j: (b, 0, 0, 0)),
                  _const_spec(wuk), _const_spec(wuvt), _const_spec(bias)],
        out_specs=pl.BlockSpec((QB, C_HEADS * C_DH), lambda b, j: (b * nq + j, 0)),
        scratch_shapes=[pltpu.VMEM((C_HEADS // 2, C_DLAT, 2 * QB), BF16), pltpu.VMEM((C_HEADS // 2, 1, 2 * QB), F32),
                        pltpu.VMEM((C_HEADS // 2, 1, 2 * QB), F32), pltpu.VMEM((C_HEADS // 2, C_DLAT, 2 * QB), F32),
                        pltpu.VMEM((C_HEADS * C_DH, QB), F32)],
        compiler_params=_cparams("parallel", "parallel"),
        name="dsa_attention",
    )(h, mask, ckv.reshape(batch, nk, KB, C_DLAT), ckvt.reshape(batch, nk, C_DLAT, KB), wuk, wuvt, bias)


ROUTER_BLOCK = 512
MOE_TILE = 256


def _router_kernel(x_ref, wh_ref, wl_ref, b_ref, upper_ref, idx_ref, wgt_ref, rank_ref, cnt_ref):
    x = x_ref[...]
    xh = x.astype(BF16)
    xl = (x - xh.astype(F32)).astype(BF16)
    logit = _dot_nt(wh_ref[...], xh) + _dot_nt(wl_ref[...], xh) + _dot_nt(wh_ref[...], xl)
    aff = _sigmoid(logit)
    sel = aff + b_ref[...]
    s_rows = [sel[e:e + 1] for e in range(N_EXPERTS)]
    a_rows = [aff[e:e + 1] for e in range(N_EXPERTS)]
    n = EXPERTS_PER_GROUP

    g_best = jnp.zeros(s_rows[0].shape, I32)
    best = None
    for g in range(N_GROUPS):
        v = s_rows[g * n:(g + 1) * n]
        top2 = None
        for a in range(n):
            for b in range(a + 1, n):
                pair = v[a] + v[b]
                top2 = pair if top2 is None else jnp.maximum(top2, pair)
        if best is None:
            best = top2
        else:
            upd = top2 > best
            g_best = jnp.where(upd, g, g_best)
            best = jnp.where(upd, top2, best)

    sv, av = [], []
    for i in range(n):
        s_i, a_i = s_rows[i], a_rows[i]
        for g in range(1, N_GROUPS):
            pick = g_best == g
            s_i = jnp.where(pick, s_rows[g * n + i], s_i)
            a_i = jnp.where(pick, a_rows[g * n + i], a_i)
        sv.append(s_i)
        av.append(a_i)

    i1, s1, a1 = jnp.zeros_like(g_best), sv[0], av[0]
    for i in range(1, n):
        upd = sv[i] > s1
        i1 = jnp.where(upd, i, i1)
        s1 = jnp.where(upd, sv[i], s1)
        a1 = jnp.where(upd, av[i], a1)
    i2 = jnp.zeros_like(g_best)
    s2 = jnp.full(s1.shape, -jnp.inf, F32)
    a2 = jnp.zeros_like(a1)
    for i in range(n):
        cand = jnp.where(i1 == i, -jnp.inf, sv[i])
        upd = cand > s2
        i2 = jnp.where(upd, i, i2)
        s2 = jnp.where(upd, cand, s2)
        a2 = jnp.where(upd, av[i], a2)

    tot = a1 + a2
    e1 = g_best * n + i1
    e2 = g_best * n + i2
    idx_ref[0:1, :] = e1
    idx_ref[1:2, :] = e2
    wgt_ref[0:1, :] = a1 / tot
    wgt_ref[1:2, :] = a2 / tot

    @pl.when(pl.program_id(0) == 0)
    def _():
        cnt_ref[...] = jnp.zeros_like(cnt_ref)

    e_iota = lax.broadcasted_iota(I32, sel.shape, 0)
    oh1 = jnp.where(e_iota == e1, 1.0, 0.0)
    oh2 = jnp.where(e_iota == e2, 1.0, 0.0)
    both = oh1 + oh2
    before = cnt_ref[...] + _dot(both.astype(BF16), upper_ref[...])
    rank_ref[0:1, :] = jnp.sum(oh1 * before, axis=0, keepdims=True).astype(I32)
    rank_ref[1:2, :] = jnp.sum(oh2 * before, axis=0, keepdims=True).astype(I32)
    cnt_ref[...] = cnt_ref[...] + jnp.sum(both, axis=1, keepdims=True)


def _router(x2d, w_router, b_router):
    t, d = x2d.shape
    tb = min(ROUTER_BLOCK, t)
    wt = w_router.T.astype(F32)
    wh = wt.astype(BF16)
    wl = (wt - wh.astype(F32)).astype(BF16)
    r = np.arange(tb)
    upper = jnp.asarray(r[:, None] < r[None, :], BF16)
    pair_out = pl.BlockSpec((TOP_K, tb), lambda i: (0, i))
    fixed = lambda i: (0, 0)
    return pl.pallas_call(
        _router_kernel,
        out_shape=(jax.ShapeDtypeStruct((TOP_K, t), I32), jax.ShapeDtypeStruct((TOP_K, t), F32),
                   jax.ShapeDtypeStruct((TOP_K, t), I32), jax.ShapeDtypeStruct((N_EXPERTS, 1), F32)),
        grid=(t // tb,),
        in_specs=[pl.BlockSpec((tb, d), lambda i: (i, 0)),
                  pl.BlockSpec((N_EXPERTS, d), fixed), pl.BlockSpec((N_EXPERTS, d), fixed),
                  pl.BlockSpec((N_EXPERTS, 1), fixed), pl.BlockSpec((tb, tb), fixed)],
        out_specs=(pair_out, pair_out, pair_out, pl.BlockSpec((N_EXPERTS, 1), fixed)),
        compiler_params=_cparams("arbitrary"),
        name="moe_router",
    )(x2d, wh, wl, b_router.reshape(-1, 1).astype(F32), upper)


def _route_tables(idx, rank, counts, tm, n_items):
    cnt = counts.reshape(-1).astype(I32)
    start = jnp.cumsum(cnt) - cnt
    experts = jnp.arange(N_EXPERTS, dtype=I32)
    pos = rank + jnp.sum(jnp.where(idx[..., None] == experts, start, 0), axis=-1)
    first_tile = start // tm
    n_e = jnp.where(cnt > 0, (start + cnt - 1) // tm - first_tile + 1, 0)
    item_end = jnp.cumsum(n_e)
    item = jnp.arange(n_items, dtype=I32)
    used = item < item_end[-1]
    e_i = jnp.minimum(jnp.sum((item[:, None] >= item_end[None, :]).astype(I32), axis=1), N_EXPERTS - 1)
    e_last = jnp.max(jnp.where(cnt > 0, experts, 0))
    e_i = jnp.where(used, e_i, e_last)
    tile_i = jnp.where(used, first_tile[e_i] + item - (item_end - n_e)[e_i], (TOP_K * idx.shape[1]) // tm - 1)
    lo = jnp.where(used, jnp.maximum(start[e_i], tile_i * tm) - tile_i * tm, 0)
    hi = jnp.where(used, jnp.minimum(start[e_i] + cnt[e_i], (tile_i + 1) * tm) - tile_i * tm, 0)
    return pos.astype(I32), tile_i.astype(I32), e_i, lo.astype(I32), hi.astype(I32)


DMA_UNROLL = 8


def _dispatch_kernel(pos_ref, x_ref, xs_hbm, stage, sem, *, tb, n_tok):
    i = pl.program_id(0)
    slot = i % 2
    stage[slot] = x_ref[...]

    def issue(grp, carry):
        r0 = grp * DMA_UNROLL
        dst = [[pos_ref[s * n_tok + i * tb + r0 + u] for s in range(TOP_K)] for u in range(DMA_UNROLL)]
        for u in range(DMA_UNROLL):
            for s in range(TOP_K):
                pltpu.make_async_copy(stage.at[slot, pl.ds(r0 + u, 1)], xs_hbm.at[pl.ds(dst[u][s], 1)],
                                      sem.at[slot]).start()
        return carry

    lax.fori_loop(0, tb // DMA_UNROLL, issue, 0)

    def wait_block(s):
        for _ in range(TOP_K):
            pltpu.make_async_copy(stage.at[s], xs_hbm.at[pl.ds(0, tb)], sem.at[s]).wait()

    @pl.when(i >= 1)
    def _():
        wait_block(1 - slot)

    @pl.when(i == pl.num_programs(0) - 1)
    def _():
        wait_block(slot)


def _dispatch(x2d, pos_flat, tb):
    t, d = x2d.shape
    grid_spec = pltpu.PrefetchScalarGridSpec(
        num_scalar_prefetch=1, grid=(t // tb,),
        in_specs=[pl.BlockSpec((tb, d), lambda i, pos: (i, 0))],
        out_specs=pl.BlockSpec(memory_space=pl.ANY),
        scratch_shapes=[pltpu.VMEM((2, tb, d), F32), pltpu.SemaphoreType.DMA((2,))])
    return pl.pallas_call(
        functools.partial(_dispatch_kernel, tb=tb, n_tok=t),
        out_shape=jax.ShapeDtypeStruct((TOP_K * t, d), F32),
        grid_spec=grid_spec,
        compiler_params=_cparams("arbitrary"),
        name="moe_dispatch",
    )(pos_flat, x2d)


def _experts_kernel(tile_ref, exp_ref, lo_ref, hi_ref, x_ref, w1_ref, w3_ref, w2_ref, o_ref, w1b, w3b, w2b):
    i = pl.program_id(0)
    prev = jnp.maximum(i - 1, 0)

    @pl.when(jnp.logical_or(i == 0, exp_ref[i] != exp_ref[prev]))
    def _():
        w1b[...] = w1_ref[...].astype(BF16)
        w3b[...] = w3_ref[...].astype(BF16)
        w2b[...] = w2_ref[...].astype(BF16)

    lo = lo_ref[i]
    hi = hi_ref[i]
    first = jnp.logical_or(i == 0, tile_ref[i] != tile_ref[prev])

    @pl.when(hi > lo)
    def _():
        x = x_ref[...].astype(BF16)
        row = lax.broadcasted_iota(I32, (x.shape[0], 1), 0)
        mine = jnp.where(jnp.logical_and(row >= lo, row < hi), 1.0, 0.0)
        hid = _silu(_dot(x, w1b[...])) * _dot(x, w3b[...]) * mine
        y = _dot(hid.astype(BF16), w2b[...])

        @pl.when(first)
        def _():
            o_ref[...] = y

        @pl.when(jnp.logical_not(first))
        def _():
            o_ref[...] += y


def _experts(xs, tile_i, exp_i, lo, hi, w1, w3, w2, tm):
    n, d = xs.shape
    f = w1.shape[-1]
    by_tile = lambda i, tile, exp, lo, hi: (tile[i], 0)
    by_exp = lambda i, tile, exp, lo, hi: (exp[i], 0, 0)
    grid_spec = pltpu.PrefetchScalarGridSpec(
        num_scalar_prefetch=4, grid=(tile_i.shape[0],),
        in_specs=[pl.BlockSpec((tm, d), by_tile),
                  pl.BlockSpec((None, d, f), by_exp), pl.BlockSpec((None, d, f), by_exp),
                  pl.BlockSpec((None, f, d), by_exp)],
        out_specs=pl.BlockSpec((tm, d), by_tile),
        scratch_shapes=[pltpu.VMEM((d, f), BF16), pltpu.VMEM((d, f), BF16), pltpu.VMEM((f, d), BF16)])
    return pl.pallas_call(
        _experts_kernel,
        out_shape=jax.ShapeDtypeStruct((n, d), F32),
        grid_spec=grid_spec,
        compiler_params=_cparams("arbitrary"),
        name="moe_experts",
    )(tile_i, exp_i, lo, hi, xs, w1, w3, w2)


def _combine_ln_kernel(pos_ref, x_ref, w_ref, g_ref, b_ref, ys_hbm, o_ref, gbuf, sem, *, tb, n_tok):
    i = pl.program_id(0)
    slot = i % 2

    def start(blk, s):
        def issue(grp, carry):
            r0 = grp * DMA_UNROLL
            src = [[pos_ref[k * n_tok + blk * tb + r0 + u] for k in range(TOP_K)] for u in range(DMA_UNROLL)]
            for u in range(DMA_UNROLL):
                for k in range(TOP_K):
                    pltpu.make_async_copy(ys_hbm.at[pl.ds(src[u][k], 1)], gbuf.at[s, k, pl.ds(r0 + u, 1)],
                                          sem.at[s]).start()
            return carry
        lax.fori_loop(0, tb // DMA_UNROLL, issue, 0)

    @pl.when(i == 0)
    def _():
        start(0, 0)

    @pl.when(i + 1 < pl.num_programs(0))
    def _():
        start(i + 1, 1 - slot)

    for k in range(TOP_K):
        pltpu.make_async_copy(ys_hbm.at[pl.ds(0, tb)], gbuf.at[slot, k], sem.at[slot]).wait()
    w = w_ref[...]
    y = gbuf[slot, 0] * w[:, 0:1] + gbuf[slot, 1] * w[:, 1:2]
    o_ref[...] = _layer_norm_rows(DN_ALPHA * x_ref[...] + y, g_ref[...], b_ref[...])


def _combine_ln(x2d, ys, pos_flat, wgt_cols, g, b, tb):
    t, d = x2d.shape
    row = lambda i, pos: (i, 0)
    fixed = lambda i, pos: (0, 0)
    grid_spec = pltpu.PrefetchScalarGridSpec(
        num_scalar_prefetch=1, grid=(t // tb,),
        in_specs=[pl.BlockSpec((tb, d), row), pl.BlockSpec((tb, TOP_K), row),
                  pl.BlockSpec((1, d), fixed), pl.BlockSpec((1, d), fixed),
                  pl.BlockSpec(memory_space=pl.ANY)],
        out_specs=pl.BlockSpec((tb, d), row),
        scratch_shapes=[pltpu.VMEM((2, TOP_K, tb, d), F32), pltpu.SemaphoreType.DMA((2,))])
    return pl.pallas_call(
        functools.partial(_combine_ln_kernel, tb=tb, n_tok=t),
        out_shape=jax.ShapeDtypeStruct((t, d), F32),
        grid_spec=grid_spec,
        compiler_params=_cparams("arbitrary"),
        name="moe_combine_ln",
    )(pos_flat, x2d, wgt_cols, g.reshape(1, d), b.reshape(1, d), ys)


def _moe_ln(x2d, w_router, b_router, w1, w3, w2, g, b):
    t = x2d.shape[0]
    tm = min(MOE_TILE, t)
    idx, wgt, rank, counts = _router(x2d, w_router, b_router)
    n_items = TOP_K * t // tm + N_EXPERTS - 1
    pos, tile_i, exp_i, lo, hi = _route_tables(idx, rank, counts, tm, n_items)
    pos_flat = pos.reshape(-1)
    xs = _dispatch(x2d, pos_flat, tm)
    ys = _experts(xs, tile_i, exp_i, lo, hi, w1, w3, w2, tm)
    return _combine_ln(x2d, ys, pos_flat, wgt.T, g, b, tm)


def _gate_rows(col, batch, seq):
    tb = min(TIME_BLOCK, seq)
    return col.reshape(batch, seq, D_HEADS).transpose(0, 2, 1).reshape(batch, D_HEADS, seq // tb, tb // CHUNK, CHUNK)


def _mix_cd(x2d, w_in, rel_table, ckv_g, w_uk, w_uv, conv_w, gate_b, d_norm_g, batch, seq):
    t = x2d.shape[0]
    h = _project(x2d, *_pack_cd(w_in), min(1024, t), PROJ_TILE)
    ckv, ckvt, kdup = _dsa_prep(h, ckv_g, t)
    mask = _indexer(h, kdup, batch, seq)
    oc = _dsa_attention(h, mask, ckv, ckvt, w_uk, w_uv, rel_table, batch, seq)
    tail = h[:, CD_TAIL:CD_TAIL + LANES]
    ig_rows = _gate_rows(tail[:, TAIL_DI:TAIL_DI + D_HEADS], batch, seq)
    fg_rows = _gate_rows(tail[:, TAIL_DF:TAIL_DF + D_HEADS], batch, seq)
    od = _mlstm(h, conv_w, ig_rows, fg_rows, gate_b, d_norm_g, batch, seq,
                q_off=CD_DQ // (D_HEADS * D_DK), k_off=CD_DK // (D_HEADS * D_DK),
                v_off=CD_DV // (D_HEADS * D_DV), g_off=CD_DOG // (D_HEADS * D_DV))
    return oc, od


def kernel(x, w_in_ab, w_out_ab, hgrn_lb_logits, a_norm_g, gla_wa2, gla_ba2, b_norm_g, w_in_cd, w_out_cd,
           ckv_norm_g, w_uk, w_uv, mlstm_conv_w, mlstm_gate_b, d_norm_g, rel_table, w_router, b_router,
           moe_w1, moe_w3, moe_w2, ln_g, ln_b):
    batch, seq, d = x.shape
    x2d = x.reshape(batch * seq, d)
    for layer in range(DEPTH):
        li = layer // 2
        if layer % 2 == 0:
            mix_a, mix_b = _mix_ab(x2d, w_in_ab[li], hgrn_lb_logits, li, a_norm_g[li], gla_wa2[li], gla_ba2[li],
                                   b_norm_g[li], batch, seq)
            w_out = w_out_ab[li]
        else:
            mix_a, mix_b = _mix_cd(x2d, w_in_cd[li], rel_table, ckv_norm_g[li], w_uk[li], w_uv[li],
                                   mlstm_conv_w[li], mlstm_gate_b[li], d_norm_g[li], batch, seq)
            w_out = w_out_cd[li]
        ka = mix_a.shape[1]
        x2d = _outproj_ln(x2d, mix_a, mix_b, w_out[:ka].astype(BF16), w_out[ka:].astype(BF16),
                          ln_g[layer, 0], ln_b[layer, 0])
        x2d = _moe_ln(x2d, w_router, b_router, moe_w1[layer], moe_w3[layer], moe_w2[layer],
                      ln_g[layer, 1], ln_b[layer, 1])
    return x2d.reshape(batch, seq, d)
```

```python
import functools
import math

import numpy as np
import jax
import jax.numpy as jnp
from jax import lax
from jax.experimental import pallas as pl
from jax.experimental.pallas import tpu as pltpu

F32 = jnp.float32
BF16 = jnp.bfloat16
I32 = jnp.int32

D_MODEL = 2048
DEPTH = 2
A_HEADS, A_DK, A_DV = 8, 128, 128
B_HEADS, B_DK, B_DV = 4, 128, 256
B_GATE_RANK, B_GATE_TAU = 16, 16.0
C_HEADS, C_DH, C_DLAT = 8, 128, 256
IDX_HEADS, IDX_DIM = 16, 64
TOPK_MAX = 256
D_HEADS, D_DK, D_DV = 4, 128, 256
CONV_K = 4
REL_BUCKETS, REL_MAX_DIST = 32, 128
N_EXPERTS, N_GROUPS, TOP_K, D_EXPERT = 16, 4, 2, 512
EXPERTS_PER_GROUP = N_EXPERTS // N_GROUPS
DN_ALPHA = (2 * DEPTH) ** 0.25
EPS = 1e-5

LANES = 128
SUBLANES = 8
VMEM_LIMIT = 56 * 1024 * 1024

CHUNK = 128
N_LEVELS = 7
TIME_BLOCK = 256
QB = 128
NEG = -1e30
INT_MIN = -2 ** 31
INT_MAX = 2 ** 31 - 1


def _cparams(*sem):
    return pltpu.CompilerParams(dimension_semantics=sem, vmem_limit_bytes=VMEM_LIMIT)


def _dot(a, b):
    return jnp.dot(a, b, preferred_element_type=F32)


def _dot_nt(a, b):
    return lax.dot_general(a, b, (((1,), (1,)), ((), ())), preferred_element_type=F32)


def _dot_tn(a, b):
    return lax.dot_general(a, b, (((0,), (0,)), ((), ())), preferred_element_type=F32)


def _split3(a):
    hi = a.astype(BF16)
    r1 = a - hi.astype(F32)
    mid = r1.astype(BF16)
    lo = (r1 - mid.astype(F32)).astype(BF16)
    return hi, mid, lo


def _dot01(m01, a):
    hi, mid, lo = _split3(a)
    return _dot(m01, hi) + _dot(m01, mid) + _dot(m01, lo)


def _sigmoid(x):
    return 1.0 / (1.0 + jnp.exp(-x))


def _silu(x):
    return x * _sigmoid(x)


def _log_sigmoid(x):
    return jnp.minimum(x, 0.0) - jnp.log(1.0 + jnp.exp(-jnp.abs(x)))


def _proj_kernel(x_ref, w_ref, wt_ref, o_ref, xb_ref):
    j = pl.program_id(1)

    @pl.when(j == 0)
    def _():
        xb_ref[...] = x_ref[...].astype(BF16)

    @pl.when(j < pl.num_programs(1) - 1)
    def _():
        o_ref[...] = _dot(xb_ref[...], w_ref[...])

    @pl.when(j == pl.num_programs(1) - 1)
    def _():
        o_ref[...] = _dot(xb_ref[...], wt_ref[...])


def _cast_kernel(w_ref, o_ref):
    o_ref[...] = w_ref[...].astype(o_ref.dtype)


def _cast_columns(w, n_cols, tn):
    k = w.shape[0]
    return pl.pallas_call(
        _cast_kernel,
        out_shape=jax.ShapeDtypeStruct((k, n_cols), BF16),
        grid=(n_cols // tn,),
        in_specs=[pl.BlockSpec((k, tn), lambda j: (0, j))],
        out_specs=pl.BlockSpec((k, tn), lambda j: (0, j)),
        compiler_params=_cparams("parallel"),
        name="weight_cast",
    )(w)


def _project(x, w, w_tail, tm, tn):
    m, k = x.shape
    n_main = w.shape[1] // tn
    return pl.pallas_call(
        _proj_kernel,
        out_shape=jax.ShapeDtypeStruct((m, (n_main + 1) * tn), F32),
        grid=(m // tm, n_main + 1),
        in_specs=[pl.BlockSpec((tm, k), lambda i, j: (i, 0)),
                  pl.BlockSpec((k, tn), lambda i, j: (0, jnp.minimum(j, n_main - 1))),
                  pl.BlockSpec((k, tn), lambda i, j: (0, 0))],
        out_specs=pl.BlockSpec((tm, tn), lambda i, j: (i, j)),
        scratch_shapes=[pltpu.VMEM((tm, k), BF16)],
        compiler_params=_cparams("parallel", "arbitrary"),
        name="in_proj",
    )(x, w, w_tail)


def _layer_norm_rows(z, g, b):
    mu = jnp.mean(z, axis=-1, keepdims=True)
    zc = z - mu
    var = jnp.mean(zc * zc, axis=-1, keepdims=True)
    return zc * lax.rsqrt(var + EPS) * g + b


def _outproj_ln_kernel(x_ref, ma_ref, mb_ref, wa_ref, wb_ref, g_ref, b_ref, o_ref):
    mixed = _dot(ma_ref[...].astype(BF16), wa_ref[...]) + _dot(mb_ref[...].astype(BF16), wb_ref[...])
    o_ref[...] = _layer_norm_rows(DN_ALPHA * x_ref[...] + mixed, g_ref[...], b_ref[...])


def _outproj_ln(x, mix_a, mix_b, w_a, w_b, g, b, tm=512):
    m, d = x.shape
    ka, kb = mix_a.shape[1], mix_b.shape[1]
    row = lambda i: (i, 0)
    fixed = lambda i: (0, 0)
    return pl.pallas_call(
        _outproj_ln_kernel,
        out_shape=jax.ShapeDtypeStruct((m, d), F32),
        grid=(m // tm,),
        in_specs=[pl.BlockSpec((tm, d), row), pl.BlockSpec((tm, ka), row), pl.BlockSpec((tm, kb), row),
                  pl.BlockSpec((ka, d), fixed), pl.BlockSpec((kb, d), fixed),
                  pl.BlockSpec((1, d), fixed), pl.BlockSpec((1, d), fixed)],
        out_specs=pl.BlockSpec((tm, d), row),
        compiler_params=_cparams("parallel"),
        name="out_proj_ln",
    )(x, mix_a, mix_b, w_a, w_b, g.reshape(1, d), b.reshape(1, d))


def _chunk_constants():
    t = np.arange(CHUNK)
    tri = (t[:, None] >= t[None, :]).astype(np.float32)
    pair, odd = [], []
    for lev in range(1, N_LEVELS + 1):
        c = CHUNK >> lev
        pair.append((t[:, None] // (2 * c) == t[None, :] // (2 * c)).astype(np.float32))
        odd.append(np.broadcast_to((((t // c) & 1) == 1).astype(np.float32)[:, None], (CHUNK, LANES)))
    pair.append(np.eye(CHUNK, dtype=np.float32))
    return jnp.asarray(tri, BF16), jnp.asarray(np.stack(pair), F32), jnp.asarray(np.stack(odd), F32)


def _level_log_decay(la, bcum, lev):
    c = CHUNK >> lev
    if 2 * c >= SUBLANES:
        mids = [jnp.broadcast_to(bcum[g * 2 * c + c - 1:g * 2 * c + c], (2 * c, bcum.shape[1]))
                for g in range(CHUNK // (2 * c))]
        return -jnp.abs(bcum - (mids[0] if len(mids) == 1 else jnp.concatenate(mids, axis=0)))
    r = lax.broadcasted_iota(I32, la.shape, 0) & (2 * c - 1)
    if c == 2:
        nxt = pltpu.roll(la, CHUNK - 1, axis=0)
        prv = pltpu.roll(la, 1, axis=0)
        return jnp.where(r == 0, nxt, jnp.where(r == 1, 0.0, jnp.where(r == 2, la, la + prv)))
    return jnp.where(r == 1, la, 0.0)


def _glr_chunk(q, k, v, la, st_ref, cum_ref, pair_ref, odd_ref):
    bcum = _dot01(cum_ref[...], la)
    attn = pair_ref[N_LEVELS] * _dot_nt(q.astype(BF16), k.astype(BF16))
    for lev in range(1, N_LEVELS + 1):
        e = jnp.exp(_level_log_decay(la, bcum, lev))
        eq = e * odd_ref[lev - 1]
        ql = (q * eq).astype(BF16)
        kl = (k * (e - eq)).astype(BF16)
        attn = attn + pair_ref[lev - 1] * _dot_nt(ql, kl)
    st = st_ref[...]
    o = _dot_nt((q * jnp.exp(bcum)).astype(BF16), st.astype(BF16)) + _dot(attn.astype(BF16), v.astype(BF16))
    b_last = bcum[CHUNK - 1:CHUNK]
    kdec = (k * jnp.exp(b_last - bcum)).astype(BF16)
    st_ref[...] = st * jnp.exp(b_last) + _dot_tn(v.astype(BF16), kdec)
    return o


def _rms_gate(o, g, gate):
    ms = jnp.mean(o * o, axis=-1, keepdims=True)
    return o * lax.rsqrt(ms + EPS) * g * gate


HEAD_GROUP = 4


def _hgrn2_kernel(q_ref, f_ref, i_ref, g_ref, lb_ref, ng_ref, cum_ref, pair_ref, odd_ref, o_ref, st_ref):
    @pl.when(pl.program_id(2) == 0)
    def _():
        st_ref[...] = jnp.zeros_like(st_ref)

    def body(c, carry):
        rows = pl.ds(pl.multiple_of(c * CHUNK, CHUNK), CHUNK)
        for hh in range(HEAD_GROUP):
            ck = slice(A_DK * hh, A_DK * (hh + 1))
            cv = slice(A_DV * hh, A_DV * (hh + 1))
            lb = lb_ref[:, ck]
            f = lb + (1.0 - lb) * _sigmoid(f_ref[rows, ck])
            o = _glr_chunk(_silu(q_ref[rows, ck]), 1.0 - f, i_ref[rows, cv], jnp.log(f),
                           st_ref.at[hh], cum_ref, pair_ref, odd_ref)
            o_ref[rows, cv] = _rms_gate(o, ng_ref[...], _silu(g_ref[rows, cv]))
        return carry

    lax.fori_loop(0, q_ref.shape[0] // CHUNK, body, 0)


def _gla_kernel(q_ref, k_ref, v_ref, g_ref, r_ref, wa_ref, ba_ref, ng_ref, cum_ref, pair_ref, odd_ref,
                o_ref, st_ref):
    @pl.when(pl.program_id(2) == 0)
    def _():
        st_ref[...] = jnp.zeros_like(st_ref)

    def body(c, carry):
        rows = pl.ds(pl.multiple_of(c * CHUNK, CHUNK), CHUNK)
        pre = _dot(r_ref[rows, :].astype(BF16), wa_ref[...]) + ba_ref[...]
        la = _log_sigmoid(pre) * (1.0 / B_GATE_TAU)
        for hh in range(HEAD_GROUP):
            ck = slice(B_DK * hh, B_DK * (hh + 1))
            cv = slice(B_DV * hh, B_DV * (hh + 1))
            o = _glr_chunk(q_ref[rows, ck] * (B_DK ** -0.5), k_ref[rows, ck], v_ref[rows, cv], la[:, ck],
                           st_ref.at[hh], cum_ref, pair_ref, odd_ref)
            o_ref[rows, cv] = _rms_gate(o, ng_ref[...], _silu(g_ref[rows, cv]))
        return carry

    lax.fori_loop(0, q_ref.shape[0] // CHUNK, body, 0)


def _const_spec(arr):
    nd = arr.ndim
    return pl.BlockSpec(arr.shape, lambda *_: (0,) * nd)


def _hgrn2(h, lb, norm_g, batch, seq):
    tb = min(TIME_BLOCK, seq)
    nt = seq // tb
    ng = A_HEADS // HEAD_GROUP
    wk, wv = HEAD_GROUP * A_DK, HEAD_GROUP * A_DV
    consts = _chunk_constants()
    col = lambda seg, w: pl.BlockSpec((tb, w), lambda b, g, t, seg=seg: (b * nt + t, seg * ng + g))
    return pl.pallas_call(
        _hgrn2_kernel,
        out_shape=jax.ShapeDtypeStruct((batch * seq, A_HEADS * A_DV), F32),
        grid=(batch, ng, nt),
        in_specs=[col(0, wk), col(1, wk), col(2, wv), col(3, wv),
                  pl.BlockSpec((1, wk), lambda b, g, t: (0, g)),
                  pl.BlockSpec((1, A_DV), lambda b, g, t: (0, 0))] + [_const_spec(c) for c in consts],
        out_specs=pl.BlockSpec((tb, wv), lambda b, g, t: (b * nt + t, g)),
        scratch_shapes=[pltpu.VMEM((HEAD_GROUP, A_DV, A_DK), F32)],
        compiler_params=_cparams("parallel", "parallel", "arbitrary"),
        name="hgrn2",
    )(h, h, h, h, lb.reshape(1, -1), norm_g.reshape(1, -1), *consts)


def _gla(h, wa2p, ba2, norm_g, batch, seq, q_off, k_off, v_off, g_off, r_off):
    tb = min(TIME_BLOCK, seq)
    nt = seq // tb
    wk, wv = B_HEADS * B_DK, B_HEADS * B_DV
    consts = _chunk_constants()
    col = lambda off, w: pl.BlockSpec((tb, w), lambda b, g, t, off=off: (b * nt + t, off))
    fixed = lambda b, g, t: (0, 0)
    return pl.pallas_call(
        _gla_kernel,
        out_shape=jax.ShapeDtypeStruct((batch * seq, wv), F32),
        grid=(batch, 1, nt),
        in_specs=[col(q_off, wk), col(k_off, wk), col(v_off, wv), col(g_off, wv), col(r_off, LANES),
                  pl.BlockSpec((LANES, wk), fixed), pl.BlockSpec((1, wk), fixed),
                  pl.BlockSpec((1, B_DV), fixed)] + [_const_spec(c) for c in consts],
        out_specs=pl.BlockSpec((tb, wv), lambda b, g, t: (b * nt + t, 0)),
        scratch_shapes=[pltpu.VMEM((B_HEADS, B_DV, B_DK), F32)],
        compiler_params=_cparams("parallel", "parallel", "arbitrary"),
        name="gla",
    )(h, h, h, h, h, wa2p, ba2.reshape(1, -1), norm_g.reshape(1, -1), *consts)


PROJ_TILE = 512
AB_MAIN = 4 * A_HEADS * A_DK + 2 * B_HEADS * B_DK + 2 * B_HEADS * B_DV
AB_PAD = AB_MAIN + PROJ_TILE


def _mix_ab(x2d, w_in, lb_logits, li, a_norm_g, wa2, ba2, b_norm_g, batch, seq):
    d = x2d.shape[1]
    w_tail = jnp.pad(w_in[:, AB_MAIN:], ((0, 0), (0, AB_PAD - w_in.shape[1]))).astype(BF16)
    h = _project(x2d, _cast_columns(w_in, AB_MAIN, PROJ_TILE), w_tail, min(1024, x2d.shape[0]), PROJ_TILE)
    lb = jnp.cumsum(jax.nn.softmax(lb_logits.astype(F32), axis=0), axis=0)[li]
    oa = _hgrn2(h, lb, a_norm_g, batch, seq)
    wa2p = jnp.concatenate([wa2, jnp.zeros((LANES - B_GATE_RANK, wa2.shape[1]), F32)], axis=0).astype(BF16)
    a_cols = 4 * A_HEADS * A_DK
    wk, wv = B_HEADS * B_DK, B_HEADS * B_DV
    ob = _gla(h, wa2p, ba2, b_norm_g, batch, seq, q_off=a_cols // wk, k_off=a_cols // wk + 1,
              v_off=(a_cols + 2 * wk) // wv, g_off=(a_cols + 2 * wk) // wv + 1, r_off=AB_MAIN // LANES)
    return oa, ob


CONV_HALO = 8


def _causal_conv(x_ref, w_ref, buf_ref, tail_ref):
    tb = x_ref.shape[0]
    x = x_ref[...]
    buf_ref[0:CONV_HALO, :] = tail_ref[...]
    buf_ref[CONV_HALO:CONV_HALO + tb, :] = x
    tail_ref[...] = x[tb - CONV_HALO:tb]
    y = w_ref[CONV_K - 1:CONV_K, :] * x
    for j in range(CONV_K - 1):
        y = y + w_ref[j:j + 1, :] * buf_ref[pl.ds(CONV_HALO - (CONV_K - 1) + j, tb), :]
    return y


def _row_to_col(row, eye):
    return jnp.sum(jnp.where(eye, row, 0.0), axis=1, keepdims=True)


def _mlstm_kernel(q_ref, k_ref, v_ref, og_ref, wq_ref, wk_ref, ig_ref, fg_ref, gb_ref, ng_ref, tri_ref,
                  o_ref, ct_ref, n_ref, m_ref, qt_ref, kt_ref, qs_ref, ks_ref, buf_ref):
    @pl.when(pl.program_id(2) == 0)
    def _():
        ct_ref[...] = jnp.zeros_like(ct_ref)
        n_ref[...] = jnp.zeros_like(n_ref)
        m_ref[...] = jnp.zeros_like(m_ref)
        qt_ref[...] = jnp.zeros_like(qt_ref)
        kt_ref[...] = jnp.zeros_like(kt_ref)

    qs_ref[...] = _silu(_causal_conv(q_ref, wq_ref, buf_ref, qt_ref))
    ks_ref[...] = _silu(_causal_conv(k_ref, wk_ref, buf_ref, kt_ref)) * (D_DK ** -0.5)

    r_i = lax.broadcasted_iota(I32, (CHUNK, CHUNK), 0)
    c_i = lax.broadcasted_iota(I32, (CHUNK, CHUNK), 1)
    eye = r_i == c_i
    causal = r_i >= c_i

    def body(c, carry):
        rows = pl.ds(pl.multiple_of(c * CHUNK, CHUNK), CHUNK)
        tri = tri_ref[...]
        for hh in range(D_HEADS):
            ck = slice(D_DK * hh, D_DK * (hh + 1))
            cv = slice(D_DV * hh, D_DV * (hh + 1))
            q = qs_ref[rows, ck]
            k = ks_ref[rows, ck]
            v = v_ref[rows, cv].astype(BF16)
            qb = q.astype(BF16)
            ig_row = ig_ref[hh, pl.ds(c, 1), :] + gb_ref[0, hh]
            lf_row = _log_sigmoid(fg_ref[hh, pl.ds(c, 1), :] + gb_ref[1, hh])
            hi, mid, lo = _split3(lf_row)
            bcum_row = _dot(hi, tri) + _dot(mid, tri) + _dot(lo, tri)
            bcum_col = _row_to_col(bcum_row, eye)
            ig_col = _row_to_col(ig_row, eye)
            m_prev = m_ref[hh, :, 0:1]
            log_w = jnp.where(causal, bcum_col - bcum_row + ig_row, NEG)
            log_inter = bcum_col + m_prev
            m_t = jnp.maximum(jnp.max(log_w, axis=1, keepdims=True), log_inter)
            s = _dot_nt(qb, k.astype(BF16)) * jnp.exp(log_w - m_t)
            w_inter = jnp.exp(log_inter - m_t)
            num = _dot(s.astype(BF16), v) + w_inter * _dot_nt(qb, ct_ref[hh].astype(BF16))
            qn = jnp.sum(s, axis=1, keepdims=True) + w_inter * jnp.sum(q * n_ref[hh], axis=1, keepdims=True)
            h = num / jnp.maximum(jnp.abs(qn), jnp.exp(-m_t))
            o_ref[rows, cv] = _rms_gate(h, ng_ref[...], _sigmoid(og_ref[rows, cv]))
            b_last = bcum_row[:, CHUNK - 1:CHUNK]
            log_u = b_last - bcum_col + ig_col
            m_new = jnp.maximum(b_last + m_prev, jnp.max(log_u, axis=0, keepdims=True))
            decay = jnp.exp(b_last + m_prev - m_new)
            ku = k * jnp.exp(log_u - m_new)
            ct_ref[hh] = decay * ct_ref[hh] + _dot_tn(v, ku.astype(BF16))
            n_ref[hh] = decay * n_ref[hh] + jnp.sum(ku, axis=0, keepdims=True)
            m_ref[hh] = jnp.broadcast_to(m_new, (1, LANES))
        return carry

    lax.fori_loop(0, q_ref.shape[0] // CHUNK, body, 0)


def _mlstm(h, conv_w, ig_rows, fg_rows, gate_b, norm_g, batch, seq, q_off, k_off, v_off, g_off):
    tb = min(TIME_BLOCK, seq)
    nt = seq // tb
    nc = tb // CHUNK
    wk, wv = D_HEADS * D_DK, D_HEADS * D_DV
    t = np.arange(CHUNK)
    tri = jnp.asarray(t[:, None] <= t[None, :], BF16)
    gb = jnp.broadcast_to(gate_b.reshape(2, D_HEADS, 1, 1), (2, D_HEADS, 1, CHUNK)).astype(F32)
    col = lambda off, w: pl.BlockSpec((tb, w), lambda b, g, t, off=off: (b * nt + t, off))
    gate = pl.BlockSpec((None, D_HEADS, None, nc, CHUNK), lambda b, g, t: (b, 0, t, 0, 0))
    fixed = lambda b, g, t: (0, 0)
    return pl.pallas_call(
        _mlstm_kernel,
        out_shape=jax.ShapeDtypeStruct((batch * seq, wv), F32),
        grid=(batch, 1, nt),
        in_specs=[col(q_off, wk), col(k_off, wk), col(v_off, wv), col(g_off, wv),
                  pl.BlockSpec((CONV_K, wk), lambda b, g, t: (0, 0)),
                  pl.BlockSpec((CONV_K, wk), lambda b, g, t: (0, 1)),
                  gate, gate,
                  pl.BlockSpec((2, D_HEADS, 1, CHUNK), lambda b, g, t: (0, 0, 0, 0)),
                  pl.BlockSpec((1, D_DV), fixed), pl.BlockSpec((CHUNK, CHUNK), fixed)],
        out_specs=pl.BlockSpec((tb, wv), lambda b, g, t: (b * nt + t, 0)),
        scratch_shapes=[pltpu.VMEM((D_HEADS, D_DV, D_DK), F32), pltpu.VMEM((D_HEADS, 1, D_DK), F32),
                        pltpu.VMEM((D_HEADS, 1, LANES), F32),
                        pltpu.VMEM((CONV_HALO, wk), F32), pltpu.VMEM((CONV_HALO, wk), F32),
                        pltpu.VMEM((tb, wk), F32), pltpu.VMEM((tb, wk), F32),
                        pltpu.VMEM((tb + CONV_HALO, wk), F32)],
        compiler_params=_cparams("parallel", "parallel", "arbitrary"),
        name="mlstm",
    )(h, h, h, h, conv_w, conv_w, ig_rows, fg_rows, gb, norm_g.reshape(1, -1), tri)


CD_CQ, CD_IQ, CD_DQ, CD_DK, CD_DV, CD_DOG, CD_CKV, CD_TAIL = 0, 1024, 2048, 2560, 3072, 4096, 5120, 5376
CD_PAD = 5632
TAIL_IK, TAIL_IW, TAIL_DI, TAIL_DF = 0, 64, 80, 84


def _pack_cd(w_in):
    cq, ckv, iq, ik, iw, dq, dk, dv, di, df, dog = jnp.split(
        w_in, [int(i) for i in np.cumsum(
            (C_HEADS * C_DH, C_DLAT, IDX_HEADS * IDX_DIM, IDX_DIM, IDX_HEADS, D_HEADS * D_DK, D_HEADS * D_DK,
             D_HEADS * D_DV, D_HEADS, D_HEADS))], axis=1)
    used = CD_TAIL + IDX_DIM + IDX_HEADS + 2 * D_HEADS
    pad = jnp.zeros((w_in.shape[0], CD_PAD - used), F32)
    main = jnp.concatenate([cq, iq, dq, dk, dv, dog], axis=1).astype(BF16)
    tail = jnp.concatenate([ckv, ik, iw, di, df, pad], axis=1).astype(BF16)
    return main, tail


KB = 2 * QB
SUM_ROWS = 16


def _dsa_prep_kernel(ckv_ref, tail_ref, g_ref, ckv_o, ckvt_o, kdup_o):
    c = ckv_ref[...]
    cn = c * lax.rsqrt(jnp.mean(c * c, axis=-1, keepdims=True) + EPS) * g_ref[...]
    ckv_o[...] = cn.astype(BF16)
    ckvt_o[0:C_DLAT, :] = cn.T.astype(BF16)
    ckvt_o[C_DLAT:C_DLAT + SUM_ROWS, :] = jnp.ones((SUM_ROWS, KB), BF16)
    tail = tail_ref[...]
    lane = lax.broadcasted_iota(I32, tail.shape, 1)
    kdup_o[...] = jnp.where(lane < IDX_DIM, tail, pltpu.roll(tail, IDX_DIM, axis=1)).astype(BF16)


def _dsa_prep(h, ckv_g, n_rows):
    nb = n_rows // KB
    return pl.pallas_call(
        _dsa_prep_kernel,
        out_shape=(jax.ShapeDtypeStruct((nb, KB, C_DLAT), BF16), jax.ShapeDtypeStruct((nb, C_DLAT + SUM_ROWS, KB), BF16),
                   jax.ShapeDtypeStruct((nb, KB, LANES), BF16)),
        grid=(nb,),
        in_specs=[pl.BlockSpec((KB, C_DLAT), lambda i: (i, CD_CKV // C_DLAT)),
                  pl.BlockSpec((KB, LANES), lambda i: (i, CD_TAIL // LANES)),
                  pl.BlockSpec((1, C_DLAT), lambda i: (0, 0))],
        out_specs=(pl.BlockSpec((None, KB, C_DLAT), lambda i: (i, 0, 0)),
                   pl.BlockSpec((None, C_DLAT + SUM_ROWS, KB), lambda i: (i, 0, 0)),
                   pl.BlockSpec((None, KB, LANES), lambda i: (i, 0, 0))),
        compiler_params=_cparams("parallel"),
        name="dsa_prep",
    )(h, h, ckv_g.reshape(1, -1))


def _sortable_key(x):
    b = lax.bitcast_convert_type(x, I32)
    key = b ^ ((b >> 31) & 0x7FFFFFFF)
    return jnp.where(key == -1, 0, key)


BISECT_STEPS = 4


def _indexer_kernel(iq_ref, tail_ref, kdup_ref, tri_ref, mask_ref, key_ref, wst_ref, *, k_sel):
    j = pl.program_id(1)
    nk = key_ref.shape[0]
    n_live = (j * QB + QB + KB - 1) // KB
    w_t = tail_ref[...].T
    lane = lax.broadcasted_iota(I32, (QB, LANES), 1)
    for p in range(IDX_HEADS // 2):
        pair = iq_ref[:, LANES * p:LANES * (p + 1)]
        wst_ref[p, 0:QB, :] = jnp.where(lane < IDX_DIM, pair, 0.0).astype(BF16)
        wst_ref[p, QB:2 * QB, :] = jnp.where(lane >= IDX_DIM, pair, 0.0).astype(BF16)

    s_loc = lax.broadcasted_iota(I32, (KB, QB), 0)
    t_abs = j * QB + lax.broadcasted_iota(I32, (1, QB), 1)

    def score_chunk(kc, carry):
        kd = kdup_ref[kc]
        acc = jnp.zeros((KB, QB), F32)
        for p in range(IDX_HEADS // 2):
            r0 = TAIL_IW + 2 * p
            both = _dot_nt(kd, wst_ref[p])
            acc = acc + jnp.maximum(both[:, 0:QB], 0.0) * w_t[r0:r0 + 1, :]
            acc = acc + jnp.maximum(both[:, QB:2 * QB], 0.0) * w_t[r0 + 1:r0 + 2, :]
        key = jnp.where(kc * KB + s_loc > t_abs, INT_MIN, _sortable_key(acc))
        key_ref[kc] = key
        key = key.reshape(KB // SUBLANES, SUBLANES, QB)
        k_max, k_min = carry
        k_max = jnp.maximum(k_max, jnp.max(key, axis=0))
        k_min = jnp.minimum(k_min, jnp.min(jnp.where(key == INT_MIN, INT_MAX, key), axis=0))
        return k_max, k_min

    k_max, k_min = lax.fori_loop(0, n_live, score_chunk, (jnp.full((SUBLANES, QB), INT_MIN, I32),
                                                           jnp.full((SUBLANES, QB), INT_MAX, I32)))
    for shift in (4, 2, 1):
        k_max = jnp.maximum(k_max, pltpu.roll(k_max, shift, axis=0))
        k_min = jnp.minimum(k_min, pltpu.roll(k_min, shift, axis=0))
    k_row = jnp.minimum(k_sel, j * QB + lax.broadcasted_iota(I32, (SUBLANES, QB), 1) + 1)

    def count(pred):
        def add(kc, acc):
            hit = jnp.where(pred(key_ref[kc].reshape(KB // SUBLANES, SUBLANES, QB)), 1, 0)
            return acc + jnp.sum(hit, axis=0)
        acc = lax.fori_loop(0, n_live, add, jnp.zeros((SUBLANES, QB), I32))
        for shift in (4, 2, 1):
            acc = acc + pltpu.roll(acc, shift, axis=0)
        return acc

    def unfinished(state):
        lo, hi = state
        return jnp.max(jnp.where(lo < hi, 1, 0)) > 0

    def halve(state):
        lo, hi = state
        mid = (lo >> 1) + (hi >> 1) + (((lo & 1) + (hi & 1) + 1) >> 1)
        cnt = count(lambda k: k >= mid)
        enough = cnt >= k_row
        lo_n = jnp.where(enough, mid, lo)
        hi_n = jnp.where(cnt == k_row, mid, jnp.where(enough, hi, mid - 1))
        return lo_n, hi_n

    def halve_steps(state):
        for _ in range(BISECT_STEPS):
            state = halve(state)
        return state

    tau8, _ = lax.while_loop(unfinished, halve_steps, (k_min, k_max))
    n_ge = count(lambda k: k >= tau8)
    has_tie = jnp.max(jnp.where(n_ge != k_row, 1, 0)) > 0
    tau = tau8[0:1, :]

    @pl.when(jnp.logical_not(has_tie))
    def _():
        def put(kc, carry):
            mask_ref[kc] = jnp.where(key_ref[kc] >= tau, 1.0, 0.0).astype(BF16)
            return carry
        lax.fori_loop(0, n_live, put, 0)

    @pl.when(has_tie)
    def _():
        need = (k_row - count(lambda k: k > tau8)).astype(F32)[0:1, :]

        def put(kc, seen):
            k = key_ref[kc]
            eq = jnp.where(k == tau, 1.0, 0.0)
            before = _dot(tri_ref[...], eq.astype(BF16)) + seen
            take = jnp.where(k > tau, 1.0, jnp.where(before < need, eq, 0.0))
            mask_ref[kc] = take.astype(BF16)
            return seen + jnp.sum(eq, axis=0, keepdims=True)
        lax.fori_loop(0, n_live, put, jnp.zeros((1, QB), F32))

    def clear(kc, carry):
        mask_ref[kc] = jnp.zeros((KB, QB), BF16)
        return carry
    lax.fori_loop(n_live, nk, clear, 0)


def _indexer(h, kdup, batch, seq):
    nq = seq // QB
    nk = seq // KB
    k_sel = min(TOPK_MAX, seq // 4)
    r = np.arange(KB)
    tri = jnp.asarray(r[None, :] < r[:, None], BF16)
    return pl.pallas_call(
        functools.partial(_indexer_kernel, k_sel=k_sel),
        out_shape=jax.ShapeDtypeStruct((batch, nk, KB, seq), BF16),
        grid=(batch, nq),
        in_specs=[pl.BlockSpec((QB, IDX_HEADS * IDX_DIM), lambda b, j: (b * nq + j, CD_IQ // (IDX_HEADS * IDX_DIM))),
                  pl.BlockSpec((QB, LANES), lambda b, j: (b * nq + j, CD_TAIL // LANES)),
                  pl.BlockSpec((None, nk, KB, LANES), lambda b, j: (b, 0, 0, 0)),
                  pl.BlockSpec((KB, KB), lambda b, j: (0, 0))],
        out_specs=pl.BlockSpec((None, nk, KB, QB), lambda b, j: (b, 0, 0, j)),
        scratch_shapes=[pltpu.VMEM((nk, KB, QB), I32), pltpu.VMEM((IDX_HEADS // 2, 2 * QB, LANES), BF16)],
        compiler_params=_cparams("parallel", "parallel"),
        name="dsa_indexer",
    )(h, h, kdup.reshape(batch, nk, KB, LANES), tri)


def _dsa_attn_kernel(cq_ref, mask_ref, ckv_ref, ckvt_ref, wuk_ref, wuvt_ref, bias_ref, o_ref,
                     qt_ref, m_ref, acc_ref, ot_ref):
    j = pl.program_id(1)
    for hh in range(C_HEADS):
        q_h = cq_ref[:, C_DH * hh:C_DH * (hh + 1)].astype(BF16)
        half = slice(QB * (hh % 2), QB * (hh % 2 + 1))
        qt_ref[hh // 2, :, half] = (_dot_nt(wuk_ref[hh], q_h) * (C_DH ** -0.5)).astype(BF16)
    m_ref[...] = jnp.full(m_ref.shape, NEG, F32)
    acc_ref[...] = jnp.zeros_like(acc_ref)

    n_live = (j * QB + QB + KB - 1) // KB
    odd = (j % 2) == 1

    def body(kc, carry):
        ck = ckv_ref[kc]
        ckt = ckvt_ref[kc]
        drop = jnp.where(mask_ref[kc].astype(F32) > 0.5, 0.0, NEG)
        drop = jnp.concatenate([drop, drop], axis=1)
        back = n_live - 1 - kc
        which = jnp.where(back == 0, jnp.where(odd, 0, 1), jnp.where(jnp.logical_and(back == 1, jnp.logical_not(odd)), 2, 3))
        for hp in range(C_HEADS // 2):
            logit = _dot(ck, qt_ref[hp]) + (bias_ref[hp, which] + drop)
            m_old = m_ref[hp]
            m_new = jnp.maximum(m_old, jnp.max(logit, axis=0, keepdims=True))
            alpha = jnp.exp(m_old - m_new)
            p = jnp.exp(logit - m_new)
            acc_ref[hp] = alpha * acc_ref[hp] + _dot(ckt, p.astype(BF16))
            m_ref[hp] = m_new
        return carry

    lax.fori_loop(0, n_live, body, 0)
    for hh in range(C_HEADS):
        half = slice(QB * (hh % 2), QB * (hh % 2 + 1))
        total = acc_ref[hh // 2, C_DLAT:C_DLAT + 1, half]
        o_lat = (acc_ref[hh // 2, 0:C_DLAT, half] * (1.0 / total)).astype(BF16)
        ot_ref[C_DH * hh:C_DH * (hh + 1), :] = _dot(wuvt_ref[hh], o_lat)
    o_ref[...] = ot_ref[...].T


def _rel_bias_tiles(rel_table):
    s = np.arange(QB)[:, None]
    t = np.arange(QB)[None, :]
    diag, prev, far = np.maximum(t - s, 0), QB + t - s, np.full((QB, QB), 2 * QB)
    kinds = [(prev, diag), (diag, far), (far, prev), (far, far)]
    n = jnp.asarray(np.stack([np.concatenate(k, axis=0) for k in kinds]).astype(np.int32))
    max_exact = REL_BUCKETS // 2
    large = max_exact + (jnp.log(jnp.maximum(n, 1).astype(F32) / max_exact)
                         / math.log(REL_MAX_DIST / max_exact) * (REL_BUCKETS - max_exact)).astype(I32)
    bucket = jnp.where(n < max_exact, n, jnp.minimum(large, REL_BUCKETS - 1))
    onehot = (bucket[..., None] == jnp.arange(REL_BUCKETS, dtype=I32)).astype(F32)
    bias = jnp.einsum("kstb,bh->hkst", onehot, rel_table.astype(F32), precision=lax.Precision.HIGHEST)
    return bias.reshape(C_HEADS // 2, 2, 4, KB, QB).transpose(0, 2, 3, 1, 4).reshape(C_HEADS // 2, 4, KB, 2 * QB)


def _dsa_attention(h, mask, ckv, ckvt, w_uk, w_uv, rel_table, batch, seq):
    nq = seq // QB
    nk = seq // KB
    wuk = w_uk.transpose(1, 0, 2).astype(BF16)
    wuvt = w_uv.transpose(1, 2, 0).astype(BF16)
    bias = _rel_bias_tiles(rel_table)
    return pl.pallas_call(
        _dsa_attn_kernel,
        out_shape=jax.ShapeDtypeStruct((batch * seq, C_HEADS * C_DH), F32),
        grid=(batch, nq),
        in_specs=[pl.BlockSpec((QB, C_HEADS * C_DH), lambda b, j: (b * nq + j, CD_CQ // (C_HEADS * C_DH))),
                  pl.BlockSpec((None, nk, KB, QB), lambda b, j: (b, 0, 0, j)),
                  pl.BlockSpec((None, nk, KB, C_DLAT), lambda b, j: (b, 0, 0, 0)),
                  pl.BlockSpec((None, nk, C_DLAT + SUM_ROWS, KB), lambda b, j: (b, 0, 0, 0)),
                  _const_spec(wuk), _const_spec(wuvt), _const_spec(bias)],
        out_specs=pl.BlockSpec((QB, C_HEADS * C_DH), lambda b, j: (b * nq + j, 0)),
        scratch_shapes=[pltpu.VMEM((C_HEADS // 2, C_DLAT, 2 * QB), BF16), pltpu.VMEM((C_HEADS // 2, 1, 2 * QB), F32),
                        pltpu.VMEM((C_HEADS // 2, C_DLAT + SUM_ROWS, 2 * QB), F32),
                        pltpu.VMEM((C_HEADS * C_DH, QB), F32)],
        compiler_params=_cparams("parallel", "parallel"),
        name="dsa_attention",
    )(h, mask, ckv.reshape(batch, nk, KB, C_DLAT), ckvt.reshape(batch, nk, C_DLAT + SUM_ROWS, KB), wuk, wuvt, bias)


ROUTER_BLOCK = 512
MOE_TILE = 512


def _router_kernel(x_ref, wh_ref, wl_ref, b_ref, upper_ref, idx_ref, wgt_ref, rank_ref, cnt_ref):
    x = x_ref[...]
    xh = x.astype(BF16)
    xl = (x - xh.astype(F32)).astype(BF16)
    logit = _dot_nt(wh_ref[...], xh) + _dot_nt(wl_ref[...], xh) + _dot_nt(wh_ref[...], xl)
    aff = _sigmoid(logit)
    sel = aff + b_ref[...]
    s_rows = [sel[e:e + 1] for e in range(N_EXPERTS)]
    a_rows = [aff[e:e + 1] for e in range(N_EXPERTS)]
    n = EXPERTS_PER_GROUP

    g_best = jnp.zeros(s_rows[0].shape, I32)
    best = None
    for g in range(N_GROUPS):
        v = s_rows[g * n:(g + 1) * n]
        top2 = None
        for a in range(n):
            for b in range(a + 1, n):
                pair = v[a] + v[b]
                top2 = pair if top2 is None else jnp.maximum(top2, pair)
        if best is None:
            best = top2
        else:
            upd = top2 > best
            g_best = jnp.where(upd, g, g_best)
            best = jnp.where(upd, top2, best)

    sv, av = [], []
    for i in range(n):
        s_i, a_i = s_rows[i], a_rows[i]
        for g in range(1, N_GROUPS):
            pick = g_best == g
            s_i = jnp.where(pick, s_rows[g * n + i], s_i)
            a_i = jnp.where(pick, a_rows[g * n + i], a_i)
        sv.append(s_i)
        av.append(a_i)

    i1, s1, a1 = jnp.zeros_like(g_best), sv[0], av[0]
    for i in range(1, n):
        upd = sv[i] > s1
        i1 = jnp.where(upd, i, i1)
        s1 = jnp.where(upd, sv[i], s1)
        a1 = jnp.where(upd, av[i], a1)
    i2 = jnp.zeros_like(g_best)
    s2 = jnp.full(s1.shape, -jnp.inf, F32)
    a2 = jnp.zeros_like(a1)
    for i in range(n):
        cand = jnp.where(i1 == i, -jnp.inf, sv[i])
        upd = cand > s2
        i2 = jnp.where(upd, i, i2)
        s2 = jnp.where(upd, cand, s2)
        a2 = jnp.where(upd, av[i], a2)

    tot = a1 + a2
    e1 = g_best * n + i1
    e2 = g_best * n + i2
    idx_ref[0:1, :] = e1
    idx_ref[1:2, :] = e2
    wgt_ref[0:1, :] = a1 / tot
    wgt_ref[1:2, :] = a2 / tot

    @pl.when(pl.program_id(0) == 0)
    def _():
        cnt_ref[...] = jnp.zeros_like(cnt_ref)

    e_iota = lax.broadcasted_iota(I32, sel.shape, 0)
    oh1 = jnp.where(e_iota == e1, 1.0, 0.0)
    oh2 = jnp.where(e_iota == e2, 1.0, 0.0)
    both = oh1 + oh2
    before = cnt_ref[...] + _dot(both.astype(BF16), upper_ref[...])
    rank_ref[0:1, :] = jnp.sum(oh1 * before, axis=0, keepdims=True).astype(I32)
    rank_ref[1:2, :] = jnp.sum(oh2 * before, axis=0, keepdims=True).astype(I32)
    cnt_ref[...] = cnt_ref[...] + jnp.sum(both, axis=1, keepdims=True)


def _router(x2d, w_router, b_router):
    t, d = x2d.shape
    tb = min(ROUTER_BLOCK, t)
    wt = w_router.T.astype(F32)
    wh = wt.astype(BF16)
    wl = (wt - wh.astype(F32)).astype(BF16)
    r = np.arange(tb)
    upper = jnp.asarray(r[:, None] < r[None, :], BF16)
    pair_out = pl.BlockSpec((TOP_K, tb), lambda i: (0, i))
    fixed = lambda i: (0, 0)
    return pl.pallas_call(
        _router_kernel,
        out_shape=(jax.ShapeDtypeStruct((TOP_K, t), I32), jax.ShapeDtypeStruct((TOP_K, t), F32),
                   jax.ShapeDtypeStruct((TOP_K, t), I32), jax.ShapeDtypeStruct((N_EXPERTS, 1), F32)),
        grid=(t // tb,),
        in_specs=[pl.BlockSpec((tb, d), lambda i: (i, 0)),
                  pl.BlockSpec((N_EXPERTS, d), fixed), pl.BlockSpec((N_EXPERTS, d), fixed),
                  pl.BlockSpec((N_EXPERTS, 1), fixed), pl.BlockSpec((tb, tb), fixed)],
        out_specs=(pair_out, pair_out, pair_out, pl.BlockSpec((N_EXPERTS, 1), fixed)),
        compiler_params=_cparams("arbitrary"),
        name="moe_router",
    )(x2d, wh, wl, b_router.reshape(-1, 1).astype(F32), upper)


def _route_tables(idx, rank, counts, tm, n_items):
    cnt = counts.reshape(-1).astype(I32)
    start = jnp.cumsum(cnt) - cnt
    experts = jnp.arange(N_EXPERTS, dtype=I32)
    pos = rank + jnp.sum(jnp.where(idx[..., None] == experts, start, 0), axis=-1)
    first_tile = start // tm
    n_e = jnp.where(cnt > 0, (start + cnt - 1) // tm - first_tile + 1, 0)
    item_end = jnp.cumsum(n_e)
    item = jnp.arange(n_items, dtype=I32)
    used = item < item_end[-1]
    e_i = jnp.minimum(jnp.sum((item[:, None] >= item_end[None, :]).astype(I32), axis=1), N_EXPERTS - 1)
    e_last = jnp.max(jnp.where(cnt > 0, experts, 0))
    e_i = jnp.where(used, e_i, e_last)
    tile_i = jnp.where(used, first_tile[e_i] + item - (item_end - n_e)[e_i], (TOP_K * idx.shape[1]) // tm - 1)
    lo = jnp.where(used, jnp.maximum(start[e_i], tile_i * tm) - tile_i * tm, 0)
    hi = jnp.where(used, jnp.minimum(start[e_i] + cnt[e_i], (tile_i + 1) * tm) - tile_i * tm, 0)
    return pos.astype(I32), tile_i.astype(I32), e_i, lo.astype(I32), hi.astype(I32)


DMA_UNROLL = 8


def _dispatch_kernel(pos_ref, x_ref, xs_hbm, stage, sem, *, tb, n_tok):
    i = pl.program_id(0)
    slot = i % 2
    stage[slot] = x_ref[...]

    def issue(grp, carry):
        r0 = grp * DMA_UNROLL
        dst = [[pos_ref[s * n_tok + i * tb + r0 + u] for s in range(TOP_K)] for u in range(DMA_UNROLL)]
        for u in range(DMA_UNROLL):
            for s in range(TOP_K):
                pltpu.make_async_copy(stage.at[slot, pl.ds(r0 + u, 1)], xs_hbm.at[pl.ds(dst[u][s], 1)],
                                      sem.at[slot]).start()
        return carry

    lax.fori_loop(0, tb // DMA_UNROLL, issue, 0)

    def wait_block(s):
        for _ in range(TOP_K):
            pltpu.make_async_copy(stage.at[s], xs_hbm.at[pl.ds(0, tb)], sem.at[s]).wait()

    @pl.when(i >= 1)
    def _():
        wait_block(1 - slot)

    @pl.when(i == pl.num_programs(0) - 1)
    def _():
        wait_block(slot)


def _dispatch(x2d, pos_flat, tb):
    t, d = x2d.shape
    grid_spec = pltpu.PrefetchScalarGridSpec(
        num_scalar_prefetch=1, grid=(t // tb,),
        in_specs=[pl.BlockSpec((tb, d), lambda i, pos: (i, 0))],
        out_specs=pl.BlockSpec(memory_space=pl.ANY),
        scratch_shapes=[pltpu.VMEM((2, tb, d), F32), pltpu.SemaphoreType.DMA((2,))])
    return pl.pallas_call(
        functools.partial(_dispatch_kernel, tb=tb, n_tok=t),
        out_shape=jax.ShapeDtypeStruct((TOP_K * t, d), F32),
        grid_spec=grid_spec,
        compiler_params=_cparams("arbitrary"),
        name="moe_dispatch",
    )(pos_flat, x2d)


def _experts_kernel(tile_ref, exp_ref, lo_ref, hi_ref, x_ref, w1_ref, w3_ref, w2_ref, o_ref, w1b, w3b, w2b):
    i = pl.program_id(0)
    prev = jnp.maximum(i - 1, 0)

    @pl.when(jnp.logical_or(i == 0, exp_ref[i] != exp_ref[prev]))
    def _():
        w1b[...] = w1_ref[...].astype(BF16)
        w3b[...] = w3_ref[...].astype(BF16)
        w2b[...] = w2_ref[...].astype(BF16)

    lo = lo_ref[i]
    hi = hi_ref[i]
    first = jnp.logical_or(i == 0, tile_ref[i] != tile_ref[prev])

    @pl.when(hi > lo)
    def _():
        x = x_ref[...].astype(BF16)
        row = lax.broadcasted_iota(I32, (x.shape[0], 1), 0)
        mine = jnp.where(jnp.logical_and(row >= lo, row < hi), 1.0, 0.0)
        hid = _silu(_dot(x, w1b[...])) * _dot(x, w3b[...]) * mine
        y = _dot(hid.astype(BF16), w2b[...])

        @pl.when(first)
        def _():
            o_ref[...] = y

        @pl.when(jnp.logical_not(first))
        def _():
            o_ref[...] += y


def _experts(xs, tile_i, exp_i, lo, hi, w1, w3, w2, layer, tm):
    n, d = xs.shape
    f = w1.shape[-1]
    by_tile = lambda i, tile, exp, lo, hi: (tile[i], 0)
    by_exp = lambda i, tile, exp, lo, hi: (layer, exp[i], 0, 0)
    grid_spec = pltpu.PrefetchScalarGridSpec(
        num_scalar_prefetch=4, grid=(tile_i.shape[0],),
        in_specs=[pl.BlockSpec((tm, d), by_tile),
                  pl.BlockSpec((None, None, d, f), by_exp), pl.BlockSpec((None, None, d, f), by_exp),
                  pl.BlockSpec((None, None, f, d), by_exp)],
        out_specs=pl.BlockSpec((tm, d), by_tile),
        scratch_shapes=[pltpu.VMEM((d, f), BF16), pltpu.VMEM((d, f), BF16), pltpu.VMEM((f, d), BF16)])
    return pl.pallas_call(
        _experts_kernel,
        out_shape=jax.ShapeDtypeStruct((n, d), F32),
        grid_spec=grid_spec,
        compiler_params=_cparams("arbitrary"),
        name="moe_experts",
    )(tile_i, exp_i, lo, hi, xs, w1, w3, w2)


def _combine_ln_kernel(pos_ref, x_ref, w_ref, g_ref, b_ref, ys_hbm, o_ref, gbuf, sem, *, tb, n_tok):
    i = pl.program_id(0)
    slot = i % 2

    def start(blk, s):
        def issue(grp, carry):
            r0 = grp * DMA_UNROLL
            src = [[pos_ref[k * n_tok + blk * tb + r0 + u] for k in range(TOP_K)] for u in range(DMA_UNROLL)]
            for u in range(DMA_UNROLL):
                for k in range(TOP_K):
                    pltpu.make_async_copy(ys_hbm.at[pl.ds(src[u][k], 1)], gbuf.at[s, k, pl.ds(r0 + u, 1)],
                                          sem.at[s]).start()
            return carry
        lax.fori_loop(0, tb // DMA_UNROLL, issue, 0)

    @pl.when(i == 0)
    def _():
        start(0, 0)

    @pl.when(i + 1 < pl.num_programs(0))
    def _():
        start(i + 1, 1 - slot)

    for k in range(TOP_K):
        pltpu.make_async_copy(ys_hbm.at[pl.ds(0, tb)], gbuf.at[slot, k], sem.at[slot]).wait()
    w = w_ref[...]
    y = gbuf[slot, 0] * w[:, 0:1] + gbuf[slot, 1] * w[:, 1:2]
    o_ref[...] = _layer_norm_rows(DN_ALPHA * x_ref[...] + y, g_ref[...], b_ref[...])


def _combine_ln(x2d, ys, pos_flat, wgt_cols, g, b, tb):
    t, d = x2d.shape
    row = lambda i, pos: (i, 0)
    fixed = lambda i, pos: (0, 0)
    grid_spec = pltpu.PrefetchScalarGridSpec(
        num_scalar_prefetch=1, grid=(t // tb,),
        in_specs=[pl.BlockSpec((tb, d), row), pl.BlockSpec((tb, TOP_K), row),
                  pl.BlockSpec((1, d), fixed), pl.BlockSpec((1, d), fixed),
                  pl.BlockSpec(memory_space=pl.ANY)],
        out_specs=pl.BlockSpec((tb, d), row),
        scratch_shapes=[pltpu.VMEM((2, TOP_K, tb, d), F32), pltpu.SemaphoreType.DMA((2,))])
    return pl.pallas_call(
        functools.partial(_combine_ln_kernel, tb=tb, n_tok=t),
        out_shape=jax.ShapeDtypeStruct((t, d), F32),
        grid_spec=grid_spec,
        compiler_params=_cparams("arbitrary"),
        name="moe_combine_ln",
    )(pos_flat, x2d, wgt_cols, g.reshape(1, d), b.reshape(1, d), ys)


def _moe_ln(x2d, w_router, b_router, w1, w3, w2, layer, g, b):
    t = x2d.shape[0]
    tm = min(MOE_TILE, t)
    idx, wgt, rank, counts = _router(x2d, w_router, b_router)
    n_items = TOP_K * t // tm + N_EXPERTS - 1
    pos, tile_i, exp_i, lo, hi = _route_tables(idx, rank, counts, tm, n_items)
    pos_flat = pos.reshape(-1)
    xs = _dispatch(x2d, pos_flat, tm)
    ys = _experts(xs, tile_i, exp_i, lo, hi, w1, w3, w2, layer, tm)
    return _combine_ln(x2d, ys, pos_flat, wgt.T, g, b, tm)


def _gate_rows(col, batch, seq):
    tb = min(TIME_BLOCK, seq)
    return col.reshape(batch, seq, D_HEADS).transpose(0, 2, 1).reshape(batch, D_HEADS, seq // tb, tb // CHUNK, CHUNK)


def _mix_cd(x2d, w_in, rel_table, ckv_g, w_uk, w_uv, conv_w, gate_b, d_norm_g, batch, seq):
    t = x2d.shape[0]
    h = _project(x2d, *_pack_cd(w_in), min(1024, t), PROJ_TILE)
    ckv, ckvt, kdup = _dsa_prep(h, ckv_g, t)
    mask = _indexer(h, kdup, batch, seq)
    oc = _dsa_attention(h, mask, ckv, ckvt, w_uk, w_uv, rel_table, batch, seq)
    tail = h[:, CD_TAIL:CD_TAIL + LANES]
    ig_rows = _gate_rows(tail[:, TAIL_DI:TAIL_DI + D_HEADS], batch, seq)
    fg_rows = _gate_rows(tail[:, TAIL_DF:TAIL_DF + D_HEADS], batch, seq)
    od = _mlstm(h, conv_w, ig_rows, fg_rows, gate_b, d_norm_g, batch, seq,
                q_off=CD_DQ // (D_HEADS * D_DK), k_off=CD_DK // (D_HEADS * D_DK),
                v_off=CD_DV // (D_HEADS * D_DV), g_off=CD_DOG // (D_HEADS * D_DV))
    return oc, od


def kernel(x, w_in_ab, w_out_ab, hgrn_lb_logits, a_norm_g, gla_wa2, gla_ba2, b_norm_g, w_in_cd, w_out_cd,
           ckv_norm_g, w_uk, w_uv, mlstm_conv_w, mlstm_gate_b, d_norm_g, rel_table, w_router, b_router,
           moe_w1, moe_w3, moe_w2, ln_g, ln_b):
    batch, seq, d = x.shape
    x2d = x.reshape(batch * seq, d)
    for layer in range(DEPTH):
        li = layer // 2
        if layer % 2 == 0:
            mix_a, mix_b = _mix_ab(x2d, w_in_ab[li], hgrn_lb_logits, li, a_norm_g[li], gla_wa2[li], gla_ba2[li],
                                   b_norm_g[li], batch, seq)
            w_out = w_out_ab[li]
        else:
            mix_a, mix_b = _mix_cd(x2d, w_in_cd[li], rel_table, ckv_norm_g[li], w_uk[li], w_uv[li],
                                   mlstm_conv_w[li], mlstm_gate_b[li], d_norm_g[li], batch, seq)
            w_out = w_out_cd[li]
        ka = mix_a.shape[1]
        x2d = _outproj_ln(x2d, mix_a, mix_b, w_out[:ka].astype(BF16), w_out[ka:].astype(BF16),
                          ln_g[layer, 0], ln_b[layer, 0])
        x2d = _moe_ln(x2d, w_router, b_router, moe_w1, moe_w3, moe_w2, layer,
                      ln_g[layer, 1], ln_b[layer, 1])
    return x2d.reshape(batch, seq, d)
```

```python
import functools
import math

import numpy as np
import jax
import jax.numpy as jnp
from jax import lax
from jax.experimental import pallas as pl
from jax.experimental.pallas import tpu as pltpu

F32 = jnp.float32
BF16 = jnp.bfloat16
I32 = jnp.int32

D_MODEL = 2048
DEPTH = 2
A_HEADS, A_DK, A_DV = 8, 128, 128
B_HEADS, B_DK, B_DV = 4, 128, 256
B_GATE_RANK, B_GATE_TAU = 16, 16.0
C_HEADS, C_DH, C_DLAT = 8, 128, 256
IDX_HEADS, IDX_DIM = 16, 64
TOPK_MAX = 256
D_HEADS, D_DK, D_DV = 4, 128, 256
CONV_K = 4
REL_BUCKETS, REL_MAX_DIST = 32, 128
N_EXPERTS, N_GROUPS, TOP_K, D_EXPERT = 16, 4, 2, 512
EXPERTS_PER_GROUP = N_EXPERTS // N_GROUPS
DN_ALPHA = (2 * DEPTH) ** 0.25
EPS = 1e-5

LANES = 128
SUBLANES = 8
VMEM_LIMIT = 56 * 1024 * 1024

CHUNK = 128
N_LEVELS = 7
TIME_BLOCK = 256
QB = 128
NEG = -1e30
INT_MIN = -2 ** 31
INT_MAX = 2 ** 31 - 1


def _cparams(*sem):
    return pltpu.CompilerParams(dimension_semantics=sem, vmem_limit_bytes=VMEM_LIMIT)


def _dot(a, b):
    return jnp.dot(a, b, preferred_element_type=F32)


def _dot_nt(a, b):
    return lax.dot_general(a, b, (((1,), (1,)), ((), ())), preferred_element_type=F32)


def _dot_tn(a, b):
    return lax.dot_general(a, b, (((0,), (0,)), ((), ())), preferred_element_type=F32)


def _split3(a):
    hi = a.astype(BF16)
    r1 = a - hi.astype(F32)
    mid = r1.astype(BF16)
    lo = (r1 - mid.astype(F32)).astype(BF16)
    return hi, mid, lo


def _dot01(m01, a):
    hi, mid, lo = _split3(a)
    return _dot(m01, hi) + _dot(m01, mid) + _dot(m01, lo)


def _sigmoid(x):
    return 1.0 / (1.0 + jnp.exp(-x))


def _silu(x):
    return x * _sigmoid(x)


def _log_sigmoid(x):
    return jnp.minimum(x, 0.0) - jnp.log(1.0 + jnp.exp(-jnp.abs(x)))


def _proj_kernel(x_ref, w_ref, wt_ref, o_ref, xb_ref):
    j = pl.program_id(1)

    @pl.when(j == 0)
    def _():
        xb_ref[...] = x_ref[...].astype(BF16)

    @pl.when(j < pl.num_programs(1) - 1)
    def _():
        o_ref[...] = _dot(xb_ref[...], w_ref[...])

    @pl.when(j == pl.num_programs(1) - 1)
    def _():
        o_ref[...] = _dot(xb_ref[...], wt_ref[...])


def _cast_kernel(w_ref, o_ref):
    o_ref[...] = w_ref[...].astype(o_ref.dtype)


def _cast_columns(w, n_cols, tn):
    k = w.shape[0]
    return pl.pallas_call(
        _cast_kernel,
        out_shape=jax.ShapeDtypeStruct((k, n_cols), BF16),
        grid=(n_cols // tn,),
        in_specs=[pl.BlockSpec((k, tn), lambda j: (0, j))],
        out_specs=pl.BlockSpec((k, tn), lambda j: (0, j)),
        compiler_params=_cparams("parallel"),
        name="weight_cast",
    )(w)


def _project(x, w, w_tail, tm, tn):
    m, k = x.shape
    n_main = w.shape[1] // tn
    return pl.pallas_call(
        _proj_kernel,
        out_shape=jax.ShapeDtypeStruct((m, (n_main + 1) * tn), F32),
        grid=(m // tm, n_main + 1),
        in_specs=[pl.BlockSpec((tm, k), lambda i, j: (i, 0)),
                  pl.BlockSpec((k, tn), lambda i, j: (0, jnp.minimum(j, n_main - 1))),
                  pl.BlockSpec((k, tn), lambda i, j: (0, 0))],
        out_specs=pl.BlockSpec((tm, tn), lambda i, j: (i, j)),
        scratch_shapes=[pltpu.VMEM((tm, k), BF16)],
        compiler_params=_cparams("parallel", "arbitrary"),
        name="in_proj",
    )(x, w, w_tail)


def _layer_norm_rows(z, g, b):
    mu = jnp.mean(z, axis=-1, keepdims=True)
    zc = z - mu
    var = jnp.mean(zc * zc, axis=-1, keepdims=True)
    return zc * lax.rsqrt(var + EPS) * g + b


def _outproj_ln_kernel(x_ref, ma_ref, mb_ref, wa_ref, wb_ref, g_ref, b_ref, o_ref):
    mixed = _dot(ma_ref[...].astype(BF16), wa_ref[...]) + _dot(mb_ref[...].astype(BF16), wb_ref[...])
    o_ref[...] = _layer_norm_rows(DN_ALPHA * x_ref[...] + mixed, g_ref[...], b_ref[...])


def _outproj_ln(x, mix_a, mix_b, w_a, w_b, g, b, tm=512):
    m, d = x.shape
    ka, kb = mix_a.shape[1], mix_b.shape[1]
    row = lambda i: (i, 0)
    fixed = lambda i: (0, 0)
    return pl.pallas_call(
        _outproj_ln_kernel,
        out_shape=jax.ShapeDtypeStruct((m, d), F32),
        grid=(m // tm,),
        in_specs=[pl.BlockSpec((tm, d), row), pl.BlockSpec((tm, ka), row), pl.BlockSpec((tm, kb), row),
                  pl.BlockSpec((ka, d), fixed), pl.BlockSpec((kb, d), fixed),
                  pl.BlockSpec((1, d), fixed), pl.BlockSpec((1, d), fixed)],
        out_specs=pl.BlockSpec((tm, d), row),
        compiler_params=_cparams("parallel"),
        name="out_proj_ln",
    )(x, mix_a, mix_b, w_a, w_b, g.reshape(1, d), b.reshape(1, d))


def _chunk_constants():
    t = np.arange(CHUNK)
    tri = (t[:, None] >= t[None, :]).astype(np.float32)
    pair, odd = [], []
    for lev in range(1, N_LEVELS + 1):
        c = CHUNK >> lev
        pair.append((t[:, None] // (2 * c) == t[None, :] // (2 * c)).astype(np.float32))
        odd.append(np.broadcast_to((((t // c) & 1) == 1).astype(np.float32)[:, None], (CHUNK, LANES)))
    pair.append(np.eye(CHUNK, dtype=np.float32))
    return jnp.asarray(tri, BF16), jnp.asarray(np.stack(pair), F32), jnp.asarray(np.stack(odd), F32)


def _level_log_decay(la, bcum, lev):
    c = CHUNK >> lev
    if 2 * c >= SUBLANES:
        mids = [jnp.broadcast_to(bcum[g * 2 * c + c - 1:g * 2 * c + c], (2 * c, bcum.shape[1]))
                for g in range(CHUNK // (2 * c))]
        return -jnp.abs(bcum - (mids[0] if len(mids) == 1 else jnp.concatenate(mids, axis=0)))
    r = lax.broadcasted_iota(I32, la.shape, 0) & (2 * c - 1)
    if c == 2:
        nxt = pltpu.roll(la, CHUNK - 1, axis=0)
        prv = pltpu.roll(la, 1, axis=0)
        return jnp.where(r == 0, nxt, jnp.where(r == 1, 0.0, jnp.where(r == 2, la, la + prv)))
    return jnp.where(r == 1, la, 0.0)


def _glr_chunk(q, k, v, la, st_ref, cum_ref, pair_ref, odd_ref):
    bcum = _dot01(cum_ref[...], la)
    attn = pair_ref[N_LEVELS] * _dot_nt(q.astype(BF16), k.astype(BF16))
    for lev in range(1, N_LEVELS + 1):
        e = jnp.exp(_level_log_decay(la, bcum, lev))
        eq = e * odd_ref[lev - 1]
        ql = (q * eq).astype(BF16)
        kl = (k * (e - eq)).astype(BF16)
        attn = attn + pair_ref[lev - 1] * _dot_nt(ql, kl)
    st = st_ref[...]
    o = _dot_nt((q * jnp.exp(bcum)).astype(BF16), st.astype(BF16)) + _dot(attn.astype(BF16), v.astype(BF16))
    b_last = bcum[CHUNK - 1:CHUNK]
    kdec = (k * jnp.exp(b_last - bcum)).astype(BF16)
    st_ref[...] = st * jnp.exp(b_last) + _dot_tn(v.astype(BF16), kdec)
    return o


def _rms_gate(o, g, gate):
    ms = jnp.mean(o * o, axis=-1, keepdims=True)
    return o * lax.rsqrt(ms + EPS) * g * gate


HEAD_GROUP = 4


def _hgrn2_kernel(q_ref, f_ref, i_ref, g_ref, lb_ref, ng_ref, cum_ref, pair_ref, odd_ref, o_ref, st_ref):
    @pl.when(pl.program_id(2) == 0)
    def _():
        st_ref[...] = jnp.zeros_like(st_ref)

    def body(c, carry):
        rows = pl.ds(pl.multiple_of(c * CHUNK, CHUNK), CHUNK)
        for hh in range(HEAD_GROUP):
            ck = slice(A_DK * hh, A_DK * (hh + 1))
            cv = slice(A_DV * hh, A_DV * (hh + 1))
            lb = lb_ref[:, ck]
            f = lb + (1.0 - lb) * _sigmoid(f_ref[rows, ck])
            o = _glr_chunk(_silu(q_ref[rows, ck]), 1.0 - f, i_ref[rows, cv], jnp.log(f),
                           st_ref.at[hh], cum_ref, pair_ref, odd_ref)
            o_ref[rows, cv] = _rms_gate(o, ng_ref[...], _silu(g_ref[rows, cv]))
        return carry

    lax.fori_loop(0, q_ref.shape[0] // CHUNK, body, 0)


def _gla_kernel(q_ref, k_ref, v_ref, g_ref, r_ref, wa_ref, ba_ref, ng_ref, cum_ref, pair_ref, odd_ref,
                o_ref, st_ref):
    @pl.when(pl.program_id(2) == 0)
    def _():
        st_ref[...] = jnp.zeros_like(st_ref)

    def body(c, carry):
        rows = pl.ds(pl.multiple_of(c * CHUNK, CHUNK), CHUNK)
        pre = _dot(r_ref[rows, :].astype(BF16), wa_ref[...]) + ba_ref[...]
        la = _log_sigmoid(pre) * (1.0 / B_GATE_TAU)
        for hh in range(HEAD_GROUP):
            ck = slice(B_DK * hh, B_DK * (hh + 1))
            cv = slice(B_DV * hh, B_DV * (hh + 1))
            o = _glr_chunk(q_ref[rows, ck] * (B_DK ** -0.5), k_ref[rows, ck], v_ref[rows, cv], la[:, ck],
                           st_ref.at[hh], cum_ref, pair_ref, odd_ref)
            o_ref[rows, cv] = _rms_gate(o, ng_ref[...], _silu(g_ref[rows, cv]))
        return carry

    lax.fori_loop(0, q_ref.shape[0] // CHUNK, body, 0)


def _const_spec(arr):
    nd = arr.ndim
    return pl.BlockSpec(arr.shape, lambda *_: (0,) * nd)


def _hgrn2(h, lb, norm_g, batch, seq):
    tb = min(TIME_BLOCK, seq)
    nt = seq // tb
    ng = A_HEADS // HEAD_GROUP
    wk, wv = HEAD_GROUP * A_DK, HEAD_GROUP * A_DV
    consts = _chunk_constants()
    col = lambda seg, w: pl.BlockSpec((tb, w), lambda b, g, t, seg=seg: (b * nt + t, seg * ng + g))
    return pl.pallas_call(
        _hgrn2_kernel,
        out_shape=jax.ShapeDtypeStruct((batch * seq, A_HEADS * A_DV), F32),
        grid=(batch, ng, nt),
        in_specs=[col(0, wk), col(1, wk), col(2, wv), col(3, wv),
                  pl.BlockSpec((1, wk), lambda b, g, t: (0, g)),
                  pl.BlockSpec((1, A_DV), lambda b, g, t: (0, 0))] + [_const_spec(c) for c in consts],
        out_specs=pl.BlockSpec((tb, wv), lambda b, g, t: (b * nt + t, g)),
        scratch_shapes=[pltpu.VMEM((HEAD_GROUP, A_DV, A_DK), F32)],
        compiler_params=_cparams("parallel", "parallel", "arbitrary"),
        name="hgrn2",
    )(h, h, h, h, lb.reshape(1, -1), norm_g.reshape(1, -1), *consts)


def _gla(h, wa2p, ba2, norm_g, batch, seq, q_off, k_off, v_off, g_off, r_off):
    tb = min(TIME_BLOCK, seq)
    nt = seq // tb
    wk, wv = B_HEADS * B_DK, B_HEADS * B_DV
    consts = _chunk_constants()
    col = lambda off, w: pl.BlockSpec((tb, w), lambda b, g, t, off=off: (b * nt + t, off))
    fixed = lambda b, g, t: (0, 0)
    return pl.pallas_call(
        _gla_kernel,
        out_shape=jax.ShapeDtypeStruct((batch * seq, wv), F32),
        grid=(batch, 1, nt),
        in_specs=[col(q_off, wk), col(k_off, wk), col(v_off, wv), col(g_off, wv), col(r_off, LANES),
                  pl.BlockSpec((LANES, wk), fixed), pl.BlockSpec((1, wk), fixed),
                  pl.BlockSpec((1, B_DV), fixed)] + [_const_spec(c) for c in consts],
        out_specs=pl.BlockSpec((tb, wv), lambda b, g, t: (b * nt + t, 0)),
        scratch_shapes=[pltpu.VMEM((B_HEADS, B_DV, B_DK), F32)],
        compiler_params=_cparams("parallel", "parallel", "arbitrary"),
        name="gla",
    )(h, h, h, h, h, wa2p, ba2.reshape(1, -1), norm_g.reshape(1, -1), *consts)


PROJ_TILE = 512
AB_MAIN = 4 * A_HEADS * A_DK + 2 * B_HEADS * B_DK + 2 * B_HEADS * B_DV
AB_PAD = AB_MAIN + PROJ_TILE


def _mix_ab(x2d, w_in, lb_logits, li, a_norm_g, wa2, ba2, b_norm_g, batch, seq):
    d = x2d.shape[1]
    w_tail = jnp.pad(w_in[:, AB_MAIN:], ((0, 0), (0, AB_PAD - w_in.shape[1]))).astype(BF16)
    h = _project(x2d, _cast_columns(w_in, AB_MAIN, PROJ_TILE), w_tail, min(1024, x2d.shape[0]), PROJ_TILE)
    lb = jnp.cumsum(jax.nn.softmax(lb_logits.astype(F32), axis=0), axis=0)[li]
    oa = _hgrn2(h, lb, a_norm_g, batch, seq)
    wa2p = jnp.concatenate([wa2, jnp.zeros((LANES - B_GATE_RANK, wa2.shape[1]), F32)], axis=0).astype(BF16)
    a_cols = 4 * A_HEADS * A_DK
    wk, wv = B_HEADS * B_DK, B_HEADS * B_DV
    ob = _gla(h, wa2p, ba2, b_norm_g, batch, seq, q_off=a_cols // wk, k_off=a_cols // wk + 1,
              v_off=(a_cols + 2 * wk) // wv, g_off=(a_cols + 2 * wk) // wv + 1, r_off=AB_MAIN // LANES)
    return oa, ob


CONV_HALO = 8


def _causal_conv(x_ref, w_ref, buf_ref, tail_ref):
    tb = x_ref.shape[0]
    x = x_ref[...]
    buf_ref[0:CONV_HALO, :] = tail_ref[...]
    buf_ref[CONV_HALO:CONV_HALO + tb, :] = x
    tail_ref[...] = x[tb - CONV_HALO:tb]
    y = w_ref[CONV_K - 1:CONV_K, :] * x
    for j in range(CONV_K - 1):
        y = y + w_ref[j:j + 1, :] * buf_ref[pl.ds(CONV_HALO - (CONV_K - 1) + j, tb), :]
    return y


def _row_to_col(row, eye):
    return jnp.sum(jnp.where(eye, row, 0.0), axis=1, keepdims=True)


def _mlstm_kernel(q_ref, k_ref, v_ref, og_ref, wq_ref, wk_ref, ig_ref, fg_ref, gb_ref, ng_ref, tri_ref,
                  o_ref, ct_ref, n_ref, m_ref, qt_ref, kt_ref, qs_ref, ks_ref, buf_ref):
    @pl.when(pl.program_id(2) == 0)
    def _():
        ct_ref[...] = jnp.zeros_like(ct_ref)
        n_ref[...] = jnp.zeros_like(n_ref)
        m_ref[...] = jnp.zeros_like(m_ref)
        qt_ref[...] = jnp.zeros_like(qt_ref)
        kt_ref[...] = jnp.zeros_like(kt_ref)

    qs_ref[...] = _silu(_causal_conv(q_ref, wq_ref, buf_ref, qt_ref))
    ks_ref[...] = _silu(_causal_conv(k_ref, wk_ref, buf_ref, kt_ref)) * (D_DK ** -0.5)

    r_i = lax.broadcasted_iota(I32, (CHUNK, CHUNK), 0)
    c_i = lax.broadcasted_iota(I32, (CHUNK, CHUNK), 1)
    eye = r_i == c_i
    causal = r_i >= c_i

    def body(c, carry):
        rows = pl.ds(pl.multiple_of(c * CHUNK, CHUNK), CHUNK)
        tri = tri_ref[...]
        for hh in range(D_HEADS):
            ck = slice(D_DK * hh, D_DK * (hh + 1))
            cv = slice(D_DV * hh, D_DV * (hh + 1))
            q = qs_ref[rows, ck]
            k = ks_ref[rows, ck]
            v = v_ref[rows, cv].astype(BF16)
            qb = q.astype(BF16)
            ig_row = ig_ref[hh, pl.ds(c, 1), :] + gb_ref[0, hh]
            lf_row = _log_sigmoid(fg_ref[hh, pl.ds(c, 1), :] + gb_ref[1, hh])
            hi, mid, lo = _split3(lf_row)
            bcum_row = _dot(hi, tri) + _dot(mid, tri) + _dot(lo, tri)
            bcum_col = _row_to_col(bcum_row, eye)
            ig_col = _row_to_col(ig_row, eye)
            m_prev = m_ref[hh, :, 0:1]
            log_w = jnp.where(causal, bcum_col - bcum_row + ig_row, NEG)
            log_inter = bcum_col + m_prev
            m_t = jnp.maximum(jnp.max(log_w, axis=1, keepdims=True), log_inter)
            s = _dot_nt(qb, k.astype(BF16)) * jnp.exp(log_w - m_t)
            w_inter = jnp.exp(log_inter - m_t)
            num = _dot(s.astype(BF16), v) + w_inter * _dot_nt(qb, ct_ref[hh].astype(BF16))
            qn = jnp.sum(s, axis=1, keepdims=True) + w_inter * jnp.sum(q * n_ref[hh], axis=1, keepdims=True)
            h = num / jnp.maximum(jnp.abs(qn), jnp.exp(-m_t))
            o_ref[rows, cv] = _rms_gate(h, ng_ref[...], _sigmoid(og_ref[rows, cv]))
            b_last = bcum_row[:, CHUNK - 1:CHUNK]
            log_u = b_last - bcum_col + ig_col
            m_new = jnp.maximum(b_last + m_prev, jnp.max(log_u, axis=0, keepdims=True))
            decay = jnp.exp(b_last + m_prev - m_new)
            ku = k * jnp.exp(log_u - m_new)
            ct_ref[hh] = decay * ct_ref[hh] + _dot_tn(v, ku.astype(BF16))
            n_ref[hh] = decay * n_ref[hh] + jnp.sum(ku, axis=0, keepdims=True)
            m_ref[hh] = jnp.broadcast_to(m_new, (1, LANES))
        return carry

    lax.fori_loop(0, q_ref.shape[0] // CHUNK, body, 0)


def _mlstm(h, conv_w, ig_rows, fg_rows, gate_b, norm_g, batch, seq, q_off, k_off, v_off, g_off):
    tb = min(TIME_BLOCK, seq)
    nt = seq // tb
    nc = tb // CHUNK
    wk, wv = D_HEADS * D_DK, D_HEADS * D_DV
    t = np.arange(CHUNK)
    tri = jnp.asarray(t[:, None] <= t[None, :], BF16)
    gb = jnp.broadcast_to(gate_b.reshape(2, D_HEADS, 1, 1), (2, D_HEADS, 1, CHUNK)).astype(F32)
    col = lambda off, w: pl.BlockSpec((tb, w), lambda b, g, t, off=off: (b * nt + t, off))
    gate = pl.BlockSpec((None, D_HEADS, None, nc, CHUNK), lambda b, g, t: (b, 0, t, 0, 0))
    fixed = lambda b, g, t: (0, 0)
    return pl.pallas_call(
        _mlstm_kernel,
        out_shape=jax.ShapeDtypeStruct((batch * seq, wv), F32),
        grid=(batch, 1, nt),
        in_specs=[col(q_off, wk), col(k_off, wk), col(v_off, wv), col(g_off, wv),
                  pl.BlockSpec((CONV_K, wk), lambda b, g, t: (0, 0)),
                  pl.BlockSpec((CONV_K, wk), lambda b, g, t: (0, 1)),
                  gate, gate,
                  pl.BlockSpec((2, D_HEADS, 1, CHUNK), lambda b, g, t: (0, 0, 0, 0)),
                  pl.BlockSpec((1, D_DV), fixed), pl.BlockSpec((CHUNK, CHUNK), fixed)],
        out_specs=pl.BlockSpec((tb, wv), lambda b, g, t: (b * nt + t, 0)),
        scratch_shapes=[pltpu.VMEM((D_HEADS, D_DV, D_DK), F32), pltpu.VMEM((D_HEADS, 1, D_DK), F32),
                        pltpu.VMEM((D_HEADS, 1, LANES), F32),
                        pltpu.VMEM((CONV_HALO, wk), F32), pltpu.VMEM((CONV_HALO, wk), F32),
                        pltpu.VMEM((tb, wk), F32), pltpu.VMEM((tb, wk), F32),
                        pltpu.VMEM((tb + CONV_HALO, wk), F32)],
        compiler_params=_cparams("parallel", "parallel", "arbitrary"),
        name="mlstm",
    )(h, h, h, h, conv_w, conv_w, ig_rows, fg_rows, gb, norm_g.reshape(1, -1), tri)


CD_CQ, CD_IQ, CD_DQ, CD_DK, CD_DV, CD_DOG, CD_CKV, CD_TAIL = 0, 1024, 2048, 2560, 3072, 4096, 5120, 5376
CD_PAD = 5632
TAIL_IK, TAIL_IW, TAIL_DI, TAIL_DF = 0, 64, 80, 84


def _pack_cd(w_in):
    cq, ckv, iq, ik, iw, dq, dk, dv, di, df, dog = jnp.split(
        w_in, [int(i) for i in np.cumsum(
            (C_HEADS * C_DH, C_DLAT, IDX_HEADS * IDX_DIM, IDX_DIM, IDX_HEADS, D_HEADS * D_DK, D_HEADS * D_DK,
             D_HEADS * D_DV, D_HEADS, D_HEADS))], axis=1)
    used = CD_TAIL + IDX_DIM + IDX_HEADS + 2 * D_HEADS
    pad = jnp.zeros((w_in.shape[0], CD_PAD - used), F32)
    main = jnp.concatenate([cq, iq, dq, dk, dv, dog], axis=1).astype(BF16)
    tail = jnp.concatenate([ckv, ik, iw, di, df, pad], axis=1).astype(BF16)
    return main, tail


KB = 2 * QB
IDX_GROUP = 4
ATT_GROUP = 4
SUM_ROWS = 16


def _dsa_prep_kernel(ckv_ref, tail_ref, g_ref, ckv_o, ckvt_o, kdup_o):
    c = ckv_ref[...]
    cn = c * lax.rsqrt(jnp.mean(c * c, axis=-1, keepdims=True) + EPS) * g_ref[...]
    ckv_o[...] = cn.astype(BF16)
    ckvt_o[0:C_DLAT, :] = cn.T.astype(BF16)
    ckvt_o[C_DLAT:C_DLAT + SUM_ROWS, :] = jnp.ones((SUM_ROWS, KB), BF16)
    tail = tail_ref[...]
    lane = lax.broadcasted_iota(I32, tail.shape, 1)
    kdup_o[...] = jnp.where(lane < IDX_DIM, tail, pltpu.roll(tail, IDX_DIM, axis=1)).astype(BF16)


def _dsa_prep(h, ckv_g, n_rows):
    nb = n_rows // KB
    return pl.pallas_call(
        _dsa_prep_kernel,
        out_shape=(jax.ShapeDtypeStruct((nb, KB, C_DLAT), BF16), jax.ShapeDtypeStruct((nb, C_DLAT + SUM_ROWS, KB), BF16),
                   jax.ShapeDtypeStruct((nb, KB, LANES), BF16)),
        grid=(nb,),
        in_specs=[pl.BlockSpec((KB, C_DLAT), lambda i: (i, CD_CKV // C_DLAT)),
                  pl.BlockSpec((KB, LANES), lambda i: (i, CD_TAIL // LANES)),
                  pl.BlockSpec((1, C_DLAT), lambda i: (0, 0))],
        out_specs=(pl.BlockSpec((None, KB, C_DLAT), lambda i: (i, 0, 0)),
                   pl.BlockSpec((None, C_DLAT + SUM_ROWS, KB), lambda i: (i, 0, 0)),
                   pl.BlockSpec((None, KB, LANES), lambda i: (i, 0, 0))),
        compiler_params=_cparams("parallel"),
        name="dsa_prep",
    )(h, h, ckv_g.reshape(1, -1))


def _sortable_key(x):
    b = lax.bitcast_convert_type(x, I32)
    key = b ^ ((b >> 31) & 0x7FFFFFFF)
    return jnp.where(key == -1, 0, key)


BISECT_STEPS = 4


def _indexer_kernel(iq_ref, tail_ref, kdup_ref, tri_ref, mask_ref, key_ref, wst_ref, *, k_sel):
    j = pl.program_id(1)
    nk = key_ref.shape[0]
    n_live = (j * QB + QB + KB - 1) // KB
    w_t = tail_ref[...].T
    lane = lax.broadcasted_iota(I32, (QB, LANES), 1)
    for p in range(IDX_HEADS // 2):
        pair = iq_ref[:, LANES * p:LANES * (p + 1)]
        g, r = divmod(2 * p, IDX_GROUP)
        wst_ref[g, r * QB:(r + 1) * QB, :] = jnp.where(lane < IDX_DIM, pair, 0.0).astype(BF16)
        wst_ref[g, (r + 1) * QB:(r + 2) * QB, :] = jnp.where(lane >= IDX_DIM, pair, 0.0).astype(BF16)

    s_loc = lax.broadcasted_iota(I32, (KB, QB), 0)
    t_abs = j * QB + lax.broadcasted_iota(I32, (1, QB), 1)

    def score_chunk(kc, carry):
        kd = kdup_ref[kc]
        acc = jnp.zeros((KB, QB), F32)
        for g in range(IDX_HEADS // IDX_GROUP):
            dots = _dot_nt(kd, wst_ref[g])
            for r in range(IDX_GROUP):
                row = TAIL_IW + g * IDX_GROUP + r
                acc = acc + jnp.maximum(dots[:, r * QB:(r + 1) * QB], 0.0) * w_t[row:row + 1, :]
        key = jnp.where(kc * KB + s_loc > t_abs, INT_MIN, _sortable_key(acc))
        key_ref[kc] = key
        key = key.reshape(KB // SUBLANES, SUBLANES, QB)
        k_max, k_min = carry
        k_max = jnp.maximum(k_max, jnp.max(key, axis=0))
        k_min = jnp.minimum(k_min, jnp.min(jnp.where(key == INT_MIN, INT_MAX, key), axis=0))
        return k_max, k_min

    def score_two(i, carry):
        return score_chunk(2 * i + 1, score_chunk(2 * i, carry))

    k_max, k_min = lax.fori_loop(0, (n_live + 1) // 2, score_two, (jnp.full((SUBLANES, QB), INT_MIN, I32),
                                                                   jnp.full((SUBLANES, QB), INT_MAX, I32)))
    for shift in (4, 2, 1):
        k_max = jnp.maximum(k_max, pltpu.roll(k_max, shift, axis=0))
        k_min = jnp.minimum(k_min, pltpu.roll(k_min, shift, axis=0))
    k_row = jnp.minimum(k_sel, j * QB + lax.broadcasted_iota(I32, (SUBLANES, QB), 1) + 1)

    def count(pred):
        def add(kc, acc):
            hit = jnp.where(pred(key_ref[kc].reshape(KB // SUBLANES, SUBLANES, QB)), 1, 0)
            return acc + jnp.sum(hit, axis=0)
        acc = lax.fori_loop(0, n_live, add, jnp.zeros((SUBLANES, QB), I32))
        for shift in (4, 2, 1):
            acc = acc + pltpu.roll(acc, shift, axis=0)
        return acc

    def unfinished(state):
        lo, hi = state
        return jnp.max(jnp.where(lo < hi, 1, 0)) > 0

    def halve(state):
        lo, hi = state
        mid = (lo >> 1) + (hi >> 1) + (((lo & 1) + (hi & 1) + 1) >> 1)
        cnt = count(lambda k: k >= mid)
        enough = cnt >= k_row
        lo_n = jnp.where(enough, mid, lo)
        hi_n = jnp.where(cnt == k_row, mid, jnp.where(enough, hi, mid - 1))
        return lo_n, hi_n

    def halve_steps(state):
        for _ in range(BISECT_STEPS):
            state = halve(state)
        return state

    tau8, _ = lax.while_loop(unfinished, halve_steps, (k_min, k_max))
    n_ge = count(lambda k: k >= tau8)
    has_tie = jnp.max(jnp.where(n_ge != k_row, 1, 0)) > 0
    tau = tau8[0:1, :]

    @pl.when(jnp.logical_not(has_tie))
    def _():
        def put(kc, carry):
            mask_ref[kc] = jnp.where(key_ref[kc] >= tau, 1.0, 0.0).astype(BF16)
            return carry
        lax.fori_loop(0, n_live, put, 0)

    @pl.when(has_tie)
    def _():
        need = (k_row - count(lambda k: k > tau8)).astype(F32)[0:1, :]

        def put(kc, seen):
            k = key_ref[kc]
            eq = jnp.where(k == tau, 1.0, 0.0)
            before = _dot(tri_ref[...], eq.astype(BF16)) + seen
            take = jnp.where(k > tau, 1.0, jnp.where(before < need, eq, 0.0))
            mask_ref[kc] = take.astype(BF16)
            return seen + jnp.sum(eq, axis=0, keepdims=True)
        lax.fori_loop(0, n_live, put, jnp.zeros((1, QB), F32))

    def clear(kc, carry):
        mask_ref[kc] = jnp.zeros((KB, QB), BF16)
        return carry
    lax.fori_loop(n_live, nk, clear, 0)


def _indexer(h, kdup, batch, seq):
    nq = seq // QB
    nk = seq // KB
    k_sel = min(TOPK_MAX, seq // 4)
    r = np.arange(KB)
    tri = jnp.asarray(r[None, :] < r[:, None], BF16)
    return pl.pallas_call(
        functools.partial(_indexer_kernel, k_sel=k_sel),
        out_shape=jax.ShapeDtypeStruct((batch, nk, KB, seq), BF16),
        grid=(batch, nq),
        in_specs=[pl.BlockSpec((QB, IDX_HEADS * IDX_DIM), lambda b, j: (b * nq + j, CD_IQ // (IDX_HEADS * IDX_DIM))),
                  pl.BlockSpec((QB, LANES), lambda b, j: (b * nq + j, CD_TAIL // LANES)),
                  pl.BlockSpec((None, nk, KB, LANES), lambda b, j: (b, 0, 0, 0)),
                  pl.BlockSpec((KB, KB), lambda b, j: (0, 0))],
        out_specs=pl.BlockSpec((None, nk, KB, QB), lambda b, j: (b, 0, 0, j)),
        scratch_shapes=[pltpu.VMEM((nk, KB, QB), I32),
                        pltpu.VMEM((IDX_HEADS // IDX_GROUP, IDX_GROUP * QB, LANES), BF16)],
        compiler_params=_cparams("parallel", "parallel"),
        name="dsa_indexer",
    )(h, h, kdup.reshape(batch, nk, KB, LANES), tri)


def _dsa_attn_kernel(cq_ref, mask_ref, ckv_ref, ckvt_ref, wuk_ref, wuvt_ref, bias_ref, o_ref,
                     qt_ref, m_ref, acc_ref, ot_ref, p_ref, alpha_ref):
    j = pl.program_id(1)
    for hh in range(C_HEADS):
        q_h = cq_ref[:, C_DH * hh:C_DH * (hh + 1)].astype(BF16)
        part = slice(QB * (hh % ATT_GROUP), QB * (hh % ATT_GROUP + 1))
        qt_ref[hh // ATT_GROUP, :, part] = (_dot_nt(wuk_ref[hh], q_h) * (C_DH ** -0.5)).astype(BF16)
    m_ref[...] = jnp.full(m_ref.shape, NEG, F32)
    acc_ref[...] = jnp.zeros_like(acc_ref)
    p_ref[...] = jnp.zeros_like(p_ref)
    alpha_ref[...] = jnp.ones_like(alpha_ref)

    n_live = (j * QB + QB + KB - 1) // KB
    odd = (j % 2) == 1

    def accumulate(kc_done, hp):
        acc_ref[hp] = alpha_ref[hp] * acc_ref[hp] + _dot(ckvt_ref[kc_done], p_ref[hp])

    def body(kc, carry):
        ck = ckv_ref[kc]
        kc_prev = jnp.maximum(kc - 1, 0)
        drop = jnp.where(mask_ref[kc].astype(F32) > 0.5, 0.0, NEG)
        drop = jnp.concatenate([drop] * ATT_GROUP, axis=1)
        back = n_live - 1 - kc
        which = jnp.where(back == 0, jnp.where(odd, 0, 1), jnp.where(jnp.logical_and(back == 1, jnp.logical_not(odd)), 2, 3))
        for hp in range(C_HEADS // ATT_GROUP):
            accumulate(kc_prev, hp)
            logit = _dot(ck, qt_ref[hp]) + (bias_ref[hp, which] + drop)
            m_old = m_ref[hp]
            m_new = jnp.maximum(m_old, jnp.max(logit, axis=0, keepdims=True))
            alpha_ref[hp] = jnp.exp(m_old - m_new)
            p_ref[hp] = jnp.exp(logit - m_new).astype(BF16)
            m_ref[hp] = m_new
        return carry

    def body_two(i, carry):
        return body(2 * i + 1, body(2 * i, carry))

    n_pairs = (n_live + 1) // 2
    lax.fori_loop(0, n_pairs, body_two, 0)
    for hp in range(C_HEADS // ATT_GROUP):
        accumulate(2 * n_pairs - 1, hp)
    for hh in range(C_HEADS):
        part = slice(QB * (hh % ATT_GROUP), QB * (hh % ATT_GROUP + 1))
        total = acc_ref[hh // ATT_GROUP, C_DLAT:C_DLAT + 1, part]
        o_lat = (acc_ref[hh // ATT_GROUP, 0:C_DLAT, part] * (1.0 / total)).astype(BF16)
        ot_ref[C_DH * hh:C_DH * (hh + 1), :] = _dot(wuvt_ref[hh], o_lat)
    o_ref[...] = ot_ref[...].T


def _rel_bias_tiles(rel_table):
    s = np.arange(QB)[:, None]
    t = np.arange(QB)[None, :]
    diag, prev, far = np.maximum(t - s, 0), QB + t - s, np.full((QB, QB), 2 * QB)
    kinds = [(prev, diag), (diag, far), (far, prev), (far, far)]
    n = jnp.asarray(np.stack([np.concatenate(k, axis=0) for k in kinds]).astype(np.int32))
    max_exact = REL_BUCKETS // 2
    large = max_exact + (jnp.log(jnp.maximum(n, 1).astype(F32) / max_exact)
                         / math.log(REL_MAX_DIST / max_exact) * (REL_BUCKETS - max_exact)).astype(I32)
    bucket = jnp.where(n < max_exact, n, jnp.minimum(large, REL_BUCKETS - 1))
    onehot = (bucket[..., None] == jnp.arange(REL_BUCKETS, dtype=I32)).astype(F32)
    bias = jnp.einsum("kstb,bh->hkst", onehot, rel_table.astype(F32), precision=lax.Precision.HIGHEST)
    ng = C_HEADS // ATT_GROUP
    return bias.reshape(ng, ATT_GROUP, 4, KB, QB).transpose(0, 2, 3, 1, 4).reshape(ng, 4, KB, ATT_GROUP * QB)


def _dsa_attention(h, mask, ckv, ckvt, w_uk, w_uv, rel_table, batch, seq):
    nq = seq // QB
    nk = seq // KB
    wuk = w_uk.transpose(1, 0, 2).astype(BF16)
    wuvt = w_uv.transpose(1, 2, 0).astype(BF16)
    bias = _rel_bias_tiles(rel_table)
    return pl.pallas_call(
        _dsa_attn_kernel,
        out_shape=jax.ShapeDtypeStruct((batch * seq, C_HEADS * C_DH), F32),
        grid=(batch, nq),
        in_specs=[pl.BlockSpec((QB, C_HEADS * C_DH), lambda b, j: (b * nq + j, CD_CQ // (C_HEADS * C_DH))),
                  pl.BlockSpec((None, nk, KB, QB), lambda b, j: (b, 0, 0, j)),
                  pl.BlockSpec((None, nk, KB, C_DLAT), lambda b, j: (b, 0, 0, 0)),
                  pl.BlockSpec((None, nk, C_DLAT + SUM_ROWS, KB), lambda b, j: (b, 0, 0, 0)),
                  _const_spec(wuk), _const_spec(wuvt), _const_spec(bias)],
        out_specs=pl.BlockSpec((QB, C_HEADS * C_DH), lambda b, j: (b * nq + j, 0)),
        scratch_shapes=[pltpu.VMEM((C_HEADS // ATT_GROUP, C_DLAT, ATT_GROUP * QB), BF16),
                        pltpu.VMEM((C_HEADS // ATT_GROUP, 1, ATT_GROUP * QB), F32),
                        pltpu.VMEM((C_HEADS // ATT_GROUP, C_DLAT + SUM_ROWS, ATT_GROUP * QB), F32),
                        pltpu.VMEM((C_HEADS * C_DH, QB), F32),
                        pltpu.VMEM((C_HEADS // ATT_GROUP, KB, ATT_GROUP * QB), BF16),
                        pltpu.VMEM((C_HEADS // ATT_GROUP, 1, ATT_GROUP * QB), F32)],
        compiler_params=_cparams("parallel", "parallel"),
        name="dsa_attention",
    )(h, mask, ckv.reshape(batch, nk, KB, C_DLAT), ckvt.reshape(batch, nk, C_DLAT + SUM_ROWS, KB), wuk, wuvt, bias)


ROUTER_BLOCK = 512
MOE_TILE = 512


def _router_kernel(x_ref, wh_ref, wl_ref, b_ref, upper_ref, idx_ref, wgt_ref, rank_ref, cnt_ref):
    x = x_ref[...]
    xh = x.astype(BF16)
    xl = (x - xh.astype(F32)).astype(BF16)
    logit = _dot_nt(wh_ref[...], xh) + _dot_nt(wl_ref[...], xh) + _dot_nt(wh_ref[...], xl)
    aff = _sigmoid(logit)
    sel = aff + b_ref[...]
    s_rows = [sel[e:e + 1] for e in range(N_EXPERTS)]
    a_rows = [aff[e:e + 1] for e in range(N_EXPERTS)]
    n = EXPERTS_PER_GROUP

    g_best = jnp.zeros(s_rows[0].shape, I32)
    best = None
    for g in range(N_GROUPS):
        v = s_rows[g * n:(g + 1) * n]
        top2 = None
        for a in range(n):
            for b in range(a + 1, n):
                pair = v[a] + v[b]
                top2 = pair if top2 is None else jnp.maximum(top2, pair)
        if best is None:
            best = top2
        else:
            upd = top2 > best
            g_best = jnp.where(upd, g, g_best)
            best = jnp.where(upd, top2, best)

    sv, av = [], []
    for i in range(n):
        s_i, a_i = s_rows[i], a_rows[i]
        for g in range(1, N_GROUPS):
            pick = g_best == g
            s_i = jnp.where(pick, s_rows[g * n + i], s_i)
            a_i = jnp.where(pick, a_rows[g * n + i], a_i)
        sv.append(s_i)
        av.append(a_i)

    i1, s1, a1 = jnp.zeros_like(g_best), sv[0], av[0]
    for i in range(1, n):
        upd = sv[i] > s1
        i1 = jnp.where(upd, i, i1)
        s1 = jnp.where(upd, sv[i], s1)
        a1 = jnp.where(upd, av[i], a1)
    i2 = jnp.zeros_like(g_best)
    s2 = jnp.full(s1.shape, -jnp.inf, F32)
    a2 = jnp.zeros_like(a1)
    for i in range(n):
        cand = jnp.where(i1 == i, -jnp.inf, sv[i])
        upd = cand > s2
        i2 = jnp.where(upd, i, i2)
        s2 = jnp.where(upd, cand, s2)
        a2 = jnp.where(upd, av[i], a2)

    tot = a1 + a2
    e1 = g_best * n + i1
    e2 = g_best * n + i2
    idx_ref[0:1, :] = e1
    idx_ref[1:2, :] = e2
    wgt_ref[0:1, :] = a1 / tot
    wgt_ref[1:2, :] = a2 / tot

    @pl.when(pl.program_id(0) == 0)
    def _():
        cnt_ref[...] = jnp.zeros_like(cnt_ref)

    e_iota = lax.broadcasted_iota(I32, sel.shape, 0)
    oh1 = jnp.where(e_iota == e1, 1.0, 0.0)
    oh2 = jnp.where(e_iota == e2, 1.0, 0.0)
    both = oh1 + oh2
    before = cnt_ref[...] + _dot(both.astype(BF16), upper_ref[...])
    rank_ref[0:1, :] = jnp.sum(oh1 * before, axis=0, keepdims=True).astype(I32)
    rank_ref[1:2, :] = jnp.sum(oh2 * before, axis=0, keepdims=True).astype(I32)
    cnt_ref[...] = cnt_ref[...] + jnp.sum(both, axis=1, keepdims=True)


def _router(x2d, w_router, b_router):
    t, d = x2d.shape
    tb = min(ROUTER_BLOCK, t)
    wt = w_router.T.astype(F32)
    wh = wt.astype(BF16)
    wl = (wt - wh.astype(F32)).astype(BF16)
    r = np.arange(tb)
    upper = jnp.asarray(r[:, None] < r[None, :], BF16)
    pair_out = pl.BlockSpec((TOP_K, tb), lambda i: (0, i))
    fixed = lambda i: (0, 0)
    return pl.pallas_call(
        _router_kernel,
        out_shape=(jax.ShapeDtypeStruct((TOP_K, t), I32), jax.ShapeDtypeStruct((TOP_K, t), F32),
                   jax.ShapeDtypeStruct((TOP_K, t), I32), jax.ShapeDtypeStruct((N_EXPERTS, 1), F32)),
        grid=(t // tb,),
        in_specs=[pl.BlockSpec((tb, d), lambda i: (i, 0)),
                  pl.BlockSpec((N_EXPERTS, d), fixed), pl.BlockSpec((N_EXPERTS, d), fixed),
                  pl.BlockSpec((N_EXPERTS, 1), fixed), pl.BlockSpec((tb, tb), fixed)],
        out_specs=(pair_out, pair_out, pair_out, pl.BlockSpec((N_EXPERTS, 1), fixed)),
        compiler_params=_cparams("arbitrary"),
        name="moe_router",
    )(x2d, wh, wl, b_router.reshape(-1, 1).astype(F32), upper)


def _route_tables(idx, rank, counts, tm, n_items):
    cnt = counts.reshape(-1).astype(I32)
    start = jnp.cumsum(cnt) - cnt
    experts = jnp.arange(N_EXPERTS, dtype=I32)
    pos = rank + jnp.sum(jnp.where(idx[..., None] == experts, start, 0), axis=-1)
    first_tile = start // tm
    n_e = jnp.where(cnt > 0, (start + cnt - 1) // tm - first_tile + 1, 0)
    item_end = jnp.cumsum(n_e)
    item = jnp.arange(n_items, dtype=I32)
    used = item < item_end[-1]
    e_i = jnp.minimum(jnp.sum((item[:, None] >= item_end[None, :]).astype(I32), axis=1), N_EXPERTS - 1)
    e_last = jnp.max(jnp.where(cnt > 0, experts, 0))
    e_i = jnp.where(used, e_i, e_last)
    tile_i = jnp.where(used, first_tile[e_i] + item - (item_end - n_e)[e_i], (TOP_K * idx.shape[1]) // tm - 1)
    lo = jnp.where(used, jnp.maximum(start[e_i], tile_i * tm) - tile_i * tm, 0)
    hi = jnp.where(used, jnp.minimum(start[e_i] + cnt[e_i], (tile_i + 1) * tm) - tile_i * tm, 0)
    return pos.astype(I32), tile_i.astype(I32), e_i, lo.astype(I32), hi.astype(I32)


def _row_of(ref, row):
    return ref.at[lax.shift_right_logical(row, SUBLANES.bit_length() - 1), pl.ds(row & (SUBLANES - 1), 1)]


def _dispatch_kernel(pos_ref, x_ref, xs_hbm, stage, sem, *, tb, n_tok):
    i = pl.program_id(0)
    slot = i % 2
    stage[slot] = x_ref[...].reshape(stage.shape[1:])

    def issue(grp, carry):
        dst = [[pos_ref[s * n_tok + i * tb + grp * SUBLANES + u] for s in range(TOP_K)] for u in range(SUBLANES)]
        for u in range(SUBLANES):
            for s in range(TOP_K):
                row = dst[u][s]
                pltpu.make_async_copy(stage.at[slot, grp, pl.ds(u, 1)], _row_of(xs_hbm, row), sem.at[slot]).start()
        return carry

    lax.fori_loop(0, tb // SUBLANES, issue, 0)

    def wait_block(s):
        for _ in range(TOP_K):
            pltpu.make_async_copy(stage.at[s], xs_hbm.at[pl.ds(0, tb // SUBLANES)], sem.at[s]).wait()

    @pl.when(i >= 1)
    def _():
        wait_block(1 - slot)

    @pl.when(i == pl.num_programs(0) - 1)
    def _():
        wait_block(slot)


def _dispatch(x2d, pos_flat, tb):
    t, d = x2d.shape
    grid_spec = pltpu.PrefetchScalarGridSpec(
        num_scalar_prefetch=1, grid=(t // tb,),
        in_specs=[pl.BlockSpec((tb, d), lambda i, pos: (i, 0))],
        out_specs=pl.BlockSpec(memory_space=pl.ANY),
        scratch_shapes=[pltpu.VMEM((2, tb // SUBLANES, SUBLANES, d), F32), pltpu.SemaphoreType.DMA((2,))])
    return pl.pallas_call(
        functools.partial(_dispatch_kernel, tb=tb, n_tok=t),
        out_shape=jax.ShapeDtypeStruct((TOP_K * t // SUBLANES, SUBLANES, d), F32),
        grid_spec=grid_spec,
        compiler_params=_cparams("arbitrary"),
        name="moe_dispatch",
    )(pos_flat, x2d).reshape(TOP_K * t, d)


def _experts_kernel(tile_ref, exp_ref, lo_ref, hi_ref, x_ref, w1_ref, w3_ref, w2_ref, o_ref, w1b, w3b, w2b):
    i = pl.program_id(0)
    prev = jnp.maximum(i - 1, 0)

    @pl.when(jnp.logical_or(i == 0, exp_ref[i] != exp_ref[prev]))
    def _():
        w1b[...] = w1_ref[...].astype(BF16)
        w3b[...] = w3_ref[...].astype(BF16)
        w2b[...] = w2_ref[...].astype(BF16)

    lo = lo_ref[i]
    hi = hi_ref[i]
    first = jnp.logical_or(i == 0, tile_ref[i] != tile_ref[prev])

    @pl.when(hi > lo)
    def _():
        x = x_ref[...].astype(BF16)
        row = lax.broadcasted_iota(I32, (x.shape[0], 1), 0)
        mine = jnp.where(jnp.logical_and(row >= lo, row < hi), 1.0, 0.0)
        hid = _silu(_dot(x, w1b[...])) * _dot(x, w3b[...]) * mine
        y = _dot(hid.astype(BF16), w2b[...])

        @pl.when(first)
        def _():
            o_ref[...] = y

        @pl.when(jnp.logical_not(first))
        def _():
            o_ref[...] += y


def _experts(xs, tile_i, exp_i, lo, hi, w1, w3, w2, layer, tm):
    n, d = xs.shape
    f = w1.shape[-1]
    by_tile = lambda i, tile, exp, lo, hi: (tile[i], 0)
    by_exp = lambda i, tile, exp, lo, hi: (layer, exp[i], 0, 0)
    grid_spec = pltpu.PrefetchScalarGridSpec(
        num_scalar_prefetch=4, grid=(tile_i.shape[0],),
        in_specs=[pl.BlockSpec((tm, d), by_tile),
                  pl.BlockSpec((None, None, d, f), by_exp), pl.BlockSpec((None, None, d, f), by_exp),
                  pl.BlockSpec((None, None, f, d), by_exp)],
        out_specs=pl.BlockSpec((tm, d), by_tile),
        scratch_shapes=[pltpu.VMEM((d, f), BF16), pltpu.VMEM((d, f), BF16), pltpu.VMEM((f, d), BF16)])
    return pl.pallas_call(
        _experts_kernel,
        out_shape=jax.ShapeDtypeStruct((n, d), F32),
        grid_spec=grid_spec,
        compiler_params=_cparams("arbitrary"),
        name="moe_experts",
    )(tile_i, exp_i, lo, hi, xs, w1, w3, w2)


def _combine_ln_kernel(pos_ref, x_ref, w_ref, g_ref, b_ref, ys_hbm, o_ref, gbuf, sem, *, tb, n_tok):
    i = pl.program_id(0)
    slot = i % 2

    def start(blk, s):
        def issue(grp, carry):
            src = [[pos_ref[k * n_tok + blk * tb + grp * SUBLANES + u] for k in range(TOP_K)] for u in range(SUBLANES)]
            for u in range(SUBLANES):
                for k in range(TOP_K):
                    pltpu.make_async_copy(_row_of(ys_hbm, src[u][k]), gbuf.at[s, k, grp, pl.ds(u, 1)], sem.at[s]).start()
            return carry
        lax.fori_loop(0, tb // SUBLANES, issue, 0)

    @pl.when(i == 0)
    def _():
        start(0, 0)

    @pl.when(i + 1 < pl.num_programs(0))
    def _():
        start(i + 1, 1 - slot)

    for k in range(TOP_K):
        pltpu.make_async_copy(ys_hbm.at[pl.ds(0, tb // SUBLANES)], gbuf.at[slot, k], sem.at[slot]).wait()
    w = w_ref[...]
    rows = x_ref.shape
    y = gbuf[slot, 0].reshape(rows) * w[:, 0:1] + gbuf[slot, 1].reshape(rows) * w[:, 1:2]
    o_ref[...] = _layer_norm_rows(DN_ALPHA * x_ref[...] + y, g_ref[...], b_ref[...])


def _combine_ln(x2d, ys, pos_flat, wgt_cols, g, b, tb):
    t, d = x2d.shape
    row = lambda i, pos: (i, 0)
    fixed = lambda i, pos: (0, 0)
    grid_spec = pltpu.PrefetchScalarGridSpec(
        num_scalar_prefetch=1, grid=(t // tb,),
        in_specs=[pl.BlockSpec((tb, d), row), pl.BlockSpec((tb, TOP_K), row),
                  pl.BlockSpec((1, d), fixed), pl.BlockSpec((1, d), fixed),
                  pl.BlockSpec(memory_space=pl.ANY)],
        out_specs=pl.BlockSpec((tb, d), row),
        scratch_shapes=[pltpu.VMEM((2, TOP_K, tb // SUBLANES, SUBLANES, d), F32), pltpu.SemaphoreType.DMA((2,))])
    return pl.pallas_call(
        functools.partial(_combine_ln_kernel, tb=tb, n_tok=t),
        out_shape=jax.ShapeDtypeStruct((t, d), F32),
        grid_spec=grid_spec,
        compiler_params=_cparams("arbitrary"),
        name="moe_combine_ln",
    )(pos_flat, x2d, wgt_cols, g.reshape(1, d), b.reshape(1, d), ys.reshape(-1, SUBLANES, d))


def _moe_ln(x2d, w_router, b_router, w1, w3, w2, layer, g, b):
    t = x2d.shape[0]
    tm = min(MOE_TILE, t)
    idx, wgt, rank, counts = _router(x2d, w_router, b_router)
    n_items = TOP_K * t // tm + N_EXPERTS - 1
    pos, tile_i, exp_i, lo, hi = _route_tables(idx, rank, counts, tm, n_items)
    pos_flat = pos.reshape(-1)
    xs = _dispatch(x2d, pos_flat, tm)
    ys = _experts(xs, tile_i, exp_i, lo, hi, w1, w3, w2, layer, tm)
    return _combine_ln(x2d, ys, pos_flat, wgt.T, g, b, tm)


def _gate_rows(col, batch, seq):
    tb = min(TIME_BLOCK, seq)
    return col.reshape(batch, seq, D_HEADS).transpose(0, 2, 1).reshape(batch, D_HEADS, seq // tb, tb // CHUNK, CHUNK)


def _mix_cd(x2d, w_in, rel_table, ckv_g, w_uk, w_uv, conv_w, gate_b, d_norm_g, batch, seq):
    t = x2d.shape[0]
    h = _project(x2d, *_pack_cd(w_in), min(1024, t), PROJ_TILE)
    ckv, ckvt, kdup = _dsa_prep(h, ckv_g, t)
    mask = _indexer(h, kdup, batch, seq)
    oc = _dsa_attention(h, mask, ckv, ckvt, w_uk, w_uv, rel_table, batch, seq)
    tail = h[:, CD_TAIL:CD_TAIL + LANES]
    ig_rows = _gate_rows(tail[:, TAIL_DI:TAIL_DI + D_HEADS], batch, seq)
    fg_rows = _gate_rows(tail[:, TAIL_DF:TAIL_DF + D_HEADS], batch, seq)
    od = _mlstm(h, conv_w, ig_rows, fg_rows, gate_b, d_norm_g, batch, seq,
                q_off=CD_DQ // (D_HEADS * D_DK), k_off=CD_DK // (D_HEADS * D_DK),
                v_off=CD_DV // (D_HEADS * D_DV), g_off=CD_DOG // (D_HEADS * D_DV))
    return oc, od


def kernel(x, w_in_ab, w_out_ab, hgrn_lb_logits, a_norm_g, gla_wa2, gla_ba2, b_norm_g, w_in_cd, w_out_cd,
           ckv_norm_g, w_uk, w_uv, mlstm_conv_w, mlstm_gate_b, d_norm_g, rel_table, w_router, b_router,
           moe_w1, moe_w3, moe_w2, ln_g, ln_b):
    batch, seq, d = x.shape
    x2d = x.reshape(batch * seq, d)
    for layer in range(DEPTH):
        li = layer // 2
        if layer % 2 == 0:
            mix_a, mix_b = _mix_ab(x2d, w_in_ab[li], hgrn_lb_logits, li, a_norm_g[li], gla_wa2[li], gla_ba2[li],
                                   b_norm_g[li], batch, seq)
            w_out = w_out_ab[li]
        else:
            mix_a, mix_b = _mix_cd(x2d, w_in_cd[li], rel_table, ckv_norm_g[li], w_uk[li], w_uv[li],
                                   mlstm_conv_w[li], mlstm_gate_b[li], d_norm_g[li], batch, seq)
            w_out = w_out_cd[li]
        ka = mix_a.shape[1]
        x2d = _outproj_ln(x2d, mix_a, mix_b, w_out[:ka].astype(BF16), w_out[ka:].astype(BF16),
                          ln_g[layer, 0], ln_b[layer, 0])
        x2d = _moe_ln(x2d, w_router, b_router, moe_w1, moe_w3, moe_w2, layer,
                      ln_g[layer, 1], ln_b[layer, 1])
    return x2d.reshape(batch, seq, d)
```

```python
import functools
import math

import numpy as np
import jax
import jax.numpy as jnp
from jax import lax
from jax.experimental import pallas as pl
from jax.experimental.pallas import tpu as pltpu

F32 = jnp.float32
BF16 = jnp.bfloat16
I32 = jnp.int32

D_MODEL = 2048
DEPTH = 2
A_HEADS, A_DK, A_DV = 8, 128, 128
B_HEADS, B_DK, B_DV = 4, 128, 256
B_GATE_RANK, B_GATE_TAU = 16, 16.0
C_HEADS, C_DH, C_DLAT = 8, 128, 256
IDX_HEADS, IDX_DIM = 16, 64
TOPK_MAX = 256
D_HEADS, D_DK, D_DV = 4, 128, 256
CONV_K = 4
REL_BUCKETS, REL_MAX_DIST = 32, 128
N_EXPERTS, N_GROUPS, TOP_K, D_EXPERT = 16, 4, 2, 512
EXPERTS_PER_GROUP = N_EXPERTS // N_GROUPS
DN_ALPHA = (2 * DEPTH) ** 0.25
EPS = 1e-5

LANES = 128
SUBLANES = 8
VMEM_LIMIT = 56 * 1024 * 1024

CHUNK = 128
N_LEVELS = 7
LOG2E = math.log2(math.e)
TIME_BLOCK = 512
QB = 128
NEG = -1e30
INT_MIN = -2 ** 31
INT_MAX = 2 ** 31 - 1


def _cparams(*sem):
    return pltpu.CompilerParams(dimension_semantics=sem, vmem_limit_bytes=VMEM_LIMIT)


def _dot(a, b):
    return jnp.dot(a, b, preferred_element_type=F32)


def _dot_nt(a, b):
    return lax.dot_general(a, b, (((1,), (1,)), ((), ())), preferred_element_type=F32)


def _dot_tn(a, b):
    return lax.dot_general(a, b, (((0,), (0,)), ((), ())), preferred_element_type=F32)


def _split3(a):
    hi = a.astype(BF16)
    r1 = a - hi.astype(F32)
    mid = r1.astype(BF16)
    lo = (r1 - mid.astype(F32)).astype(BF16)
    return hi, mid, lo


def _dot01(m01, a):
    hi, mid, lo = _split3(a)
    return _dot(m01, hi) + _dot(m01, mid) + _dot(m01, lo)


def _sigmoid(x):
    return 1.0 / (1.0 + jnp.exp(-x))


def _silu(x):
    return x * _sigmoid(x)


def _log_sigmoid(x):
    return jnp.minimum(x, 0.0) - jnp.log(1.0 + jnp.exp(-jnp.abs(x)))


def _proj_kernel(x_ref, w_ref, wt_ref, o_ref, xb_ref):
    j = pl.program_id(1)

    @pl.when(j == 0)
    def _():
        xb_ref[...] = x_ref[...].astype(BF16)

    @pl.when(j < pl.num_programs(1) - 1)
    def _():
        o_ref[...] = _dot(xb_ref[...], w_ref[...])

    @pl.when(j == pl.num_programs(1) - 1)
    def _():
        o_ref[...] = _dot(xb_ref[...], wt_ref[...])


def _cast_kernel(w_ref, o_ref):
    o_ref[...] = w_ref[...].astype(o_ref.dtype)


def _cast_columns(w, n_cols, tn):
    k = w.shape[0]
    return pl.pallas_call(
        _cast_kernel,
        out_shape=jax.ShapeDtypeStruct((k, n_cols), BF16),
        grid=(n_cols // tn,),
        in_specs=[pl.BlockSpec((k, tn), lambda j: (0, j))],
        out_specs=pl.BlockSpec((k, tn), lambda j: (0, j)),
        compiler_params=_cparams("parallel"),
        name="weight_cast",
    )(w)


def _project(x, w, w_tail, tm, tn):
    m, k = x.shape
    n_main = w.shape[1] // tn
    return pl.pallas_call(
        _proj_kernel,
        out_shape=jax.ShapeDtypeStruct((m, (n_main + 1) * tn), F32),
        grid=(m // tm, n_main + 1),
        in_specs=[pl.BlockSpec((tm, k), lambda i, j: (i, 0)),
                  pl.BlockSpec((k, tn), lambda i, j: (0, jnp.minimum(j, n_main - 1))),
                  pl.BlockSpec((k, tn), lambda i, j: (0, 0))],
        out_specs=pl.BlockSpec((tm, tn), lambda i, j: (i, j)),
        scratch_shapes=[pltpu.VMEM((tm, k), BF16)],
        compiler_params=_cparams("parallel", "arbitrary"),
        name="in_proj",
    )(x, w, w_tail)


def _layer_norm_rows(z, g, b):
    mu = jnp.mean(z, axis=-1, keepdims=True)
    zc = z - mu
    var = jnp.mean(zc * zc, axis=-1, keepdims=True)
    return zc * lax.rsqrt(var + EPS) * g + b


def _outproj_ln_kernel(x_ref, ma_ref, mb_ref, wa_ref, wb_ref, g_ref, b_ref, o_ref):
    mixed = _dot(ma_ref[...].astype(BF16), wa_ref[...]) + _dot(mb_ref[...].astype(BF16), wb_ref[...])
    o_ref[...] = _layer_norm_rows(DN_ALPHA * x_ref[...] + mixed, g_ref[...], b_ref[...])


def _outproj_ln(x, mix_a, mix_b, w_a, w_b, g, b, tm=512):
    m, d = x.shape
    ka, kb = mix_a.shape[1], mix_b.shape[1]
    row = lambda i: (i, 0)
    fixed = lambda i: (0, 0)
    return pl.pallas_call(
        _outproj_ln_kernel,
        out_shape=jax.ShapeDtypeStruct((m, d), F32),
        grid=(m // tm,),
        in_specs=[pl.BlockSpec((tm, d), row), pl.BlockSpec((tm, ka), row), pl.BlockSpec((tm, kb), row),
                  pl.BlockSpec((ka, d), fixed), pl.BlockSpec((kb, d), fixed),
                  pl.BlockSpec((1, d), fixed), pl.BlockSpec((1, d), fixed)],
        out_specs=pl.BlockSpec((tm, d), row),
        compiler_params=_cparams("parallel"),
        name="out_proj_ln",
    )(x, mix_a, mix_b, w_a, w_b, g.reshape(1, d), b.reshape(1, d))


def _chunk_constants():
    t = np.arange(CHUNK)
    tri = (t[:, None] >= t[None, :]).astype(np.float32)
    pair, odd = [], []
    for lev in range(1, N_LEVELS + 1):
        c = CHUNK >> lev
        pair.append((t[:, None] // (2 * c) == t[None, :] // (2 * c)).astype(np.float32))
        odd.append(np.broadcast_to((((t // c) & 1) == 1).astype(np.float32)[:, None], (CHUNK, LANES)))
    pair.append(np.eye(CHUNK, dtype=np.float32))
    return jnp.asarray(tri, BF16), jnp.asarray(np.stack(pair), F32), jnp.asarray(np.stack(odd), F32)


def _level_log_decay(la, bcum, lev):
    c = CHUNK >> lev
    if 2 * c >= SUBLANES:
        mids = [jnp.broadcast_to(bcum[g * 2 * c + c - 1:g * 2 * c + c], (2 * c, bcum.shape[1]))
                for g in range(CHUNK // (2 * c))]
        return -jnp.abs(bcum - (mids[0] if len(mids) == 1 else jnp.concatenate(mids, axis=0)))
    r = lax.broadcasted_iota(I32, la.shape, 0) & (2 * c - 1)
    if c == 2:
        nxt = pltpu.roll(la, CHUNK - 1, axis=0)
        prv = pltpu.roll(la, 1, axis=0)
        return jnp.where(r == 0, nxt, jnp.where(r == 1, 0.0, jnp.where(r == 2, la, la + prv)))
    return jnp.where(r == 1, la, 0.0)


def _glr_chunk(q, k, v, la, st_ref, cum_ref, pair_ref, odd_ref):
    la = la * LOG2E
    bcum = _dot01(cum_ref[...], la)
    attn = pair_ref[N_LEVELS] * _dot_nt(q.astype(BF16), k.astype(BF16))
    for lev in range(1, N_LEVELS + 1):
        e = jnp.exp2(_level_log_decay(la, bcum, lev))
        eq = e * odd_ref[lev - 1]
        ql = (q * eq).astype(BF16)
        kl = (k * (e - eq)).astype(BF16)
        attn = attn + pair_ref[lev - 1] * _dot_nt(ql, kl)
    st = st_ref[...]
    o = _dot_nt((q * jnp.exp2(bcum)).astype(BF16), st.astype(BF16)) + _dot(attn.astype(BF16), v.astype(BF16))
    b_last = bcum[CHUNK - 1:CHUNK]
    kdec = (k * jnp.exp2(b_last - bcum)).astype(BF16)
    st_ref[...] = st * jnp.exp2(b_last) + _dot_tn(v.astype(BF16), kdec)
    return o


def _rms_gate(o, g, gate):
    ms = jnp.mean(o * o, axis=-1, keepdims=True)
    return o * lax.rsqrt(ms + EPS) * g * gate


HEAD_GROUP = 4


def _hgrn2_kernel(q_ref, f_ref, i_ref, g_ref, lb_ref, ng_ref, cum_ref, pair_ref, odd_ref, o_ref, st_ref):
    @pl.when(pl.program_id(2) == 0)
    def _():
        st_ref[...] = jnp.zeros_like(st_ref)

    def body(c, carry):
        rows = pl.ds(pl.multiple_of(c * CHUNK, CHUNK), CHUNK)
        for hh in range(HEAD_GROUP):
            ck = slice(A_DK * hh, A_DK * (hh + 1))
            cv = slice(A_DV * hh, A_DV * (hh + 1))
            lb = lb_ref[:, ck]
            f = lb + (1.0 - lb) * _sigmoid(f_ref[rows, ck])
            o = _glr_chunk(_silu(q_ref[rows, ck]), 1.0 - f, i_ref[rows, cv], jnp.log(f),
                           st_ref.at[hh], cum_ref, pair_ref, odd_ref)
            o_ref[rows, cv] = _rms_gate(o, ng_ref[...], _silu(g_ref[rows, cv]))
        return carry

    lax.fori_loop(0, q_ref.shape[0] // CHUNK, body, 0)


def _gla_kernel(q_ref, k_ref, v_ref, g_ref, r_ref, wa_ref, ba_ref, ng_ref, cum_ref, pair_ref, odd_ref,
                o_ref, st_ref):
    @pl.when(pl.program_id(2) == 0)
    def _():
        st_ref[...] = jnp.zeros_like(st_ref)

    def body(c, carry):
        rows = pl.ds(pl.multiple_of(c * CHUNK, CHUNK), CHUNK)
        pre = _dot(r_ref[rows, :].astype(BF16), wa_ref[...]) + ba_ref[...]
        la = _log_sigmoid(pre) * (1.0 / B_GATE_TAU)
        for hh in range(HEAD_GROUP):
            ck = slice(B_DK * hh, B_DK * (hh + 1))
            cv = slice(B_DV * hh, B_DV * (hh + 1))
            o = _glr_chunk(q_ref[rows, ck] * (B_DK ** -0.5), k_ref[rows, ck], v_ref[rows, cv], la[:, ck],
                           st_ref.at[hh], cum_ref, pair_ref, odd_ref)
            o_ref[rows, cv] = _rms_gate(o, ng_ref[...], _silu(g_ref[rows, cv]))
        return carry

    lax.fori_loop(0, q_ref.shape[0] // CHUNK, body, 0)


def _const_spec(arr):
    nd = arr.ndim
    return pl.BlockSpec(arr.shape, lambda *_: (0,) * nd)


def _hgrn2(h, lb, norm_g, batch, seq):
    tb = min(TIME_BLOCK, seq)
    nt = seq // tb
    ng = A_HEADS // HEAD_GROUP
    wk, wv = HEAD_GROUP * A_DK, HEAD_GROUP * A_DV
    consts = _chunk_constants()
    col = lambda seg, w: pl.BlockSpec((tb, w), lambda b, g, t, seg=seg: (b * nt + t, seg * ng + g))
    return pl.pallas_call(
        _hgrn2_kernel,
        out_shape=jax.ShapeDtypeStruct((batch * seq, A_HEADS * A_DV), F32),
        grid=(batch, ng, nt),
        in_specs=[col(0, wk), col(1, wk), col(2, wv), col(3, wv),
                  pl.BlockSpec((1, wk), lambda b, g, t: (0, g)),
                  pl.BlockSpec((1, A_DV), lambda b, g, t: (0, 0))] + [_const_spec(c) for c in consts],
        out_specs=pl.BlockSpec((tb, wv), lambda b, g, t: (b * nt + t, g)),
        scratch_shapes=[pltpu.VMEM((HEAD_GROUP, A_DV, A_DK), F32)],
        compiler_params=_cparams("parallel", "parallel", "arbitrary"),
        name="hgrn2",
    )(h, h, h, h, lb.reshape(1, -1), norm_g.reshape(1, -1), *consts)


def _gla(h, wa2p, ba2, norm_g, batch, seq, q_off, k_off, v_off, g_off, r_off):
    tb = min(TIME_BLOCK, seq)
    nt = seq // tb
    wk, wv = B_HEADS * B_DK, B_HEADS * B_DV
    consts = _chunk_constants()
    col = lambda off, w: pl.BlockSpec((tb, w), lambda b, g, t, off=off: (b * nt + t, off))
    fixed = lambda b, g, t: (0, 0)
    return pl.pallas_call(
        _gla_kernel,
        out_shape=jax.ShapeDtypeStruct((batch * seq, wv), F32),
        grid=(batch, 1, nt),
        in_specs=[col(q_off, wk), col(k_off, wk), col(v_off, wv), col(g_off, wv), col(r_off, LANES),
                  pl.BlockSpec((LANES, wk), fixed), pl.BlockSpec((1, wk), fixed),
                  pl.BlockSpec((1, B_DV), fixed)] + [_const_spec(c) for c in consts],
        out_specs=pl.BlockSpec((tb, wv), lambda b, g, t: (b * nt + t, 0)),
        scratch_shapes=[pltpu.VMEM((B_HEADS, B_DV, B_DK), F32)],
        compiler_params=_cparams("parallel", "parallel", "arbitrary"),
        name="gla",
    )(h, h, h, h, h, wa2p, ba2.reshape(1, -1), norm_g.reshape(1, -1), *consts)


PROJ_TILE = 512
AB_MAIN = 4 * A_HEADS * A_DK + 2 * B_HEADS * B_DK + 2 * B_HEADS * B_DV
AB_PAD = AB_MAIN + PROJ_TILE


def _mix_ab(x2d, w_in, lb_logits, li, a_norm_g, wa2, ba2, b_norm_g, batch, seq):
    d = x2d.shape[1]
    w_tail = jnp.pad(w_in[:, AB_MAIN:], ((0, 0), (0, AB_PAD - w_in.shape[1]))).astype(BF16)
    h = _project(x2d, _cast_columns(w_in, AB_MAIN, PROJ_TILE), w_tail, min(1024, x2d.shape[0]), PROJ_TILE)
    lb = jnp.cumsum(jax.nn.softmax(lb_logits.astype(F32), axis=0), axis=0)[li]
    oa = _hgrn2(h, lb, a_norm_g, batch, seq)
    wa2p = jnp.concatenate([wa2, jnp.zeros((LANES - B_GATE_RANK, wa2.shape[1]), F32)], axis=0).astype(BF16)
    a_cols = 4 * A_HEADS * A_DK
    wk, wv = B_HEADS * B_DK, B_HEADS * B_DV
    ob = _gla(h, wa2p, ba2, b_norm_g, batch, seq, q_off=a_cols // wk, k_off=a_cols // wk + 1,
              v_off=(a_cols + 2 * wk) // wv, g_off=(a_cols + 2 * wk) // wv + 1, r_off=AB_MAIN // LANES)
    return oa, ob


CONV_HALO = 8


def _causal_conv(x_ref, w_ref, buf_ref, tail_ref):
    tb = x_ref.shape[0]
    x = x_ref[...]
    buf_ref[0:CONV_HALO, :] = tail_ref[...]
    buf_ref[CONV_HALO:CONV_HALO + tb, :] = x
    tail_ref[...] = x[tb - CONV_HALO:tb]
    y = w_ref[CONV_K - 1:CONV_K, :] * x
    for j in range(CONV_K - 1):
        y = y + w_ref[j:j + 1, :] * buf_ref[pl.ds(CONV_HALO - (CONV_K - 1) + j, tb), :]
    return y


def _row_to_col(row, eye):
    return jnp.sum(jnp.where(eye, row, 0.0), axis=1, keepdims=True)


def _mlstm_kernel(q_ref, k_ref, v_ref, og_ref, wq_ref, wk_ref, ig_ref, fg_ref, gb_ref, ng_ref, tri_ref,
                  o_ref, ct_ref, n_ref, m_ref, qt_ref, kt_ref, qs_ref, ks_ref, buf_ref):
    @pl.when(pl.program_id(2) == 0)
    def _():
        ct_ref[...] = jnp.zeros_like(ct_ref)
        n_ref[...] = jnp.zeros_like(n_ref)
        m_ref[...] = jnp.zeros_like(m_ref)
        qt_ref[...] = jnp.zeros_like(qt_ref)
        kt_ref[...] = jnp.zeros_like(kt_ref)

    qs_ref[...] = _silu(_causal_conv(q_ref, wq_ref, buf_ref, qt_ref))
    ks_ref[...] = _silu(_causal_conv(k_ref, wk_ref, buf_ref, kt_ref)) * (D_DK ** -0.5)

    r_i = lax.broadcasted_iota(I32, (CHUNK, CHUNK), 0)
    c_i = lax.broadcasted_iota(I32, (CHUNK, CHUNK), 1)
    eye = r_i == c_i
    causal = r_i >= c_i

    def body(c, carry):
        rows = pl.ds(pl.multiple_of(c * CHUNK, CHUNK), CHUNK)
        tri = tri_ref[...]
        for hh in range(D_HEADS):
            ck = slice(D_DK * hh, D_DK * (hh + 1))
            cv = slice(D_DV * hh, D_DV * (hh + 1))
            q = qs_ref[rows, ck]
            k = ks_ref[rows, ck]
            v = v_ref[rows, cv].astype(BF16)
            qb = q.astype(BF16)
            ig_row = ig_ref[hh, pl.ds(c, 1), :] + gb_ref[0, hh]
            lf_row = _log_sigmoid(fg_ref[hh, pl.ds(c, 1), :] + gb_ref[1, hh])
            hi, mid, lo = _split3(lf_row)
            bcum_row = _dot(hi, tri) + _dot(mid, tri) + _dot(lo, tri)
            bcum_col = _row_to_col(bcum_row, eye)
            ig_col = _row_to_col(ig_row, eye)
            m_prev = m_ref[hh, :, 0:1]
            log_w = jnp.where(causal, bcum_col - bcum_row + ig_row, NEG)
            log_inter = bcum_col + m_prev
            m_t = jnp.maximum(jnp.max(log_w, axis=1, keepdims=True), log_inter)
            s = _dot_nt(qb, k.astype(BF16)) * jnp.exp(log_w - m_t)
            w_inter = jnp.exp(log_inter - m_t)
            num = _dot(s.astype(BF16), v) + w_inter * _dot_nt(qb, ct_ref[hh].astype(BF16))
            qn = jnp.sum(s, axis=1, keepdims=True) + w_inter * jnp.sum(q * n_ref[hh], axis=1, keepdims=True)
            h = num / jnp.maximum(jnp.abs(qn), jnp.exp(-m_t))
            o_ref[rows, cv] = _rms_gate(h, ng_ref[...], _sigmoid(og_ref[rows, cv]))
            b_last = bcum_row[:, CHUNK - 1:CHUNK]
            log_u = b_last - bcum_col + ig_col
            m_new = jnp.maximum(b_last + m_prev, jnp.max(log_u, axis=0, keepdims=True))
            decay = jnp.exp(b_last + m_prev - m_new)
            ku = k * jnp.exp(log_u - m_new)
            ct_ref[hh] = decay * ct_ref[hh] + _dot_tn(v, ku.astype(BF16))
            n_ref[hh] = decay * n_ref[hh] + jnp.sum(ku, axis=0, keepdims=True)
            m_ref[hh] = jnp.broadcast_to(m_new, (1, LANES))
        return carry

    lax.fori_loop(0, q_ref.shape[0] // CHUNK, body, 0)


def _mlstm(h, conv_w, ig_rows, fg_rows, gate_b, norm_g, batch, seq, q_off, k_off, v_off, g_off):
    tb = min(TIME_BLOCK, seq)
    nt = seq // tb
    nc = tb // CHUNK
    wk, wv = D_HEADS * D_DK, D_HEADS * D_DV
    t = np.arange(CHUNK)
    tri = jnp.asarray(t[:, None] <= t[None, :], BF16)
    gb = jnp.broadcast_to(gate_b.reshape(2, D_HEADS, 1, 1), (2, D_HEADS, 1, CHUNK)).astype(F32)
    col = lambda off, w: pl.BlockSpec((tb, w), lambda b, g, t, off=off: (b * nt + t, off))
    gate = pl.BlockSpec((None, D_HEADS, None, nc, CHUNK), lambda b, g, t: (b, 0, t, 0, 0))
    fixed = lambda b, g, t: (0, 0)
    return pl.pallas_call(
        _mlstm_kernel,
        out_shape=jax.ShapeDtypeStruct((batch * seq, wv), F32),
        grid=(batch, 1, nt),
        in_specs=[col(q_off, wk), col(k_off, wk), col(v_off, wv), col(g_off, wv),
                  pl.BlockSpec((CONV_K, wk), lambda b, g, t: (0, 0)),
                  pl.BlockSpec((CONV_K, wk), lambda b, g, t: (0, 1)),
                  gate, gate,
                  pl.BlockSpec((2, D_HEADS, 1, CHUNK), lambda b, g, t: (0, 0, 0, 0)),
                  pl.BlockSpec((1, D_DV), fixed), pl.BlockSpec((CHUNK, CHUNK), fixed)],
        out_specs=pl.BlockSpec((tb, wv), lambda b, g, t: (b * nt + t, 0)),
        scratch_shapes=[pltpu.VMEM((D_HEADS, D_DV, D_DK), F32), pltpu.VMEM((D_HEADS, 1, D_DK), F32),
                        pltpu.VMEM((D_HEADS, 1, LANES), F32),
                        pltpu.VMEM((CONV_HALO, wk), F32), pltpu.VMEM((CONV_HALO, wk), F32),
                        pltpu.VMEM((tb, wk), F32), pltpu.VMEM((tb, wk), F32),
                        pltpu.VMEM((tb + CONV_HALO, wk), F32)],
        compiler_params=_cparams("parallel", "parallel", "arbitrary"),
        name="mlstm",
    )(h, h, h, h, conv_w, conv_w, ig_rows, fg_rows, gb, norm_g.reshape(1, -1), tri)


CD_CQ, CD_IQ, CD_DQ, CD_DK, CD_DV, CD_DOG, CD_CKV, CD_TAIL = 0, 1024, 2048, 2560, 3072, 4096, 5120, 5376
CD_PAD = 5632
TAIL_IK, TAIL_IW, TAIL_DI, TAIL_DF = 0, 64, 80, 84


def _pack_cd(w_in):
    cq, ckv, iq, ik, iw, dq, dk, dv, di, df, dog = jnp.split(
        w_in, [int(i) for i in np.cumsum(
            (C_HEADS * C_DH, C_DLAT, IDX_HEADS * IDX_DIM, IDX_DIM, IDX_HEADS, D_HEADS * D_DK, D_HEADS * D_DK,
             D_HEADS * D_DV, D_HEADS, D_HEADS))], axis=1)
    used = CD_TAIL + IDX_DIM + IDX_HEADS + 2 * D_HEADS
    pad = jnp.zeros((w_in.shape[0], CD_PAD - used), F32)
    main = jnp.concatenate([cq, iq, dq, dk, dv, dog], axis=1).astype(BF16)
    tail = jnp.concatenate([ckv, ik, iw, di, df, pad], axis=1).astype(BF16)
    return main, tail


KB = 2 * QB
IDX_GROUP = 4
ATT_GROUP = 4
SUM_ROWS = 16


def _dsa_prep_kernel(ckv_ref, tail_ref, g_ref, ckv_o, ckvt_o, kdup_o):
    c = ckv_ref[...]
    cn = c * lax.rsqrt(jnp.mean(c * c, axis=-1, keepdims=True) + EPS) * g_ref[...]
    ckv_o[...] = cn.astype(BF16)
    ckvt_o[0:C_DLAT, :] = cn.T.astype(BF16)
    ckvt_o[C_DLAT:C_DLAT + SUM_ROWS, :] = jnp.ones((SUM_ROWS, KB), BF16)
    tail = tail_ref[...]
    lane = lax.broadcasted_iota(I32, tail.shape, 1)
    kdup_o[...] = jnp.where(lane < IDX_DIM, tail, pltpu.roll(tail, IDX_DIM, axis=1)).astype(BF16)


def _dsa_prep(h, ckv_g, n_rows):
    nb = n_rows // KB
    return pl.pallas_call(
        _dsa_prep_kernel,
        out_shape=(jax.ShapeDtypeStruct((nb, KB, C_DLAT), BF16), jax.ShapeDtypeStruct((nb, C_DLAT + SUM_ROWS, KB), BF16),
                   jax.ShapeDtypeStruct((nb, KB, LANES), BF16)),
        grid=(nb,),
        in_specs=[pl.BlockSpec((KB, C_DLAT), lambda i: (i, CD_CKV // C_DLAT)),
                  pl.BlockSpec((KB, LANES), lambda i: (i, CD_TAIL // LANES)),
                  pl.BlockSpec((1, C_DLAT), lambda i: (0, 0))],
        out_specs=(pl.BlockSpec((None, KB, C_DLAT), lambda i: (i, 0, 0)),
                   pl.BlockSpec((None, C_DLAT + SUM_ROWS, KB), lambda i: (i, 0, 0)),
                   pl.BlockSpec((None, KB, LANES), lambda i: (i, 0, 0))),
        compiler_params=_cparams("parallel"),
        name="dsa_prep",
    )(h, h, ckv_g.reshape(1, -1))


def _sortable_key(x):
    b = lax.bitcast_convert_type(x, I32)
    key = b ^ ((b >> 31) & 0x7FFFFFFF)
    return jnp.where(key == -1, 0, key)


BISECT_STEPS = 4


def _indexer_kernel(iq_ref, tail_ref, kdup_ref, tri_ref, mask_ref, key_ref, wst_ref, *, k_sel):
    j = pl.program_id(1)
    nk = key_ref.shape[0]
    n_live = (j * QB + QB + KB - 1) // KB
    w_t = tail_ref[...].T
    lane = lax.broadcasted_iota(I32, (QB, LANES), 1)
    for p in range(IDX_HEADS // 2):
        pair = iq_ref[:, LANES * p:LANES * (p + 1)]
        g, r = divmod(2 * p, IDX_GROUP)
        wst_ref[g, r * QB:(r + 1) * QB, :] = jnp.where(lane < IDX_DIM, pair, 0.0).astype(BF16)
        wst_ref[g, (r + 1) * QB:(r + 2) * QB, :] = jnp.where(lane >= IDX_DIM, pair, 0.0).astype(BF16)

    s_loc = lax.broadcasted_iota(I32, (KB, QB), 0)
    t_abs = j * QB + lax.broadcasted_iota(I32, (1, QB), 1)

    def score_chunk(kc, carry):
        kd = kdup_ref[kc]
        acc = jnp.zeros((KB, QB), F32)
        for g in range(IDX_HEADS // IDX_GROUP):
            dots = _dot_nt(kd, wst_ref[g])
            for r in range(IDX_GROUP):
                row = TAIL_IW + g * IDX_GROUP + r
                acc = acc + jnp.maximum(dots[:, r * QB:(r + 1) * QB], 0.0) * w_t[row:row + 1, :]
        key = jnp.where(kc * KB + s_loc > t_abs, INT_MIN, _sortable_key(acc))
        key_ref[kc] = key
        key = key.reshape(KB // SUBLANES, SUBLANES, QB)
        k_max, k_min = carry
        k_max = jnp.maximum(k_max, jnp.max(key, axis=0))
        k_min = jnp.minimum(k_min, jnp.min(jnp.where(key == INT_MIN, INT_MAX, key), axis=0))
        return k_max, k_min

    def score_two(i, carry):
        return score_chunk(2 * i + 1, score_chunk(2 * i, carry))

    k_max, k_min = lax.fori_loop(0, (n_live + 1) // 2, score_two, (jnp.full((SUBLANES, QB), INT_MIN, I32),
                                                                   jnp.full((SUBLANES, QB), INT_MAX, I32)))
    for shift in (4, 2, 1):
        k_max = jnp.maximum(k_max, pltpu.roll(k_max, shift, axis=0))
        k_min = jnp.minimum(k_min, pltpu.roll(k_min, shift, axis=0))
    k_row = jnp.minimum(k_sel, j * QB + lax.broadcasted_iota(I32, (SUBLANES, QB), 1) + 1)

    def count(pred):
        def add(kc, acc):
            hit = jnp.where(pred(key_ref[kc].reshape(KB // SUBLANES, SUBLANES, QB)), 1, 0)
            return acc + jnp.sum(hit, axis=0)
        acc = lax.fori_loop(0, n_live, add, jnp.zeros((SUBLANES, QB), I32))
        for shift in (4, 2, 1):
            acc = acc + pltpu.roll(acc, shift, axis=0)
        return acc

    def unfinished(state):
        lo, hi = state
        return jnp.max(jnp.where(lo < hi, 1, 0)) > 0

    def halve(state):
        lo, hi = state
        mid = (lo >> 1) + (hi >> 1) + (((lo & 1) + (hi & 1) + 1) >> 1)
        cnt = count(lambda k: k >= mid)
        enough = cnt >= k_row
        lo_n = jnp.where(enough, mid, lo)
        hi_n = jnp.where(cnt == k_row, mid, jnp.where(enough, hi, mid - 1))
        return lo_n, hi_n

    def halve_steps(state):
        for _ in range(BISECT_STEPS):
            state = halve(state)
        return state

    tau8, _ = lax.while_loop(unfinished, halve_steps, (k_min, k_max))
    n_ge = count(lambda k: k >= tau8)
    has_tie = jnp.max(jnp.where(n_ge != k_row, 1, 0)) > 0
    tau = tau8[0:1, :]

    @pl.when(jnp.logical_not(has_tie))
    def _():
        def put(kc, carry):
            mask_ref[kc] = jnp.where(key_ref[kc] >= tau, 1.0, 0.0).astype(BF16)
            return carry
        lax.fori_loop(0, n_live, put, 0)

    @pl.when(has_tie)
    def _():
        need = (k_row - count(lambda k: k > tau8)).astype(F32)[0:1, :]

        def put(kc, seen):
            k = key_ref[kc]
            eq = jnp.where(k == tau, 1.0, 0.0)
            before = _dot(tri_ref[...], eq.astype(BF16)) + seen
            take = jnp.where(k > tau, 1.0, jnp.where(before < need, eq, 0.0))
            mask_ref[kc] = take.astype(BF16)
            return seen + jnp.sum(eq, axis=0, keepdims=True)
        lax.fori_loop(0, n_live, put, jnp.zeros((1, QB), F32))

    def clear(kc, carry):
        mask_ref[kc] = jnp.zeros((KB, QB), BF16)
        return carry
    lax.fori_loop(n_live, nk, clear, 0)


def _indexer(h, kdup, batch, seq):
    nq = seq // QB
    nk = seq // KB
    k_sel = min(TOPK_MAX, seq // 4)
    r = np.arange(KB)
    tri = jnp.asarray(r[None, :] < r[:, None], BF16)
    return pl.pallas_call(
        functools.partial(_indexer_kernel, k_sel=k_sel),
        out_shape=jax.ShapeDtypeStruct((batch, nk, KB, seq), BF16),
        grid=(batch, nq),
        in_specs=[pl.BlockSpec((QB, IDX_HEADS * IDX_DIM), lambda b, j: (b * nq + j, CD_IQ // (IDX_HEADS * IDX_DIM))),
                  pl.BlockSpec((QB, LANES), lambda b, j: (b * nq + j, CD_TAIL // LANES)),
                  pl.BlockSpec((None, nk, KB, LANES), lambda b, j: (b, 0, 0, 0)),
                  pl.BlockSpec((KB, KB), lambda b, j: (0, 0))],
        out_specs=pl.BlockSpec((None, nk, KB, QB), lambda b, j: (b, 0, 0, j)),
        scratch_shapes=[pltpu.VMEM((nk, KB, QB), I32),
                        pltpu.VMEM((IDX_HEADS // IDX_GROUP, IDX_GROUP * QB, LANES), BF16)],
        compiler_params=_cparams("parallel", "parallel"),
        name="dsa_indexer",
    )(h, h, kdup.reshape(batch, nk, KB, LANES), tri)


def _dsa_attn_kernel(cq_ref, mask_ref, ckv_ref, ckvt_ref, wuk_ref, wuvt_ref, bias_ref, o_ref,
                     qt_ref, m_ref, acc_ref, ot_ref, p_ref, alpha_ref):
    j = pl.program_id(1)
    for hh in range(C_HEADS):
        q_h = cq_ref[:, C_DH * hh:C_DH * (hh + 1)].astype(BF16)
        part = slice(QB * (hh % ATT_GROUP), QB * (hh % ATT_GROUP + 1))
        qt_ref[hh // ATT_GROUP, :, part] = (_dot_nt(wuk_ref[hh], q_h) * (C_DH ** -0.5 * LOG2E)).astype(BF16)
    m_ref[...] = jnp.full(m_ref.shape, NEG, F32)
    acc_ref[...] = jnp.zeros_like(acc_ref)
    p_ref[...] = jnp.zeros_like(p_ref)
    alpha_ref[...] = jnp.ones_like(alpha_ref)

    n_live = (j * QB + QB + KB - 1) // KB
    odd = (j % 2) == 1

    def accumulate(kc_done, hp):
        acc_ref[hp] = alpha_ref[hp] * acc_ref[hp] + _dot(ckvt_ref[kc_done], p_ref[hp])

    def body(kc, carry):
        ck = ckv_ref[kc]
        kc_prev = jnp.maximum(kc - 1, 0)
        drop = jnp.where(mask_ref[kc].astype(F32) > 0.5, 0.0, NEG)
        drop = jnp.concatenate([drop] * ATT_GROUP, axis=1)
        back = n_live - 1 - kc
        which = jnp.where(back == 0, jnp.where(odd, 0, 1), jnp.where(jnp.logical_and(back == 1, jnp.logical_not(odd)), 2, 3))
        for hp in range(C_HEADS // ATT_GROUP):
            accumulate(kc_prev, hp)
            logit = _dot(ck, qt_ref[hp]) + (bias_ref[hp, which] + drop)
            m_old = m_ref[hp]
            m_new = jnp.maximum(m_old, jnp.max(logit, axis=0, keepdims=True))
            alpha_ref[hp] = jnp.exp2(m_old - m_new)
            p_ref[hp] = jnp.exp2(logit - m_new).astype(BF16)
            m_ref[hp] = m_new
        return carry

    def body_two(i, carry):
        return body(2 * i + 1, body(2 * i, carry))

    n_pairs = (n_live + 1) // 2
    lax.fori_loop(0, n_pairs, body_two, 0)
    for hp in range(C_HEADS // ATT_GROUP):
        accumulate(2 * n_pairs - 1, hp)
    for hh in range(C_HEADS):
        part = slice(QB * (hh % ATT_GROUP), QB * (hh % ATT_GROUP + 1))
        total = acc_ref[hh // ATT_GROUP, C_DLAT:C_DLAT + 1, part]
        o_lat = (acc_ref[hh // ATT_GROUP, 0:C_DLAT, part] * (1.0 / total)).astype(BF16)
        ot_ref[C_DH * hh:C_DH * (hh + 1), :] = _dot(wuvt_ref[hh], o_lat)
    o_ref[...] = ot_ref[...].T


def _rel_bias_tiles(rel_table):
    s = np.arange(QB)[:, None]
    t = np.arange(QB)[None, :]
    diag, prev, far = np.maximum(t - s, 0), QB + t - s, np.full((QB, QB), 2 * QB)
    kinds = [(prev, diag), (diag, far), (far, prev), (far, far)]
    n = jnp.asarray(np.stack([np.concatenate(k, axis=0) for k in kinds]).astype(np.int32))
    max_exact = REL_BUCKETS // 2
    large = max_exact + (jnp.log(jnp.maximum(n, 1).astype(F32) / max_exact)
                         / math.log(REL_MAX_DIST / max_exact) * (REL_BUCKETS - max_exact)).astype(I32)
    bucket = jnp.where(n < max_exact, n, jnp.minimum(large, REL_BUCKETS - 1))
    onehot = (bucket[..., None] == jnp.arange(REL_BUCKETS, dtype=I32)).astype(F32)
    bias = jnp.einsum("kstb,bh->hkst", onehot, rel_table.astype(F32) * LOG2E, precision=lax.Precision.HIGHEST)
    ng = C_HEADS // ATT_GROUP
    return bias.reshape(ng, ATT_GROUP, 4, KB, QB).transpose(0, 2, 3, 1, 4).reshape(ng, 4, KB, ATT_GROUP * QB)


def _dsa_attention(h, mask, ckv, ckvt, w_uk, w_uv, rel_table, batch, seq):
    nq = seq // QB
    nk = seq // KB
    wuk = w_uk.transpose(1, 0, 2).astype(BF16)
    wuvt = w_uv.transpose(1, 2, 0).astype(BF16)
    bias = _rel_bias_tiles(rel_table)
    return pl.pallas_call(
        _dsa_attn_kernel,
        out_shape=jax.ShapeDtypeStruct((batch * seq, C_HEADS * C_DH), F32),
        grid=(batch, nq),
        in_specs=[pl.BlockSpec((QB, C_HEADS * C_DH), lambda b, j: (b * nq + j, CD_CQ // (C_HEADS * C_DH))),
                  pl.BlockSpec((None, nk, KB, QB), lambda b, j: (b, 0, 0, j)),
                  pl.BlockSpec((None, nk, KB, C_DLAT), lambda b, j: (b, 0, 0, 0)),
                  pl.BlockSpec((None, nk, C_DLAT + SUM_ROWS, KB), lambda b, j: (b, 0, 0, 0)),
                  _const_spec(wuk), _const_spec(wuvt), _const_spec(bias)],
        out_specs=pl.BlockSpec((QB, C_HEADS * C_DH), lambda b, j: (b * nq + j, 0)),
        scratch_shapes=[pltpu.VMEM((C_HEADS // ATT_GROUP, C_DLAT, ATT_GROUP * QB), BF16),
                        pltpu.VMEM((C_HEADS // ATT_GROUP, 1, ATT_GROUP * QB), F32),
                        pltpu.VMEM((C_HEADS // ATT_GROUP, C_DLAT + SUM_ROWS, ATT_GROUP * QB), F32),
                        pltpu.VMEM((C_HEADS * C_DH, QB), F32),
                        pltpu.VMEM((C_HEADS // ATT_GROUP, KB, ATT_GROUP * QB), BF16),
                        pltpu.VMEM((C_HEADS // ATT_GROUP, 1, ATT_GROUP * QB), F32)],
        compiler_params=_cparams("parallel", "parallel"),
        name="dsa_attention",
    )(h, mask, ckv.reshape(batch, nk, KB, C_DLAT), ckvt.reshape(batch, nk, C_DLAT + SUM_ROWS, KB), wuk, wuvt, bias)


ROUTER_BLOCK = 512
MOE_TILE = 512


def _router_kernel(x_ref, wh_ref, wl_ref, b_ref, upper_ref, idx_ref, wgt_ref, rank_ref, cnt_ref):
    x = x_ref[...]
    xh = x.astype(BF16)
    xl = (x - xh.astype(F32)).astype(BF16)
    logit = _dot_nt(wh_ref[...], xh) + _dot_nt(wl_ref[...], xh) + _dot_nt(wh_ref[...], xl)
    aff = _sigmoid(logit)
    sel = aff + b_ref[...]
    s_rows = [sel[e:e + 1] for e in range(N_EXPERTS)]
    a_rows = [aff[e:e + 1] for e in range(N_EXPERTS)]
    n = EXPERTS_PER_GROUP

    g_best = jnp.zeros(s_rows[0].shape, I32)
    best = None
    for g in range(N_GROUPS):
        v = s_rows[g * n:(g + 1) * n]
        top2 = None
        for a in range(n):
            for b in range(a + 1, n):
                pair = v[a] + v[b]
                top2 = pair if top2 is None else jnp.maximum(top2, pair)
        if best is None:
            best = top2
        else:
            upd = top2 > best
            g_best = jnp.where(upd, g, g_best)
            best = jnp.where(upd, top2, best)

    sv, av = [], []
    for i in range(n):
        s_i, a_i = s_rows[i], a_rows[i]
        for g in range(1, N_GROUPS):
            pick = g_best == g
            s_i = jnp.where(pick, s_rows[g * n + i], s_i)
            a_i = jnp.where(pick, a_rows[g * n + i], a_i)
        sv.append(s_i)
        av.append(a_i)

    i1, s1, a1 = jnp.zeros_like(g_best), sv[0], av[0]
    for i in range(1, n):
        upd = sv[i] > s1
        i1 = jnp.where(upd, i, i1)
        s1 = jnp.where(upd, sv[i], s1)
        a1 = jnp.where(upd, av[i], a1)
    i2 = jnp.zeros_like(g_best)
    s2 = jnp.full(s1.shape, -jnp.inf, F32)
    a2 = jnp.zeros_like(a1)
    for i in range(n):
        cand = jnp.where(i1 == i, -jnp.inf, sv[i])
        upd = cand > s2
        i2 = jnp.where(upd, i, i2)
        s2 = jnp.where(upd, cand, s2)
        a2 = jnp.where(upd, av[i], a2)

    tot = a1 + a2
    e1 = g_best * n + i1
    e2 = g_best * n + i2
    idx_ref[0:1, :] = e1
    idx_ref[1:2, :] = e2
    wgt_ref[0:1, :] = a1 / tot
    wgt_ref[1:2, :] = a2 / tot

    @pl.when(pl.program_id(0) == 0)
    def _():
        cnt_ref[...] = jnp.zeros_like(cnt_ref)

    e_iota = lax.broadcasted_iota(I32, sel.shape, 0)
    oh1 = jnp.where(e_iota == e1, 1.0, 0.0)
    oh2 = jnp.where(e_iota == e2, 1.0, 0.0)
    both = oh1 + oh2
    before = cnt_ref[...] + _dot(both.astype(BF16), upper_ref[...])
    rank_ref[0:1, :] = jnp.sum(oh1 * before, axis=0, keepdims=True).astype(I32)
    rank_ref[1:2, :] = jnp.sum(oh2 * before, axis=0, keepdims=True).astype(I32)
    cnt_ref[...] = cnt_ref[...] + jnp.sum(both, axis=1, keepdims=True)


def _router(x2d, w_router, b_router):
    t, d = x2d.shape
    tb = min(ROUTER_BLOCK, t)
    wt = w_router.T.astype(F32)
    wh = wt.astype(BF16)
    wl = (wt - wh.astype(F32)).astype(BF16)
    r = np.arange(tb)
    upper = jnp.asarray(r[:, None] < r[None, :], BF16)
    pair_out = pl.BlockSpec((TOP_K, tb), lambda i: (0, i))
    fixed = lambda i: (0, 0)
    return pl.pallas_call(
        _router_kernel,
        out_shape=(jax.ShapeDtypeStruct((TOP_K, t), I32), jax.ShapeDtypeStruct((TOP_K, t), F32),
                   jax.ShapeDtypeStruct((TOP_K, t), I32), jax.ShapeDtypeStruct((N_EXPERTS, 1), F32)),
        grid=(t // tb,),
        in_specs=[pl.BlockSpec((tb, d), lambda i: (i, 0)),
                  pl.BlockSpec((N_EXPERTS, d), fixed), pl.BlockSpec((N_EXPERTS, d), fixed),
                  pl.BlockSpec((N_EXPERTS, 1), fixed), pl.BlockSpec((tb, tb), fixed)],
        out_specs=(pair_out, pair_out, pair_out, pl.BlockSpec((N_EXPERTS, 1), fixed)),
        compiler_params=_cparams("arbitrary"),
        name="moe_router",
    )(x2d, wh, wl, b_router.reshape(-1, 1).astype(F32), upper)


def _route_tables(idx, rank, counts, tm, n_items):
    cnt = counts.reshape(-1).astype(I32)
    start = jnp.cumsum(cnt) - cnt
    experts = jnp.arange(N_EXPERTS, dtype=I32)
    pos = rank + jnp.sum(jnp.where(idx[..., None] == experts, start, 0), axis=-1)
    first_tile = start // tm
    n_e = jnp.where(cnt > 0, (start + cnt - 1) // tm - first_tile + 1, 0)
    item_end = jnp.cumsum(n_e)
    item = jnp.arange(n_items, dtype=I32)
    used = item < item_end[-1]
    e_i = jnp.minimum(jnp.sum((item[:, None] >= item_end[None, :]).astype(I32), axis=1), N_EXPERTS - 1)
    e_last = jnp.max(jnp.where(cnt > 0, experts, 0))
    e_i = jnp.where(used, e_i, e_last)
    tile_i = jnp.where(used, first_tile[e_i] + item - (item_end - n_e)[e_i], (TOP_K * idx.shape[1]) // tm - 1)
    lo = jnp.where(used, jnp.maximum(start[e_i], tile_i * tm) - tile_i * tm, 0)
    hi = jnp.where(used, jnp.minimum(start[e_i] + cnt[e_i], (tile_i + 1) * tm) - tile_i * tm, 0)
    return pos.astype(I32), tile_i.astype(I32), e_i, lo.astype(I32), hi.astype(I32)


def _row_of(ref, row):
    return ref.at[lax.shift_right_logical(row, SUBLANES.bit_length() - 1), pl.ds(row & (SUBLANES - 1), 1)]


def _dispatch_kernel(pos_ref, x_ref, xs_hbm, stage, sem, *, tb, n_tok):
    i = pl.program_id(0)
    slot = i % 2
    stage[slot] = x_ref[...].reshape(stage.shape[1:])

    def issue(grp, carry):
        dst = [[pos_ref[s * n_tok + i * tb + grp * SUBLANES + u] for s in range(TOP_K)] for u in range(SUBLANES)]
        for u in range(SUBLANES):
            for s in range(TOP_K):
                row = dst[u][s]
                pltpu.make_async_copy(stage.at[slot, grp, pl.ds(u, 1)], _row_of(xs_hbm, row), sem.at[slot]).start()
        return carry

    lax.fori_loop(0, tb // SUBLANES, issue, 0)

    def wait_block(s):
        for _ in range(TOP_K):
            pltpu.make_async_copy(stage.at[s], xs_hbm.at[pl.ds(0, tb // SUBLANES)], sem.at[s]).wait()

    @pl.when(i >= 1)
    def _():
        wait_block(1 - slot)

    @pl.when(i == pl.num_programs(0) - 1)
    def _():
        wait_block(slot)


def _dispatch(x2d, pos_flat, tb):
    t, d = x2d.shape
    grid_spec = pltpu.PrefetchScalarGridSpec(
        num_scalar_prefetch=1, grid=(t // tb,),
        in_specs=[pl.BlockSpec((tb, d), lambda i, pos: (i, 0))],
        out_specs=pl.BlockSpec(memory_space=pl.ANY),
        scratch_shapes=[pltpu.VMEM((2, tb // SUBLANES, SUBLANES, d), F32), pltpu.SemaphoreType.DMA((2,))])
    return pl.pallas_call(
        functools.partial(_dispatch_kernel, tb=tb, n_tok=t),
        out_shape=jax.ShapeDtypeStruct((TOP_K * t // SUBLANES, SUBLANES, d), F32),
        grid_spec=grid_spec,
        compiler_params=_cparams("arbitrary"),
        name="moe_dispatch",
    )(pos_flat, x2d).reshape(TOP_K * t, d)


def _experts_kernel(tile_ref, exp_ref, lo_ref, hi_ref, x_ref, w1_ref, w3_ref, w2_ref, o_ref, w1b, w3b, w2b):
    i = pl.program_id(0)
    prev = jnp.maximum(i - 1, 0)

    @pl.when(jnp.logical_or(i == 0, exp_ref[i] != exp_ref[prev]))
    def _():
        w1b[...] = w1_ref[...].astype(BF16)
        w3b[...] = w3_ref[...].astype(BF16)
        w2b[...] = w2_ref[...].astype(BF16)

    lo = lo_ref[i]
    hi = hi_ref[i]
    first = jnp.logical_or(i == 0, tile_ref[i] != tile_ref[prev])

    @pl.when(hi > lo)
    def _():
        x = x_ref[...].astype(BF16)
        row = lax.broadcasted_iota(I32, (x.shape[0], 1), 0)
        mine = jnp.where(jnp.logical_and(row >= lo, row < hi), 1.0, 0.0)
        hid = _silu(_dot(x, w1b[...])) * _dot(x, w3b[...]) * mine
        y = _dot(hid.astype(BF16), w2b[...])

        @pl.when(first)
        def _():
            o_ref[...] = y

        @pl.when(jnp.logical_not(first))
        def _():
            o_ref[...] += y


def _experts(xs, tile_i, exp_i, lo, hi, w1, w3, w2, layer, tm):
    n, d = xs.shape
    f = w1.shape[-1]
    by_tile = lambda i, tile, exp, lo, hi: (tile[i], 0)
    by_exp = lambda i, tile, exp, lo, hi: (layer, exp[i], 0, 0)
    grid_spec = pltpu.PrefetchScalarGridSpec(
        num_scalar_prefetch=4, grid=(tile_i.shape[0],),
        in_specs=[pl.BlockSpec((tm, d), by_tile),
                  pl.BlockSpec((None, None, d, f), by_exp), pl.BlockSpec((None, None, d, f), by_exp),
                  pl.BlockSpec((None, None, f, d), by_exp)],
        out_specs=pl.BlockSpec((tm, d), by_tile),
        scratch_shapes=[pltpu.VMEM((d, f), BF16), pltpu.VMEM((d, f), BF16), pltpu.VMEM((f, d), BF16)])
    return pl.pallas_call(
        _experts_kernel,
        out_shape=jax.ShapeDtypeStruct((n, d), F32),
        grid_spec=grid_spec,
        compiler_params=_cparams("arbitrary"),
        name="moe_experts",
    )(tile_i, exp_i, lo, hi, xs, w1, w3, w2)


def _combine_ln_kernel(pos_ref, x_ref, w_ref, g_ref, b_ref, ys_hbm, o_ref, gbuf, sem, *, tb, n_tok):
    i = pl.program_id(0)
    slot = i % 2

    def start(blk, s):
        def issue(grp, carry):
            src = [[pos_ref[k * n_tok + blk * tb + grp * SUBLANES + u] for k in range(TOP_K)] for u in range(SUBLANES)]
            for u in range(SUBLANES):
                for k in range(TOP_K):
                    pltpu.make_async_copy(_row_of(ys_hbm, src[u][k]), gbuf.at[s, k, grp, pl.ds(u, 1)], sem.at[s]).start()
            return carry
        lax.fori_loop(0, tb // SUBLANES, issue, 0)

    @pl.when(i == 0)
    def _():
        start(0, 0)

    @pl.when(i + 1 < pl.num_programs(0))
    def _():
        start(i + 1, 1 - slot)

    for k in range(TOP_K):
        pltpu.make_async_copy(ys_hbm.at[pl.ds(0, tb // SUBLANES)], gbuf.at[slot, k], sem.at[slot]).wait()
    w = w_ref[...]
    rows = x_ref.shape
    y = gbuf[slot, 0].reshape(rows) * w[:, 0:1] + gbuf[slot, 1].reshape(rows) * w[:, 1:2]
    o_ref[...] = _layer_norm_rows(DN_ALPHA * x_ref[...] + y, g_ref[...], b_ref[...])


def _combine_ln(x2d, ys, pos_flat, wgt_cols, g, b, tb):
    t, d = x2d.shape
    row = lambda i, pos: (i, 0)
    fixed = lambda i, pos: (0, 0)
    grid_spec = pltpu.PrefetchScalarGridSpec(
        num_scalar_prefetch=1, grid=(t // tb,),
        in_specs=[pl.BlockSpec((tb, d), row), pl.BlockSpec((tb, TOP_K), row),
                  pl.BlockSpec((1, d), fixed), pl.BlockSpec((1, d), fixed),
                  pl.BlockSpec(memory_space=pl.ANY)],
        out_specs=pl.BlockSpec((tb, d), row),
        scratch_shapes=[pltpu.VMEM((2, TOP_K, tb // SUBLANES, SUBLANES, d), F32), pltpu.SemaphoreType.DMA((2,))])
    return pl.pallas_call(
        functools.partial(_combine_ln_kernel, tb=tb, n_tok=t),
        out_shape=jax.ShapeDtypeStruct((t, d), F32),
        grid_spec=grid_spec,
        compiler_params=_cparams("arbitrary"),
        name="moe_combine_ln",
    )(pos_flat, x2d, wgt_cols, g.reshape(1, d), b.reshape(1, d), ys.reshape(-1, SUBLANES, d))


def _moe_ln(x2d, w_router, b_router, w1, w3, w2, layer, g, b):
    t = x2d.shape[0]
    tm = min(MOE_TILE, t)
    idx, wgt, rank, counts = _router(x2d, w_router, b_router)
    n_items = TOP_K * t // tm + N_EXPERTS - 1
    pos, tile_i, exp_i, lo, hi = _route_tables(idx, rank, counts, tm, n_items)
    pos_flat = pos.reshape(-1)
    xs = _dispatch(x2d, pos_flat, tm)
    ys = _experts(xs, tile_i, exp_i, lo, hi, w1, w3, w2, layer, tm)
    return _combine_ln(x2d, ys, pos_flat, wgt.T, g, b, tm)


def _gate_rows(col, batch, seq):
    tb = min(TIME_BLOCK, seq)
    return col.reshape(batch, seq, D_HEADS).transpose(0, 2, 1).reshape(batch, D_HEADS, seq // tb, tb // CHUNK, CHUNK)


def _mix_cd(x2d, w_in, rel_table, ckv_g, w_uk, w_uv, conv_w, gate_b, d_norm_g, batch, seq):
    t = x2d.shape[0]
    h = _project(x2d, *_pack_cd(w_in), min(1024, t), PROJ_TILE)
    ckv, ckvt, kdup = _dsa_prep(h, ckv_g, t)
    mask = _indexer(h, kdup, batch, seq)
    oc = _dsa_attention(h, mask, ckv, ckvt, w_uk, w_uv, rel_table, batch, seq)
    tail = h[:, CD_TAIL:CD_TAIL + LANES]
    ig_rows = _gate_rows(tail[:, TAIL_DI:TAIL_DI + D_HEADS], batch, seq)
    fg_rows = _gate_rows(tail[:, TAIL_DF:TAIL_DF + D_HEADS], batch, seq)
    od = _mlstm(h, conv_w, ig_rows, fg_rows, gate_b, d_norm_g, batch, seq,
                q_off=CD_DQ // (D_HEADS * D_DK), k_off=CD_DK // (D_HEADS * D_DK),
                v_off=CD_DV // (D_HEADS * D_DV), g_off=CD_DOG // (D_HEADS * D_DV))
    return oc, od


def kernel(x, w_in_ab, w_out_ab, hgrn_lb_logits, a_norm_g, gla_wa2, gla_ba2, b_norm_g, w_in_cd, w_out_cd,
           ckv_norm_g, w_uk, w_uv, mlstm_conv_w, mlstm_gate_b, d_norm_g, rel_table, w_router, b_router,
           moe_w1, moe_w3, moe_w2, ln_g, ln_b):
    batch, seq, d = x.shape
    x2d = x.reshape(batch * seq, d)
    for layer in range(DEPTH):
        li = layer // 2
        if layer % 2 == 0:
            mix_a, mix_b = _mix_ab(x2d, w_in_ab[li], hgrn_lb_logits, li, a_norm_g[li], gla_wa2[li], gla_ba2[li],
                                   b_norm_g[li], batch, seq)
            w_out = w_out_ab[li]
        else:
            mix_a, mix_b = _mix_cd(x2d, w_in_cd[li], rel_table, ckv_norm_g[li], w_uk[li], w_uv[li],
                                   mlstm_conv_w[li], mlstm_gate_b[li], d_norm_g[li], batch, seq)
            w_out = w_out_cd[li]
        ka = mix_a.shape[1]
        x2d = _outproj_ln(x2d, mix_a, mix_b, w_out[:ka].astype(BF16), w_out[ka:].astype(BF16),
                          ln_g[layer, 0], ln_b[layer, 0])
        x2d = _moe_ln(x2d, w_router, b_router, moe_w1, moe_w3, moe_w2, layer,
                      ln_g[layer, 1], ln_b[layer, 1])
    return x2d.reshape(batch, seq, d)
```

```python
import functools
import math

import numpy as np
import jax
import jax.numpy as jnp
from jax import lax
from jax.experimental import pallas as pl
from jax.experimental.pallas import tpu as pltpu

F32 = jnp.float32
BF16 = jnp.bfloat16
I32 = jnp.int32

D_MODEL = 2048
DEPTH = 2
A_HEADS, A_DK, A_DV = 8, 128, 128
B_HEADS, B_DK, B_DV = 4, 128, 256
B_GATE_RANK, B_GATE_TAU = 16, 16.0
C_HEADS, C_DH, C_DLAT = 8, 128, 256
IDX_HEADS, IDX_DIM = 16, 64
TOPK_MAX = 256
D_HEADS, D_DK, D_DV = 4, 128, 256
CONV_K = 4
REL_BUCKETS, REL_MAX_DIST = 32, 128
N_EXPERTS, N_GROUPS, TOP_K, D_EXPERT = 16, 4, 2, 512
EXPERTS_PER_GROUP = N_EXPERTS // N_GROUPS
DN_ALPHA = (2 * DEPTH) ** 0.25
EPS = 1e-5

LANES = 128
SUBLANES = 8
VMEM_LIMIT = 56 * 1024 * 1024

CHUNK = 128
N_LEVELS = 7
LOG2E = math.log2(math.e)
TIME_BLOCK = 512
QB = 128
NEG = -1e30
INT_MIN = -2 ** 31
INT_MAX = 2 ** 31 - 1


def _cparams(*sem):
    return pltpu.CompilerParams(dimension_semantics=sem, vmem_limit_bytes=VMEM_LIMIT)


def _dot(a, b):
    return jnp.dot(a, b, preferred_element_type=F32)


def _dot_nt(a, b):
    return lax.dot_general(a, b, (((1,), (1,)), ((), ())), preferred_element_type=F32)


def _dot_tn(a, b):
    return lax.dot_general(a, b, (((0,), (0,)), ((), ())), preferred_element_type=F32)


def _split3(a):
    hi = a.astype(BF16)
    r1 = a - hi.astype(F32)
    mid = r1.astype(BF16)
    lo = (r1 - mid.astype(F32)).astype(BF16)
    return hi, mid, lo


def _dot01(m01, a):
    hi, mid, lo = _split3(a)
    return _dot(m01, hi) + _dot(m01, mid) + _dot(m01, lo)


def _sigmoid(x):
    return 1.0 / (1.0 + jnp.exp(-x))


def _silu(x):
    return x * _sigmoid(x)


def _log_sigmoid(x):
    return jnp.minimum(x, 0.0) - jnp.log(1.0 + jnp.exp(-jnp.abs(x)))


def _proj_kernel(x_ref, w_ref, wt_ref, o_ref, xb_ref):
    j = pl.program_id(1)

    @pl.when(j == 0)
    def _():
        xb_ref[...] = x_ref[...].astype(BF16)

    @pl.when(j < pl.num_programs(1) - 1)
    def _():
        o_ref[...] = _dot(xb_ref[...], w_ref[...])

    @pl.when(j == pl.num_programs(1) - 1)
    def _():
        o_ref[...] = _dot(xb_ref[...], wt_ref[...])


def _cast_kernel(w_ref, o_ref):
    o_ref[...] = w_ref[...].astype(o_ref.dtype)


def _cast_columns(w, n_cols, tn):
    k = w.shape[0]
    return pl.pallas_call(
        _cast_kernel,
        out_shape=jax.ShapeDtypeStruct((k, n_cols), BF16),
        grid=(n_cols // tn,),
        in_specs=[pl.BlockSpec((k, tn), lambda j: (0, j))],
        out_specs=pl.BlockSpec((k, tn), lambda j: (0, j)),
        compiler_params=_cparams("parallel"),
        name="weight_cast",
    )(w)


def _project(x, w, w_tail, tm, tn):
    m, k = x.shape
    n_main = w.shape[1] // tn
    return pl.pallas_call(
        _proj_kernel,
        out_shape=jax.ShapeDtypeStruct((m, (n_main + 1) * tn), F32),
        grid=(m // tm, n_main + 1),
        in_specs=[pl.BlockSpec((tm, k), lambda i, j: (i, 0)),
                  pl.BlockSpec((k, tn), lambda i, j: (0, jnp.minimum(j, n_main - 1))),
                  pl.BlockSpec((k, tn), lambda i, j: (0, 0))],
        out_specs=pl.BlockSpec((tm, tn), lambda i, j: (i, j)),
        scratch_shapes=[pltpu.VMEM((tm, k), BF16)],
        compiler_params=_cparams("parallel", "arbitrary"),
        name="in_proj",
    )(x, w, w_tail)


def _layer_norm_rows(z, g, b):
    mu = jnp.mean(z, axis=-1, keepdims=True)
    zc = z - mu
    var = jnp.mean(zc * zc, axis=-1, keepdims=True)
    return zc * lax.rsqrt(var + EPS) * g + b


def _outproj_ln_kernel(x_ref, ma_ref, mb_ref, wa_ref, wb_ref, g_ref, b_ref, wh_ref, wl_ref, rb_ref, upper_ref,
                       o_ref, idx_ref, wgt_ref, rank_ref, cnt_ref):
    mixed = _dot(ma_ref[...].astype(BF16), wa_ref[...]) + _dot(mb_ref[...].astype(BF16), wb_ref[...])
    z = _layer_norm_rows(DN_ALPHA * x_ref[...] + mixed, g_ref[...], b_ref[...])
    o_ref[...] = z
    _route_block(z, wh_ref, wl_ref, rb_ref, upper_ref, idx_ref, wgt_ref, rank_ref, cnt_ref)


def _outproj_ln(x, mix_a, mix_b, w_a, w_b, g, b, w_router, b_router):
    m, d = x.shape
    tm = min(ROUTER_BLOCK, m)
    ka, kb = mix_a.shape[1], mix_b.shape[1]
    wt = w_router.T.astype(F32)
    wh = wt.astype(BF16)
    wl = (wt - wh.astype(F32)).astype(BF16)
    r = np.arange(tm)
    upper = jnp.asarray(r[:, None] < r[None, :], BF16)
    row = lambda i: (i, 0)
    fixed = lambda i: (0, 0)
    pair_out = pl.BlockSpec((TOP_K, tm), lambda i: (0, i))
    return pl.pallas_call(
        _outproj_ln_kernel,
        out_shape=(jax.ShapeDtypeStruct((m, d), F32),
                   jax.ShapeDtypeStruct((TOP_K, m), I32), jax.ShapeDtypeStruct((TOP_K, m), F32),
                   jax.ShapeDtypeStruct((TOP_K, m), I32), jax.ShapeDtypeStruct((N_EXPERTS, 1), F32)),
        grid=(m // tm,),
        in_specs=[pl.BlockSpec((tm, d), row), pl.BlockSpec((tm, ka), row), pl.BlockSpec((tm, kb), row),
                  pl.BlockSpec((ka, d), fixed), pl.BlockSpec((kb, d), fixed),
                  pl.BlockSpec((1, d), fixed), pl.BlockSpec((1, d), fixed),
                  pl.BlockSpec((N_EXPERTS, d), fixed), pl.BlockSpec((N_EXPERTS, d), fixed),
                  pl.BlockSpec((N_EXPERTS, 1), fixed), pl.BlockSpec((tm, tm), fixed)],
        out_specs=(pl.BlockSpec((tm, d), row), pair_out, pair_out, pair_out, pl.BlockSpec((N_EXPERTS, 1), fixed)),
        compiler_params=_cparams("arbitrary"),
        name="out_proj_ln",
    )(x, mix_a, mix_b, w_a, w_b, g.reshape(1, d), b.reshape(1, d), wh, wl, b_router.reshape(-1, 1).astype(F32), upper)


def _chunk_constants():
    t = np.arange(CHUNK)
    tri = (t[:, None] >= t[None, :]).astype(np.float32)
    pair, odd = [], []
    for lev in range(1, N_LEVELS + 1):
        c = CHUNK >> lev
        pair.append((t[:, None] // (2 * c) == t[None, :] // (2 * c)).astype(np.float32))
        odd.append(np.broadcast_to((((t // c) & 1) == 1).astype(np.float32)[:, None], (CHUNK, LANES)))
    pair.append(np.eye(CHUNK, dtype=np.float32))
    return jnp.asarray(tri, BF16), jnp.asarray(np.stack(pair), F32), jnp.asarray(np.stack(odd), F32)


def _level_log_decay(la, bcum, lev):
    c = CHUNK >> lev
    if 2 * c >= SUBLANES:
        mids = [jnp.broadcast_to(bcum[g * 2 * c + c - 1:g * 2 * c + c], (2 * c, bcum.shape[1]))
                for g in range(CHUNK // (2 * c))]
        return -jnp.abs(bcum - (mids[0] if len(mids) == 1 else jnp.concatenate(mids, axis=0)))
    r = lax.broadcasted_iota(I32, la.shape, 0) & (2 * c - 1)
    if c == 2:
        nxt = pltpu.roll(la, CHUNK - 1, axis=0)
        prv = pltpu.roll(la, 1, axis=0)
        return jnp.where(r == 0, nxt, jnp.where(r == 1, 0.0, jnp.where(r == 2, la, la + prv)))
    return jnp.where(r == 1, la, 0.0)


def _glr_chunk(q, k, v, la, st_ref, cum_ref, pair_ref, odd_ref):
    la = la * LOG2E
    bcum = _dot01(cum_ref[...], la)
    attn = pair_ref[N_LEVELS] * _dot_nt(q.astype(BF16), k.astype(BF16))
    for lev in range(1, N_LEVELS + 1):
        e = jnp.exp2(_level_log_decay(la, bcum, lev))
        c = CHUNK >> lev
        if c >= SUBLANES:
            zero = jnp.zeros((c, q.shape[1]), F32)
            blocks = [slice(b * c, (b + 1) * c) for b in range(CHUNK // c)]
            ql = jnp.concatenate([q[s] * e[s] if b % 2 else zero for b, s in enumerate(blocks)], axis=0)
            kl = jnp.concatenate([zero if b % 2 else k[s] * e[s] for b, s in enumerate(blocks)], axis=0)
        else:
            eq = e * odd_ref[lev - 1]
            ql = q * eq
            kl = k * (e - eq)
        attn = attn + pair_ref[lev - 1] * _dot_nt(ql.astype(BF16), kl.astype(BF16))
    st = st_ref[...]
    o = _dot_nt((q * jnp.exp2(bcum)).astype(BF16), st.astype(BF16)) + _dot(attn.astype(BF16), v.astype(BF16))
    b_last = bcum[CHUNK - 1:CHUNK]
    kdec = (k * jnp.exp2(b_last - bcum)).astype(BF16)
    st_ref[...] = st * jnp.exp2(b_last) + _dot_tn(v.astype(BF16), kdec)
    return o


def _rms_gate(o, g, gate):
    ms = jnp.mean(o * o, axis=-1, keepdims=True)
    return o * lax.rsqrt(ms + EPS) * g * gate


HEAD_GROUP = 4


def _hgrn2_kernel(q_ref, f_ref, i_ref, g_ref, lb_ref, ng_ref, cum_ref, pair_ref, odd_ref, o_ref, st_ref):
    @pl.when(pl.program_id(2) == 0)
    def _():
        st_ref[...] = jnp.zeros_like(st_ref)

    def body(c, carry):
        rows = pl.ds(pl.multiple_of(c * CHUNK, CHUNK), CHUNK)
        for hh in range(HEAD_GROUP):
            ck = slice(A_DK * hh, A_DK * (hh + 1))
            cv = slice(A_DV * hh, A_DV * (hh + 1))
            lb = lb_ref[:, ck]
            f = lb + (1.0 - lb) * _sigmoid(f_ref[rows, ck])
            o = _glr_chunk(_silu(q_ref[rows, ck]), 1.0 - f, i_ref[rows, cv], jnp.log(f),
                           st_ref.at[hh], cum_ref, pair_ref, odd_ref)
            o_ref[rows, cv] = _rms_gate(o, ng_ref[...], _silu(g_ref[rows, cv]))
        return carry

    lax.fori_loop(0, q_ref.shape[0] // CHUNK, body, 0)


def _gla_kernel(q_ref, k_ref, v_ref, g_ref, r_ref, wa_ref, ba_ref, ng_ref, cum_ref, pair_ref, odd_ref,
                o_ref, st_ref):
    @pl.when(pl.program_id(2) == 0)
    def _():
        st_ref[...] = jnp.zeros_like(st_ref)

    def body(c, carry):
        rows = pl.ds(pl.multiple_of(c * CHUNK, CHUNK), CHUNK)
        pre = _dot(r_ref[rows, :].astype(BF16), wa_ref[...]) + ba_ref[...]
        la = _log_sigmoid(pre) * (1.0 / B_GATE_TAU)
        for hh in range(HEAD_GROUP):
            ck = slice(B_DK * hh, B_DK * (hh + 1))
            cv = slice(B_DV * hh, B_DV * (hh + 1))
            o = _glr_chunk(q_ref[rows, ck] * (B_DK ** -0.5), k_ref[rows, ck], v_ref[rows, cv], la[:, ck],
                           st_ref.at[hh], cum_ref, pair_ref, odd_ref)
            o_ref[rows, cv] = _rms_gate(o, ng_ref[...], _silu(g_ref[rows, cv]))
        return carry

    lax.fori_loop(0, q_ref.shape[0] // CHUNK, body, 0)


def _const_spec(arr):
    nd = arr.ndim
    return pl.BlockSpec(arr.shape, lambda *_: (0,) * nd)


def _hgrn2(h, lb, norm_g, batch, seq):
    tb = min(TIME_BLOCK, seq)
    nt = seq // tb
    ng = A_HEADS // HEAD_GROUP
    wk, wv = HEAD_GROUP * A_DK, HEAD_GROUP * A_DV
    consts = _chunk_constants()
    col = lambda seg, w: pl.BlockSpec((tb, w), lambda b, g, t, seg=seg: (b * nt + t, seg * ng + g))
    return pl.pallas_call(
        _hgrn2_kernel,
        out_shape=jax.ShapeDtypeStruct((batch * seq, A_HEADS * A_DV), F32),
        grid=(batch, ng, nt),
        in_specs=[col(0, wk), col(1, wk), col(2, wv), col(3, wv),
                  pl.BlockSpec((1, wk), lambda b, g, t: (0, g)),
                  pl.BlockSpec((1, A_DV), lambda b, g, t: (0, 0))] + [_const_spec(c) for c in consts],
        out_specs=pl.BlockSpec((tb, wv), lambda b, g, t: (b * nt + t, g)),
        scratch_shapes=[pltpu.VMEM((HEAD_GROUP, A_DV, A_DK), F32)],
        compiler_params=_cparams("parallel", "parallel", "arbitrary"),
        name="hgrn2",
    )(h, h, h, h, lb.reshape(1, -1), norm_g.reshape(1, -1), *consts)


def _gla(h, wa2p, ba2, norm_g, batch, seq, q_off, k_off, v_off, g_off, r_off):
    tb = min(TIME_BLOCK, seq)
    nt = seq // tb
    wk, wv = B_HEADS * B_DK, B_HEADS * B_DV
    consts = _chunk_constants()
    col = lambda off, w: pl.BlockSpec((tb, w), lambda b, g, t, off=off: (b * nt + t, off))
    fixed = lambda b, g, t: (0, 0)
    return pl.pallas_call(
        _gla_kernel,
        out_shape=jax.ShapeDtypeStruct((batch * seq, wv), F32),
        grid=(batch, 1, nt),
        in_specs=[col(q_off, wk), col(k_off, wk), col(v_off, wv), col(g_off, wv), col(r_off, LANES),
                  pl.BlockSpec((LANES, wk), fixed), pl.BlockSpec((1, wk), fixed),
                  pl.BlockSpec((1, B_DV), fixed)] + [_const_spec(c) for c in consts],
        out_specs=pl.BlockSpec((tb, wv), lambda b, g, t: (b * nt + t, 0)),
        scratch_shapes=[pltpu.VMEM((B_HEADS, B_DV, B_DK), F32)],
        compiler_params=_cparams("parallel", "parallel", "arbitrary"),
        name="gla",
    )(h, h, h, h, h, wa2p, ba2.reshape(1, -1), norm_g.reshape(1, -1), *consts)


PROJ_TILE = 512
AB_MAIN = 4 * A_HEADS * A_DK + 2 * B_HEADS * B_DK + 2 * B_HEADS * B_DV
AB_PAD = AB_MAIN + PROJ_TILE


def _mix_ab(x2d, w_in, lb_logits, li, a_norm_g, wa2, ba2, b_norm_g, batch, seq):
    d = x2d.shape[1]
    w_tail = jnp.pad(w_in[:, AB_MAIN:], ((0, 0), (0, AB_PAD - w_in.shape[1]))).astype(BF16)
    h = _project(x2d, _cast_columns(w_in, AB_MAIN, PROJ_TILE), w_tail, min(1024, x2d.shape[0]), PROJ_TILE)
    lb = jnp.cumsum(jax.nn.softmax(lb_logits.astype(F32), axis=0), axis=0)[li]
    oa = _hgrn2(h, lb, a_norm_g, batch, seq)
    wa2p = jnp.concatenate([wa2, jnp.zeros((LANES - B_GATE_RANK, wa2.shape[1]), F32)], axis=0).astype(BF16)
    a_cols = 4 * A_HEADS * A_DK
    wk, wv = B_HEADS * B_DK, B_HEADS * B_DV
    ob = _gla(h, wa2p, ba2, b_norm_g, batch, seq, q_off=a_cols // wk, k_off=a_cols // wk + 1,
              v_off=(a_cols + 2 * wk) // wv, g_off=(a_cols + 2 * wk) // wv + 1, r_off=AB_MAIN // LANES)
    return oa, ob


CONV_HALO = 8


def _causal_conv(x_ref, w_ref, buf_ref, tail_ref):
    tb = x_ref.shape[0]
    x = x_ref[...]
    buf_ref[0:CONV_HALO, :] = tail_ref[...]
    buf_ref[CONV_HALO:CONV_HALO + tb, :] = x
    tail_ref[...] = x[tb - CONV_HALO:tb]
    y = w_ref[CONV_K - 1:CONV_K, :] * x
    for j in range(CONV_K - 1):
        y = y + w_ref[j:j + 1, :] * buf_ref[pl.ds(CONV_HALO - (CONV_K - 1) + j, tb), :]
    return y


def _row_to_col(row, eye):
    return jnp.sum(jnp.where(eye, row, 0.0), axis=1, keepdims=True)


def _mlstm_kernel(q_ref, k_ref, v_ref, og_ref, wq_ref, wk_ref, ig_ref, fg_ref, gb_ref, ng_ref, tri_ref,
                  o_ref, ct_ref, n_ref, m_ref, qt_ref, kt_ref, qs_ref, ks_ref, buf_ref):
    @pl.when(pl.program_id(2) == 0)
    def _():
        ct_ref[...] = jnp.zeros_like(ct_ref)
        n_ref[...] = jnp.zeros_like(n_ref)
        m_ref[...] = jnp.zeros_like(m_ref)
        qt_ref[...] = jnp.zeros_like(qt_ref)
        kt_ref[...] = jnp.zeros_like(kt_ref)

    qs_ref[...] = _silu(_causal_conv(q_ref, wq_ref, buf_ref, qt_ref))
    ks_ref[...] = _silu(_causal_conv(k_ref, wk_ref, buf_ref, kt_ref)) * (D_DK ** -0.5)

    r_i = lax.broadcasted_iota(I32, (CHUNK, CHUNK), 0)
    c_i = lax.broadcasted_iota(I32, (CHUNK, CHUNK), 1)
    eye = r_i == c_i
    causal = r_i >= c_i

    def body(c, carry):
        rows = pl.ds(pl.multiple_of(c * CHUNK, CHUNK), CHUNK)
        tri = tri_ref[...]
        for hh in range(D_HEADS):
            ck = slice(D_DK * hh, D_DK * (hh + 1))
            cv = slice(D_DV * hh, D_DV * (hh + 1))
            q = qs_ref[rows, ck]
            k = ks_ref[rows, ck]
            v = v_ref[rows, cv].astype(BF16)
            qb = q.astype(BF16)
            ig_row = ig_ref[hh, pl.ds(c, 1), :] + gb_ref[0, hh]
            lf_row = _log_sigmoid(fg_ref[hh, pl.ds(c, 1), :] + gb_ref[1, hh])
            hi, mid, lo = _split3(lf_row)
            bcum_row = _dot(hi, tri) + _dot(mid, tri) + _dot(lo, tri)
            bcum_col = _row_to_col(bcum_row, eye)
            ig_col = _row_to_col(ig_row, eye)
            m_prev = m_ref[hh, :, 0:1]
            log_w = jnp.where(causal, bcum_col - bcum_row + ig_row, NEG)
            log_inter = bcum_col + m_prev
            m_t = jnp.maximum(jnp.max(log_w, axis=1, keepdims=True), log_inter)
            s = _dot_nt(qb, k.astype(BF16)) * jnp.exp(log_w - m_t)
            w_inter = jnp.exp(log_inter - m_t)
            num = _dot(s.astype(BF16), v) + w_inter * _dot_nt(qb, ct_ref[hh].astype(BF16))
            qn = jnp.sum(s, axis=1, keepdims=True) + w_inter * jnp.sum(q * n_ref[hh], axis=1, keepdims=True)
            h = num / jnp.maximum(jnp.abs(qn), jnp.exp(-m_t))
            o_ref[rows, cv] = _rms_gate(h, ng_ref[...], _sigmoid(og_ref[rows, cv]))
            b_last = bcum_row[:, CHUNK - 1:CHUNK]
            log_u = b_last - bcum_col + ig_col
            m_new = jnp.maximum(b_last + m_prev, jnp.max(log_u, axis=0, keepdims=True))
            decay = jnp.exp(b_last + m_prev - m_new)
            ku = k * jnp.exp(log_u - m_new)
            ct_ref[hh] = decay * ct_ref[hh] + _dot_tn(v, ku.astype(BF16))
            n_ref[hh] = decay * n_ref[hh] + jnp.sum(ku, axis=0, keepdims=True)
            m_ref[hh] = jnp.broadcast_to(m_new, (1, LANES))
        return carry

    lax.fori_loop(0, q_ref.shape[0] // CHUNK, body, 0)


def _mlstm(h, conv_w, ig_rows, fg_rows, gate_b, norm_g, batch, seq, q_off, k_off, v_off, g_off):
    tb = min(TIME_BLOCK, seq)
    nt = seq // tb
    nc = tb // CHUNK
    wk, wv = D_HEADS * D_DK, D_HEADS * D_DV
    t = np.arange(CHUNK)
    tri = jnp.asarray(t[:, None] <= t[None, :], BF16)
    gb = jnp.broadcast_to(gate_b.reshape(2, D_HEADS, 1, 1), (2, D_HEADS, 1, CHUNK)).astype(F32)
    col = lambda off, w: pl.BlockSpec((tb, w), lambda b, g, t, off=off: (b * nt + t, off))
    gate = pl.BlockSpec((None, D_HEADS, None, nc, CHUNK), lambda b, g, t: (b, 0, t, 0, 0))
    fixed = lambda b, g, t: (0, 0)
    return pl.pallas_call(
        _mlstm_kernel,
        out_shape=jax.ShapeDtypeStruct((batch * seq, wv), F32),
        grid=(batch, 1, nt),
        in_specs=[col(q_off, wk), col(k_off, wk), col(v_off, wv), col(g_off, wv),
                  pl.BlockSpec((CONV_K, wk), lambda b, g, t: (0, 0)),
                  pl.BlockSpec((CONV_K, wk), lambda b, g, t: (0, 1)),
                  gate, gate,
                  pl.BlockSpec((2, D_HEADS, 1, CHUNK), lambda b, g, t: (0, 0, 0, 0)),
                  pl.BlockSpec((1, D_DV), fixed), pl.BlockSpec((CHUNK, CHUNK), fixed)],
        out_specs=pl.BlockSpec((tb, wv), lambda b, g, t: (b * nt + t, 0)),
        scratch_shapes=[pltpu.VMEM((D_HEADS, D_DV, D_DK), F32), pltpu.VMEM((D_HEADS, 1, D_DK), F32),
                        pltpu.VMEM((D_HEADS, 1, LANES), F32),
                        pltpu.VMEM((CONV_HALO, wk), F32), pltpu.VMEM((CONV_HALO, wk), F32),
                        pltpu.VMEM((tb, wk), F32), pltpu.VMEM((tb, wk), F32),
                        pltpu.VMEM((tb + CONV_HALO, wk), F32)],
        compiler_params=_cparams("parallel", "parallel", "arbitrary"),
        name="mlstm",
    )(h, h, h, h, conv_w, conv_w, ig_rows, fg_rows, gb, norm_g.reshape(1, -1), tri)


CD_CQ, CD_IQ, CD_DQ, CD_DK, CD_DV, CD_DOG, CD_CKV, CD_TAIL = 0, 1024, 2048, 2560, 3072, 4096, 5120, 5376
CD_PAD = 5632
TAIL_IK, TAIL_IW, TAIL_DI, TAIL_DF = 0, 64, 80, 84


def _pack_cd(w_in):
    cq, ckv, iq, ik, iw, dq, dk, dv, di, df, dog = jnp.split(
        w_in, [int(i) for i in np.cumsum(
            (C_HEADS * C_DH, C_DLAT, IDX_HEADS * IDX_DIM, IDX_DIM, IDX_HEADS, D_HEADS * D_DK, D_HEADS * D_DK,
             D_HEADS * D_DV, D_HEADS, D_HEADS))], axis=1)
    used = CD_TAIL + IDX_DIM + IDX_HEADS + 2 * D_HEADS
    pad = jnp.zeros((w_in.shape[0], CD_PAD - used), F32)
    main = jnp.concatenate([cq, iq, dq, dk, dv, dog], axis=1).astype(BF16)
    tail = jnp.concatenate([ckv, ik, iw, di, df, pad], axis=1).astype(BF16)
    return main, tail


KB = 2 * QB
IDX_GROUP = 4
ATT_GROUP = 4
SUM_ROWS = 16


def _dsa_prep_kernel(ckv_ref, tail_ref, g_ref, ckv_o, ckvt_o, kdup_o):
    c = ckv_ref[...]
    cn = c * lax.rsqrt(jnp.mean(c * c, axis=-1, keepdims=True) + EPS) * g_ref[...]
    ckv_o[...] = cn.astype(BF16)
    ckvt_o[0:C_DLAT, :] = cn.T.astype(BF16)
    ckvt_o[C_DLAT:C_DLAT + SUM_ROWS, :] = jnp.ones((SUM_ROWS, KB), BF16)
    tail = tail_ref[...]
    lane = lax.broadcasted_iota(I32, tail.shape, 1)
    kdup_o[...] = jnp.where(lane < IDX_DIM, tail, pltpu.roll(tail, IDX_DIM, axis=1)).astype(BF16)


def _dsa_prep(h, ckv_g, n_rows):
    nb = n_rows // KB
    return pl.pallas_call(
        _dsa_prep_kernel,
        out_shape=(jax.ShapeDtypeStruct((nb, KB, C_DLAT), BF16), jax.ShapeDtypeStruct((nb, C_DLAT + SUM_ROWS, KB), BF16),
                   jax.ShapeDtypeStruct((nb, KB, LANES), BF16)),
        grid=(nb,),
        in_specs=[pl.BlockSpec((KB, C_DLAT), lambda i: (i, CD_CKV // C_DLAT)),
                  pl.BlockSpec((KB, LANES), lambda i: (i, CD_TAIL // LANES)),
                  pl.BlockSpec((1, C_DLAT), lambda i: (0, 0))],
        out_specs=(pl.BlockSpec((None, KB, C_DLAT), lambda i: (i, 0, 0)),
                   pl.BlockSpec((None, C_DLAT + SUM_ROWS, KB), lambda i: (i, 0, 0)),
                   pl.BlockSpec((None, KB, LANES), lambda i: (i, 0, 0))),
        compiler_params=_cparams("parallel"),
        name="dsa_prep",
    )(h, h, ckv_g.reshape(1, -1))


def _sortable_key(x):
    b = lax.bitcast_convert_type(x, I32)
    key = b ^ ((b >> 31) & 0x7FFFFFFF)
    return jnp.where(key == -1, 0, key)


BISECT_STEPS = 4


def _indexer_kernel(iq_ref, tail_ref, kdup_ref, tri_ref, mask_ref, key_ref, wst_ref, *, k_sel):
    j = pl.program_id(1)
    nk = key_ref.shape[0]
    n_live = (j * QB + QB + KB - 1) // KB
    w_t = tail_ref[...].T
    lane = lax.broadcasted_iota(I32, (QB, LANES), 1)
    for p in range(IDX_HEADS // 2):
        pair = iq_ref[:, LANES * p:LANES * (p + 1)]
        g, r = divmod(2 * p, IDX_GROUP)
        wst_ref[g, r * QB:(r + 1) * QB, :] = jnp.where(lane < IDX_DIM, pair, 0.0).astype(BF16)
        wst_ref[g, (r + 1) * QB:(r + 2) * QB, :] = jnp.where(lane >= IDX_DIM, pair, 0.0).astype(BF16)

    s_loc = lax.broadcasted_iota(I32, (KB, QB), 0)
    t_abs = j * QB + lax.broadcasted_iota(I32, (1, QB), 1)

    def score_chunk(kc, carry):
        kd = kdup_ref[kc]
        acc = jnp.zeros((KB, QB), F32)
        for g in range(IDX_HEADS // IDX_GROUP):
            dots = _dot_nt(kd, wst_ref[g])
            for r in range(IDX_GROUP):
                row = TAIL_IW + g * IDX_GROUP + r
                acc = acc + jnp.maximum(dots[:, r * QB:(r + 1) * QB], 0.0) * w_t[row:row + 1, :]
        key = jnp.where(kc * KB + s_loc > t_abs, INT_MIN, _sortable_key(acc))
        key_ref[kc] = key
        key = key.reshape(KB // SUBLANES, SUBLANES, QB)
        k_max, k_min = carry
        k_max = jnp.maximum(k_max, jnp.max(key, axis=0))
        k_min = jnp.minimum(k_min, jnp.min(jnp.where(key == INT_MIN, INT_MAX, key), axis=0))
        return k_max, k_min

    def score_two(i, carry):
        return score_chunk(2 * i + 1, score_chunk(2 * i, carry))

    k_max, k_min = lax.fori_loop(0, (n_live + 1) // 2, score_two, (jnp.full((SUBLANES, QB), INT_MIN, I32),
                                                                   jnp.full((SUBLANES, QB), INT_MAX, I32)))
    for shift in (4, 2, 1):
        k_max = jnp.maximum(k_max, pltpu.roll(k_max, shift, axis=0))
        k_min = jnp.minimum(k_min, pltpu.roll(k_min, shift, axis=0))
    k_row = jnp.minimum(k_sel, j * QB + lax.broadcasted_iota(I32, (SUBLANES, QB), 1) + 1)

    def count(pred):
        def add(kc, acc):
            hit = jnp.where(pred(key_ref[kc].reshape(KB // SUBLANES, SUBLANES, QB)), 1, 0)
            return acc + jnp.sum(hit, axis=0)

        def add_two(i, acc):
            return add(2 * i + 1, add(2 * i, acc))
        acc = lax.fori_loop(0, (n_live + 1) // 2, add_two, jnp.zeros((SUBLANES, QB), I32))
        for shift in (4, 2, 1):
            acc = acc + pltpu.roll(acc, shift, axis=0)
        return acc

    def unfinished(state):
        lo, hi = state
        return jnp.max(jnp.where(lo < hi, 1, 0)) > 0

    def halve(state):
        lo, hi = state
        mid = (lo >> 1) + (hi >> 1) + (((lo & 1) + (hi & 1) + 1) >> 1)
        cnt = count(lambda k: k >= mid)
        enough = cnt >= k_row
        lo_n = jnp.where(enough, mid, lo)
        hi_n = jnp.where(cnt == k_row, mid, jnp.where(enough, hi, mid - 1))
        return lo_n, hi_n

    def halve_steps(state):
        for _ in range(BISECT_STEPS):
            state = halve(state)
        return state

    tau8, _ = lax.while_loop(unfinished, halve_steps, (k_min, k_max))
    n_ge = count(lambda k: k >= tau8)
    has_tie = jnp.max(jnp.where(n_ge != k_row, 1, 0)) > 0
    tau = tau8[0:1, :]

    @pl.when(jnp.logical_not(has_tie))
    def _():
        def put(kc, carry):
            mask_ref[kc] = jnp.where(key_ref[kc] >= tau, 1.0, 0.0).astype(BF16)
            return carry
        lax.fori_loop(0, n_live, put, 0)

    @pl.when(has_tie)
    def _():
        need = (k_row - count(lambda k: k > tau8)).astype(F32)[0:1, :]

        def put(kc, seen):
            k = key_ref[kc]
            eq = jnp.where(k == tau, 1.0, 0.0)
            before = _dot(tri_ref[...], eq.astype(BF16)) + seen
            take = jnp.where(k > tau, 1.0, jnp.where(before < need, eq, 0.0))
            mask_ref[kc] = take.astype(BF16)
            return seen + jnp.sum(eq, axis=0, keepdims=True)
        lax.fori_loop(0, n_live, put, jnp.zeros((1, QB), F32))

    def clear(kc, carry):
        mask_ref[kc] = jnp.zeros((KB, QB), BF16)
        return carry
    lax.fori_loop(n_live, nk, clear, 0)


def _indexer(h, kdup, batch, seq):
    nq = seq // QB
    nk = seq // KB
    k_sel = min(TOPK_MAX, seq // 4)
    r = np.arange(KB)
    tri = jnp.asarray(r[None, :] < r[:, None], BF16)
    return pl.pallas_call(
        functools.partial(_indexer_kernel, k_sel=k_sel),
        out_shape=jax.ShapeDtypeStruct((batch, nk, KB, seq), BF16),
        grid=(batch, nq),
        in_specs=[pl.BlockSpec((QB, IDX_HEADS * IDX_DIM), lambda b, j: (b * nq + j, CD_IQ // (IDX_HEADS * IDX_DIM))),
                  pl.BlockSpec((QB, LANES), lambda b, j: (b * nq + j, CD_TAIL // LANES)),
                  pl.BlockSpec((None, nk, KB, LANES), lambda b, j: (b, 0, 0, 0)),
                  pl.BlockSpec((KB, KB), lambda b, j: (0, 0))],
        out_specs=pl.BlockSpec((None, nk, KB, QB), lambda b, j: (b, 0, 0, j)),
        scratch_shapes=[pltpu.VMEM((nk, KB, QB), I32),
                        pltpu.VMEM((IDX_HEADS // IDX_GROUP, IDX_GROUP * QB, LANES), BF16)],
        compiler_params=_cparams("parallel", "parallel"),
        name="dsa_indexer",
    )(h, h, kdup.reshape(batch, nk, KB, LANES), tri)


def _dsa_attn_kernel(cq_ref, mask_ref, ckv_ref, ckvt_ref, wuk_ref, wuvt_ref, bias_ref, o_ref,
                     qt_ref, m_ref, acc_ref, ot_ref, p_ref, alpha_ref):
    j = pl.program_id(1)
    for hh in range(C_HEADS):
        q_h = cq_ref[:, C_DH * hh:C_DH * (hh + 1)].astype(BF16)
        part = slice(QB * (hh % ATT_GROUP), QB * (hh % ATT_GROUP + 1))
        qt_ref[hh // ATT_GROUP, :, part] = (_dot_nt(wuk_ref[hh], q_h) * (C_DH ** -0.5 * LOG2E)).astype(BF16)
    m_ref[...] = jnp.full(m_ref.shape, NEG, F32)
    acc_ref[...] = jnp.zeros_like(acc_ref)
    p_ref[...] = jnp.zeros_like(p_ref)
    alpha_ref[...] = jnp.ones_like(alpha_ref)

    n_live = (j * QB + QB + KB - 1) // KB
    odd = (j % 2) == 1

    def accumulate(kc_done, hp):
        acc_ref[hp] = alpha_ref[hp] * acc_ref[hp] + _dot(ckvt_ref[kc_done], p_ref[hp])

    def body(kc, carry):
        ck = ckv_ref[kc]
        kc_prev = jnp.maximum(kc - 1, 0)
        drop = jnp.where(mask_ref[kc].astype(F32) > 0.5, 0.0, NEG)
        drop = jnp.concatenate([drop] * ATT_GROUP, axis=1)
        back = n_live - 1 - kc
        which = jnp.where(back == 0, jnp.where(odd, 0, 1), jnp.where(jnp.logical_and(back == 1, jnp.logical_not(odd)), 2, 3))
        for hp in range(C_HEADS // ATT_GROUP):
            accumulate(kc_prev, hp)
            logit = _dot(ck, qt_ref[hp]) + (bias_ref[hp, which] + drop)
            m_old = m_ref[hp]
            m_new = jnp.maximum(m_old, jnp.max(logit, axis=0, keepdims=True))
            alpha_ref[hp] = jnp.exp2(m_old - m_new)
            p_ref[hp] = jnp.exp2(logit - m_new).astype(BF16)
            m_ref[hp] = m_new
        return carry

    def body_two(i, carry):
        return body(2 * i + 1, body(2 * i, carry))

    n_pairs = (n_live + 1) // 2
    lax.fori_loop(0, n_pairs, body_two, 0)
    for hp in range(C_HEADS // ATT_GROUP):
        accumulate(2 * n_pairs - 1, hp)
    for hh in range(C_HEADS):
        part = slice(QB * (hh % ATT_GROUP), QB * (hh % ATT_GROUP + 1))
        total = acc_ref[hh // ATT_GROUP, C_DLAT:C_DLAT + 1, part]
        o_lat = (acc_ref[hh // ATT_GROUP, 0:C_DLAT, part] * (1.0 / total)).astype(BF16)
        ot_ref[C_DH * hh:C_DH * (hh + 1), :] = _dot(wuvt_ref[hh], o_lat)
    o_ref[...] = ot_ref[...].T


def _rel_bias_tiles(rel_table):
    s = np.arange(QB)[:, None]
    t = np.arange(QB)[None, :]
    diag, prev, far = np.maximum(t - s, 0), QB + t - s, np.full((QB, QB), 2 * QB)
    kinds = [(prev, diag), (diag, far), (far, prev), (far, far)]
    n = jnp.asarray(np.stack([np.concatenate(k, axis=0) for k in kinds]).astype(np.int32))
    max_exact = REL_BUCKETS // 2
    large = max_exact + (jnp.log(jnp.maximum(n, 1).astype(F32) / max_exact)
                         / math.log(REL_MAX_DIST / max_exact) * (REL_BUCKETS - max_exact)).astype(I32)
    bucket = jnp.where(n < max_exact, n, jnp.minimum(large, REL_BUCKETS - 1))
    onehot = (bucket[..., None] == jnp.arange(REL_BUCKETS, dtype=I32)).astype(F32)
    bias = jnp.einsum("kstb,bh->hkst", onehot, rel_table.astype(F32) * LOG2E, precision=lax.Precision.HIGHEST)
    ng = C_HEADS // ATT_GROUP
    return bias.reshape(ng, ATT_GROUP, 4, KB, QB).transpose(0, 2, 3, 1, 4).reshape(ng, 4, KB, ATT_GROUP * QB)


def _dsa_attention(h, mask, ckv, ckvt, w_uk, w_uv, rel_table, batch, seq):
    nq = seq // QB
    nk = seq // KB
    wuk = w_uk.transpose(1, 0, 2).astype(BF16)
    wuvt = w_uv.transpose(1, 2, 0).astype(BF16)
    bias = _rel_bias_tiles(rel_table)
    return pl.pallas_call(
        _dsa_attn_kernel,
        out_shape=jax.ShapeDtypeStruct((batch * seq, C_HEADS * C_DH), F32),
        grid=(batch, nq),
        in_specs=[pl.BlockSpec((QB, C_HEADS * C_DH), lambda b, j: (b * nq + j, CD_CQ // (C_HEADS * C_DH))),
                  pl.BlockSpec((None, nk, KB, QB), lambda b, j: (b, 0, 0, j)),
                  pl.BlockSpec((None, nk, KB, C_DLAT), lambda b, j: (b, 0, 0, 0)),
                  pl.BlockSpec((None, nk, C_DLAT + SUM_ROWS, KB), lambda b, j: (b, 0, 0, 0)),
                  _const_spec(wuk), _const_spec(wuvt), _const_spec(bias)],
        out_specs=pl.BlockSpec((QB, C_HEADS * C_DH), lambda b, j: (b * nq + j, 0)),
        scratch_shapes=[pltpu.VMEM((C_HEADS // ATT_GROUP, C_DLAT, ATT_GROUP * QB), BF16),
                        pltpu.VMEM((C_HEADS // ATT_GROUP, 1, ATT_GROUP * QB), F32),
                        pltpu.VMEM((C_HEADS // ATT_GROUP, C_DLAT + SUM_ROWS, ATT_GROUP * QB), F32),
                        pltpu.VMEM((C_HEADS * C_DH, QB), F32),
                        pltpu.VMEM((C_HEADS // ATT_GROUP, KB, ATT_GROUP * QB), BF16),
                        pltpu.VMEM((C_HEADS // ATT_GROUP, 1, ATT_GROUP * QB), F32)],
        compiler_params=_cparams("parallel", "parallel"),
        name="dsa_attention",
    )(h, mask, ckv.reshape(batch, nk, KB, C_DLAT), ckvt.reshape(batch, nk, C_DLAT + SUM_ROWS, KB), wuk, wuvt, bias)


ROUTER_BLOCK = 512
MOE_TILE = 512


def _route_block(x, wh_ref, wl_ref, b_ref, upper_ref, idx_ref, wgt_ref, rank_ref, cnt_ref):
    xh = x.astype(BF16)
    xl = (x - xh.astype(F32)).astype(BF16)
    logit = _dot_nt(wh_ref[...], xh) + _dot_nt(wl_ref[...], xh) + _dot_nt(wh_ref[...], xl)
    aff = _sigmoid(logit)
    sel = aff + b_ref[...]
    s_rows = [sel[e:e + 1] for e in range(N_EXPERTS)]
    a_rows = [aff[e:e + 1] for e in range(N_EXPERTS)]
    n = EXPERTS_PER_GROUP

    g_best = jnp.zeros(s_rows[0].shape, I32)
    best = None
    for g in range(N_GROUPS):
        v = s_rows[g * n:(g + 1) * n]
        top2 = None
        for a in range(n):
            for b in range(a + 1, n):
                pair = v[a] + v[b]
                top2 = pair if top2 is None else jnp.maximum(top2, pair)
        if best is None:
            best = top2
        else:
            upd = top2 > best
            g_best = jnp.where(upd, g, g_best)
            best = jnp.where(upd, top2, best)

    sv, av = [], []
    for i in range(n):
        s_i, a_i = s_rows[i], a_rows[i]
        for g in range(1, N_GROUPS):
            pick = g_best == g
            s_i = jnp.where(pick, s_rows[g * n + i], s_i)
            a_i = jnp.where(pick, a_rows[g * n + i], a_i)
        sv.append(s_i)
        av.append(a_i)

    i1, s1, a1 = jnp.zeros_like(g_best), sv[0], av[0]
    for i in range(1, n):
        upd = sv[i] > s1
        i1 = jnp.where(upd, i, i1)
        s1 = jnp.where(upd, sv[i], s1)
        a1 = jnp.where(upd, av[i], a1)
    i2 = jnp.zeros_like(g_best)
    s2 = jnp.full(s1.shape, -jnp.inf, F32)
    a2 = jnp.zeros_like(a1)
    for i in range(n):
        cand = jnp.where(i1 == i, -jnp.inf, sv[i])
        upd = cand > s2
        i2 = jnp.where(upd, i, i2)
        s2 = jnp.where(upd, cand, s2)
        a2 = jnp.where(upd, av[i], a2)

    tot = a1 + a2
    e1 = g_best * n + i1
    e2 = g_best * n + i2
    idx_ref[0:1, :] = e1
    idx_ref[1:2, :] = e2
    wgt_ref[0:1, :] = a1 / tot
    wgt_ref[1:2, :] = a2 / tot

    @pl.when(pl.program_id(0) == 0)
    def _():
        cnt_ref[...] = jnp.zeros_like(cnt_ref)

    e_iota = lax.broadcasted_iota(I32, sel.shape, 0)
    oh1 = jnp.where(e_iota == e1, 1.0, 0.0)
    oh2 = jnp.where(e_iota == e2, 1.0, 0.0)
    both = oh1 + oh2
    before = cnt_ref[...] + _dot(both.astype(BF16), upper_ref[...])
    rank_ref[0:1, :] = jnp.sum(oh1 * before, axis=0, keepdims=True).astype(I32)
    rank_ref[1:2, :] = jnp.sum(oh2 * before, axis=0, keepdims=True).astype(I32)
    cnt_ref[...] = cnt_ref[...] + jnp.sum(both, axis=1, keepdims=True)


def _route_tables(idx, rank, counts, tm, n_items):
    cnt = counts.reshape(-1).astype(I32)
    start = jnp.cumsum(cnt) - cnt
    experts = jnp.arange(N_EXPERTS, dtype=I32)
    pos = rank + jnp.sum(jnp.where(idx[..., None] == experts, start, 0), axis=-1)
    first_tile = start // tm
    n_e = jnp.where(cnt > 0, (start + cnt - 1) // tm - first_tile + 1, 0)
    item_end = jnp.cumsum(n_e)
    item = jnp.arange(n_items, dtype=I32)
    used = item < item_end[-1]
    e_i = jnp.minimum(jnp.sum((item[:, None] >= item_end[None, :]).astype(I32), axis=1), N_EXPERTS - 1)
    e_last = jnp.max(jnp.where(cnt > 0, experts, 0))
    e_i = jnp.where(used, e_i, e_last)
    tile_i = jnp.where(used, first_tile[e_i] + item - (item_end - n_e)[e_i], (TOP_K * idx.shape[1]) // tm - 1)
    lo = jnp.where(used, jnp.maximum(start[e_i], tile_i * tm) - tile_i * tm, 0)
    hi = jnp.where(used, jnp.minimum(start[e_i] + cnt[e_i], (tile_i + 1) * tm) - tile_i * tm, 0)
    return pos.astype(I32), tile_i.astype(I32), e_i, lo.astype(I32), hi.astype(I32)


def _row_of(ref, row):
    return ref.at[lax.shift_right_logical(row, SUBLANES.bit_length() - 1), pl.ds(row & (SUBLANES - 1), 1)]


def _dispatch_kernel(pos_ref, x_ref, xs_hbm, stage, sem, *, tb, n_tok):
    i = pl.program_id(0)
    slot = i % 2
    stage[slot] = x_ref[...].reshape(stage.shape[1:])

    def issue(grp, carry):
        dst = [[pos_ref[s * n_tok + i * tb + grp * SUBLANES + u] for s in range(TOP_K)] for u in range(SUBLANES)]
        for u in range(SUBLANES):
            for s in range(TOP_K):
                row = dst[u][s]
                pltpu.make_async_copy(stage.at[slot, grp, pl.ds(u, 1)], _row_of(xs_hbm, row), sem.at[slot]).start()
        return carry

    lax.fori_loop(0, tb // SUBLANES, issue, 0)

    def wait_block(s):
        for _ in range(TOP_K):
            pltpu.make_async_copy(stage.at[s], xs_hbm.at[pl.ds(0, tb // SUBLANES)], sem.at[s]).wait()

    @pl.when(i >= 1)
    def _():
        wait_block(1 - slot)

    @pl.when(i == pl.num_programs(0) - 1)
    def _():
        wait_block(slot)


def _dispatch(x2d, pos_flat, tb):
    t, d = x2d.shape
    grid_spec = pltpu.PrefetchScalarGridSpec(
        num_scalar_prefetch=1, grid=(t // tb,),
        in_specs=[pl.BlockSpec((tb, d), lambda i, pos: (i, 0))],
        out_specs=pl.BlockSpec(memory_space=pl.ANY),
        scratch_shapes=[pltpu.VMEM((2, tb // SUBLANES, SUBLANES, d), F32), pltpu.SemaphoreType.DMA((2,))])
    return pl.pallas_call(
        functools.partial(_dispatch_kernel, tb=tb, n_tok=t),
        out_shape=jax.ShapeDtypeStruct((TOP_K * t // SUBLANES, SUBLANES, d), F32),
        grid_spec=grid_spec,
        compiler_params=_cparams("arbitrary"),
        name="moe_dispatch",
    )(pos_flat, x2d).reshape(TOP_K * t, d)


def _experts_kernel(tile_ref, exp_ref, lo_ref, hi_ref, x_ref, w1_ref, w3_ref, w2_ref, o_ref, w1b, w3b, w2b):
    i = pl.program_id(0)
    prev = jnp.maximum(i - 1, 0)

    @pl.when(jnp.logical_or(i == 0, exp_ref[i] != exp_ref[prev]))
    def _():
        w1b[...] = w1_ref[...].astype(BF16)
        w3b[...] = w3_ref[...].astype(BF16)
        w2b[...] = w2_ref[...].astype(BF16)

    lo = lo_ref[i]
    hi = hi_ref[i]
    first = jnp.logical_or(i == 0, tile_ref[i] != tile_ref[prev])

    @pl.when(hi > lo)
    def _():
        x = x_ref[...].astype(BF16)
        row = lax.broadcasted_iota(I32, (x.shape[0], 1), 0)
        mine = jnp.where(jnp.logical_and(row >= lo, row < hi), 1.0, 0.0)
        hid = _silu(_dot(x, w1b[...])) * _dot(x, w3b[...]) * mine
        y = _dot(hid.astype(BF16), w2b[...])

        @pl.when(first)
        def _():
            o_ref[...] = y

        @pl.when(jnp.logical_not(first))
        def _():
            o_ref[...] += y


def _experts(xs, tile_i, exp_i, lo, hi, w1, w3, w2, layer, tm):
    n, d = xs.shape
    f = w1.shape[-1]
    by_tile = lambda i, tile, exp, lo, hi: (tile[i], 0)
    by_exp = lambda i, tile, exp, lo, hi: (layer, exp[i], 0, 0)
    grid_spec = pltpu.PrefetchScalarGridSpec(
        num_scalar_prefetch=4, grid=(tile_i.shape[0],),
        in_specs=[pl.BlockSpec((tm, d), by_tile),
                  pl.BlockSpec((None, None, d, f), by_exp), pl.BlockSpec((None, None, d, f), by_exp),
                  pl.BlockSpec((None, None, f, d), by_exp)],
        out_specs=pl.BlockSpec((tm, d), by_tile),
        scratch_shapes=[pltpu.VMEM((d, f), BF16), pltpu.VMEM((d, f), BF16), pltpu.VMEM((f, d), BF16)])
    return pl.pallas_call(
        _experts_kernel,
        out_shape=jax.ShapeDtypeStruct((n, d), F32),
        grid_spec=grid_spec,
        compiler_params=_cparams("arbitrary"),
        name="moe_experts",
    )(tile_i, exp_i, lo, hi, xs, w1, w3, w2)


def _combine_ln_kernel(pos_ref, x_ref, w_ref, g_ref, b_ref, ys_hbm, o_ref, gbuf, sem, *, tb, n_tok):
    i = pl.program_id(0)
    slot = i % 2

    def start(blk, s):
        def issue(grp, carry):
            src = [[pos_ref[k * n_tok + blk * tb + grp * SUBLANES + u] for k in range(TOP_K)] for u in range(SUBLANES)]
            for u in range(SUBLANES):
                for k in range(TOP_K):
                    pltpu.make_async_copy(_row_of(ys_hbm, src[u][k]), gbuf.at[s, k, grp, pl.ds(u, 1)], sem.at[s]).start()
            return carry
        lax.fori_loop(0, tb // SUBLANES, issue, 0)

    @pl.when(i == 0)
    def _():
        start(0, 0)

    @pl.when(i + 1 < pl.num_programs(0))
    def _():
        start(i + 1, 1 - slot)

    for k in range(TOP_K):
        pltpu.make_async_copy(ys_hbm.at[pl.ds(0, tb // SUBLANES)], gbuf.at[slot, k], sem.at[slot]).wait()
    w = w_ref[...]
    rows = x_ref.shape
    y = gbuf[slot, 0].reshape(rows) * w[:, 0:1] + gbuf[slot, 1].reshape(rows) * w[:, 1:2]
    o_ref[...] = _layer_norm_rows(DN_ALPHA * x_ref[...] + y, g_ref[...], b_ref[...])


def _combine_ln(x2d, ys, pos_flat, wgt_cols, g, b, tb):
    t, d = x2d.shape
    row = lambda i, pos: (i, 0)
    fixed = lambda i, pos: (0, 0)
    grid_spec = pltpu.PrefetchScalarGridSpec(
        num_scalar_prefetch=1, grid=(t // tb,),
        in_specs=[pl.BlockSpec((tb, d), row), pl.BlockSpec((tb, TOP_K), row),
                  pl.BlockSpec((1, d), fixed), pl.BlockSpec((1, d), fixed),
                  pl.BlockSpec(memory_space=pl.ANY)],
        out_specs=pl.BlockSpec((tb, d), row),
        scratch_shapes=[pltpu.VMEM((2, TOP_K, tb // SUBLANES, SUBLANES, d), F32), pltpu.SemaphoreType.DMA((2,))])
    return pl.pallas_call(
        functools.partial(_combine_ln_kernel, tb=tb, n_tok=t),
        out_shape=jax.ShapeDtypeStruct((t, d), F32),
        grid_spec=grid_spec,
        compiler_params=_cparams("arbitrary"),
        name="moe_combine_ln",
    )(pos_flat, x2d, wgt_cols, g.reshape(1, d), b.reshape(1, d), ys.reshape(-1, SUBLANES, d))


def _moe_ln(x2d, routing, w1, w3, w2, layer, g, b):
    t = x2d.shape[0]
    tm = min(MOE_TILE, t)
    idx, wgt, rank, counts = routing
    n_items = TOP_K * t // tm + N_EXPERTS - 1
    pos, tile_i, exp_i, lo, hi = _route_tables(idx, rank, counts, tm, n_items)
    pos_flat = pos.reshape(-1)
    xs = _dispatch(x2d, pos_flat, tm)
    ys = _experts(xs, tile_i, exp_i, lo, hi, w1, w3, w2, layer, tm)
    return _combine_ln(x2d, ys, pos_flat, wgt.T, g, b, tm)


def _gate_rows(col, batch, seq):
    tb = min(TIME_BLOCK, seq)
    return col.reshape(batch, seq, D_HEADS).transpose(0, 2, 1).reshape(batch, D_HEADS, seq // tb, tb // CHUNK, CHUNK)


def _mix_cd(x2d, w_in, rel_table, ckv_g, w_uk, w_uv, conv_w, gate_b, d_norm_g, batch, seq):
    t = x2d.shape[0]
    h = _project(x2d, *_pack_cd(w_in), min(1024, t), PROJ_TILE)
    ckv, ckvt, kdup = _dsa_prep(h, ckv_g, t)
    mask = _indexer(h, kdup, batch, seq)
    oc = _dsa_attention(h, mask, ckv, ckvt, w_uk, w_uv, rel_table, batch, seq)
    tail = h[:, CD_TAIL:CD_TAIL + LANES]
    ig_rows = _gate_rows(tail[:, TAIL_DI:TAIL_DI + D_HEADS], batch, seq)
    fg_rows = _gate_rows(tail[:, TAIL_DF:TAIL_DF + D_HEADS], batch, seq)
    od = _mlstm(h, conv_w, ig_rows, fg_rows, gate_b, d_norm_g, batch, seq,
                q_off=CD_DQ // (D_HEADS * D_DK), k_off=CD_DK // (D_HEADS * D_DK),
                v_off=CD_DV // (D_HEADS * D_DV), g_off=CD_DOG // (D_HEADS * D_DV))
    return oc, od


def kernel(x, w_in_ab, w_out_ab, hgrn_lb_logits, a_norm_g, gla_wa2, gla_ba2, b_norm_g, w_in_cd, w_out_cd,
           ckv_norm_g, w_uk, w_uv, mlstm_conv_w, mlstm_gate_b, d_norm_g, rel_table, w_router, b_router,
           moe_w1, moe_w3, moe_w2, ln_g, ln_b):
    batch, seq, d = x.shape
    x2d = x.reshape(batch * seq, d)
    for layer in range(DEPTH):
        li = layer // 2
        if layer % 2 == 0:
            mix_a, mix_b = _mix_ab(x2d, w_in_ab[li], hgrn_lb_logits, li, a_norm_g[li], gla_wa2[li], gla_ba2[li],
                                   b_norm_g[li], batch, seq)
            w_out = w_out_ab[li]
        else:
            mix_a, mix_b = _mix_cd(x2d, w_in_cd[li], rel_table, ckv_norm_g[li], w_uk[li], w_uv[li],
                                   mlstm_conv_w[li], mlstm_gate_b[li], d_norm_g[li], batch, seq)
            w_out = w_out_cd[li]
        ka = mix_a.shape[1]
        x2d, *routing = _outproj_ln(x2d, mix_a, mix_b, w_out[:ka].astype(BF16), w_out[ka:].astype(BF16),
                                    ln_g[layer, 0], ln_b[layer, 0], w_router, b_router)
        x2d = _moe_ln(x2d, routing, moe_w1, moe_w3, moe_w2, layer, ln_g[layer, 1], ln_b[layer, 1])
    return x2d.reshape(batch, seq, d)
```

```python
import functools
import math

import numpy as np
import jax
import jax.numpy as jnp
from jax import lax
from jax.experimental import pallas as pl
from jax.experimental.pallas import tpu as pltpu

F32 = jnp.float32
BF16 = jnp.bfloat16
I32 = jnp.int32

D_MODEL = 2048
DEPTH = 2
A_HEADS, A_DK, A_DV = 8, 128, 128
B_HEADS, B_DK, B_DV = 4, 128, 256
B_GATE_RANK, B_GATE_TAU = 16, 16.0
C_HEADS, C_DH, C_DLAT = 8, 128, 256
IDX_HEADS, IDX_DIM = 16, 64
TOPK_MAX = 256
D_HEADS, D_DK, D_DV = 4, 128, 256
CONV_K = 4
REL_BUCKETS, REL_MAX_DIST = 32, 128
N_EXPERTS, N_GROUPS, TOP_K, D_EXPERT = 16, 4, 2, 512
EXPERTS_PER_GROUP = N_EXPERTS // N_GROUPS
DN_ALPHA = (2 * DEPTH) ** 0.25
EPS = 1e-5

LANES = 128
SUBLANES = 8
VMEM_LIMIT = 56 * 1024 * 1024

CHUNK = 128
N_LEVELS = 7
LOG2E = math.log2(math.e)
TIME_BLOCK = 512
QB = 128
NEG = -1e30
INT_MIN = -2 ** 31
INT_MAX = 2 ** 31 - 1


def _cparams(*sem):
    return pltpu.CompilerParams(dimension_semantics=sem, vmem_limit_bytes=VMEM_LIMIT)


def _dot(a, b):
    return jnp.dot(a, b, preferred_element_type=F32)


def _dot_nt(a, b):
    return lax.dot_general(a, b, (((1,), (1,)), ((), ())), preferred_element_type=F32)


def _dot_tn(a, b):
    return lax.dot_general(a, b, (((0,), (0,)), ((), ())), preferred_element_type=F32)


def _split3(a):
    hi = a.astype(BF16)
    r1 = a - hi.astype(F32)
    mid = r1.astype(BF16)
    lo = (r1 - mid.astype(F32)).astype(BF16)
    return hi, mid, lo


def _dot01(m01, a):
    hi, mid, lo = _split3(a)
    return _dot(m01, hi) + _dot(m01, mid) + _dot(m01, lo)


def _sigmoid(x):
    return 1.0 / (1.0 + jnp.exp(-x))


def _silu(x):
    return x * _sigmoid(x)


def _log_sigmoid(x):
    return jnp.minimum(x, 0.0) - jnp.log(1.0 + jnp.exp(-jnp.abs(x)))


def _proj_kernel(x_ref, w_ref, wt_ref, o_ref, xb_ref):
    j = pl.program_id(1)

    @pl.when(j == 0)
    def _():
        xb_ref[...] = x_ref[...].astype(BF16)

    @pl.when(j < pl.num_programs(1) - 1)
    def _():
        o_ref[...] = _dot(xb_ref[...], w_ref[...])

    @pl.when(j == pl.num_programs(1) - 1)
    def _():
        o_ref[...] = _dot(xb_ref[...], wt_ref[...])


def _cast_kernel(w_ref, o_ref):
    o_ref[...] = w_ref[...].astype(o_ref.dtype)


def _cast_columns(w, n_cols, tn):
    k = w.shape[0]
    return pl.pallas_call(
        _cast_kernel,
        out_shape=jax.ShapeDtypeStruct((k, n_cols), BF16),
        grid=(n_cols // tn,),
        in_specs=[pl.BlockSpec((k, tn), lambda j: (0, j))],
        out_specs=pl.BlockSpec((k, tn), lambda j: (0, j)),
        compiler_params=_cparams("parallel"),
        name="weight_cast",
    )(w)


def _project(x, w, w_tail, tm, tn):
    m, k = x.shape
    n_main = w.shape[1] // tn
    return pl.pallas_call(
        _proj_kernel,
        out_shape=jax.ShapeDtypeStruct((m, (n_main + 1) * tn), F32),
        grid=(m // tm, n_main + 1),
        in_specs=[pl.BlockSpec((tm, k), lambda i, j: (i, 0)),
                  pl.BlockSpec((k, tn), lambda i, j: (0, jnp.minimum(j, n_main - 1))),
                  pl.BlockSpec((k, tn), lambda i, j: (0, 0))],
        out_specs=pl.BlockSpec((tm, tn), lambda i, j: (i, j)),
        scratch_shapes=[pltpu.VMEM((tm, k), BF16)],
        compiler_params=_cparams("parallel", "arbitrary"),
        name="in_proj",
    )(x, w, w_tail)


def _layer_norm_rows(z, g, b):
    mu = jnp.mean(z, axis=-1, keepdims=True)
    zc = z - mu
    var = jnp.mean(zc * zc, axis=-1, keepdims=True)
    return zc * lax.rsqrt(var + EPS) * g + b


def _outproj_ln_kernel(x_ref, ma_ref, mb_ref, wa_ref, wb_ref, g_ref, b_ref, wh_ref, wl_ref, rb_ref, upper_ref,
                       o_ref, idx_ref, wgt_ref, rank_ref, cnt_ref):
    mixed = _dot(ma_ref[...].astype(BF16), wa_ref[...]) + _dot(mb_ref[...].astype(BF16), wb_ref[...])
    z = _layer_norm_rows(DN_ALPHA * x_ref[...] + mixed, g_ref[...], b_ref[...])
    o_ref[...] = z
    _route_block(z, wh_ref, wl_ref, rb_ref, upper_ref, idx_ref, wgt_ref, rank_ref, cnt_ref)


def _outproj_ln(x, mix_a, mix_b, w_a, w_b, g, b, w_router, b_router):
    m, d = x.shape
    tm = min(ROUTER_BLOCK, m)
    ka, kb = mix_a.shape[1], mix_b.shape[1]
    wt = w_router.T.astype(F32)
    wh = wt.astype(BF16)
    wl = (wt - wh.astype(F32)).astype(BF16)
    r = np.arange(tm)
    upper = jnp.asarray(r[:, None] < r[None, :], BF16)
    row = lambda i: (i, 0)
    fixed = lambda i: (0, 0)
    pair_out = pl.BlockSpec((TOP_K, tm), lambda i: (0, i))
    return pl.pallas_call(
        _outproj_ln_kernel,
        out_shape=(jax.ShapeDtypeStruct((m, d), F32),
                   jax.ShapeDtypeStruct((TOP_K, m), I32), jax.ShapeDtypeStruct((TOP_K, m), F32),
                   jax.ShapeDtypeStruct((TOP_K, m), I32), jax.ShapeDtypeStruct((N_EXPERTS, 1), F32)),
        grid=(m // tm,),
        in_specs=[pl.BlockSpec((tm, d), row), pl.BlockSpec((tm, ka), row), pl.BlockSpec((tm, kb), row),
                  pl.BlockSpec((ka, d), fixed), pl.BlockSpec((kb, d), fixed),
                  pl.BlockSpec((1, d), fixed), pl.BlockSpec((1, d), fixed),
                  pl.BlockSpec((N_EXPERTS, d), fixed), pl.BlockSpec((N_EXPERTS, d), fixed),
                  pl.BlockSpec((N_EXPERTS, 1), fixed), pl.BlockSpec((tm, tm), fixed)],
        out_specs=(pl.BlockSpec((tm, d), row), pair_out, pair_out, pair_out, pl.BlockSpec((N_EXPERTS, 1), fixed)),
        compiler_params=_cparams("arbitrary"),
        name="out_proj_ln",
    )(x, mix_a, mix_b, w_a, w_b, g.reshape(1, d), b.reshape(1, d), wh, wl, b_router.reshape(-1, 1).astype(F32), upper)


def _chunk_constants():
    t = np.arange(CHUNK)
    tri = (t[:, None] >= t[None, :]).astype(np.float32)
    pair, odd = [], []
    for lev in range(1, N_LEVELS + 1):
        c = CHUNK >> lev
        pair.append((t[:, None] // (2 * c) == t[None, :] // (2 * c)).astype(np.float32))
        odd.append(np.broadcast_to((((t // c) & 1) == 1).astype(np.float32)[:, None], (CHUNK, LANES)))
    pair.append(np.eye(CHUNK, dtype=np.float32))
    return jnp.asarray(tri, BF16), jnp.asarray(np.stack(pair), F32), jnp.asarray(np.stack(odd), F32)


def _level_log_decay(la, bcum, lev):
    c = CHUNK >> lev
    if 2 * c >= SUBLANES:
        mids = [jnp.broadcast_to(bcum[g * 2 * c + c - 1:g * 2 * c + c], (2 * c, bcum.shape[1]))
                for g in range(CHUNK // (2 * c))]
        return -jnp.abs(bcum - (mids[0] if len(mids) == 1 else jnp.concatenate(mids, axis=0)))
    r = lax.broadcasted_iota(I32, la.shape, 0) & (2 * c - 1)
    if c == 2:
        nxt = pltpu.roll(la, CHUNK - 1, axis=0)
        prv = pltpu.roll(la, 1, axis=0)
        return jnp.where(r == 0, nxt, jnp.where(r == 1, 0.0, jnp.where(r == 2, la, la + prv)))
    return jnp.where(r == 1, la, 0.0)


def _glr_chunk(q, k, v, la, st_ref, cum_ref, pair_ref, odd_ref):
    la = la * LOG2E
    bcum = _dot01(cum_ref[...], la)
    attn = pair_ref[N_LEVELS] * _dot_nt(q.astype(BF16), k.astype(BF16))
    for lev in range(1, N_LEVELS + 1):
        e = jnp.exp2(_level_log_decay(la, bcum, lev))
        c = CHUNK >> lev
        if c >= SUBLANES:
            zero = jnp.zeros((c, q.shape[1]), F32)
            blocks = [slice(b * c, (b + 1) * c) for b in range(CHUNK // c)]
            ql = jnp.concatenate([q[s] * e[s] if b % 2 else zero for b, s in enumerate(blocks)], axis=0)
            kl = jnp.concatenate([zero if b % 2 else k[s] * e[s] for b, s in enumerate(blocks)], axis=0)
        else:
            eq = e * odd_ref[lev - 1]
            ql = q * eq
            kl = k * (e - eq)
        attn = attn + pair_ref[lev - 1] * _dot_nt(ql.astype(BF16), kl.astype(BF16))
    st = st_ref[...]
    o = _dot_nt((q * jnp.exp2(bcum)).astype(BF16), st.astype(BF16)) + _dot(attn.astype(BF16), v.astype(BF16))
    b_last = bcum[CHUNK - 1:CHUNK]
    kdec = (k * jnp.exp2(b_last - bcum)).astype(BF16)
    st_ref[...] = st * jnp.exp2(b_last) + _dot_tn(v.astype(BF16), kdec)
    return o


def _rms_gate(o, g, gate):
    ms = jnp.mean(o * o, axis=-1, keepdims=True)
    return o * lax.rsqrt(ms + EPS) * g * gate


HEAD_GROUP = 4


def _hgrn2_kernel(q_ref, f_ref, i_ref, g_ref, lb_ref, ng_ref, cum_ref, pair_ref, odd_ref, o_ref, st_ref):
    @pl.when(pl.program_id(2) == 0)
    def _():
        st_ref[...] = jnp.zeros_like(st_ref)

    def body(c, carry):
        rows = pl.ds(pl.multiple_of(c * CHUNK, CHUNK), CHUNK)
        for hh in range(HEAD_GROUP):
            ck = slice(A_DK * hh, A_DK * (hh + 1))
            cv = slice(A_DV * hh, A_DV * (hh + 1))
            lb = lb_ref[:, ck]
            f = lb + (1.0 - lb) * _sigmoid(f_ref[rows, ck])
            o = _glr_chunk(_silu(q_ref[rows, ck]), 1.0 - f, i_ref[rows, cv], jnp.log(f),
                           st_ref.at[hh], cum_ref, pair_ref, odd_ref)
            o_ref[rows, cv] = _rms_gate(o, ng_ref[...], _silu(g_ref[rows, cv]))
        return carry

    lax.fori_loop(0, q_ref.shape[0] // CHUNK, body, 0)


def _gla_kernel(q_ref, k_ref, v_ref, g_ref, r_ref, wa_ref, ba_ref, ng_ref, cum_ref, pair_ref, odd_ref,
                o_ref, st_ref):
    @pl.when(pl.program_id(2) == 0)
    def _():
        st_ref[...] = jnp.zeros_like(st_ref)

    def body(c, carry):
        rows = pl.ds(pl.multiple_of(c * CHUNK, CHUNK), CHUNK)
        pre = _dot(r_ref[rows, :].astype(BF16), wa_ref[...]) + ba_ref[...]
        la = _log_sigmoid(pre) * (1.0 / B_GATE_TAU)
        for hh in range(HEAD_GROUP):
            ck = slice(B_DK * hh, B_DK * (hh + 1))
            cv = slice(B_DV * hh, B_DV * (hh + 1))
            o = _glr_chunk(q_ref[rows, ck] * (B_DK ** -0.5), k_ref[rows, ck], v_ref[rows, cv], la[:, ck],
                           st_ref.at[hh], cum_ref, pair_ref, odd_ref)
            o_ref[rows, cv] = _rms_gate(o, ng_ref[...], _silu(g_ref[rows, cv]))
        return carry

    lax.fori_loop(0, q_ref.shape[0] // CHUNK, body, 0)


def _const_spec(arr):
    nd = arr.ndim
    return pl.BlockSpec(arr.shape, lambda *_: (0,) * nd)


def _hgrn2(h, lb, norm_g, batch, seq):
    tb = min(TIME_BLOCK, seq)
    nt = seq // tb
    ng = A_HEADS // HEAD_GROUP
    wk, wv = HEAD_GROUP * A_DK, HEAD_GROUP * A_DV
    consts = _chunk_constants()
    col = lambda seg, w: pl.BlockSpec((tb, w), lambda b, g, t, seg=seg: (b * nt + t, seg * ng + g))
    return pl.pallas_call(
        _hgrn2_kernel,
        out_shape=jax.ShapeDtypeStruct((batch * seq, A_HEADS * A_DV), F32),
        grid=(batch, ng, nt),
        in_specs=[col(0, wk), col(1, wk), col(2, wv), col(3, wv),
                  pl.BlockSpec((1, wk), lambda b, g, t: (0, g)),
                  pl.BlockSpec((1, A_DV), lambda b, g, t: (0, 0))] + [_const_spec(c) for c in consts],
        out_specs=pl.BlockSpec((tb, wv), lambda b, g, t: (b * nt + t, g)),
        scratch_shapes=[pltpu.VMEM((HEAD_GROUP, A_DV, A_DK), F32)],
        compiler_params=_cparams("parallel", "parallel", "arbitrary"),
        name="hgrn2",
    )(h, h, h, h, lb.reshape(1, -1), norm_g.reshape(1, -1), *consts)


def _gla(h, wa2p, ba2, norm_g, batch, seq, q_off, k_off, v_off, g_off, r_off):
    tb = min(TIME_BLOCK, seq)
    nt = seq // tb
    wk, wv = B_HEADS * B_DK, B_HEADS * B_DV
    consts = _chunk_constants()
    col = lambda off, w: pl.BlockSpec((tb, w), lambda b, g, t, off=off: (b * nt + t, off))
    fixed = lambda b, g, t: (0, 0)
    return pl.pallas_call(
        _gla_kernel,
        out_shape=jax.ShapeDtypeStruct((batch * seq, wv), F32),
        grid=(batch, 1, nt),
        in_specs=[col(q_off, wk), col(k_off, wk), col(v_off, wv), col(g_off, wv), col(r_off, LANES),
                  pl.BlockSpec((LANES, wk), fixed), pl.BlockSpec((1, wk), fixed),
                  pl.BlockSpec((1, B_DV), fixed)] + [_const_spec(c) for c in consts],
        out_specs=pl.BlockSpec((tb, wv), lambda b, g, t: (b * nt + t, 0)),
        scratch_shapes=[pltpu.VMEM((B_HEADS, B_DV, B_DK), F32)],
        compiler_params=_cparams("parallel", "parallel", "arbitrary"),
        name="gla",
    )(h, h, h, h, h, wa2p, ba2.reshape(1, -1), norm_g.reshape(1, -1), *consts)


PROJ_TILE = 512
AB_MAIN = 4 * A_HEADS * A_DK + 2 * B_HEADS * B_DK + 2 * B_HEADS * B_DV
AB_PAD = AB_MAIN + PROJ_TILE


def _mix_ab(x2d, w_in, lb_logits, li, a_norm_g, wa2, ba2, b_norm_g, batch, seq):
    d = x2d.shape[1]
    w_tail = jnp.pad(w_in[:, AB_MAIN:], ((0, 0), (0, AB_PAD - w_in.shape[1]))).astype(BF16)
    h = _project(x2d, _cast_columns(w_in, AB_MAIN, PROJ_TILE), w_tail, min(1024, x2d.shape[0]), PROJ_TILE)
    lb = jnp.cumsum(jax.nn.softmax(lb_logits.astype(F32), axis=0), axis=0)[li]
    oa = _hgrn2(h, lb, a_norm_g, batch, seq)
    wa2p = jnp.concatenate([wa2, jnp.zeros((LANES - B_GATE_RANK, wa2.shape[1]), F32)], axis=0).astype(BF16)
    a_cols = 4 * A_HEADS * A_DK
    wk, wv = B_HEADS * B_DK, B_HEADS * B_DV
    ob = _gla(h, wa2p, ba2, b_norm_g, batch, seq, q_off=a_cols // wk, k_off=a_cols // wk + 1,
              v_off=(a_cols + 2 * wk) // wv, g_off=(a_cols + 2 * wk) // wv + 1, r_off=AB_MAIN // LANES)
    return oa, ob


CONV_HALO = 8


def _causal_conv(x_ref, w_ref, buf_ref, tail_ref):
    tb = x_ref.shape[0]
    x = x_ref[...]
    buf_ref[0:CONV_HALO, :] = tail_ref[...]
    buf_ref[CONV_HALO:CONV_HALO + tb, :] = x
    tail_ref[...] = x[tb - CONV_HALO:tb]
    y = w_ref[CONV_K - 1:CONV_K, :] * x
    for j in range(CONV_K - 1):
        y = y + w_ref[j:j + 1, :] * buf_ref[pl.ds(CONV_HALO - (CONV_K - 1) + j, tb), :]
    return y


def _row_to_col(row, eye):
    return jnp.sum(jnp.where(eye, row, 0.0), axis=1, keepdims=True)


def _mlstm_kernel(q_ref, k_ref, v_ref, og_ref, wq_ref, wk_ref, ig_ref, fg_ref, gb_ref, ng_ref, tri_ref,
                  o_ref, ct_ref, n_ref, m_ref, qt_ref, kt_ref, qs_ref, ks_ref, buf_ref):
    @pl.when(pl.program_id(2) == 0)
    def _():
        ct_ref[...] = jnp.zeros_like(ct_ref)
        n_ref[...] = jnp.zeros_like(n_ref)
        m_ref[...] = jnp.zeros_like(m_ref)
        qt_ref[...] = jnp.zeros_like(qt_ref)
        kt_ref[...] = jnp.zeros_like(kt_ref)

    qs_ref[...] = _silu(_causal_conv(q_ref, wq_ref, buf_ref, qt_ref))
    ks_ref[...] = _silu(_causal_conv(k_ref, wk_ref, buf_ref, kt_ref)) * (D_DK ** -0.5)

    r_i = lax.broadcasted_iota(I32, (CHUNK, CHUNK), 0)
    c_i = lax.broadcasted_iota(I32, (CHUNK, CHUNK), 1)
    eye = r_i == c_i
    causal = r_i >= c_i

    def body(c, carry):
        rows = pl.ds(pl.multiple_of(c * CHUNK, CHUNK), CHUNK)
        tri = tri_ref[...]
        for hh in range(D_HEADS):
            ck = slice(D_DK * hh, D_DK * (hh + 1))
            cv = slice(D_DV * hh, D_DV * (hh + 1))
            q = qs_ref[rows, ck]
            k = ks_ref[rows, ck]
            v = v_ref[rows, cv].astype(BF16)
            qb = q.astype(BF16)
            ig_row = ig_ref[hh, pl.ds(c, 1), :] + gb_ref[0, hh]
            lf_row = _log_sigmoid(fg_ref[hh, pl.ds(c, 1), :] + gb_ref[1, hh])
            hi, mid, lo = _split3(lf_row)
            bcum_row = _dot(hi, tri) + _dot(mid, tri) + _dot(lo, tri)
            bcum_col = _row_to_col(bcum_row, eye)
            ig_col = _row_to_col(ig_row, eye)
            m_prev = m_ref[hh, :, 0:1]
            log_w = jnp.where(causal, bcum_col - bcum_row + ig_row, NEG)
            log_inter = bcum_col + m_prev
            m_t = jnp.maximum(jnp.max(log_w, axis=1, keepdims=True), log_inter)
            s = _dot_nt(qb, k.astype(BF16)) * jnp.exp(log_w - m_t)
            w_inter = jnp.exp(log_inter - m_t)
            num = _dot(s.astype(BF16), v) + w_inter * _dot_nt(qb, ct_ref[hh].astype(BF16))
            qn = jnp.sum(s, axis=1, keepdims=True) + w_inter * jnp.sum(q * n_ref[hh], axis=1, keepdims=True)
            h = num / jnp.maximum(jnp.abs(qn), jnp.exp(-m_t))
            o_ref[rows, cv] = _rms_gate(h, ng_ref[...], _sigmoid(og_ref[rows, cv]))
            b_last = bcum_row[:, CHUNK - 1:CHUNK]
            log_u = b_last - bcum_col + ig_col
            m_new = jnp.maximum(b_last + m_prev, jnp.max(log_u, axis=0, keepdims=True))
            decay = jnp.exp(b_last + m_prev - m_new)
            ku = k * jnp.exp(log_u - m_new)
            ct_ref[hh] = decay * ct_ref[hh] + _dot_tn(v, ku.astype(BF16))
            n_ref[hh] = decay * n_ref[hh] + jnp.sum(ku, axis=0, keepdims=True)
            m_ref[hh] = jnp.broadcast_to(m_new, (1, LANES))
        return carry

    lax.fori_loop(0, q_ref.shape[0] // CHUNK, body, 0)


def _mlstm(h, conv_w, ig_rows, fg_rows, gate_b, norm_g, batch, seq, q_off, k_off, v_off, g_off):
    tb = min(TIME_BLOCK, seq)
    nt = seq // tb
    nc = tb // CHUNK
    wk, wv = D_HEADS * D_DK, D_HEADS * D_DV
    t = np.arange(CHUNK)
    tri = jnp.asarray(t[:, None] <= t[None, :], BF16)
    gb = jnp.broadcast_to(gate_b.reshape(2, D_HEADS, 1, 1), (2, D_HEADS, 1, CHUNK)).astype(F32)
    col = lambda off, w: pl.BlockSpec((tb, w), lambda b, g, t, off=off: (b * nt + t, off))
    gate = pl.BlockSpec((None, D_HEADS, None, nc, CHUNK), lambda b, g, t: (b, 0, t, 0, 0))
    fixed = lambda b, g, t: (0, 0)
    return pl.pallas_call(
        _mlstm_kernel,
        out_shape=jax.ShapeDtypeStruct((batch * seq, wv), F32),
        grid=(batch, 1, nt),
        in_specs=[col(q_off, wk), col(k_off, wk), col(v_off, wv), col(g_off, wv),
                  pl.BlockSpec((CONV_K, wk), lambda b, g, t: (0, 0)),
                  pl.BlockSpec((CONV_K, wk), lambda b, g, t: (0, 1)),
                  gate, gate,
                  pl.BlockSpec((2, D_HEADS, 1, CHUNK), lambda b, g, t: (0, 0, 0, 0)),
                  pl.BlockSpec((1, D_DV), fixed), pl.BlockSpec((CHUNK, CHUNK), fixed)],
        out_specs=pl.BlockSpec((tb, wv), lambda b, g, t: (b * nt + t, 0)),
        scratch_shapes=[pltpu.VMEM((D_HEADS, D_DV, D_DK), F32), pltpu.VMEM((D_HEADS, 1, D_DK), F32),
                        pltpu.VMEM((D_HEADS, 1, LANES), F32),
                        pltpu.VMEM((CONV_HALO, wk), F32), pltpu.VMEM((CONV_HALO, wk), F32),
                        pltpu.VMEM((tb, wk), F32), pltpu.VMEM((tb, wk), F32),
                        pltpu.VMEM((tb + CONV_HALO, wk), F32)],
        compiler_params=_cparams("parallel", "parallel", "arbitrary"),
        name="mlstm",
    )(h, h, h, h, conv_w, conv_w, ig_rows, fg_rows, gb, norm_g.reshape(1, -1), tri)


CD_CQ, CD_IQ, CD_DQ, CD_DK, CD_DV, CD_DOG, CD_CKV, CD_TAIL = 0, 1024, 2048, 2560, 3072, 4096, 5120, 5376
CD_PAD = 5632
TAIL_IK, TAIL_IW, TAIL_DI, TAIL_DF = 0, 64, 80, 84


def _pack_cd(w_in):
    cq, ckv, iq, ik, iw, dq, dk, dv, di, df, dog = jnp.split(
        w_in, [int(i) for i in np.cumsum(
            (C_HEADS * C_DH, C_DLAT, IDX_HEADS * IDX_DIM, IDX_DIM, IDX_HEADS, D_HEADS * D_DK, D_HEADS * D_DK,
             D_HEADS * D_DV, D_HEADS, D_HEADS))], axis=1)
    used = CD_TAIL + IDX_DIM + IDX_HEADS + 2 * D_HEADS
    pad = jnp.zeros((w_in.shape[0], CD_PAD - used), F32)
    main = jnp.concatenate([cq, iq, dq, dk, dv, dog], axis=1).astype(BF16)
    tail = jnp.concatenate([ckv, ik, iw, di, df, pad], axis=1).astype(BF16)
    return main, tail


KB = 2 * QB
IDX_GROUP = 4
ATT_GROUP = 4
SUM_ROWS = 16


def _dsa_prep_kernel(ckv_ref, tail_ref, g_ref, ckv_o, ckvt_o, kdup_o):
    c = ckv_ref[...]
    cn = c * lax.rsqrt(jnp.mean(c * c, axis=-1, keepdims=True) + EPS) * g_ref[...]
    ckv_o[...] = cn.astype(BF16)
    ckvt_o[0:C_DLAT, :] = cn.T.astype(BF16)
    ckvt_o[C_DLAT:C_DLAT + SUM_ROWS, :] = jnp.ones((SUM_ROWS, KB), BF16)
    tail = tail_ref[...]
    lane = lax.broadcasted_iota(I32, tail.shape, 1)
    kdup_o[...] = jnp.where(lane < IDX_DIM, tail, pltpu.roll(tail, IDX_DIM, axis=1)).astype(BF16)


def _dsa_prep(h, ckv_g, n_rows):
    nb = n_rows // KB
    return pl.pallas_call(
        _dsa_prep_kernel,
        out_shape=(jax.ShapeDtypeStruct((nb, KB, C_DLAT), BF16), jax.ShapeDtypeStruct((nb, C_DLAT + SUM_ROWS, KB), BF16),
                   jax.ShapeDtypeStruct((nb, KB, LANES), BF16)),
        grid=(nb,),
        in_specs=[pl.BlockSpec((KB, C_DLAT), lambda i: (i, CD_CKV // C_DLAT)),
                  pl.BlockSpec((KB, LANES), lambda i: (i, CD_TAIL // LANES)),
                  pl.BlockSpec((1, C_DLAT), lambda i: (0, 0))],
        out_specs=(pl.BlockSpec((None, KB, C_DLAT), lambda i: (i, 0, 0)),
                   pl.BlockSpec((None, C_DLAT + SUM_ROWS, KB), lambda i: (i, 0, 0)),
                   pl.BlockSpec((None, KB, LANES), lambda i: (i, 0, 0))),
        compiler_params=_cparams("parallel"),
        name="dsa_prep",
    )(h, h, ckv_g.reshape(1, -1))


def _sortable_key(x):
    b = lax.bitcast_convert_type(x, I32)
    key = b ^ ((b >> 31) & 0x7FFFFFFF)
    return jnp.where(key == -1, 0, key)


BISECT_STEPS = 4


def _indexer_kernel(iq_ref, tail_ref, kdup_ref, tri_ref, mask_ref, key_ref, wst_ref, gmax_ref, *, k_sel):
    j = pl.program_id(1)
    nk = key_ref.shape[0]
    n_live = (j * QB + QB + KB - 1) // KB
    w_t = tail_ref[...].T
    lane = lax.broadcasted_iota(I32, (QB, LANES), 1)
    for p in range(IDX_HEADS // 2):
        pair = iq_ref[:, LANES * p:LANES * (p + 1)]
        g, r = divmod(2 * p, IDX_GROUP)
        wst_ref[g, r * QB:(r + 1) * QB, :] = jnp.where(lane < IDX_DIM, pair, 0.0).astype(BF16)
        wst_ref[g, (r + 1) * QB:(r + 2) * QB, :] = jnp.where(lane >= IDX_DIM, pair, 0.0).astype(BF16)

    s_loc = lax.broadcasted_iota(I32, (KB, QB), 0)
    t_abs = j * QB + lax.broadcasted_iota(I32, (1, QB), 1)

    def score_chunk(kc, carry):
        kd = kdup_ref[kc]
        acc = jnp.zeros((KB, QB), F32)
        for g in range(IDX_HEADS // IDX_GROUP):
            dots = _dot_nt(kd, wst_ref[g])
            for r in range(IDX_GROUP):
                row = TAIL_IW + g * IDX_GROUP + r
                acc = acc + jnp.maximum(dots[:, r * QB:(r + 1) * QB], 0.0) * w_t[row:row + 1, :]
        key = jnp.where(kc * KB + s_loc > t_abs, INT_MIN, _sortable_key(acc))
        key_ref[kc] = key
        gmax_ref[...] = jnp.maximum(gmax_ref[...], key)
        key = key.reshape(KB // SUBLANES, SUBLANES, QB)
        k_max, k_min = carry
        k_max = jnp.maximum(k_max, jnp.max(key, axis=0))
        k_min = jnp.minimum(k_min, jnp.min(jnp.where(key == INT_MIN, INT_MAX, key), axis=0))
        return k_max, k_min

    def score_two(i, carry):
        return score_chunk(2 * i + 1, score_chunk(2 * i, carry))

    gmax_ref[...] = jnp.full(gmax_ref.shape, INT_MIN, I32)
    k_max, k_min = lax.fori_loop(0, (n_live + 1) // 2, score_two, (jnp.full((SUBLANES, QB), INT_MIN, I32),
                                                                   jnp.full((SUBLANES, QB), INT_MAX, I32)))
    g_low = jnp.min(gmax_ref[...].reshape(KB // SUBLANES, SUBLANES, QB), axis=0)
    for shift in (4, 2, 1):
        k_max = jnp.maximum(k_max, pltpu.roll(k_max, shift, axis=0))
        k_min = jnp.minimum(k_min, pltpu.roll(k_min, shift, axis=0))
        g_low = jnp.minimum(g_low, pltpu.roll(g_low, shift, axis=0))
    k_low = jnp.maximum(k_min, g_low)
    k_row = jnp.minimum(k_sel, j * QB + lax.broadcasted_iota(I32, (SUBLANES, QB), 1) + 1)

    def count(pred):
        def add(kc, acc):
            hit = jnp.where(pred(key_ref[kc].reshape(KB // SUBLANES, SUBLANES, QB)), 1, 0)
            return acc + jnp.sum(hit, axis=0)

        def add_two(i, acc):
            return add(2 * i + 1, add(2 * i, acc))
        acc = lax.fori_loop(0, (n_live + 1) // 2, add_two, jnp.zeros((SUBLANES, QB), I32))
        for shift in (4, 2, 1):
            acc = acc + pltpu.roll(acc, shift, axis=0)
        return acc

    def unfinished(state):
        lo, hi = state
        return jnp.max(jnp.where(lo < hi, 1, 0)) > 0

    def halve(state):
        lo, hi = state
        mid = (lo >> 1) + (hi >> 1) + (((lo & 1) + (hi & 1) + 1) >> 1)
        cnt = count(lambda k: k >= mid)
        enough = cnt >= k_row
        lo_n = jnp.where(enough, mid, lo)
        hi_n = jnp.where(cnt == k_row, mid, jnp.where(enough, hi, mid - 1))
        return lo_n, hi_n

    def halve_steps(state):
        for _ in range(BISECT_STEPS):
            state = halve(state)
        return state

    tau8, _ = lax.while_loop(unfinished, halve_steps, (k_low, k_max))
    n_ge = count(lambda k: k >= tau8)
    has_tie = jnp.max(jnp.where(n_ge != k_row, 1, 0)) > 0
    tau = tau8[0:1, :]

    @pl.when(jnp.logical_not(has_tie))
    def _():
        def put(kc, carry):
            mask_ref[kc] = jnp.where(key_ref[kc] >= tau, 1.0, 0.0).astype(BF16)
            return carry
        lax.fori_loop(0, n_live, put, 0)

    @pl.when(has_tie)
    def _():
        need = (k_row - count(lambda k: k > tau8)).astype(F32)[0:1, :]

        def put(kc, seen):
            k = key_ref[kc]
            eq = jnp.where(k == tau, 1.0, 0.0)
            before = _dot(tri_ref[...], eq.astype(BF16)) + seen
            take = jnp.where(k > tau, 1.0, jnp.where(before < need, eq, 0.0))
            mask_ref[kc] = take.astype(BF16)
            return seen + jnp.sum(eq, axis=0, keepdims=True)
        lax.fori_loop(0, n_live, put, jnp.zeros((1, QB), F32))

    def clear(kc, carry):
        mask_ref[kc] = jnp.zeros((KB, QB), BF16)
        return carry
    lax.fori_loop(n_live, nk, clear, 0)


def _indexer(h, kdup, batch, seq):
    nq = seq // QB
    nk = seq // KB
    k_sel = min(TOPK_MAX, seq // 4)
    r = np.arange(KB)
    tri = jnp.asarray(r[None, :] < r[:, None], BF16)
    return pl.pallas_call(
        functools.partial(_indexer_kernel, k_sel=k_sel),
        out_shape=jax.ShapeDtypeStruct((batch, nk, KB, seq), BF16),
        grid=(batch, nq),
        in_specs=[pl.BlockSpec((QB, IDX_HEADS * IDX_DIM), lambda b, j: (b * nq + j, CD_IQ // (IDX_HEADS * IDX_DIM))),
                  pl.BlockSpec((QB, LANES), lambda b, j: (b * nq + j, CD_TAIL // LANES)),
                  pl.BlockSpec((None, nk, KB, LANES), lambda b, j: (b, 0, 0, 0)),
                  pl.BlockSpec((KB, KB), lambda b, j: (0, 0))],
        out_specs=pl.BlockSpec((None, nk, KB, QB), lambda b, j: (b, 0, 0, j)),
        scratch_shapes=[pltpu.VMEM((nk, KB, QB), I32),
                        pltpu.VMEM((IDX_HEADS // IDX_GROUP, IDX_GROUP * QB, LANES), BF16),
                        pltpu.VMEM((KB, QB), I32)],
        compiler_params=_cparams("parallel", "parallel"),
        name="dsa_indexer",
    )(h, h, kdup.reshape(batch, nk, KB, LANES), tri)


def _dsa_attn_kernel(cq_ref, mask_ref, ckv_ref, ckvt_ref, wuk_ref, wuvt_ref, bias_ref, o_ref,
                     qt_ref, m_ref, acc_ref, ot_ref, p_ref, alpha_ref):
    j = pl.program_id(1)
    for hh in range(C_HEADS):
        q_h = cq_ref[:, C_DH * hh:C_DH * (hh + 1)].astype(BF16)
        part = slice(QB * (hh % ATT_GROUP), QB * (hh % ATT_GROUP + 1))
        qt_ref[hh // ATT_GROUP, :, part] = (_dot_nt(wuk_ref[hh], q_h) * (C_DH ** -0.5 * LOG2E)).astype(BF16)
    m_ref[...] = jnp.full(m_ref.shape, NEG, F32)
    acc_ref[...] = jnp.zeros_like(acc_ref)
    p_ref[...] = jnp.zeros_like(p_ref)
    alpha_ref[...] = jnp.ones_like(alpha_ref)

    n_live = (j * QB + QB + KB - 1) // KB
    odd = (j % 2) == 1

    def accumulate(kc_done, hp):
        acc_ref[hp] = alpha_ref[hp] * acc_ref[hp] + _dot(ckvt_ref[kc_done], p_ref[hp])

    def body(kc, carry, near):
        ck = ckv_ref[kc]
        kc_prev = jnp.maximum(kc - 1, 0)
        drop = jnp.where(mask_ref[kc].astype(F32) > 0.5, 0.0, NEG)
        drop = jnp.concatenate([drop] * ATT_GROUP, axis=1)
        back = n_live - 1 - kc
        which = jnp.where(back == 0, jnp.where(odd, 0, 1), jnp.where(jnp.logical_and(back == 1, jnp.logical_not(odd)), 2, 3))
        for hp in range(C_HEADS // ATT_GROUP):
            accumulate(kc_prev, hp)
            logit = _dot(ck, qt_ref[hp]) + ((bias_ref[hp, which] + drop) if near else drop)
            m_old = m_ref[hp]
            m_new = jnp.maximum(m_old, jnp.max(logit, axis=0, keepdims=True))
            alpha_ref[hp] = jnp.exp2(m_old - m_new)
            p_ref[hp] = jnp.exp2(logit - m_new).astype(BF16)
            m_ref[hp] = m_new
        return carry

    def body_two(i, carry, near):
        return body(2 * i + 1, body(2 * i, carry, near), near)

    n_pairs = (n_live + 1) // 2
    n_far = jnp.maximum(n_pairs - 2, 0)
    lax.fori_loop(0, n_far, functools.partial(body_two, near=False), 0)
    lax.fori_loop(n_far, n_pairs, functools.partial(body_two, near=True), 0)
    for hp in range(C_HEADS // ATT_GROUP):
        accumulate(2 * n_pairs - 1, hp)
    for hh in range(C_HEADS):
        part = slice(QB * (hh % ATT_GROUP), QB * (hh % ATT_GROUP + 1))
        total = acc_ref[hh // ATT_GROUP, C_DLAT:C_DLAT + 1, part]
        o_lat = (acc_ref[hh // ATT_GROUP, 0:C_DLAT, part] * (1.0 / total)).astype(BF16)
        ot_ref[C_DH * hh:C_DH * (hh + 1), :] = _dot(wuvt_ref[hh], o_lat)
    o_ref[...] = ot_ref[...].T


def _rel_bias_tiles(rel_table):
    s = np.arange(QB)[:, None]
    t = np.arange(QB)[None, :]
    diag, prev, far = np.maximum(t - s, 0), QB + t - s, np.full((QB, QB), 2 * QB)
    kinds = [(prev, diag), (diag, far), (far, prev), (far, far)]
    n = jnp.asarray(np.stack([np.concatenate(k, axis=0) for k in kinds]).astype(np.int32))
    max_exact = REL_BUCKETS // 2
    large = max_exact + (jnp.log(jnp.maximum(n, 1).astype(F32) / max_exact)
                         / math.log(REL_MAX_DIST / max_exact) * (REL_BUCKETS - max_exact)).astype(I32)
    bucket = jnp.where(n < max_exact, n, jnp.minimum(large, REL_BUCKETS - 1))
    onehot = (bucket[..., None] == jnp.arange(REL_BUCKETS, dtype=I32)).astype(F32)
    bias = jnp.einsum("kstb,bh->hkst", onehot, rel_table.astype(F32) * LOG2E, precision=lax.Precision.HIGHEST)
    bias = bias - bias[:, 3:4, 0:1, 0:1]
    ng = C_HEADS // ATT_GROUP
    return bias.reshape(ng, ATT_GROUP, 4, KB, QB).transpose(0, 2, 3, 1, 4).reshape(ng, 4, KB, ATT_GROUP * QB)


def _dsa_attention(h, mask, ckv, ckvt, w_uk, w_uv, rel_table, batch, seq):
    nq = seq // QB
    nk = seq // KB
    wuk = w_uk.transpose(1, 0, 2).astype(BF16)
    wuvt = w_uv.transpose(1, 2, 0).astype(BF16)
    bias = _rel_bias_tiles(rel_table)
    return pl.pallas_call(
        _dsa_attn_kernel,
        out_shape=jax.ShapeDtypeStruct((batch * seq, C_HEADS * C_DH), F32),
        grid=(batch, nq),
        in_specs=[pl.BlockSpec((QB, C_HEADS * C_DH), lambda b, j: (b * nq + j, CD_CQ // (C_HEADS * C_DH))),
                  pl.BlockSpec((None, nk, KB, QB), lambda b, j: (b, 0, 0, j)),
                  pl.BlockSpec((None, nk, KB, C_DLAT), lambda b, j: (b, 0, 0, 0)),
                  pl.BlockSpec((None, nk, C_DLAT + SUM_ROWS, KB), lambda b, j: (b, 0, 0, 0)),
                  _const_spec(wuk), _const_spec(wuvt), _const_spec(bias)],
        out_specs=pl.BlockSpec((QB, C_HEADS * C_DH), lambda b, j: (b * nq + j, 0)),
        scratch_shapes=[pltpu.VMEM((C_HEADS // ATT_GROUP, C_DLAT, ATT_GROUP * QB), BF16),
                        pltpu.VMEM((C_HEADS // ATT_GROUP, 1, ATT_GROUP * QB), F32),
                        pltpu.VMEM((C_HEADS // ATT_GROUP, C_DLAT + SUM_ROWS, ATT_GROUP * QB), F32),
                        pltpu.VMEM((C_HEADS * C_DH, QB), F32),
                        pltpu.VMEM((C_HEADS // ATT_GROUP, KB, ATT_GROUP * QB), BF16),
                        pltpu.VMEM((C_HEADS // ATT_GROUP, 1, ATT_GROUP * QB), F32)],
        compiler_params=_cparams("parallel", "parallel"),
        name="dsa_attention",
    )(h, mask, ckv.reshape(batch, nk, KB, C_DLAT), ckvt.reshape(batch, nk, C_DLAT + SUM_ROWS, KB), wuk, wuvt, bias)


ROUTER_BLOCK = 512
MOE_TILE = 512


def _route_block(x, wh_ref, wl_ref, b_ref, upper_ref, idx_ref, wgt_ref, rank_ref, cnt_ref):
    xh = x.astype(BF16)
    xl = (x - xh.astype(F32)).astype(BF16)
    logit = _dot_nt(wh_ref[...], xh) + _dot_nt(wl_ref[...], xh) + _dot_nt(wh_ref[...], xl)
    aff = _sigmoid(logit)
    sel = aff + b_ref[...]
    s_rows = [sel[e:e + 1] for e in range(N_EXPERTS)]
    a_rows = [aff[e:e + 1] for e in range(N_EXPERTS)]
    n = EXPERTS_PER_GROUP

    g_best = jnp.zeros(s_rows[0].shape, I32)
    best = None
    for g in range(N_GROUPS):
        v = s_rows[g * n:(g + 1) * n]
        top2 = None
        for a in range(n):
            for b in range(a + 1, n):
                pair = v[a] + v[b]
                top2 = pair if top2 is None else jnp.maximum(top2, pair)
        if best is None:
            best = top2
        else:
            upd = top2 > best
            g_best = jnp.where(upd, g, g_best)
            best = jnp.where(upd, top2, best)

    sv, av = [], []
    for i in range(n):
        s_i, a_i = s_rows[i], a_rows[i]
        for g in range(1, N_GROUPS):
            pick = g_best == g
            s_i = jnp.where(pick, s_rows[g * n + i], s_i)
            a_i = jnp.where(pick, a_rows[g * n + i], a_i)
        sv.append(s_i)
        av.append(a_i)

    i1, s1, a1 = jnp.zeros_like(g_best), sv[0], av[0]
    for i in range(1, n):
        upd = sv[i] > s1
        i1 = jnp.where(upd, i, i1)
        s1 = jnp.where(upd, sv[i], s1)
        a1 = jnp.where(upd, av[i], a1)
    i2 = jnp.zeros_like(g_best)
    s2 = jnp.full(s1.shape, -jnp.inf, F32)
    a2 = jnp.zeros_like(a1)
    for i in range(n):
        cand = jnp.where(i1 == i, -jnp.inf, sv[i])
        upd = cand > s2
        i2 = jnp.where(upd, i, i2)
        s2 = jnp.where(upd, cand, s2)
        a2 = jnp.where(upd, av[i], a2)

    tot = a1 + a2
    e1 = g_best * n + i1
    e2 = g_best * n + i2
    idx_ref[0:1, :] = e1
    idx_ref[1:2, :] = e2
    wgt_ref[0:1, :] = a1 / tot
    wgt_ref[1:2, :] = a2 / tot

    @pl.when(pl.program_id(0) == 0)
    def _():
        cnt_ref[...] = jnp.zeros_like(cnt_ref)

    e_iota = lax.broadcasted_iota(I32, sel.shape, 0)
    oh1 = jnp.where(e_iota == e1, 1.0, 0.0)
    oh2 = jnp.where(e_iota == e2, 1.0, 0.0)
    both = oh1 + oh2
    before = cnt_ref[...] + _dot(both.astype(BF16), upper_ref[...])
    rank_ref[0:1, :] = jnp.sum(oh1 * before, axis=0, keepdims=True).astype(I32)
    rank_ref[1:2, :] = jnp.sum(oh2 * before, axis=0, keepdims=True).astype(I32)
    cnt_ref[...] = cnt_ref[...] + jnp.sum(both, axis=1, keepdims=True)


def _route_tables(idx, rank, counts, tm, n_items):
    cnt = counts.reshape(-1).astype(I32)
    start = jnp.cumsum(cnt) - cnt
    experts = jnp.arange(N_EXPERTS, dtype=I32)
    pos = rank + jnp.sum(jnp.where(idx[..., None] == experts, start, 0), axis=-1)
    first_tile = start // tm
    n_e = jnp.where(cnt > 0, (start + cnt - 1) // tm - first_tile + 1, 0)
    item_end = jnp.cumsum(n_e)
    item = jnp.arange(n_items, dtype=I32)
    used = item < item_end[-1]
    e_i = jnp.minimum(jnp.sum((item[:, None] >= item_end[None, :]).astype(I32), axis=1), N_EXPERTS - 1)
    e_last = jnp.max(jnp.where(cnt > 0, experts, 0))
    e_i = jnp.where(used, e_i, e_last)
    tile_i = jnp.where(used, first_tile[e_i] + item - (item_end - n_e)[e_i], (TOP_K * idx.shape[1]) // tm - 1)
    lo = jnp.where(used, jnp.maximum(start[e_i], tile_i * tm) - tile_i * tm, 0)
    hi = jnp.where(used, jnp.minimum(start[e_i] + cnt[e_i], (tile_i + 1) * tm) - tile_i * tm, 0)
    return pos.astype(I32), tile_i.astype(I32), e_i, lo.astype(I32), hi.astype(I32)


def _row_of(ref, row):
    return ref.at[lax.shift_right_logical(row, SUBLANES.bit_length() - 1), pl.ds(row & (SUBLANES - 1), 1)]


def _dispatch_kernel(pos_ref, x_ref, xs_hbm, stage, sem, *, tb, n_tok):
    i = pl.program_id(0)
    slot = i % 2
    stage[slot] = x_ref[...].reshape(stage.shape[1:])

    def issue(grp, carry):
        dst = [[pos_ref[s * n_tok + i * tb + grp * SUBLANES + u] for s in range(TOP_K)] for u in range(SUBLANES)]
        for u in range(SUBLANES):
            for s in range(TOP_K):
                row = dst[u][s]
                pltpu.make_async_copy(stage.at[slot, grp, pl.ds(u, 1)], _row_of(xs_hbm, row), sem.at[slot]).start()
        return carry

    lax.fori_loop(0, tb // SUBLANES, issue, 0)

    def wait_block(s):
        for _ in range(TOP_K):
            pltpu.make_async_copy(stage.at[s], xs_hbm.at[pl.ds(0, tb // SUBLANES)], sem.at[s]).wait()

    @pl.when(i >= 1)
    def _():
        wait_block(1 - slot)

    @pl.when(i == pl.num_programs(0) - 1)
    def _():
        wait_block(slot)


def _dispatch(x2d, pos_flat, tb):
    t, d = x2d.shape
    grid_spec = pltpu.PrefetchScalarGridSpec(
        num_scalar_prefetch=1, grid=(t // tb,),
        in_specs=[pl.BlockSpec((tb, d), lambda i, pos: (i, 0))],
        out_specs=pl.BlockSpec(memory_space=pl.ANY),
        scratch_shapes=[pltpu.VMEM((2, tb // SUBLANES, SUBLANES, d), F32), pltpu.SemaphoreType.DMA((2,))])
    return pl.pallas_call(
        functools.partial(_dispatch_kernel, tb=tb, n_tok=t),
        out_shape=jax.ShapeDtypeStruct((TOP_K * t // SUBLANES, SUBLANES, d), F32),
        grid_spec=grid_spec,
        compiler_params=_cparams("arbitrary"),
        name="moe_dispatch",
    )(pos_flat, x2d).reshape(TOP_K * t, d)


def _experts_kernel(tile_ref, exp_ref, lo_ref, hi_ref, x_ref, w1_ref, w3_ref, w2_ref, o_ref, w1b, w3b, w2b):
    i = pl.program_id(0)
    prev = jnp.maximum(i - 1, 0)

    @pl.when(jnp.logical_or(i == 0, exp_ref[i] != exp_ref[prev]))
    def _():
        w1b[...] = w1_ref[...].astype(BF16)
        w3b[...] = w3_ref[...].astype(BF16)
        w2b[...] = w2_ref[...].astype(BF16)

    lo = lo_ref[i]
    hi = hi_ref[i]
    first = jnp.logical_or(i == 0, tile_ref[i] != tile_ref[prev])

    @pl.when(hi > lo)
    def _():
        x = x_ref[...].astype(BF16)
        row = lax.broadcasted_iota(I32, (x.shape[0], 1), 0)
        mine = jnp.where(jnp.logical_and(row >= lo, row < hi), 1.0, 0.0)
        hid = _silu(_dot(x, w1b[...])) * _dot(x, w3b[...]) * mine
        y = _dot(hid.astype(BF16), w2b[...])

        @pl.when(first)
        def _():
            o_ref[...] = y

        @pl.when(jnp.logical_not(first))
        def _():
            o_ref[...] += y


def _experts(xs, tile_i, exp_i, lo, hi, w1, w3, w2, layer, tm):
    n, d = xs.shape
    f = w1.shape[-1]
    by_tile = lambda i, tile, exp, lo, hi: (tile[i], 0)
    by_exp = lambda i, tile, exp, lo, hi: (layer, exp[i], 0, 0)
    grid_spec = pltpu.PrefetchScalarGridSpec(
        num_scalar_prefetch=4, grid=(tile_i.shape[0],),
        in_specs=[pl.BlockSpec((tm, d), by_tile),
                  pl.BlockSpec((None, None, d, f), by_exp), pl.BlockSpec((None, None, d, f), by_exp),
                  pl.BlockSpec((None, None, f, d), by_exp)],
        out_specs=pl.BlockSpec((tm, d), by_tile),
        scratch_shapes=[pltpu.VMEM((d, f), BF16), pltpu.VMEM((d, f), BF16), pltpu.VMEM((f, d), BF16)])
    return pl.pallas_call(
        _experts_kernel,
        out_shape=jax.ShapeDtypeStruct((n, d), F32),
        grid_spec=grid_spec,
        compiler_params=_cparams("arbitrary"),
        name="moe_experts",
    )(tile_i, exp_i, lo, hi, xs, w1, w3, w2)


def _combine_ln_kernel(pos_ref, x_ref, w_ref, g_ref, b_ref, ys_hbm, o_ref, gbuf, sem, *, tb, n_tok):
    i = pl.program_id(0)
    slot = i % 2

    def start(blk, s):
        def issue(grp, carry):
            src = [[pos_ref[k * n_tok + blk * tb + grp * SUBLANES + u] for k in range(TOP_K)] for u in range(SUBLANES)]
            for u in range(SUBLANES):
                for k in range(TOP_K):
                    pltpu.make_async_copy(_row_of(ys_hbm, src[u][k]), gbuf.at[s, k, grp, pl.ds(u, 1)], sem.at[s]).start()
            return carry
        lax.fori_loop(0, tb // SUBLANES, issue, 0)

    @pl.when(i == 0)
    def _():
        start(0, 0)

    @pl.when(i + 1 < pl.num_programs(0))
    def _():
        start(i + 1, 1 - slot)

    for k in range(TOP_K):
        pltpu.make_async_copy(ys_hbm.at[pl.ds(0, tb // SUBLANES)], gbuf.at[slot, k], sem.at[slot]).wait()
    w = w_ref[...]
    rows = x_ref.shape
    y = gbuf[slot, 0].reshape(rows) * w[:, 0:1] + gbuf[slot, 1].reshape(rows) * w[:, 1:2]
    o_ref[...] = _layer_norm_rows(DN_ALPHA * x_ref[...] + y, g_ref[...], b_ref[...])


def _combine_ln(x2d, ys, pos_flat, wgt_cols, g, b, tb):
    t, d = x2d.shape
    row = lambda i, pos: (i, 0)
    fixed = lambda i, pos: (0, 0)
    grid_spec = pltpu.PrefetchScalarGridSpec(
        num_scalar_prefetch=1, grid=(t // tb,),
        in_specs=[pl.BlockSpec((tb, d), row), pl.BlockSpec((tb, TOP_K), row),
                  pl.BlockSpec((1, d), fixed), pl.BlockSpec((1, d), fixed),
                  pl.BlockSpec(memory_space=pl.ANY)],
        out_specs=pl.BlockSpec((tb, d), row),
        scratch_shapes=[pltpu.VMEM((2, TOP_K, tb // SUBLANES, SUBLANES, d), F32), pltpu.SemaphoreType.DMA((2,))])
    return pl.pallas_call(
        functools.partial(_combine_ln_kernel, tb=tb, n_tok=t),
        out_shape=jax.ShapeDtypeStruct((t, d), F32),
        grid_spec=grid_spec,
        compiler_params=_cparams("arbitrary"),
        name="moe_combine_ln",
    )(pos_flat, x2d, wgt_cols, g.reshape(1, d), b.reshape(1, d), ys.reshape(-1, SUBLANES, d))


def _moe_ln(x2d, routing, w1, w3, w2, layer, g, b):
    t = x2d.shape[0]
    tm = min(MOE_TILE, t)
    idx, wgt, rank, counts = routing
    n_items = TOP_K * t // tm + N_EXPERTS - 1
    pos, tile_i, exp_i, lo, hi = _route_tables(idx, rank, counts, tm, n_items)
    pos_flat = pos.reshape(-1)
    xs = _dispatch(x2d, pos_flat, tm)
    ys = _experts(xs, tile_i, exp_i, lo, hi, w1, w3, w2, layer, tm)
    return _combine_ln(x2d, ys, pos_flat, wgt.T, g, b, tm)


def _gate_rows(col, batch, seq):
    tb = min(TIME_BLOCK, seq)
    return col.reshape(batch, seq, D_HEADS).transpose(0, 2, 1).reshape(batch, D_HEADS, seq // tb, tb // CHUNK, CHUNK)


def _mix_cd(x2d, w_in, rel_table, ckv_g, w_uk, w_uv, conv_w, gate_b, d_norm_g, batch, seq):
    t = x2d.shape[0]
    h = _project(x2d, *_pack_cd(w_in), min(1024, t), PROJ_TILE)
    ckv, ckvt, kdup = _dsa_prep(h, ckv_g, t)
    mask = _indexer(h, kdup, batch, seq)
    oc = _dsa_attention(h, mask, ckv, ckvt, w_uk, w_uv, rel_table, batch, seq)
    tail = h[:, CD_TAIL:CD_TAIL + LANES]
    ig_rows = _gate_rows(tail[:, TAIL_DI:TAIL_DI + D_HEADS], batch, seq)
    fg_rows = _gate_rows(tail[:, TAIL_DF:TAIL_DF + D_HEADS], batch, seq)
    od = _mlstm(h, conv_w, ig_rows, fg_rows, gate_b, d_norm_g, batch, seq,
                q_off=CD_DQ // (D_HEADS * D_DK), k_off=CD_DK // (D_HEADS * D_DK),
                v_off=CD_DV // (D_HEADS * D_DV), g_off=CD_DOG // (D_HEADS * D_DV))
    return oc, od


def kernel(x, w_in_ab, w_out_ab, hgrn_lb_logits, a_norm_g, gla_wa2, gla_ba2, b_norm_g, w_in_cd, w_out_cd,
           ckv_norm_g, w_uk, w_uv, mlstm_conv_w, mlstm_gate_b, d_norm_g, rel_table, w_router, b_router,
           moe_w1, moe_w3, moe_w2, ln_g, ln_b):
    batch, seq, d = x.shape
    x2d = x.reshape(batch * seq, d)
    for layer in range(DEPTH):
        li = layer // 2
        if layer % 2 == 0:
            mix_a, mix_b = _mix_ab(x2d, w_in_ab[li], hgrn_lb_logits, li, a_norm_g[li], gla_wa2[li], gla_ba2[li],
                                   b_norm_g[li], batch, seq)
            w_out = w_out_ab[li]
        else:
            mix_a, mix_b = _mix_cd(x2d, w_in_cd[li], rel_table, ckv_norm_g[li], w_uk[li], w_uv[li],
                                   mlstm_conv_w[li], mlstm_gate_b[li], d_norm_g[li], batch, seq)
            w_out = w_out_cd[li]
        ka = mix_a.shape[1]
        x2d, *routing = _outproj_ln(x2d, mix_a, mix_b, w_out[:ka].astype(BF16), w_out[ka:].astype(BF16),
                                    ln_g[layer, 0], ln_b[layer, 0], w_router, b_router)
        x2d = _moe_ln(x2d, routing, moe_w1, moe_w3, moe_w2, layer, ln_g[layer, 1], ln_b[layer, 1])
    return x2d.reshape(batch, seq, d)
```

```python
import functools
import math

import numpy as np
import jax
import jax.numpy as jnp
from jax import lax
from jax.experimental import pallas as pl
from jax.experimental.pallas import tpu as pltpu

F32 = jnp.float32
BF16 = jnp.bfloat16
I32 = jnp.int32

D_MODEL = 2048
DEPTH = 2
A_HEADS, A_DK, A_DV = 8, 128, 128
B_HEADS, B_DK, B_DV = 4, 128, 256
B_GATE_RANK, B_GATE_TAU = 16, 16.0
C_HEADS, C_DH, C_DLAT = 8, 128, 256
IDX_HEADS, IDX_DIM = 16, 64
TOPK_MAX = 256
D_HEADS, D_DK, D_DV = 4, 128, 256
CONV_K = 4
REL_BUCKETS, REL_MAX_DIST = 32, 128
N_EXPERTS, N_GROUPS, TOP_K, D_EXPERT = 16, 4, 2, 512
EXPERTS_PER_GROUP = N_EXPERTS // N_GROUPS
DN_ALPHA = (2 * DEPTH) ** 0.25
EPS = 1e-5

LANES = 128
SUBLANES = 8
VMEM_LIMIT = 56 * 1024 * 1024

CHUNK = 128
N_LEVELS = 7
LOG2E = math.log2(math.e)
TIME_BLOCK = 512
QB = 128
NEG = -1e30
INT_MIN = -2 ** 31
INT_MAX = 2 ** 31 - 1


def _cparams(*sem):
    return pltpu.CompilerParams(dimension_semantics=sem, vmem_limit_bytes=VMEM_LIMIT)


def _dot(a, b):
    return jnp.dot(a, b, preferred_element_type=F32)


def _dot_nt(a, b):
    return lax.dot_general(a, b, (((1,), (1,)), ((), ())), preferred_element_type=F32)


def _dot_tn(a, b):
    return lax.dot_general(a, b, (((0,), (0,)), ((), ())), preferred_element_type=F32)


def _split3(a):
    hi = a.astype(BF16)
    r1 = a - hi.astype(F32)
    mid = r1.astype(BF16)
    lo = (r1 - mid.astype(F32)).astype(BF16)
    return hi, mid, lo


def _dot01(m01, a):
    hi, mid, lo = _split3(a)
    return _dot(m01, hi) + _dot(m01, mid) + _dot(m01, lo)


def _sigmoid(x):
    return 1.0 / (1.0 + jnp.exp(-x))


def _silu(x):
    return x * _sigmoid(x)


def _log_sigmoid(x):
    return jnp.minimum(x, 0.0) - jnp.log(1.0 + jnp.exp(-jnp.abs(x)))


def _proj_kernel(x_ref, w_ref, wt_ref, o_ref, xb_ref):
    j = pl.program_id(1)

    @pl.when(j == 0)
    def _():
        xb_ref[...] = x_ref[...].astype(BF16)

    @pl.when(j < pl.num_programs(1) - 1)
    def _():
        o_ref[...] = _dot(xb_ref[...], w_ref[...])

    @pl.when(j == pl.num_programs(1) - 1)
    def _():
        o_ref[...] = _dot(xb_ref[...], wt_ref[...])


def _cast_kernel(w_ref, o_ref):
    o_ref[...] = w_ref[...].astype(o_ref.dtype)


def _cast_columns(w, n_cols, tn):
    k = w.shape[0]
    return pl.pallas_call(
        _cast_kernel,
        out_shape=jax.ShapeDtypeStruct((k, n_cols), BF16),
        grid=(n_cols // tn,),
        in_specs=[pl.BlockSpec((k, tn), lambda j: (0, j))],
        out_specs=pl.BlockSpec((k, tn), lambda j: (0, j)),
        compiler_params=_cparams("parallel"),
        name="weight_cast",
    )(w)


def _project(x, w, w_tail, tm, tn):
    m, k = x.shape
    n_main = w.shape[1] // tn
    return pl.pallas_call(
        _proj_kernel,
        out_shape=jax.ShapeDtypeStruct((m, (n_main + 1) * tn), F32),
        grid=(m // tm, n_main + 1),
        in_specs=[pl.BlockSpec((tm, k), lambda i, j: (i, 0), pipeline_mode=pl.Buffered(1)),
                  pl.BlockSpec((k, tn), lambda i, j: (0, jnp.minimum(j, n_main - 1))),
                  pl.BlockSpec((k, tn), lambda i, j: (0, 0))],
        out_specs=pl.BlockSpec((tm, tn), lambda i, j: (i, j)),
        scratch_shapes=[pltpu.VMEM((tm, k), BF16)],
        compiler_params=_cparams("parallel", "arbitrary"),
        name="in_proj",
    )(x, w, w_tail)


def _layer_norm_rows(z, g, b):
    mu = jnp.mean(z, axis=-1, keepdims=True)
    zc = z - mu
    var = jnp.mean(zc * zc, axis=-1, keepdims=True)
    return zc * lax.rsqrt(var + EPS) * g + b


def _outproj_ln_kernel(x_ref, ma_ref, mb_ref, wa_ref, wb_ref, g_ref, b_ref, wh_ref, wl_ref, rb_ref, upper_ref,
                       o_ref, idx_ref, wgt_ref, rank_ref, cnt_ref):
    mixed = _dot(ma_ref[...].astype(BF16), wa_ref[...]) + _dot(mb_ref[...].astype(BF16), wb_ref[...])
    z = _layer_norm_rows(DN_ALPHA * x_ref[...] + mixed, g_ref[...], b_ref[...])
    o_ref[...] = z
    _route_block(z, wh_ref, wl_ref, rb_ref, upper_ref, idx_ref, wgt_ref, rank_ref, cnt_ref)


def _outproj_ln(x, mix_a, mix_b, w_a, w_b, g, b, w_router, b_router):
    m, d = x.shape
    tm = min(ROUTER_BLOCK, m)
    ka, kb = mix_a.shape[1], mix_b.shape[1]
    wt = w_router.T.astype(F32)
    wh = wt.astype(BF16)
    wl = (wt - wh.astype(F32)).astype(BF16)
    r = np.arange(tm)
    upper = jnp.asarray(r[:, None] < r[None, :], BF16)
    row = lambda i: (i, 0)
    fixed = lambda i: (0, 0)
    pair_out = pl.BlockSpec((TOP_K, tm), lambda i: (0, i))
    return pl.pallas_call(
        _outproj_ln_kernel,
        out_shape=(jax.ShapeDtypeStruct((m, d), F32),
                   jax.ShapeDtypeStruct((TOP_K, m), I32), jax.ShapeDtypeStruct((TOP_K, m), F32),
                   jax.ShapeDtypeStruct((TOP_K, m), I32), jax.ShapeDtypeStruct((N_EXPERTS, 1), F32)),
        grid=(m // tm,),
        in_specs=[pl.BlockSpec((tm, d), row), pl.BlockSpec((tm, ka), row), pl.BlockSpec((tm, kb), row),
                  pl.BlockSpec((ka, d), fixed), pl.BlockSpec((kb, d), fixed),
                  pl.BlockSpec((1, d), fixed), pl.BlockSpec((1, d), fixed),
                  pl.BlockSpec((N_EXPERTS, d), fixed), pl.BlockSpec((N_EXPERTS, d), fixed),
                  pl.BlockSpec((N_EXPERTS, 1), fixed), pl.BlockSpec((tm, tm), fixed)],
        out_specs=(pl.BlockSpec((tm, d), row), pair_out, pair_out, pair_out, pl.BlockSpec((N_EXPERTS, 1), fixed)),
        compiler_params=_cparams("arbitrary"),
        name="out_proj_ln",
    )(x, mix_a, mix_b, w_a, w_b, g.reshape(1, d), b.reshape(1, d), wh, wl, b_router.reshape(-1, 1).astype(F32), upper)


def _chunk_constants():
    t = np.arange(CHUNK)
    tri = (t[:, None] >= t[None, :]).astype(np.float32)
    pair, odd = [], []
    for lev in range(1, N_LEVELS + 1):
        c = CHUNK >> lev
        pair.append((t[:, None] // (2 * c) == t[None, :] // (2 * c)).astype(np.float32))
        odd.append(np.broadcast_to((((t // c) & 1) == 1).astype(np.float32)[:, None], (CHUNK, LANES)))
    pair.append(np.eye(CHUNK, dtype=np.float32))
    return jnp.asarray(tri, BF16), jnp.asarray(np.stack(pair), F32), jnp.asarray(np.stack(odd), F32)


def _level_log_decay(la, bcum, lev):
    c = CHUNK >> lev
    if 2 * c >= SUBLANES:
        mids = [jnp.broadcast_to(bcum[g * 2 * c + c - 1:g * 2 * c + c], (2 * c, bcum.shape[1]))
                for g in range(CHUNK // (2 * c))]
        return -jnp.abs(bcum - (mids[0] if len(mids) == 1 else jnp.concatenate(mids, axis=0)))
    r = lax.broadcasted_iota(I32, la.shape, 0) & (2 * c - 1)
    if c == 2:
        nxt = pltpu.roll(la, CHUNK - 1, axis=0)
        prv = pltpu.roll(la, 1, axis=0)
        return jnp.where(r == 0, nxt, jnp.where(r == 1, 0.0, jnp.where(r == 2, la, la + prv)))
    return jnp.where(r == 1, la, 0.0)


def _glr_chunk(q, k, v, la, st_ref, cum_ref, pair_ref, odd_ref):
    la = la * LOG2E
    bcum = _dot01(cum_ref[...], la)
    attn = pair_ref[N_LEVELS] * _dot_nt(q.astype(BF16), k.astype(BF16))
    for lev in range(1, N_LEVELS + 1):
        e = jnp.exp2(_level_log_decay(la, bcum, lev))
        c = CHUNK >> lev
        if c >= SUBLANES:
            zero = jnp.zeros((c, q.shape[1]), F32)
            blocks = [slice(b * c, (b + 1) * c) for b in range(CHUNK // c)]
            ql = jnp.concatenate([q[s] * e[s] if b % 2 else zero for b, s in enumerate(blocks)], axis=0)
            kl = jnp.concatenate([zero if b % 2 else k[s] * e[s] for b, s in enumerate(blocks)], axis=0)
        else:
            eq = e * odd_ref[lev - 1]
            ql = q * eq
            kl = k * (e - eq)
        attn = attn + pair_ref[lev - 1] * _dot_nt(ql.astype(BF16), kl.astype(BF16))
    st = st_ref[...]
    o = _dot_nt((q * jnp.exp2(bcum)).astype(BF16), st.astype(BF16)) + _dot(attn.astype(BF16), v.astype(BF16))
    b_last = bcum[CHUNK - 1:CHUNK]
    kdec = (k * jnp.exp2(b_last - bcum)).astype(BF16)
    st_ref[...] = st * jnp.exp2(b_last) + _dot_tn(v.astype(BF16), kdec)
    return o


def _rms_gate(o, g, gate):
    ms = jnp.mean(o * o, axis=-1, keepdims=True)
    return o * lax.rsqrt(ms + EPS) * g * gate


HEAD_GROUP = 4


def _hgrn2_kernel(q_ref, f_ref, i_ref, g_ref, lb_ref, ng_ref, cum_ref, pair_ref, odd_ref, o_ref, st_ref):
    @pl.when(pl.program_id(2) == 0)
    def _():
        st_ref[...] = jnp.zeros_like(st_ref)

    def body(c, carry):
        rows = pl.ds(pl.multiple_of(c * CHUNK, CHUNK), CHUNK)
        for hh in range(HEAD_GROUP):
            ck = slice(A_DK * hh, A_DK * (hh + 1))
            cv = slice(A_DV * hh, A_DV * (hh + 1))
            lb = lb_ref[:, ck]
            f = lb + (1.0 - lb) * _sigmoid(f_ref[rows, ck])
            o = _glr_chunk(_silu(q_ref[rows, ck]), 1.0 - f, i_ref[rows, cv], jnp.log(f),
                           st_ref.at[hh], cum_ref, pair_ref, odd_ref)
            o_ref[rows, cv] = _rms_gate(o, ng_ref[...], _silu(g_ref[rows, cv]))
        return carry

    lax.fori_loop(0, q_ref.shape[0] // CHUNK, body, 0)


def _gla_kernel(q_ref, k_ref, v_ref, g_ref, r_ref, wa_ref, ba_ref, ng_ref, cum_ref, pair_ref, odd_ref,
                o_ref, st_ref):
    @pl.when(pl.program_id(2) == 0)
    def _():
        st_ref[...] = jnp.zeros_like(st_ref)

    def body(c, carry):
        rows = pl.ds(pl.multiple_of(c * CHUNK, CHUNK), CHUNK)
        pre = _dot(r_ref[rows, :].astype(BF16), wa_ref[...]) + ba_ref[...]
        la = _log_sigmoid(pre) * (1.0 / B_GATE_TAU)
        for hh in range(HEAD_GROUP):
            ck = slice(B_DK * hh, B_DK * (hh + 1))
            cv = slice(B_DV * hh, B_DV * (hh + 1))
            o = _glr_chunk(q_ref[rows, ck] * (B_DK ** -0.5), k_ref[rows, ck], v_ref[rows, cv], la[:, ck],
                           st_ref.at[hh], cum_ref, pair_ref, odd_ref)
            o_ref[rows, cv] = _rms_gate(o, ng_ref[...], _silu(g_ref[rows, cv]))
        return carry

    lax.fori_loop(0, q_ref.shape[0] // CHUNK, body, 0)


def _const_spec(arr):
    nd = arr.ndim
    return pl.BlockSpec(arr.shape, lambda *_: (0,) * nd)


def _hgrn2(h, lb, norm_g, batch, seq):
    tb = min(TIME_BLOCK, seq)
    nt = seq // tb
    ng = A_HEADS // HEAD_GROUP
    wk, wv = HEAD_GROUP * A_DK, HEAD_GROUP * A_DV
    consts = _chunk_constants()
    col = lambda seg, w: pl.BlockSpec((tb, w), lambda b, g, t, seg=seg: (b * nt + t, seg * ng + g))
    return pl.pallas_call(
        _hgrn2_kernel,
        out_shape=jax.ShapeDtypeStruct((batch * seq, A_HEADS * A_DV), F32),
        grid=(batch, ng, nt),
        in_specs=[col(0, wk), col(1, wk), col(2, wv), col(3, wv),
                  pl.BlockSpec((1, wk), lambda b, g, t: (0, g)),
                  pl.BlockSpec((1, A_DV), lambda b, g, t: (0, 0))] + [_const_spec(c) for c in consts],
        out_specs=pl.BlockSpec((tb, wv), lambda b, g, t: (b * nt + t, g)),
        scratch_shapes=[pltpu.VMEM((HEAD_GROUP, A_DV, A_DK), F32)],
        compiler_params=_cparams("parallel", "parallel", "arbitrary"),
        name="hgrn2",
    )(h, h, h, h, lb.reshape(1, -1), norm_g.reshape(1, -1), *consts)


def _gla(h, wa2p, ba2, norm_g, batch, seq, q_off, k_off, v_off, g_off, r_off):
    tb = min(TIME_BLOCK, seq)
    nt = seq // tb
    wk, wv = B_HEADS * B_DK, B_HEADS * B_DV
    consts = _chunk_constants()
    col = lambda off, w: pl.BlockSpec((tb, w), lambda b, g, t, off=off: (b * nt + t, off))
    fixed = lambda b, g, t: (0, 0)
    return pl.pallas_call(
        _gla_kernel,
        out_shape=jax.ShapeDtypeStruct((batch * seq, wv), F32),
        grid=(batch, 1, nt),
        in_specs=[col(q_off, wk), col(k_off, wk), col(v_off, wv), col(g_off, wv), col(r_off, LANES),
                  pl.BlockSpec((LANES, wk), fixed), pl.BlockSpec((1, wk), fixed),
                  pl.BlockSpec((1, B_DV), fixed)] + [_const_spec(c) for c in consts],
        out_specs=pl.BlockSpec((tb, wv), lambda b, g, t: (b * nt + t, 0)),
        scratch_shapes=[pltpu.VMEM((B_HEADS, B_DV, B_DK), F32)],
        compiler_params=_cparams("parallel", "parallel", "arbitrary"),
        name="gla",
    )(h, h, h, h, h, wa2p, ba2.reshape(1, -1), norm_g.reshape(1, -1), *consts)


PROJ_TILE = 512
PROJ_ROWS = 2048
AB_MAIN = 4 * A_HEADS * A_DK + 2 * B_HEADS * B_DK + 2 * B_HEADS * B_DV
AB_PAD = AB_MAIN + PROJ_TILE


def _mix_ab(x2d, w_in, lb_logits, li, a_norm_g, wa2, ba2, b_norm_g, batch, seq):
    d = x2d.shape[1]
    w_tail = jnp.pad(w_in[:, AB_MAIN:], ((0, 0), (0, AB_PAD - w_in.shape[1]))).astype(BF16)
    h = _project(x2d, _cast_columns(w_in, AB_MAIN, PROJ_TILE), w_tail, min(PROJ_ROWS, x2d.shape[0]), PROJ_TILE)
    lb = jnp.cumsum(jax.nn.softmax(lb_logits.astype(F32), axis=0), axis=0)[li]
    oa = _hgrn2(h, lb, a_norm_g, batch, seq)
    wa2p = jnp.concatenate([wa2, jnp.zeros((LANES - B_GATE_RANK, wa2.shape[1]), F32)], axis=0).astype(BF16)
    a_cols = 4 * A_HEADS * A_DK
    wk, wv = B_HEADS * B_DK, B_HEADS * B_DV
    ob = _gla(h, wa2p, ba2, b_norm_g, batch, seq, q_off=a_cols // wk, k_off=a_cols // wk + 1,
              v_off=(a_cols + 2 * wk) // wv, g_off=(a_cols + 2 * wk) // wv + 1, r_off=AB_MAIN // LANES)
    return oa, ob


CONV_HALO = 8


def _causal_conv(x_ref, w_ref, buf_ref, tail_ref):
    tb = x_ref.shape[0]
    x = x_ref[...]
    buf_ref[0:CONV_HALO, :] = tail_ref[...]
    buf_ref[CONV_HALO:CONV_HALO + tb, :] = x
    tail_ref[...] = x[tb - CONV_HALO:tb]
    y = w_ref[CONV_K - 1:CONV_K, :] * x
    for j in range(CONV_K - 1):
        y = y + w_ref[j:j + 1, :] * buf_ref[pl.ds(CONV_HALO - (CONV_K - 1) + j, tb), :]
    return y


def _row_to_col(row, eye):
    return jnp.sum(jnp.where(eye, row, 0.0), axis=1, keepdims=True)


def _mlstm_kernel(q_ref, k_ref, v_ref, og_ref, wq_ref, wk_ref, ig_ref, fg_ref, gb_ref, ng_ref, tri_ref,
                  o_ref, ct_ref, n_ref, m_ref, qt_ref, kt_ref, qs_ref, ks_ref, buf_ref):
    @pl.when(pl.program_id(2) == 0)
    def _():
        ct_ref[...] = jnp.zeros_like(ct_ref)
        n_ref[...] = jnp.zeros_like(n_ref)
        m_ref[...] = jnp.zeros_like(m_ref)
        qt_ref[...] = jnp.zeros_like(qt_ref)
        kt_ref[...] = jnp.zeros_like(kt_ref)

    qs_ref[...] = _silu(_causal_conv(q_ref, wq_ref, buf_ref, qt_ref))
    ks_ref[...] = _silu(_causal_conv(k_ref, wk_ref, buf_ref, kt_ref)) * (D_DK ** -0.5)

    r_i = lax.broadcasted_iota(I32, (CHUNK, CHUNK), 0)
    c_i = lax.broadcasted_iota(I32, (CHUNK, CHUNK), 1)
    eye = r_i == c_i
    causal = r_i >= c_i

    def body(c, carry):
        rows = pl.ds(pl.multiple_of(c * CHUNK, CHUNK), CHUNK)
        tri = tri_ref[...]
        for hh in range(D_HEADS):
            ck = slice(D_DK * hh, D_DK * (hh + 1))
            cv = slice(D_DV * hh, D_DV * (hh + 1))
            q = qs_ref[rows, ck]
            k = ks_ref[rows, ck]
            v = v_ref[rows, cv].astype(BF16)
            qb = q.astype(BF16)
            ig_row = ig_ref[hh, pl.ds(c, 1), :] + gb_ref[0, hh]
            lf_row = _log_sigmoid(fg_ref[hh, pl.ds(c, 1), :] + gb_ref[1, hh])
            hi, mid, lo = _split3(lf_row)
            bcum_row = _dot(hi, tri) + _dot(mid, tri) + _dot(lo, tri)
            bcum_col = _row_to_col(bcum_row, eye)
            ig_col = _row_to_col(ig_row, eye)
            m_prev = m_ref[hh, :, 0:1]
            log_w = jnp.where(causal, bcum_col - bcum_row + ig_row, NEG)
            log_inter = bcum_col + m_prev
            m_t = jnp.maximum(jnp.max(log_w, axis=1, keepdims=True), log_inter)
            s = _dot_nt(qb, k.astype(BF16)) * jnp.exp(log_w - m_t)
            w_inter = jnp.exp(log_inter - m_t)
            num = _dot(s.astype(BF16), v) + w_inter * _dot_nt(qb, ct_ref[hh].astype(BF16))
            qn = jnp.sum(s, axis=1, keepdims=True) + w_inter * jnp.sum(q * n_ref[hh], axis=1, keepdims=True)
            h = num / jnp.maximum(jnp.abs(qn), jnp.exp(-m_t))
            o_ref[rows, cv] = _rms_gate(h, ng_ref[...], _sigmoid(og_ref[rows, cv]))
            b_last = bcum_row[:, CHUNK - 1:CHUNK]
            log_u = b_last - bcum_col + ig_col
            m_new = jnp.maximum(b_last + m_prev, jnp.max(log_u, axis=0, keepdims=True))
            decay = jnp.exp(b_last + m_prev - m_new)
            ku = k * jnp.exp(log_u - m_new)
            ct_ref[hh] = decay * ct_ref[hh] + _dot_tn(v, ku.astype(BF16))
            n_ref[hh] = decay * n_ref[hh] + jnp.sum(ku, axis=0, keepdims=True)
            m_ref[hh] = jnp.broadcast_to(m_new, (1, LANES))
        return carry

    lax.fori_loop(0, q_ref.shape[0] // CHUNK, body, 0)


def _mlstm(h, conv_w, ig_rows, fg_rows, gate_b, norm_g, batch, seq, q_off, k_off, v_off, g_off):
    tb = min(TIME_BLOCK, seq)
    nt = seq // tb
    nc = tb // CHUNK
    wk, wv = D_HEADS * D_DK, D_HEADS * D_DV
    t = np.arange(CHUNK)
    tri = jnp.asarray(t[:, None] <= t[None, :], BF16)
    gb = jnp.broadcast_to(gate_b.reshape(2, D_HEADS, 1, 1), (2, D_HEADS, 1, CHUNK)).astype(F32)
    col = lambda off, w: pl.BlockSpec((tb, w), lambda b, g, t, off=off: (b * nt + t, off))
    gate = pl.BlockSpec((None, D_HEADS, None, nc, CHUNK), lambda b, g, t: (b, 0, t, 0, 0))
    fixed = lambda b, g, t: (0, 0)
    return pl.pallas_call(
        _mlstm_kernel,
        out_shape=jax.ShapeDtypeStruct((batch * seq, wv), F32),
        grid=(batch, 1, nt),
        in_specs=[col(q_off, wk), col(k_off, wk), col(v_off, wv), col(g_off, wv),
                  pl.BlockSpec((CONV_K, wk), lambda b, g, t: (0, 0)),
                  pl.BlockSpec((CONV_K, wk), lambda b, g, t: (0, 1)),
                  gate, gate,
                  pl.BlockSpec((2, D_HEADS, 1, CHUNK), lambda b, g, t: (0, 0, 0, 0)),
                  pl.BlockSpec((1, D_DV), fixed), pl.BlockSpec((CHUNK, CHUNK), fixed)],
        out_specs=pl.BlockSpec((tb, wv), lambda b, g, t: (b * nt + t, 0)),
        scratch_shapes=[pltpu.VMEM((D_HEADS, D_DV, D_DK), F32), pltpu.VMEM((D_HEADS, 1, D_DK), F32),
                        pltpu.VMEM((D_HEADS, 1, LANES), F32),
                        pltpu.VMEM((CONV_HALO, wk), F32), pltpu.VMEM((CONV_HALO, wk), F32),
                        pltpu.VMEM((tb, wk), F32), pltpu.VMEM((tb, wk), F32),
                        pltpu.VMEM((tb + CONV_HALO, wk), F32)],
        compiler_params=_cparams("parallel", "parallel", "arbitrary"),
        name="mlstm",
    )(h, h, h, h, conv_w, conv_w, ig_rows, fg_rows, gb, norm_g.reshape(1, -1), tri)


CD_CQ, CD_IQ, CD_DQ, CD_DK, CD_DV, CD_DOG, CD_CKV, CD_TAIL = 0, 1024, 2048, 2560, 3072, 4096, 5120, 5376
CD_PAD = 5632
TAIL_IK, TAIL_IW, TAIL_DI, TAIL_DF = 0, 64, 80, 84


def _pack_cd(w_in):
    cq, ckv, iq, ik, iw, dq, dk, dv, di, df, dog = jnp.split(
        w_in, [int(i) for i in np.cumsum(
            (C_HEADS * C_DH, C_DLAT, IDX_HEADS * IDX_DIM, IDX_DIM, IDX_HEADS, D_HEADS * D_DK, D_HEADS * D_DK,
             D_HEADS * D_DV, D_HEADS, D_HEADS))], axis=1)
    used = CD_TAIL + IDX_DIM + IDX_HEADS + 2 * D_HEADS
    pad = jnp.zeros((w_in.shape[0], CD_PAD - used), F32)
    main = jnp.concatenate([cq, iq, dq, dk, dv, dog], axis=1).astype(BF16)
    tail = jnp.concatenate([ckv, ik, iw, di, df, pad], axis=1).astype(BF16)
    return main, tail


KB = 2 * QB
IDX_GROUP = 4
ATT_GROUP = 4
SUM_ROWS = 16


def _dsa_prep_kernel(ckv_ref, tail_ref, g_ref, ckv_o, ckvt_o, kdup_o):
    c = ckv_ref[...]
    cn = c * lax.rsqrt(jnp.mean(c * c, axis=-1, keepdims=True) + EPS) * g_ref[...]
    ckv_o[...] = cn.astype(BF16)
    ckvt_o[0:C_DLAT, :] = cn.T.astype(BF16)
    ckvt_o[C_DLAT:C_DLAT + SUM_ROWS, :] = jnp.ones((SUM_ROWS, KB), BF16)
    tail = tail_ref[...]
    lane = lax.broadcasted_iota(I32, tail.shape, 1)
    kdup_o[...] = jnp.where(lane < IDX_DIM, tail, pltpu.roll(tail, IDX_DIM, axis=1)).astype(BF16)


def _dsa_prep(h, ckv_g, n_rows):
    nb = n_rows // KB
    return pl.pallas_call(
        _dsa_prep_kernel,
        out_shape=(jax.ShapeDtypeStruct((nb, KB, C_DLAT), BF16), jax.ShapeDtypeStruct((nb, C_DLAT + SUM_ROWS, KB), BF16),
                   jax.ShapeDtypeStruct((nb, KB, LANES), BF16)),
        grid=(nb,),
        in_specs=[pl.BlockSpec((KB, C_DLAT), lambda i: (i, CD_CKV // C_DLAT)),
                  pl.BlockSpec((KB, LANES), lambda i: (i, CD_TAIL // LANES)),
                  pl.BlockSpec((1, C_DLAT), lambda i: (0, 0))],
        out_specs=(pl.BlockSpec((None, KB, C_DLAT), lambda i: (i, 0, 0)),
                   pl.BlockSpec((None, C_DLAT + SUM_ROWS, KB), lambda i: (i, 0, 0)),
                   pl.BlockSpec((None, KB, LANES), lambda i: (i, 0, 0))),
        compiler_params=_cparams("parallel"),
        name="dsa_prep",
    )(h, h, ckv_g.reshape(1, -1))


def _sortable_key(x):
    b = lax.bitcast_convert_type(x, I32)
    key = b ^ ((b >> 31) & 0x7FFFFFFF)
    return jnp.where(key == -1, 0, key)


BISECT_STEPS = 4


def _indexer_kernel(iq_ref, tail_ref, kdup_ref, tri_ref, mask_ref, key_ref, wst_ref, gmax_ref, *, k_sel):
    j = pl.program_id(1)
    nk = key_ref.shape[0]
    n_live = (j * QB + QB + KB - 1) // KB
    w_t = tail_ref[...].T
    lane = lax.broadcasted_iota(I32, (QB, LANES), 1)
    for p in range(IDX_HEADS // 2):
        pair = iq_ref[:, LANES * p:LANES * (p + 1)]
        g, r = divmod(2 * p, IDX_GROUP)
        wst_ref[g, r * QB:(r + 1) * QB, :] = jnp.where(lane < IDX_DIM, pair, 0.0).astype(BF16)
        wst_ref[g, (r + 1) * QB:(r + 2) * QB, :] = jnp.where(lane >= IDX_DIM, pair, 0.0).astype(BF16)

    s_loc = lax.broadcasted_iota(I32, (KB, QB), 0)
    t_abs = j * QB + lax.broadcasted_iota(I32, (1, QB), 1)

    def score_chunk(kc, carry):
        kd = kdup_ref[kc]
        acc = jnp.zeros((KB, QB), F32)
        for g in range(IDX_HEADS // IDX_GROUP):
            dots = _dot_nt(kd, wst_ref[g])
            for r in range(IDX_GROUP):
                row = TAIL_IW + g * IDX_GROUP + r
                acc = acc + jnp.maximum(dots[:, r * QB:(r + 1) * QB], 0.0) * w_t[row:row + 1, :]
        key = jnp.where(kc * KB + s_loc > t_abs, INT_MIN, _sortable_key(acc))
        key_ref[kc] = key
        gmax_ref[...] = jnp.maximum(gmax_ref[...], key)
        key = key.reshape(KB // SUBLANES, SUBLANES, QB)
        k_max, k_min = carry
        k_max = jnp.maximum(k_max, jnp.max(key, axis=0))
        k_min = jnp.minimum(k_min, jnp.min(jnp.where(key == INT_MIN, INT_MAX, key), axis=0))
        return k_max, k_min

    def score_two(i, carry):
        return score_chunk(2 * i + 1, score_chunk(2 * i, carry))

    gmax_ref[...] = jnp.full(gmax_ref.shape, INT_MIN, I32)
    k_max, k_min = lax.fori_loop(0, (n_live + 1) // 2, score_two, (jnp.full((SUBLANES, QB), INT_MIN, I32),
                                                                   jnp.full((SUBLANES, QB), INT_MAX, I32)))
    g_low = jnp.min(gmax_ref[...].reshape(KB // SUBLANES, SUBLANES, QB), axis=0)
    for shift in (4, 2, 1):
        k_max = jnp.maximum(k_max, pltpu.roll(k_max, shift, axis=0))
        k_min = jnp.minimum(k_min, pltpu.roll(k_min, shift, axis=0))
        g_low = jnp.minimum(g_low, pltpu.roll(g_low, shift, axis=0))
    k_low = jnp.maximum(k_min, g_low)
    k_row = jnp.minimum(k_sel, j * QB + lax.broadcasted_iota(I32, (SUBLANES, QB), 1) + 1)

    def count(pred):
        def add(kc, acc):
            hit = jnp.where(pred(key_ref[kc].reshape(KB // SUBLANES, SUBLANES, QB)), 1, 0)
            return acc + jnp.sum(hit, axis=0)

        def add_two(i, acc):
            return add(2 * i + 1, add(2 * i, acc))
        acc = lax.fori_loop(0, (n_live + 1) // 2, add_two, jnp.zeros((SUBLANES, QB), I32))
        for shift in (4, 2, 1):
            acc = acc + pltpu.roll(acc, shift, axis=0)
        return acc

    def unfinished(state):
        lo, hi = state
        return jnp.max(jnp.where(lo < hi, 1, 0)) > 0

    def halve(state):
        lo, hi = state
        mid = (lo >> 1) + (hi >> 1) + (((lo & 1) + (hi & 1) + 1) >> 1)
        cnt = count(lambda k: k >= mid)
        enough = cnt >= k_row
        lo_n = jnp.where(enough, mid, lo)
        hi_n = jnp.where(cnt == k_row, mid, jnp.where(enough, hi, mid - 1))
        return lo_n, hi_n

    def halve_steps(state):
        for _ in range(BISECT_STEPS):
            state = halve(state)
        return state

    tau8, _ = lax.while_loop(unfinished, halve_steps, (k_low, k_max))
    n_ge = count(lambda k: k >= tau8)
    has_tie = jnp.max(jnp.where(n_ge != k_row, 1, 0)) > 0
    tau = tau8[0:1, :]

    @pl.when(jnp.logical_not(has_tie))
    def _():
        def put(kc, carry):
            mask_ref[kc] = jnp.where(key_ref[kc] >= tau, 1.0, 0.0).astype(BF16)
            return carry
        lax.fori_loop(0, n_live, put, 0)

    @pl.when(has_tie)
    def _():
        need = (k_row - count(lambda k: k > tau8)).astype(F32)[0:1, :]

        def put(kc, seen):
            k = key_ref[kc]
            eq = jnp.where(k == tau, 1.0, 0.0)
            before = _dot(tri_ref[...], eq.astype(BF16)) + seen
            take = jnp.where(k > tau, 1.0, jnp.where(before < need, eq, 0.0))
            mask_ref[kc] = take.astype(BF16)
            return seen + jnp.sum(eq, axis=0, keepdims=True)
        lax.fori_loop(0, n_live, put, jnp.zeros((1, QB), F32))

    def clear(kc, carry):
        mask_ref[kc] = jnp.zeros((KB, QB), BF16)
        return carry
    lax.fori_loop(n_live, nk, clear, 0)


def _indexer(h, kdup, batch, seq):
    nq = seq // QB
    nk = seq // KB
    k_sel = min(TOPK_MAX, seq // 4)
    r = np.arange(KB)
    tri = jnp.asarray(r[None, :] < r[:, None], BF16)
    return pl.pallas_call(
        functools.partial(_indexer_kernel, k_sel=k_sel),
        out_shape=jax.ShapeDtypeStruct((batch, nk, KB, seq), BF16),
        grid=(batch, nq),
        in_specs=[pl.BlockSpec((QB, IDX_HEADS * IDX_DIM), lambda b, j: (b * nq + j, CD_IQ // (IDX_HEADS * IDX_DIM))),
                  pl.BlockSpec((QB, LANES), lambda b, j: (b * nq + j, CD_TAIL // LANES)),
                  pl.BlockSpec((None, nk, KB, LANES), lambda b, j: (b, 0, 0, 0)),
                  pl.BlockSpec((KB, KB), lambda b, j: (0, 0))],
        out_specs=pl.BlockSpec((None, nk, KB, QB), lambda b, j: (b, 0, 0, j)),
        scratch_shapes=[pltpu.VMEM((nk, KB, QB), I32),
                        pltpu.VMEM((IDX_HEADS // IDX_GROUP, IDX_GROUP * QB, LANES), BF16),
                        pltpu.VMEM((KB, QB), I32)],
        compiler_params=_cparams("parallel", "parallel"),
        name="dsa_indexer",
    )(h, h, kdup.reshape(batch, nk, KB, LANES), tri)


def _dsa_attn_kernel(cq_ref, mask_ref, ckv_ref, ckvt_ref, wuk_ref, wuvt_ref, bias_ref, o_ref,
                     qt_ref, m_ref, acc_ref, ot_ref, p_ref, alpha_ref):
    j = pl.program_id(1)
    for hh in range(C_HEADS):
        q_h = cq_ref[:, C_DH * hh:C_DH * (hh + 1)].astype(BF16)
        part = slice(QB * (hh % ATT_GROUP), QB * (hh % ATT_GROUP + 1))
        qt_ref[hh // ATT_GROUP, :, part] = (_dot_nt(wuk_ref[hh], q_h) * (C_DH ** -0.5 * LOG2E)).astype(BF16)
    m_ref[...] = jnp.full(m_ref.shape, NEG, F32)
    acc_ref[...] = jnp.zeros_like(acc_ref)
    p_ref[...] = jnp.zeros_like(p_ref)
    alpha_ref[...] = jnp.ones_like(alpha_ref)

    n_live = (j * QB + QB + KB - 1) // KB
    odd = (j % 2) == 1

    def accumulate(kc_done, hp):
        acc_ref[hp] = alpha_ref[hp] * acc_ref[hp] + _dot(ckvt_ref[kc_done], p_ref[hp])

    def body(kc, carry, near):
        ck = ckv_ref[kc]
        kc_prev = jnp.maximum(kc - 1, 0)
        drop = jnp.where(mask_ref[kc].astype(F32) > 0.5, 0.0, NEG)
        drop = jnp.concatenate([drop] * ATT_GROUP, axis=1)
        back = n_live - 1 - kc
        which = jnp.where(back == 0, jnp.where(odd, 0, 1), jnp.where(jnp.logical_and(back == 1, jnp.logical_not(odd)), 2, 3))
        for hp in range(C_HEADS // ATT_GROUP):
            accumulate(kc_prev, hp)
            if near:
                tiles = [bias_ref[hp * ATT_GROUP + g, which] for g in range(ATT_GROUP)]
                logit = _dot(ck, qt_ref[hp]) + (jnp.concatenate(tiles, axis=1) + drop)
            else:
                logit = _dot(ck, qt_ref[hp]) + drop
            m_old = m_ref[hp]
            m_new = jnp.maximum(m_old, jnp.max(logit, axis=0, keepdims=True))
            alpha_ref[hp] = jnp.exp2(m_old - m_new)
            p_ref[hp] = jnp.exp2(logit - m_new).astype(BF16)
            m_ref[hp] = m_new
        return carry

    def body_two(i, carry, near):
        return body(2 * i + 1, body(2 * i, carry, near), near)

    n_pairs = (n_live + 1) // 2
    n_far = jnp.maximum(n_pairs - 2, 0)
    lax.fori_loop(0, n_far, functools.partial(body_two, near=False), 0)
    lax.fori_loop(n_far, n_pairs, functools.partial(body_two, near=True), 0)
    for hp in range(C_HEADS // ATT_GROUP):
        accumulate(2 * n_pairs - 1, hp)
    for hh in range(C_HEADS):
        part = slice(QB * (hh % ATT_GROUP), QB * (hh % ATT_GROUP + 1))
        total = acc_ref[hh // ATT_GROUP, C_DLAT:C_DLAT + 1, part]
        o_lat = (acc_ref[hh // ATT_GROUP, 0:C_DLAT, part] * (1.0 / total)).astype(BF16)
        ot_ref[C_DH * hh:C_DH * (hh + 1), :] = _dot(wuvt_ref[hh], o_lat)
    o_ref[...] = ot_ref[...].T


def _rel_bias_tiles(rel_table):
    s = np.arange(QB)[:, None]
    t = np.arange(QB)[None, :]
    diag, prev, far = np.maximum(t - s, 0), QB + t - s, np.full((QB, QB), 2 * QB)
    kinds = [(prev, diag), (diag, far), (far, prev), (far, far)]
    n = jnp.asarray(np.stack([np.concatenate(k, axis=0) for k in kinds]).astype(np.int32))
    max_exact = REL_BUCKETS // 2
    large = max_exact + (jnp.log(jnp.maximum(n, 1).astype(F32) / max_exact)
                         / math.log(REL_MAX_DIST / max_exact) * (REL_BUCKETS - max_exact)).astype(I32)
    bucket = jnp.where(n < max_exact, n, jnp.minimum(large, REL_BUCKETS - 1))
    onehot = (bucket[..., None] == jnp.arange(REL_BUCKETS, dtype=I32)).astype(F32)
    bias = jnp.einsum("hb,kstb->hkst", rel_table.astype(F32).T * LOG2E, onehot, precision=lax.Precision.HIGHEST)
    return bias - bias[:, 3:4, 0:1, 0:1]


def _dsa_attention(h, mask, ckv, ckvt, w_uk, w_uv, rel_table, batch, seq):
    nq = seq // QB
    nk = seq // KB
    wuk = w_uk.transpose(1, 0, 2).astype(BF16)
    wuvt = w_uv.transpose(1, 2, 0).astype(BF16)
    bias = _rel_bias_tiles(rel_table)
    return pl.pallas_call(
        _dsa_attn_kernel,
        out_shape=jax.ShapeDtypeStruct((batch * seq, C_HEADS * C_DH), F32),
        grid=(batch, nq),
        in_specs=[pl.BlockSpec((QB, C_HEADS * C_DH), lambda b, j: (b * nq + j, CD_CQ // (C_HEADS * C_DH))),
                  pl.BlockSpec((None, nk, KB, QB), lambda b, j: (b, 0, 0, j)),
                  pl.BlockSpec((None, nk, KB, C_DLAT), lambda b, j: (b, 0, 0, 0)),
                  pl.BlockSpec((None, nk, C_DLAT + SUM_ROWS, KB), lambda b, j: (b, 0, 0, 0)),
                  _const_spec(wuk), _const_spec(wuvt), _const_spec(bias)],
        out_specs=pl.BlockSpec((QB, C_HEADS * C_DH), lambda b, j: (b * nq + j, 0)),
        scratch_shapes=[pltpu.VMEM((C_HEADS // ATT_GROUP, C_DLAT, ATT_GROUP * QB), BF16),
                        pltpu.VMEM((C_HEADS // ATT_GROUP, 1, ATT_GROUP * QB), F32),
                        pltpu.VMEM((C_HEADS // ATT_GROUP, C_DLAT + SUM_ROWS, ATT_GROUP * QB), F32),
                        pltpu.VMEM((C_HEADS * C_DH, QB), F32),
                        pltpu.VMEM((C_HEADS // ATT_GROUP, KB, ATT_GROUP * QB), BF16),
                        pltpu.VMEM((C_HEADS // ATT_GROUP, 1, ATT_GROUP * QB), F32)],
        compiler_params=_cparams("parallel", "parallel"),
        name="dsa_attention",
    )(h, mask, ckv.reshape(batch, nk, KB, C_DLAT), ckvt.reshape(batch, nk, C_DLAT + SUM_ROWS, KB), wuk, wuvt, bias)


ROUTER_BLOCK = 512
MOE_TILE = 512


def _route_block(x, wh_ref, wl_ref, b_ref, upper_ref, idx_ref, wgt_ref, rank_ref, cnt_ref):
    xh = x.astype(BF16)
    xl = (x - xh.astype(F32)).astype(BF16)
    logit = _dot_nt(wh_ref[...], xh) + _dot_nt(wl_ref[...], xh) + _dot_nt(wh_ref[...], xl)
    aff = _sigmoid(logit)
    sel = aff + b_ref[...]
    s_rows = [sel[e:e + 1] for e in range(N_EXPERTS)]
    a_rows = [aff[e:e + 1] for e in range(N_EXPERTS)]
    n = EXPERTS_PER_GROUP

    g_best = jnp.zeros(s_rows[0].shape, I32)
    best = None
    for g in range(N_GROUPS):
        v = s_rows[g * n:(g + 1) * n]
        top2 = None
        for a in range(n):
            for b in range(a + 1, n):
                pair = v[a] + v[b]
                top2 = pair if top2 is None else jnp.maximum(top2, pair)
        if best is None:
            best = top2
        else:
            upd = top2 > best
            g_best = jnp.where(upd, g, g_best)
            best = jnp.where(upd, top2, best)

    sv, av = [], []
    for i in range(n):
        s_i, a_i = s_rows[i], a_rows[i]
        for g in range(1, N_GROUPS):
            pick = g_best == g
            s_i = jnp.where(pick, s_rows[g * n + i], s_i)
            a_i = jnp.where(pick, a_rows[g * n + i], a_i)
        sv.append(s_i)
        av.append(a_i)

    i1, s1, a1 = jnp.zeros_like(g_best), sv[0], av[0]
    for i in range(1, n):
        upd = sv[i] > s1
        i1 = jnp.where(upd, i, i1)
        s1 = jnp.where(upd, sv[i], s1)
        a1 = jnp.where(upd, av[i], a1)
    i2 = jnp.zeros_like(g_best)
    s2 = jnp.full(s1.shape, -jnp.inf, F32)
    a2 = jnp.zeros_like(a1)
    for i in range(n):
        cand = jnp.where(i1 == i, -jnp.inf, sv[i])
        upd = cand > s2
        i2 = jnp.where(upd, i, i2)
        s2 = jnp.where(upd, cand, s2)
        a2 = jnp.where(upd, av[i], a2)

    tot = a1 + a2
    e1 = g_best * n + i1
    e2 = g_best * n + i2
    idx_ref[0:1, :] = e1
    idx_ref[1:2, :] = e2
    wgt_ref[0:1, :] = a1 / tot
    wgt_ref[1:2, :] = a2 / tot

    @pl.when(pl.program_id(0) == 0)
    def _():
        cnt_ref[...] = jnp.zeros_like(cnt_ref)

    e_iota = lax.broadcasted_iota(I32, sel.shape, 0)
    oh1 = jnp.where(e_iota == e1, 1.0, 0.0)
    oh2 = jnp.where(e_iota == e2, 1.0, 0.0)
    both = oh1 + oh2
    before = cnt_ref[...] + _dot(both.astype(BF16), upper_ref[...])
    rank_ref[0:1, :] = jnp.sum(oh1 * before, axis=0, keepdims=True).astype(I32)
    rank_ref[1:2, :] = jnp.sum(oh2 * before, axis=0, keepdims=True).astype(I32)
    cnt_ref[...] = cnt_ref[...] + jnp.sum(both, axis=1, keepdims=True)


def _route_tables(idx, rank, counts, tm, n_items):
    cnt = counts.reshape(-1).astype(I32)
    start = jnp.cumsum(cnt) - cnt
    experts = jnp.arange(N_EXPERTS, dtype=I32)
    pos = rank + jnp.sum(jnp.where(idx[..., None] == experts, start, 0), axis=-1)
    first_tile = start // tm
    n_e = jnp.where(cnt > 0, (start + cnt - 1) // tm - first_tile + 1, 0)
    item_end = jnp.cumsum(n_e)
    item = jnp.arange(n_items, dtype=I32)
    used = item < item_end[-1]
    e_i = jnp.minimum(jnp.sum((item[:, None] >= item_end[None, :]).astype(I32), axis=1), N_EXPERTS - 1)
    e_last = jnp.max(jnp.where(cnt > 0, experts, 0))
    e_i = jnp.where(used, e_i, e_last)
    tile_i = jnp.where(used, first_tile[e_i] + item - (item_end - n_e)[e_i], (TOP_K * idx.shape[1]) // tm - 1)
    lo = jnp.where(used, jnp.maximum(start[e_i], tile_i * tm) - tile_i * tm, 0)
    hi = jnp.where(used, jnp.minimum(start[e_i] + cnt[e_i], (tile_i + 1) * tm) - tile_i * tm, 0)
    return pos.astype(I32), tile_i.astype(I32), e_i, lo.astype(I32), hi.astype(I32)


def _row_of(ref, row):
    return ref.at[lax.shift_right_logical(row, SUBLANES.bit_length() - 1), pl.ds(row & (SUBLANES - 1), 1)]


def _dispatch_kernel(pos_ref, x_ref, xs_hbm, stage, sem, *, tb, n_tok):
    i = pl.program_id(0)
    slot = i % 2
    stage[slot] = x_ref[...].reshape(stage.shape[1:])

    def issue(grp, carry):
        dst = [[pos_ref[s * n_tok + i * tb + grp * SUBLANES + u] for s in range(TOP_K)] for u in range(SUBLANES)]
        for u in range(SUBLANES):
            for s in range(TOP_K):
                row = dst[u][s]
                pltpu.make_async_copy(stage.at[slot, grp, pl.ds(u, 1)], _row_of(xs_hbm, row), sem.at[slot]).start()
        return carry

    lax.fori_loop(0, tb // SUBLANES, issue, 0)

    def wait_block(s):
        for _ in range(TOP_K):
            pltpu.make_async_copy(stage.at[s], xs_hbm.at[pl.ds(0, tb // SUBLANES)], sem.at[s]).wait()

    @pl.when(i >= 1)
    def _():
        wait_block(1 - slot)

    @pl.when(i == pl.num_programs(0) - 1)
    def _():
        wait_block(slot)


def _dispatch(x2d, pos_flat, tb):
    t, d = x2d.shape
    grid_spec = pltpu.PrefetchScalarGridSpec(
        num_scalar_prefetch=1, grid=(t // tb,),
        in_specs=[pl.BlockSpec((tb, d), lambda i, pos: (i, 0))],
        out_specs=pl.BlockSpec(memory_space=pl.ANY),
        scratch_shapes=[pltpu.VMEM((2, tb // SUBLANES, SUBLANES, d), F32), pltpu.SemaphoreType.DMA((2,))])
    return pl.pallas_call(
        functools.partial(_dispatch_kernel, tb=tb, n_tok=t),
        out_shape=jax.ShapeDtypeStruct((TOP_K * t // SUBLANES, SUBLANES, d), F32),
        grid_spec=grid_spec,
        compiler_params=_cparams("arbitrary"),
        name="moe_dispatch",
    )(pos_flat, x2d).reshape(TOP_K * t, d)


def _experts_kernel(tile_ref, exp_ref, lo_ref, hi_ref, x_ref, w1_ref, w3_ref, w2_ref, o_ref, w1b, w3b, w2b):
    i = pl.program_id(0)
    prev = jnp.maximum(i - 1, 0)

    @pl.when(jnp.logical_or(i == 0, exp_ref[i] != exp_ref[prev]))
    def _():
        w1b[...] = w1_ref[...].astype(BF16)
        w3b[...] = w3_ref[...].astype(BF16)
        w2b[...] = w2_ref[...].astype(BF16)

    lo = lo_ref[i]
    hi = hi_ref[i]
    first = jnp.logical_or(i == 0, tile_ref[i] != tile_ref[prev])

    @pl.when(hi > lo)
    def _():
        x = x_ref[...].astype(BF16)
        row = lax.broadcasted_iota(I32, (x.shape[0], 1), 0)
        mine = jnp.where(jnp.logical_and(row >= lo, row < hi), 1.0, 0.0)
        hid = _silu(_dot(x, w1b[...])) * _dot(x, w3b[...]) * mine
        y = _dot(hid.astype(BF16), w2b[...])

        @pl.when(first)
        def _():
            o_ref[...] = y

        @pl.when(jnp.logical_not(first))
        def _():
            o_ref[...] += y


def _experts(xs, tile_i, exp_i, lo, hi, w1, w3, w2, layer, tm):
    n, d = xs.shape
    f = w1.shape[-1]
    by_tile = lambda i, tile, exp, lo, hi: (tile[i], 0)
    by_exp = lambda i, tile, exp, lo, hi: (layer, exp[i], 0, 0)
    grid_spec = pltpu.PrefetchScalarGridSpec(
        num_scalar_prefetch=4, grid=(tile_i.shape[0],),
        in_specs=[pl.BlockSpec((tm, d), by_tile),
                  pl.BlockSpec((None, None, d, f), by_exp), pl.BlockSpec((None, None, d, f), by_exp),
                  pl.BlockSpec((None, None, f, d), by_exp)],
        out_specs=pl.BlockSpec((tm, d), by_tile),
        scratch_shapes=[pltpu.VMEM((d, f), BF16), pltpu.VMEM((d, f), BF16), pltpu.VMEM((f, d), BF16)])
    return pl.pallas_call(
        _experts_kernel,
        out_shape=jax.ShapeDtypeStruct((n, d), F32),
        grid_spec=grid_spec,
        compiler_params=_cparams("arbitrary"),
        name="moe_experts",
    )(tile_i, exp_i, lo, hi, xs, w1, w3, w2)


def _combine_ln_kernel(pos_ref, x_ref, w_ref, g_ref, b_ref, ys_hbm, o_ref, gbuf, sem, *, tb, n_tok):
    i = pl.program_id(0)
    slot = i % 2

    def start(blk, s):
        def issue(grp, carry):
            src = [[pos_ref[k * n_tok + blk * tb + grp * SUBLANES + u] for k in range(TOP_K)] for u in range(SUBLANES)]
            for u in range(SUBLANES):
                for k in range(TOP_K):
                    pltpu.make_async_copy(_row_of(ys_hbm, src[u][k]), gbuf.at[s, k, grp, pl.ds(u, 1)], sem.at[s]).start()
            return carry
        lax.fori_loop(0, tb // SUBLANES, issue, 0)

    @pl.when(i == 0)
    def _():
        start(0, 0)

    @pl.when(i + 1 < pl.num_programs(0))
    def _():
        start(i + 1, 1 - slot)

    for k in range(TOP_K):
        pltpu.make_async_copy(ys_hbm.at[pl.ds(0, tb // SUBLANES)], gbuf.at[slot, k], sem.at[slot]).wait()
    w = w_ref[...]
    rows = x_ref.shape
    y = gbuf[slot, 0].reshape(rows) * w[:, 0:1] + gbuf[slot, 1].reshape(rows) * w[:, 1:2]
    o_ref[...] = _layer_norm_rows(DN_ALPHA * x_ref[...] + y, g_ref[...], b_ref[...])


def _combine_ln(x2d, ys, pos_flat, wgt_cols, g, b, tb):
    t, d = x2d.shape
    row = lambda i, pos: (i, 0)
    fixed = lambda i, pos: (0, 0)
    grid_spec = pltpu.PrefetchScalarGridSpec(
        num_scalar_prefetch=1, grid=(t // tb,),
        in_specs=[pl.BlockSpec((tb, d), row), pl.BlockSpec((tb, TOP_K), row),
                  pl.BlockSpec((1, d), fixed), pl.BlockSpec((1, d), fixed),
                  pl.BlockSpec(memory_space=pl.ANY)],
        out_specs=pl.BlockSpec((tb, d), row),
        scratch_shapes=[pltpu.VMEM((2, TOP_K, tb // SUBLANES, SUBLANES, d), F32), pltpu.SemaphoreType.DMA((2,))])
    return pl.pallas_call(
        functools.partial(_combine_ln_kernel, tb=tb, n_tok=t),
        out_shape=jax.ShapeDtypeStruct((t, d), F32),
        grid_spec=grid_spec,
        compiler_params=_cparams("arbitrary"),
        name="moe_combine_ln",
    )(pos_flat, x2d, wgt_cols, g.reshape(1, d), b.reshape(1, d), ys.reshape(-1, SUBLANES, d))


def _moe_ln(x2d, routing, w1, w3, w2, layer, g, b):
    t = x2d.shape[0]
    tm = min(MOE_TILE, t)
    idx, wgt, rank, counts = routing
    n_items = TOP_K * t // tm + N_EXPERTS - 1
    pos, tile_i, exp_i, lo, hi = _route_tables(idx, rank, counts, tm, n_items)
    pos_flat = pos.reshape(-1)
    xs = _dispatch(x2d, pos_flat, tm)
    ys = _experts(xs, tile_i, exp_i, lo, hi, w1, w3, w2, layer, tm)
    return _combine_ln(x2d, ys, pos_flat, wgt.T, g, b, tm)


def _gate_rows(col, batch, seq):
    tb = min(TIME_BLOCK, seq)
    return col.reshape(batch, seq, D_HEADS).transpose(0, 2, 1).reshape(batch, D_HEADS, seq // tb, tb // CHUNK, CHUNK)


def _mix_cd(x2d, w_in, rel_table, ckv_g, w_uk, w_uv, conv_w, gate_b, d_norm_g, batch, seq):
    t = x2d.shape[0]
    h = _project(x2d, *_pack_cd(w_in), min(PROJ_ROWS, t), PROJ_TILE)
    ckv, ckvt, kdup = _dsa_prep(h, ckv_g, t)
    mask = _indexer(h, kdup, batch, seq)
    oc = _dsa_attention(h, mask, ckv, ckvt, w_uk, w_uv, rel_table, batch, seq)
    tail = h[:, CD_TAIL:CD_TAIL + LANES]
    ig_rows = _gate_rows(tail[:, TAIL_DI:TAIL_DI + D_HEADS], batch, seq)
    fg_rows = _gate_rows(tail[:, TAIL_DF:TAIL_DF + D_HEADS], batch, seq)
    od = _mlstm(h, conv_w, ig_rows, fg_rows, gate_b, d_norm_g, batch, seq,
                q_off=CD_DQ // (D_HEADS * D_DK), k_off=CD_DK // (D_HEADS * D_DK),
                v_off=CD_DV // (D_HEADS * D_DV), g_off=CD_DOG // (D_HEADS * D_DV))
    return oc, od


def kernel(x, w_in_ab, w_out_ab, hgrn_lb_logits, a_norm_g, gla_wa2, gla_ba2, b_norm_g, w_in_cd, w_out_cd,
           ckv_norm_g, w_uk, w_uv, mlstm_conv_w, mlstm_gate_b, d_norm_g, rel_table, w_router, b_router,
           moe_w1, moe_w3, moe_w2, ln_g, ln_b):
    batch, seq, d = x.shape
    x2d = x.reshape(batch * seq, d)
    for layer in range(DEPTH):
        li = layer // 2
        if layer % 2 == 0:
            mix_a, mix_b = _mix_ab(x2d, w_in_ab[li], hgrn_lb_logits, li, a_norm_g[li], gla_wa2[li], gla_ba2[li],
                                   b_norm_g[li], batch, seq)
            w_out = w_out_ab[li]
        else:
            mix_a, mix_b = _mix_cd(x2d, w_in_cd[li], rel_table, ckv_norm_g[li], w_uk[li], w_uv[li],
                                   mlstm_conv_w[li], mlstm_gate_b[li], d_norm_g[li], batch, seq)
            w_out = w_out_cd[li]
        ka = mix_a.shape[1]
        x2d, *routing = _outproj_ln(x2d, mix_a, mix_b, w_out[:ka].astype(BF16), w_out[ka:].astype(BF16),
                                    ln_g[layer, 0], ln_b[layer, 0], w_router, b_router)
        x2d = _moe_ln(x2d, routing, moe_w1, moe_w3, moe_w2, layer, ln_g[layer, 1], ln_b[layer, 1])
    return x2d.reshape(batch, seq, d)
```

```python
import functools
import math

import numpy as np
import jax
import jax.numpy as jnp
from jax import lax
from jax.experimental import pallas as pl
from jax.experimental.pallas import tpu as pltpu

F32 = jnp.float32
BF16 = jnp.bfloat16
I32 = jnp.int32

D_MODEL = 2048
DEPTH = 2
A_HEADS, A_DK, A_DV = 8, 128, 128
B_HEADS, B_DK, B_DV = 4, 128, 256
B_GATE_RANK, B_GATE_TAU = 16, 16.0
C_HEADS, C_DH, C_DLAT = 8, 128, 256
IDX_HEADS, IDX_DIM = 16, 64
TOPK_MAX = 256
D_HEADS, D_DK, D_DV = 4, 128, 256
CONV_K = 4
REL_BUCKETS, REL_MAX_DIST = 32, 128
N_EXPERTS, N_GROUPS, TOP_K, D_EXPERT = 16, 4, 2, 512
EXPERTS_PER_GROUP = N_EXPERTS // N_GROUPS
DN_ALPHA = (2 * DEPTH) ** 0.25
EPS = 1e-5

LANES = 128
SUBLANES = 8
VMEM_LIMIT = 56 * 1024 * 1024

CHUNK = 128
N_LEVELS = 7
LOG2E = math.log2(math.e)
TIME_BLOCK = 512
QB = 128
NEG = -1e30
INT_MIN = -2 ** 31
INT_MAX = 2 ** 31 - 1


def _cparams(*sem):
    return pltpu.CompilerParams(dimension_semantics=sem, vmem_limit_bytes=VMEM_LIMIT)


def _dot(a, b):
    return jnp.dot(a, b, preferred_element_type=F32)


def _dot_nt(a, b):
    return lax.dot_general(a, b, (((1,), (1,)), ((), ())), preferred_element_type=F32)


def _dot_tn(a, b):
    return lax.dot_general(a, b, (((0,), (0,)), ((), ())), preferred_element_type=F32)


def _split3(a):
    hi = a.astype(BF16)
    r1 = a - hi.astype(F32)
    mid = r1.astype(BF16)
    lo = (r1 - mid.astype(F32)).astype(BF16)
    return hi, mid, lo


def _dot01(m01, a):
    hi, mid, lo = _split3(a)
    return _dot(m01, hi) + _dot(m01, mid) + _dot(m01, lo)


def _sigmoid(x):
    return 1.0 / (1.0 + jnp.exp(-x))


def _silu(x):
    return x * _sigmoid(x)


def _log_sigmoid(x):
    return jnp.minimum(x, 0.0) - jnp.log(1.0 + jnp.exp(-jnp.abs(x)))


def _proj_kernel(x_ref, w_ref, wt_ref, o_ref, xb_ref):
    j = pl.program_id(1)

    @pl.when(j == 0)
    def _():
        xb_ref[...] = x_ref[...].astype(BF16)

    @pl.when(j < pl.num_programs(1) - 1)
    def _():
        o_ref[...] = _dot(xb_ref[...], w_ref[...])

    @pl.when(j == pl.num_programs(1) - 1)
    def _():
        o_ref[...] = _dot(xb_ref[...], wt_ref[...])


def _cast_kernel(w_ref, o_ref):
    o_ref[...] = w_ref[...].astype(o_ref.dtype)


def _cast_columns(w, n_cols, tn):
    k = w.shape[0]
    return pl.pallas_call(
        _cast_kernel,
        out_shape=jax.ShapeDtypeStruct((k, n_cols), BF16),
        grid=(n_cols // tn,),
        in_specs=[pl.BlockSpec((k, tn), lambda j: (0, j))],
        out_specs=pl.BlockSpec((k, tn), lambda j: (0, j)),
        compiler_params=_cparams("parallel"),
        name="weight_cast",
    )(w)


def _project(x, w, w_tail, tm, tn):
    m, k = x.shape
    n_main = w.shape[1] // tn
    return pl.pallas_call(
        _proj_kernel,
        out_shape=jax.ShapeDtypeStruct((m, (n_main + 1) * tn), F32),
        grid=(m // tm, n_main + 1),
        in_specs=[pl.BlockSpec((tm, k), lambda i, j: (i, 0), pipeline_mode=pl.Buffered(1)),
                  pl.BlockSpec((k, tn), lambda i, j: (0, jnp.minimum(j, n_main - 1))),
                  pl.BlockSpec((k, tn), lambda i, j: (0, 0))],
        out_specs=pl.BlockSpec((tm, tn), lambda i, j: (i, j)),
        scratch_shapes=[pltpu.VMEM((tm, k), BF16)],
        compiler_params=_cparams("parallel", "arbitrary"),
        name="in_proj",
    )(x, w, w_tail)


def _layer_norm_rows(z, g, b):
    mu = jnp.mean(z, axis=-1, keepdims=True)
    zc = z - mu
    var = jnp.mean(zc * zc, axis=-1, keepdims=True)
    return zc * lax.rsqrt(var + EPS) * g + b


def _outproj_ln_kernel(x_ref, ma_ref, mb_ref, wa_ref, wb_ref, g_ref, b_ref, wh_ref, wl_ref, rb_ref, upper_ref,
                       o_ref, idx_ref, wgt_ref, rank_ref, cnt_ref):
    mixed = _dot(ma_ref[...].astype(BF16), wa_ref[...]) + _dot(mb_ref[...].astype(BF16), wb_ref[...])
    z = _layer_norm_rows(DN_ALPHA * x_ref[...] + mixed, g_ref[...], b_ref[...])
    o_ref[...] = z
    _route_block(z, wh_ref, wl_ref, rb_ref, upper_ref, idx_ref, wgt_ref, rank_ref, cnt_ref)


def _outproj_ln(x, mix_a, mix_b, w_a, w_b, g, b, w_router, b_router):
    m, d = x.shape
    tm = min(ROUTER_BLOCK, m)
    ka, kb = mix_a.shape[1], mix_b.shape[1]
    wt = w_router.T.astype(F32)
    wh = wt.astype(BF16)
    wl = (wt - wh.astype(F32)).astype(BF16)
    r = np.arange(tm)
    upper = jnp.asarray(r[:, None] < r[None, :], BF16)
    row = lambda i: (i, 0)
    fixed = lambda i: (0, 0)
    pair_out = pl.BlockSpec((TOP_K, tm), lambda i: (0, i))
    return pl.pallas_call(
        _outproj_ln_kernel,
        out_shape=(jax.ShapeDtypeStruct((m, d), F32),
                   jax.ShapeDtypeStruct((TOP_K, m), I32), jax.ShapeDtypeStruct((TOP_K, m), F32),
                   jax.ShapeDtypeStruct((TOP_K, m), I32), jax.ShapeDtypeStruct((N_EXPERTS, 1), F32)),
        grid=(m // tm,),
        in_specs=[pl.BlockSpec((tm, d), row), pl.BlockSpec((tm, ka), row), pl.BlockSpec((tm, kb), row),
                  pl.BlockSpec((ka, d), fixed), pl.BlockSpec((kb, d), fixed),
                  pl.BlockSpec((1, d), fixed), pl.BlockSpec((1, d), fixed),
                  pl.BlockSpec((N_EXPERTS, d), fixed), pl.BlockSpec((N_EXPERTS, d), fixed),
                  pl.BlockSpec((N_EXPERTS, 1), fixed), pl.BlockSpec((tm, tm), fixed)],
        out_specs=(pl.BlockSpec((tm, d), row), pair_out, pair_out, pair_out, pl.BlockSpec((N_EXPERTS, 1), fixed)),
        compiler_params=_cparams("arbitrary"),
        name="out_proj_ln",
    )(x, mix_a, mix_b, w_a, w_b, g.reshape(1, d), b.reshape(1, d), wh, wl, b_router.reshape(-1, 1).astype(F32), upper)


def _chunk_constants():
    t = np.arange(CHUNK)
    tri = (t[:, None] >= t[None, :]).astype(np.float32)
    pair, odd = [], []
    for lev in range(1, N_LEVELS + 1):
        c = CHUNK >> lev
        pair.append((t[:, None] // (2 * c) == t[None, :] // (2 * c)).astype(np.float32))
        odd.append(np.broadcast_to((((t // c) & 1) == 1).astype(np.float32)[:, None], (CHUNK, LANES)))
    pair.append(np.eye(CHUNK, dtype=np.float32))
    return jnp.asarray(tri, BF16), jnp.asarray(np.stack(pair), F32), jnp.asarray(np.stack(odd), F32)


def _level_log_decay(la, bcum, lev):
    c = CHUNK >> lev
    if 2 * c >= SUBLANES:
        mids = [jnp.broadcast_to(bcum[g * 2 * c + c - 1:g * 2 * c + c], (2 * c, bcum.shape[1]))
                for g in range(CHUNK // (2 * c))]
        return -jnp.abs(bcum - (mids[0] if len(mids) == 1 else jnp.concatenate(mids, axis=0)))
    r = lax.broadcasted_iota(I32, la.shape, 0) & (2 * c - 1)
    if c == 2:
        nxt = pltpu.roll(la, CHUNK - 1, axis=0)
        prv = pltpu.roll(la, 1, axis=0)
        return jnp.where(r == 0, nxt, jnp.where(r == 1, 0.0, jnp.where(r == 2, la, la + prv)))
    return jnp.where(r == 1, la, 0.0)


def _glr_chunk(q, k, v, la, st_ref, cum_ref, pair_ref, odd_ref):
    la = la * LOG2E
    bcum = _dot01(cum_ref[...], la)
    attn = pair_ref[N_LEVELS] * _dot_nt(q.astype(BF16), k.astype(BF16))
    for lev in range(1, N_LEVELS + 1):
        e = jnp.exp2(_level_log_decay(la, bcum, lev))
        c = CHUNK >> lev
        if c >= SUBLANES:
            zero = jnp.zeros((c, q.shape[1]), F32)
            blocks = [slice(b * c, (b + 1) * c) for b in range(CHUNK // c)]
            ql = jnp.concatenate([q[s] * e[s] if b % 2 else zero for b, s in enumerate(blocks)], axis=0)
            kl = jnp.concatenate([zero if b % 2 else k[s] * e[s] for b, s in enumerate(blocks)], axis=0)
        else:
            eq = e * odd_ref[lev - 1]
            ql = q * eq
            kl = k * (e - eq)
        attn = attn + pair_ref[lev - 1] * _dot_nt(ql.astype(BF16), kl.astype(BF16))
    st = st_ref[...]
    o = _dot_nt((q * jnp.exp2(bcum)).astype(BF16), st.astype(BF16)) + _dot(attn.astype(BF16), v.astype(BF16))
    b_last = bcum[CHUNK - 1:CHUNK]
    kdec = (k * jnp.exp2(b_last - bcum)).astype(BF16)
    st_ref[...] = st * jnp.exp2(b_last) + _dot_tn(v.astype(BF16), kdec)
    return o


def _rms_gate(o, g, gate):
    ms = jnp.mean(o * o, axis=-1, keepdims=True)
    return o * lax.rsqrt(ms + EPS) * g * gate


HEAD_GROUP = 4


def _hgrn2_kernel(q_ref, f_ref, i_ref, g_ref, lb_ref, ng_ref, cum_ref, pair_ref, odd_ref, o_ref, st_ref):
    @pl.when(pl.program_id(2) == 0)
    def _():
        st_ref[...] = jnp.zeros_like(st_ref)

    def body(c, carry):
        rows = pl.ds(pl.multiple_of(c * CHUNK, CHUNK), CHUNK)
        for hh in range(HEAD_GROUP):
            ck = slice(A_DK * hh, A_DK * (hh + 1))
            cv = slice(A_DV * hh, A_DV * (hh + 1))
            lb = lb_ref[:, ck]
            f = lb + (1.0 - lb) * _sigmoid(f_ref[rows, ck])
            o = _glr_chunk(_silu(q_ref[rows, ck]), 1.0 - f, i_ref[rows, cv], jnp.log(f),
                           st_ref.at[hh], cum_ref, pair_ref, odd_ref)
            o_ref[rows, cv] = _rms_gate(o, ng_ref[...], _silu(g_ref[rows, cv]))
        return carry

    lax.fori_loop(0, q_ref.shape[0] // CHUNK, body, 0)


def _gla_kernel(q_ref, k_ref, v_ref, g_ref, r_ref, wa_ref, ba_ref, ng_ref, cum_ref, pair_ref, odd_ref,
                o_ref, st_ref):
    @pl.when(pl.program_id(2) == 0)
    def _():
        st_ref[...] = jnp.zeros_like(st_ref)

    def body(c, carry):
        rows = pl.ds(pl.multiple_of(c * CHUNK, CHUNK), CHUNK)
        pre = _dot(r_ref[rows, :].astype(BF16), wa_ref[...]) + ba_ref[...]
        la = _log_sigmoid(pre) * (1.0 / B_GATE_TAU)
        for hh in range(HEAD_GROUP):
            ck = slice(B_DK * hh, B_DK * (hh + 1))
            cv = slice(B_DV * hh, B_DV * (hh + 1))
            o = _glr_chunk(q_ref[rows, ck] * (B_DK ** -0.5), k_ref[rows, ck], v_ref[rows, cv], la[:, ck],
                           st_ref.at[hh], cum_ref, pair_ref, odd_ref)
            o_ref[rows, cv] = _rms_gate(o, ng_ref[...], _silu(g_ref[rows, cv]))
        return carry

    lax.fori_loop(0, q_ref.shape[0] // CHUNK, body, 0)


def _const_spec(arr):
    nd = arr.ndim
    return pl.BlockSpec(arr.shape, lambda *_: (0,) * nd)


def _hgrn2(h, lb, norm_g, batch, seq):
    tb = min(TIME_BLOCK, seq)
    nt = seq // tb
    ng = A_HEADS // HEAD_GROUP
    wk, wv = HEAD_GROUP * A_DK, HEAD_GROUP * A_DV
    consts = _chunk_constants()
    col = lambda seg, w: pl.BlockSpec((tb, w), lambda b, g, t, seg=seg: (b * nt + t, seg * ng + g))
    return pl.pallas_call(
        _hgrn2_kernel,
        out_shape=jax.ShapeDtypeStruct((batch * seq, A_HEADS * A_DV), F32),
        grid=(batch, ng, nt),
        in_specs=[col(0, wk), col(1, wk), col(2, wv), col(3, wv),
                  pl.BlockSpec((1, wk), lambda b, g, t: (0, g)),
                  pl.BlockSpec((1, A_DV), lambda b, g, t: (0, 0))] + [_const_spec(c) for c in consts],
        out_specs=pl.BlockSpec((tb, wv), lambda b, g, t: (b * nt + t, g)),
        scratch_shapes=[pltpu.VMEM((HEAD_GROUP, A_DV, A_DK), F32)],
        compiler_params=_cparams("parallel", "parallel", "arbitrary"),
        name="hgrn2",
    )(h, h, h, h, lb.reshape(1, -1), norm_g.reshape(1, -1), *consts)


def _gla(h, wa2p, ba2, norm_g, batch, seq, q_off, k_off, v_off, g_off, r_off):
    tb = min(TIME_BLOCK, seq)
    nt = seq // tb
    wk, wv = B_HEADS * B_DK, B_HEADS * B_DV
    consts = _chunk_constants()
    col = lambda off, w: pl.BlockSpec((tb, w), lambda b, g, t, off=off: (b * nt + t, off))
    fixed = lambda b, g, t: (0, 0)
    return pl.pallas_call(
        _gla_kernel,
        out_shape=jax.ShapeDtypeStruct((batch * seq, wv), F32),
        grid=(batch, 1, nt),
        in_specs=[col(q_off, wk), col(k_off, wk), col(v_off, wv), col(g_off, wv), col(r_off, LANES),
                  pl.BlockSpec((LANES, wk), fixed), pl.BlockSpec((1, wk), fixed),
                  pl.BlockSpec((1, B_DV), fixed)] + [_const_spec(c) for c in consts],
        out_specs=pl.BlockSpec((tb, wv), lambda b, g, t: (b * nt + t, 0)),
        scratch_shapes=[pltpu.VMEM((B_HEADS, B_DV, B_DK), F32)],
        compiler_params=_cparams("parallel", "parallel", "arbitrary"),
        name="gla",
    )(h, h, h, h, h, wa2p, ba2.reshape(1, -1), norm_g.reshape(1, -1), *consts)


PROJ_TILE = 512
PROJ_ROWS = 2048
AB_MAIN = 4 * A_HEADS * A_DK + 2 * B_HEADS * B_DK + 2 * B_HEADS * B_DV
AB_PAD = AB_MAIN + PROJ_TILE


def _mix_ab(x2d, w_in, lb_logits, li, a_norm_g, wa2, ba2, b_norm_g, batch, seq):
    d = x2d.shape[1]
    w_tail = jnp.pad(w_in[:, AB_MAIN:], ((0, 0), (0, AB_PAD - w_in.shape[1]))).astype(BF16)
    h = _project(x2d, _cast_columns(w_in, AB_MAIN, PROJ_TILE), w_tail, min(PROJ_ROWS, x2d.shape[0]), PROJ_TILE)
    lb = jnp.cumsum(jax.nn.softmax(lb_logits.astype(F32), axis=0), axis=0)[li]
    oa = _hgrn2(h, lb, a_norm_g, batch, seq)
    wa2p = jnp.concatenate([wa2, jnp.zeros((LANES - B_GATE_RANK, wa2.shape[1]), F32)], axis=0).astype(BF16)
    a_cols = 4 * A_HEADS * A_DK
    wk, wv = B_HEADS * B_DK, B_HEADS * B_DV
    ob = _gla(h, wa2p, ba2, b_norm_g, batch, seq, q_off=a_cols // wk, k_off=a_cols // wk + 1,
              v_off=(a_cols + 2 * wk) // wv, g_off=(a_cols + 2 * wk) // wv + 1, r_off=AB_MAIN // LANES)
    return oa, ob


CONV_HALO = 8


def _causal_conv(x_ref, w_ref, buf_ref, tail_ref):
    tb = x_ref.shape[0]
    x = x_ref[...]
    buf_ref[0:CONV_HALO, :] = tail_ref[...]
    buf_ref[CONV_HALO:CONV_HALO + tb, :] = x
    tail_ref[...] = x[tb - CONV_HALO:tb]
    y = w_ref[CONV_K - 1:CONV_K, :] * x
    for j in range(CONV_K - 1):
        y = y + w_ref[j:j + 1, :] * buf_ref[pl.ds(CONV_HALO - (CONV_K - 1) + j, tb), :]
    return y


def _row_to_col(row, eye):
    return jnp.sum(jnp.where(eye, row, 0.0), axis=1, keepdims=True)


def _mlstm_kernel(q_ref, k_ref, v_ref, og_ref, wq_ref, wk_ref, ig_ref, fg_ref, gb_ref, ng_ref, tri_ref,
                  o_ref, ct_ref, n_ref, m_ref, qt_ref, kt_ref, qs_ref, ks_ref, buf_ref):
    @pl.when(pl.program_id(2) == 0)
    def _():
        ct_ref[...] = jnp.zeros_like(ct_ref)
        n_ref[...] = jnp.zeros_like(n_ref)
        m_ref[...] = jnp.zeros_like(m_ref)
        qt_ref[...] = jnp.zeros_like(qt_ref)
        kt_ref[...] = jnp.zeros_like(kt_ref)

    qs_ref[...] = _silu(_causal_conv(q_ref, wq_ref, buf_ref, qt_ref))
    ks_ref[...] = _silu(_causal_conv(k_ref, wk_ref, buf_ref, kt_ref)) * (D_DK ** -0.5)

    r_i = lax.broadcasted_iota(I32, (CHUNK, CHUNK), 0)
    c_i = lax.broadcasted_iota(I32, (CHUNK, CHUNK), 1)
    eye = r_i == c_i
    causal = r_i >= c_i

    def body(c, carry):
        rows = pl.ds(pl.multiple_of(c * CHUNK, CHUNK), CHUNK)
        tri = tri_ref[...]
        for hh in range(D_HEADS):
            ck = slice(D_DK * hh, D_DK * (hh + 1))
            cv = slice(D_DV * hh, D_DV * (hh + 1))
            q = qs_ref[rows, ck]
            k = ks_ref[rows, ck]
            v = v_ref[rows, cv].astype(BF16)
            qb = q.astype(BF16)
            ig_row = ig_ref[hh, pl.ds(c, 1), :] + gb_ref[0, hh]
            lf_row = _log_sigmoid(fg_ref[hh, pl.ds(c, 1), :] + gb_ref[1, hh])
            hi, mid, lo = _split3(lf_row)
            bcum_row = _dot(hi, tri) + _dot(mid, tri) + _dot(lo, tri)
            bcum_col = _row_to_col(bcum_row, eye)
            ig_col = _row_to_col(ig_row, eye)
            m_prev = m_ref[hh, :, 0:1]
            log_w = jnp.where(causal, bcum_col - bcum_row + ig_row, NEG)
            log_inter = bcum_col + m_prev
            m_t = jnp.maximum(jnp.max(log_w, axis=1, keepdims=True), log_inter)
            s = _dot_nt(qb, k.astype(BF16)) * jnp.exp(log_w - m_t)
            w_inter = jnp.exp(log_inter - m_t)
            num = _dot(s.astype(BF16), v) + w_inter * _dot_nt(qb, ct_ref[hh].astype(BF16))
            qn = jnp.sum(s, axis=1, keepdims=True) + w_inter * jnp.sum(q * n_ref[hh], axis=1, keepdims=True)
            h = num / jnp.maximum(jnp.abs(qn), jnp.exp(-m_t))
            o_ref[rows, cv] = _rms_gate(h, ng_ref[...], _sigmoid(og_ref[rows, cv]))
            b_last = bcum_row[:, CHUNK - 1:CHUNK]
            log_u = b_last - bcum_col + ig_col
            m_new = jnp.maximum(b_last + m_prev, jnp.max(log_u, axis=0, keepdims=True))
            decay = jnp.exp(b_last + m_prev - m_new)
            ku = k * jnp.exp(log_u - m_new)
            ct_ref[hh] = decay * ct_ref[hh] + _dot_tn(v, ku.astype(BF16))
            n_ref[hh] = decay * n_ref[hh] + jnp.sum(ku, axis=0, keepdims=True)
            m_ref[hh] = jnp.broadcast_to(m_new, (1, LANES))
        return carry

    lax.fori_loop(0, q_ref.shape[0] // CHUNK, body, 0)


def _mlstm(h, conv_w, ig_rows, fg_rows, gate_b, norm_g, batch, seq, q_off, k_off, v_off, g_off):
    tb = min(TIME_BLOCK, seq)
    nt = seq // tb
    nc = tb // CHUNK
    wk, wv = D_HEADS * D_DK, D_HEADS * D_DV
    t = np.arange(CHUNK)
    tri = jnp.asarray(t[:, None] <= t[None, :], BF16)
    gb = jnp.broadcast_to(gate_b.reshape(2, D_HEADS, 1, 1), (2, D_HEADS, 1, CHUNK)).astype(F32)
    col = lambda off, w: pl.BlockSpec((tb, w), lambda b, g, t, off=off: (b * nt + t, off))
    gate = pl.BlockSpec((None, D_HEADS, None, nc, CHUNK), lambda b, g, t: (b, 0, t, 0, 0))
    fixed = lambda b, g, t: (0, 0)
    return pl.pallas_call(
        _mlstm_kernel,
        out_shape=jax.ShapeDtypeStruct((batch * seq, wv), F32),
        grid=(batch, 1, nt),
        in_specs=[col(q_off, wk), col(k_off, wk), col(v_off, wv), col(g_off, wv),
                  pl.BlockSpec((CONV_K, wk), lambda b, g, t: (0, 0)),
                  pl.BlockSpec((CONV_K, wk), lambda b, g, t: (0, 1)),
                  gate, gate,
                  pl.BlockSpec((2, D_HEADS, 1, CHUNK), lambda b, g, t: (0, 0, 0, 0)),
                  pl.BlockSpec((1, D_DV), fixed), pl.BlockSpec((CHUNK, CHUNK), fixed)],
        out_specs=pl.BlockSpec((tb, wv), lambda b, g, t: (b * nt + t, 0)),
        scratch_shapes=[pltpu.VMEM((D_HEADS, D_DV, D_DK), F32), pltpu.VMEM((D_HEADS, 1, D_DK), F32),
                        pltpu.VMEM((D_HEADS, 1, LANES), F32),
                        pltpu.VMEM((CONV_HALO, wk), F32), pltpu.VMEM((CONV_HALO, wk), F32),
                        pltpu.VMEM((tb, wk), F32), pltpu.VMEM((tb, wk), F32),
                        pltpu.VMEM((tb + CONV_HALO, wk), F32)],
        compiler_params=_cparams("parallel", "parallel", "arbitrary"),
        name="mlstm",
    )(h, h, h, h, conv_w, conv_w, ig_rows, fg_rows, gb, norm_g.reshape(1, -1), tri)


CD_CQ, CD_IQ, CD_DQ, CD_DK, CD_DV, CD_DOG, CD_CKV, CD_TAIL = 0, 1024, 2048, 2560, 3072, 4096, 5120, 5376
CD_PAD = 5632
TAIL_IK, TAIL_IW, TAIL_DI, TAIL_DF = 0, 64, 80, 84


def _pack_cd(w_in):
    cq, ckv, iq, ik, iw, dq, dk, dv, di, df, dog = jnp.split(
        w_in, [int(i) for i in np.cumsum(
            (C_HEADS * C_DH, C_DLAT, IDX_HEADS * IDX_DIM, IDX_DIM, IDX_HEADS, D_HEADS * D_DK, D_HEADS * D_DK,
             D_HEADS * D_DV, D_HEADS, D_HEADS))], axis=1)
    used = CD_TAIL + IDX_DIM + IDX_HEADS + 2 * D_HEADS
    pad = jnp.zeros((w_in.shape[0], CD_PAD - used), F32)
    main = jnp.concatenate([cq, iq, dq, dk, dv, dog], axis=1).astype(BF16)
    tail = jnp.concatenate([ckv, ik, iw, di, df, pad], axis=1).astype(BF16)
    return main, tail


KB = 2 * QB
IDX_GROUP = 4
ATT_GROUP = 4
SUM_ROWS = 16


def _dsa_prep_kernel(ckv_ref, tail_ref, g_ref, ckv_o, ckvt_o, kdup_o):
    c = ckv_ref[...]
    cn = c * lax.rsqrt(jnp.mean(c * c, axis=-1, keepdims=True) + EPS) * g_ref[...]
    ckv_o[...] = cn.astype(BF16)
    ckvt_o[0:C_DLAT, :] = cn.T.astype(BF16)
    ckvt_o[C_DLAT:C_DLAT + SUM_ROWS, :] = jnp.ones((SUM_ROWS, KB), BF16)
    tail = tail_ref[...]
    lane = lax.broadcasted_iota(I32, tail.shape, 1)
    kdup_o[...] = jnp.where(lane < IDX_DIM, tail, pltpu.roll(tail, IDX_DIM, axis=1)).astype(BF16)


def _dsa_prep(h, ckv_g, n_rows):
    nb = n_rows // KB
    return pl.pallas_call(
        _dsa_prep_kernel,
        out_shape=(jax.ShapeDtypeStruct((nb, KB, C_DLAT), BF16), jax.ShapeDtypeStruct((nb, C_DLAT + SUM_ROWS, KB), BF16),
                   jax.ShapeDtypeStruct((nb, KB, LANES), BF16)),
        grid=(nb,),
        in_specs=[pl.BlockSpec((KB, C_DLAT), lambda i: (i, CD_CKV // C_DLAT)),
                  pl.BlockSpec((KB, LANES), lambda i: (i, CD_TAIL // LANES)),
                  pl.BlockSpec((1, C_DLAT), lambda i: (0, 0))],
        out_specs=(pl.BlockSpec((None, KB, C_DLAT), lambda i: (i, 0, 0)),
                   pl.BlockSpec((None, C_DLAT + SUM_ROWS, KB), lambda i: (i, 0, 0)),
                   pl.BlockSpec((None, KB, LANES), lambda i: (i, 0, 0))),
        compiler_params=_cparams("parallel"),
        name="dsa_prep",
    )(h, h, ckv_g.reshape(1, -1))


def _sortable_key(x):
    b = lax.bitcast_convert_type(x, I32)
    key = b ^ ((b >> 31) & 0x7FFFFFFF)
    return jnp.where(key == -1, 0, key)


BISECT_STEPS = 4


def _indexer_kernel(iq_ref, tail_ref, kdup_ref, tri_ref, mask_ref, key_ref, wst_ref, gmax_ref, *, k_sel):
    j = pl.program_id(1)
    nk = key_ref.shape[0]
    n_live = (j * QB + QB + KB - 1) // KB
    w_t = tail_ref[...].T
    lane = lax.broadcasted_iota(I32, (QB, LANES), 1)
    for p in range(IDX_HEADS // 2):
        pair = iq_ref[:, LANES * p:LANES * (p + 1)]
        g, r = divmod(2 * p, IDX_GROUP)
        wst_ref[g, r * QB:(r + 1) * QB, :] = jnp.where(lane < IDX_DIM, pair, 0.0).astype(BF16)
        wst_ref[g, (r + 1) * QB:(r + 2) * QB, :] = jnp.where(lane >= IDX_DIM, pair, 0.0).astype(BF16)

    s_loc = lax.broadcasted_iota(I32, (KB, QB), 0)
    t_abs = j * QB + lax.broadcasted_iota(I32, (1, QB), 1)

    def score_chunk(kc, carry):
        kd = kdup_ref[kc]
        acc = jnp.zeros((KB, QB), F32)
        for g in range(IDX_HEADS // IDX_GROUP):
            dots = _dot_nt(kd, wst_ref[g])
            for r in range(IDX_GROUP):
                row = TAIL_IW + g * IDX_GROUP + r
                acc = acc + jnp.maximum(dots[:, r * QB:(r + 1) * QB], 0.0) * w_t[row:row + 1, :]
        key = jnp.where(kc * KB + s_loc > t_abs, INT_MIN, _sortable_key(acc))
        key_ref[kc] = key
        gmax_ref[...] = jnp.maximum(gmax_ref[...], key)
        key = key.reshape(KB // SUBLANES, SUBLANES, QB)
        k_max, k_min = carry
        k_max = jnp.maximum(k_max, jnp.max(key, axis=0))
        k_min = jnp.minimum(k_min, jnp.min(jnp.where(key == INT_MIN, INT_MAX, key), axis=0))
        return k_max, k_min

    def score_two(i, carry):
        return score_chunk(2 * i + 1, score_chunk(2 * i, carry))

    gmax_ref[...] = jnp.full(gmax_ref.shape, INT_MIN, I32)
    k_max, k_min = lax.fori_loop(0, (n_live + 1) // 2, score_two, (jnp.full((SUBLANES, QB), INT_MIN, I32),
                                                                   jnp.full((SUBLANES, QB), INT_MAX, I32)))
    g_low = jnp.min(gmax_ref[...].reshape(KB // SUBLANES, SUBLANES, QB), axis=0)
    for shift in (4, 2, 1):
        k_max = jnp.maximum(k_max, pltpu.roll(k_max, shift, axis=0))
        k_min = jnp.minimum(k_min, pltpu.roll(k_min, shift, axis=0))
        g_low = jnp.minimum(g_low, pltpu.roll(g_low, shift, axis=0))
    k_low = jnp.maximum(k_min, g_low)
    k_row = jnp.minimum(k_sel, j * QB + lax.broadcasted_iota(I32, (SUBLANES, QB), 1) + 1)

    def count(pred):
        def add(kc, acc):
            hit = jnp.where(pred(key_ref[kc].reshape(KB // SUBLANES, SUBLANES, QB)), 1, 0)
            return acc + jnp.sum(hit, axis=0)

        def add_two(i, acc):
            return add(2 * i + 1, add(2 * i, acc))
        acc = lax.fori_loop(0, (n_live + 1) // 2, add_two, jnp.zeros((SUBLANES, QB), I32))
        for shift in (4, 2, 1):
            acc = acc + pltpu.roll(acc, shift, axis=0)
        return acc

    def unfinished(state):
        lo, hi = state
        return jnp.max(jnp.where(lo < hi, 1, 0)) > 0

    def halve(state):
        lo, hi = state
        mid = (lo >> 1) + (hi >> 1) + (((lo & 1) + (hi & 1) + 1) >> 1)
        cnt = count(lambda k: k >= mid)
        enough = cnt >= k_row
        lo_n = jnp.where(enough, mid, lo)
        hi_n = jnp.where(cnt == k_row, mid, jnp.where(enough, hi, mid - 1))
        return lo_n, hi_n

    def halve_steps(state):
        for _ in range(BISECT_STEPS):
            state = halve(state)
        return state

    tau8, _ = lax.while_loop(unfinished, halve_steps, (k_low, k_max))
    n_ge = count(lambda k: k >= tau8)
    has_tie = jnp.max(jnp.where(n_ge != k_row, 1, 0)) > 0
    tau = tau8[0:1, :]

    @pl.when(jnp.logical_not(has_tie))
    def _():
        def put(kc, carry):
            mask_ref[kc] = jnp.where(key_ref[kc] >= tau, 1.0, 0.0).astype(BF16)
            return carry
        lax.fori_loop(0, n_live, put, 0)

    @pl.when(has_tie)
    def _():
        need = (k_row - count(lambda k: k > tau8)).astype(F32)[0:1, :]

        def put(kc, seen):
            k = key_ref[kc]
            eq = jnp.where(k == tau, 1.0, 0.0)
            before = _dot(tri_ref[...], eq.astype(BF16)) + seen
            take = jnp.where(k > tau, 1.0, jnp.where(before < need, eq, 0.0))
            mask_ref[kc] = take.astype(BF16)
            return seen + jnp.sum(eq, axis=0, keepdims=True)
        lax.fori_loop(0, n_live, put, jnp.zeros((1, QB), F32))

    def clear(kc, carry):
        mask_ref[kc] = jnp.zeros((KB, QB), BF16)
        return carry
    lax.fori_loop(n_live, nk, clear, 0)


def _indexer(h, kdup, batch, seq):
    nq = seq // QB
    nk = seq // KB
    k_sel = min(TOPK_MAX, seq // 4)
    r = np.arange(KB)
    tri = jnp.asarray(r[None, :] < r[:, None], BF16)
    return pl.pallas_call(
        functools.partial(_indexer_kernel, k_sel=k_sel),
        out_shape=jax.ShapeDtypeStruct((batch, nk, KB, seq), BF16),
        grid=(batch, nq),
        in_specs=[pl.BlockSpec((QB, IDX_HEADS * IDX_DIM), lambda b, j: (b * nq + j, CD_IQ // (IDX_HEADS * IDX_DIM))),
                  pl.BlockSpec((QB, LANES), lambda b, j: (b * nq + j, CD_TAIL // LANES)),
                  pl.BlockSpec((None, nk, KB, LANES), lambda b, j: (b, 0, 0, 0)),
                  pl.BlockSpec((KB, KB), lambda b, j: (0, 0))],
        out_specs=pl.BlockSpec((None, nk, KB, QB), lambda b, j: (b, 0, 0, j)),
        scratch_shapes=[pltpu.VMEM((nk, KB, QB), I32),
                        pltpu.VMEM((IDX_HEADS // IDX_GROUP, IDX_GROUP * QB, LANES), BF16),
                        pltpu.VMEM((KB, QB), I32)],
        compiler_params=_cparams("parallel", "parallel"),
        name="dsa_indexer",
    )(h, h, kdup.reshape(batch, nk, KB, LANES), tri)


def _dsa_attn_kernel(cq_ref, mask_ref, ckv_ref, ckvt_ref, wuk_ref, wuvt_ref, bias_ref, o_ref,
                     qt_ref, m_ref, acc_ref, ot_ref, p_ref, alpha_ref):
    j = pl.program_id(1)
    for hh in range(C_HEADS):
        q_h = cq_ref[:, C_DH * hh:C_DH * (hh + 1)].astype(BF16)
        part = slice(QB * (hh % ATT_GROUP), QB * (hh % ATT_GROUP + 1))
        qt_ref[hh // ATT_GROUP, :, part] = (_dot_nt(wuk_ref[hh], q_h) * (C_DH ** -0.5 * LOG2E)).astype(BF16)
    m_ref[...] = jnp.full(m_ref.shape, NEG, F32)
    acc_ref[...] = jnp.zeros_like(acc_ref)
    p_ref[...] = jnp.zeros_like(p_ref)
    alpha_ref[...] = jnp.ones_like(alpha_ref)

    n_live = (j * QB + QB + KB - 1) // KB
    odd = (j % 2) == 1

    def accumulate(kc_done, hp):
        acc_ref[hp] = alpha_ref[hp] * acc_ref[hp] + _dot(ckvt_ref[kc_done], p_ref[hp])

    def body(kc, carry, near):
        ck = ckv_ref[kc]
        kc_prev = jnp.maximum(kc - 1, 0)
        drop = jnp.where(mask_ref[kc].astype(F32) > 0.5, 0.0, NEG)
        drop = jnp.concatenate([drop] * ATT_GROUP, axis=1)
        back = n_live - 1 - kc
        which = jnp.where(back == 0, jnp.where(odd, 0, 1), jnp.where(jnp.logical_and(back == 1, jnp.logical_not(odd)), 2, 3))
        for hp in range(C_HEADS // ATT_GROUP):
            accumulate(kc_prev, hp)
            if near:
                tiles = [bias_ref[hp * ATT_GROUP + g, which] for g in range(ATT_GROUP)]
                logit = _dot(ck, qt_ref[hp]) + (jnp.concatenate(tiles, axis=1) + drop)
            else:
                logit = _dot(ck, qt_ref[hp]) + drop
            m_old = m_ref[hp]
            m_new = jnp.maximum(m_old, jnp.max(logit, axis=0, keepdims=True))
            alpha_ref[hp] = jnp.exp2(m_old - m_new)
            p_ref[hp] = jnp.exp2(logit - m_new).astype(BF16)
            m_ref[hp] = m_new
        return carry

    def body_two(i, carry, near):
        return body(2 * i + 1, body(2 * i, carry, near), near)

    def body_four(i, carry):
        return body_two(2 * i + 1, body_two(2 * i, carry, False), False)

    n_pairs = (n_live + 1) // 2
    n_far = jnp.maximum(n_pairs - 2, 0)
    lax.fori_loop(0, n_far // 2, body_four, 0)
    lax.fori_loop(2 * (n_far // 2), n_far, functools.partial(body_two, near=False), 0)
    lax.fori_loop(n_far, n_pairs, functools.partial(body_two, near=True), 0)
    for hp in range(C_HEADS // ATT_GROUP):
        accumulate(2 * n_pairs - 1, hp)
    for hh in range(C_HEADS):
        part = slice(QB * (hh % ATT_GROUP), QB * (hh % ATT_GROUP + 1))
        total = acc_ref[hh // ATT_GROUP, C_DLAT:C_DLAT + 1, part]
        o_lat = (acc_ref[hh // ATT_GROUP, 0:C_DLAT, part] * (1.0 / total)).astype(BF16)
        ot_ref[C_DH * hh:C_DH * (hh + 1), :] = _dot(wuvt_ref[hh], o_lat)
    o_ref[...] = ot_ref[...].T


def _rel_bias_tiles(rel_table):
    s = np.arange(QB)[:, None]
    t = np.arange(QB)[None, :]
    diag, prev, far = np.maximum(t - s, 0), QB + t - s, np.full((QB, QB), 2 * QB)
    kinds = [(prev, diag), (diag, far), (far, prev), (far, far)]
    n = jnp.asarray(np.stack([np.concatenate(k, axis=0) for k in kinds]).astype(np.int32))
    max_exact = REL_BUCKETS // 2
    large = max_exact + (jnp.log(jnp.maximum(n, 1).astype(F32) / max_exact)
                         / math.log(REL_MAX_DIST / max_exact) * (REL_BUCKETS - max_exact)).astype(I32)
    bucket = jnp.where(n < max_exact, n, jnp.minimum(large, REL_BUCKETS - 1))
    onehot = (bucket[..., None] == jnp.arange(REL_BUCKETS, dtype=I32)).astype(F32)
    bias = jnp.einsum("hb,kstb->hkst", rel_table.astype(F32).T * LOG2E, onehot, precision=lax.Precision.HIGHEST)
    return bias - bias[:, 3:4, 0:1, 0:1]


def _dsa_attention(h, mask, ckv, ckvt, w_uk, w_uv, rel_table, batch, seq):
    nq = seq // QB
    nk = seq // KB
    wuk = w_uk.transpose(1, 0, 2).astype(BF16)
    wuvt = w_uv.transpose(1, 2, 0).astype(BF16)
    bias = _rel_bias_tiles(rel_table)
    return pl.pallas_call(
        _dsa_attn_kernel,
        out_shape=jax.ShapeDtypeStruct((batch * seq, C_HEADS * C_DH), F32),
        grid=(batch, nq),
        in_specs=[pl.BlockSpec((QB, C_HEADS * C_DH), lambda b, j: (b * nq + j, CD_CQ // (C_HEADS * C_DH))),
                  pl.BlockSpec((None, nk, KB, QB), lambda b, j: (b, 0, 0, j)),
                  pl.BlockSpec((None, nk, KB, C_DLAT), lambda b, j: (b, 0, 0, 0)),
                  pl.BlockSpec((None, nk, C_DLAT + SUM_ROWS, KB), lambda b, j: (b, 0, 0, 0)),
                  _const_spec(wuk), _const_spec(wuvt), _const_spec(bias)],
        out_specs=pl.BlockSpec((QB, C_HEADS * C_DH), lambda b, j: (b * nq + j, 0)),
        scratch_shapes=[pltpu.VMEM((C_HEADS // ATT_GROUP, C_DLAT, ATT_GROUP * QB), BF16),
                        pltpu.VMEM((C_HEADS // ATT_GROUP, 1, ATT_GROUP * QB), F32),
                        pltpu.VMEM((C_HEADS // ATT_GROUP, C_DLAT + SUM_ROWS, ATT_GROUP * QB), F32),
                        pltpu.VMEM((C_HEADS * C_DH, QB), F32),
                        pltpu.VMEM((C_HEADS // ATT_GROUP, KB, ATT_GROUP * QB), BF16),
                        pltpu.VMEM((C_HEADS // ATT_GROUP, 1, ATT_GROUP * QB), F32)],
        compiler_params=_cparams("parallel", "parallel"),
        name="dsa_attention",
    )(h, mask, ckv.reshape(batch, nk, KB, C_DLAT), ckvt.reshape(batch, nk, C_DLAT + SUM_ROWS, KB), wuk, wuvt, bias)


ROUTER_BLOCK = 512
MOE_TILE = 512
COMBINE_ROWS = 64


def _route_block(x, wh_ref, wl_ref, b_ref, upper_ref, idx_ref, wgt_ref, rank_ref, cnt_ref):
    xh = x.astype(BF16)
    xl = (x - xh.astype(F32)).astype(BF16)
    logit = _dot_nt(wh_ref[...], xh) + _dot_nt(wl_ref[...], xh) + _dot_nt(wh_ref[...], xl)
    aff = _sigmoid(logit)
    sel = aff + b_ref[...]
    s_rows = [sel[e:e + 1] for e in range(N_EXPERTS)]
    a_rows = [aff[e:e + 1] for e in range(N_EXPERTS)]
    n = EXPERTS_PER_GROUP

    g_best = jnp.zeros(s_rows[0].shape, I32)
    best = None
    for g in range(N_GROUPS):
        v = s_rows[g * n:(g + 1) * n]
        top2 = None
        for a in range(n):
            for b in range(a + 1, n):
                pair = v[a] + v[b]
                top2 = pair if top2 is None else jnp.maximum(top2, pair)
        if best is None:
            best = top2
        else:
            upd = top2 > best
            g_best = jnp.where(upd, g, g_best)
            best = jnp.where(upd, top2, best)

    sv, av = [], []
    for i in range(n):
        s_i, a_i = s_rows[i], a_rows[i]
        for g in range(1, N_GROUPS):
            pick = g_best == g
            s_i = jnp.where(pick, s_rows[g * n + i], s_i)
            a_i = jnp.where(pick, a_rows[g * n + i], a_i)
        sv.append(s_i)
        av.append(a_i)

    i1, s1, a1 = jnp.zeros_like(g_best), sv[0], av[0]
    for i in range(1, n):
        upd = sv[i] > s1
        i1 = jnp.where(upd, i, i1)
        s1 = jnp.where(upd, sv[i], s1)
        a1 = jnp.where(upd, av[i], a1)
    i2 = jnp.zeros_like(g_best)
    s2 = jnp.full(s1.shape, -jnp.inf, F32)
    a2 = jnp.zeros_like(a1)
    for i in range(n):
        cand = jnp.where(i1 == i, -jnp.inf, sv[i])
        upd = cand > s2
        i2 = jnp.where(upd, i, i2)
        s2 = jnp.where(upd, cand, s2)
        a2 = jnp.where(upd, av[i], a2)

    tot = a1 + a2
    e1 = g_best * n + i1
    e2 = g_best * n + i2
    idx_ref[0:1, :] = e1
    idx_ref[1:2, :] = e2
    wgt_ref[0:1, :] = a1 / tot
    wgt_ref[1:2, :] = a2 / tot

    @pl.when(pl.program_id(0) == 0)
    def _():
        cnt_ref[...] = jnp.zeros_like(cnt_ref)

    e_iota = lax.broadcasted_iota(I32, sel.shape, 0)
    oh1 = jnp.where(e_iota == e1, 1.0, 0.0)
    oh2 = jnp.where(e_iota == e2, 1.0, 0.0)
    both = oh1 + oh2
    before = cnt_ref[...] + _dot(both.astype(BF16), upper_ref[...])
    rank_ref[0:1, :] = jnp.sum(oh1 * before, axis=0, keepdims=True).astype(I32)
    rank_ref[1:2, :] = jnp.sum(oh2 * before, axis=0, keepdims=True).astype(I32)
    cnt_ref[...] = cnt_ref[...] + jnp.sum(both, axis=1, keepdims=True)


def _route_tables(idx, rank, counts, tm, n_items):
    cnt = counts.reshape(-1).astype(I32)
    start = jnp.cumsum(cnt) - cnt
    experts = jnp.arange(N_EXPERTS, dtype=I32)
    pos = rank + jnp.sum(jnp.where(idx[..., None] == experts, start, 0), axis=-1)
    first_tile = start // tm
    n_e = jnp.where(cnt > 0, (start + cnt - 1) // tm - first_tile + 1, 0)
    item_end = jnp.cumsum(n_e)
    item = jnp.arange(n_items, dtype=I32)
    used = item < item_end[-1]
    e_i = jnp.minimum(jnp.sum((item[:, None] >= item_end[None, :]).astype(I32), axis=1), N_EXPERTS - 1)
    e_last = jnp.max(jnp.where(cnt > 0, experts, 0))
    e_i = jnp.where(used, e_i, e_last)
    tile_i = jnp.where(used, first_tile[e_i] + item - (item_end - n_e)[e_i], (TOP_K * idx.shape[1]) // tm - 1)
    lo = jnp.where(used, jnp.maximum(start[e_i], tile_i * tm) - tile_i * tm, 0)
    hi = jnp.where(used, jnp.minimum(start[e_i] + cnt[e_i], (tile_i + 1) * tm) - tile_i * tm, 0)
    return pos.astype(I32), tile_i.astype(I32), e_i, lo.astype(I32), hi.astype(I32)


def _row_of(ref, row):
    return ref.at[lax.shift_right_logical(row, SUBLANES.bit_length() - 1), pl.ds(row & (SUBLANES - 1), 1)]


def _dispatch_kernel(pos_ref, x_ref, xs_hbm, stage, sem, *, tb, n_tok):
    i = pl.program_id(0)
    slot = i % 2
    stage[slot] = x_ref[...].reshape(stage.shape[1:])

    def issue(grp, carry):
        dst = [[pos_ref[s * n_tok + i * tb + grp * SUBLANES + u] for s in range(TOP_K)] for u in range(SUBLANES)]
        for u in range(SUBLANES):
            for s in range(TOP_K):
                row = dst[u][s]
                pltpu.make_async_copy(stage.at[slot, grp, pl.ds(u, 1)], _row_of(xs_hbm, row), sem.at[slot]).start()
        return carry

    lax.fori_loop(0, tb // SUBLANES, issue, 0)

    def wait_block(s):
        for _ in range(TOP_K):
            pltpu.make_async_copy(stage.at[s], xs_hbm.at[pl.ds(0, tb // SUBLANES)], sem.at[s]).wait()

    @pl.when(i >= 1)
    def _():
        wait_block(1 - slot)

    @pl.when(i == pl.num_programs(0) - 1)
    def _():
        wait_block(slot)


def _dispatch(x2d, pos_flat, tb):
    t, d = x2d.shape
    grid_spec = pltpu.PrefetchScalarGridSpec(
        num_scalar_prefetch=1, grid=(t // tb,),
        in_specs=[pl.BlockSpec((tb, d), lambda i, pos: (i, 0))],
        out_specs=pl.BlockSpec(memory_space=pl.ANY),
        scratch_shapes=[pltpu.VMEM((2, tb // SUBLANES, SUBLANES, d), F32), pltpu.SemaphoreType.DMA((2,))])
    return pl.pallas_call(
        functools.partial(_dispatch_kernel, tb=tb, n_tok=t),
        out_shape=jax.ShapeDtypeStruct((TOP_K * t // SUBLANES, SUBLANES, d), F32),
        grid_spec=grid_spec,
        compiler_params=_cparams("arbitrary"),
        name="moe_dispatch",
    )(pos_flat, x2d).reshape(TOP_K * t, d)


def _experts_kernel(tile_ref, exp_ref, lo_ref, hi_ref, x_ref, w1_ref, w3_ref, w2_ref, o_ref, w1b, w3b, w2b):
    i = pl.program_id(0)
    prev = jnp.maximum(i - 1, 0)

    @pl.when(jnp.logical_or(i == 0, exp_ref[i] != exp_ref[prev]))
    def _():
        w1b[...] = w1_ref[...].astype(BF16)
        w3b[...] = w3_ref[...].astype(BF16)
        w2b[...] = w2_ref[...].astype(BF16)

    lo = lo_ref[i]
    hi = hi_ref[i]
    first = jnp.logical_or(i == 0, tile_ref[i] != tile_ref[prev])

    @pl.when(hi > lo)
    def _():
        x = x_ref[...].astype(BF16)
        row = lax.broadcasted_iota(I32, (x.shape[0], 1), 0)
        mine = jnp.where(jnp.logical_and(row >= lo, row < hi), 1.0, 0.0)
        hid = _silu(_dot(x, w1b[...])) * _dot(x, w3b[...]) * mine
        y = _dot(hid.astype(BF16), w2b[...])

        @pl.when(first)
        def _():
            o_ref[...] = y

        @pl.when(jnp.logical_not(first))
        def _():
            o_ref[...] += y


def _experts(xs, tile_i, exp_i, lo, hi, w1, w3, w2, layer, tm):
    n, d = xs.shape
    f = w1.shape[-1]
    by_tile = lambda i, tile, exp, lo, hi: (tile[i], 0)
    by_exp = lambda i, tile, exp, lo, hi: (layer, exp[i], 0, 0)
    grid_spec = pltpu.PrefetchScalarGridSpec(
        num_scalar_prefetch=4, grid=(tile_i.shape[0],),
        in_specs=[pl.BlockSpec((tm, d), by_tile),
                  pl.BlockSpec((None, None, d, f), by_exp), pl.BlockSpec((None, None, d, f), by_exp),
                  pl.BlockSpec((None, None, f, d), by_exp)],
        out_specs=pl.BlockSpec((tm, d), by_tile),
        scratch_shapes=[pltpu.VMEM((d, f), BF16), pltpu.VMEM((d, f), BF16), pltpu.VMEM((f, d), BF16)])
    return pl.pallas_call(
        _experts_kernel,
        out_shape=jax.ShapeDtypeStruct((n, d), F32),
        grid_spec=grid_spec,
        compiler_params=_cparams("arbitrary"),
        name="moe_experts",
    )(tile_i, exp_i, lo, hi, xs, w1, w3, w2)


def _combine_ln_kernel(pos_ref, x_ref, w_ref, g_ref, b_ref, ys_hbm, o_ref, gbuf, sem, *, tb, n_tok):
    i = pl.program_id(0)
    slot = i % 2

    def issue(blk, s, grp):
        src = [[pos_ref[k * n_tok + blk * tb + grp * SUBLANES + u] for k in range(TOP_K)] for u in range(SUBLANES)]
        for u in range(SUBLANES):
            for k in range(TOP_K):
                pltpu.make_async_copy(_row_of(ys_hbm, src[u][k]), gbuf.at[s, k, grp, pl.ds(u, 1)], sem.at[s]).start()

    def finish(part):
        tiles = COMBINE_ROWS // SUBLANES
        rows = pl.ds(pl.multiple_of(part * COMBINE_ROWS, COMBINE_ROWS), COMBINE_ROWS)
        w = w_ref[rows, :]
        g0 = gbuf[slot, 0, pl.ds(part * tiles, tiles)].reshape(COMBINE_ROWS, -1)
        g1 = gbuf[slot, 1, pl.ds(part * tiles, tiles)].reshape(COMBINE_ROWS, -1)
        y = g0 * w[:, 0:1] + g1 * w[:, 1:2]
        o_ref[rows, :] = _layer_norm_rows(DN_ALPHA * x_ref[rows, :] + y, g_ref[...], b_ref[...])

    def issue_part(blk, s, part):
        for q in range(COMBINE_ROWS // SUBLANES):
            issue(blk, s, part * (COMBINE_ROWS // SUBLANES) + q)

    @pl.when(i == 0)
    def _():
        lax.fori_loop(0, tb // SUBLANES, lambda grp, c: (issue(0, 0, grp), c)[1], 0)

    for k in range(TOP_K):
        pltpu.make_async_copy(ys_hbm.at[pl.ds(0, tb // SUBLANES)], gbuf.at[slot, k], sem.at[slot]).wait()

    @pl.when(i + 1 < pl.num_programs(0))
    def _():
        def both(part, carry):
            finish(part)
            issue_part(i + 1, 1 - slot, part)
            return carry
        lax.fori_loop(0, tb // COMBINE_ROWS, both, 0)

    @pl.when(i + 1 == pl.num_programs(0))
    def _():
        lax.fori_loop(0, tb // COMBINE_ROWS, lambda part, c: (finish(part), c)[1], 0)


def _combine_ln(x2d, ys, pos_flat, wgt_cols, g, b, tb):
    t, d = x2d.shape
    row = lambda i, pos: (i, 0)
    fixed = lambda i, pos: (0, 0)
    grid_spec = pltpu.PrefetchScalarGridSpec(
        num_scalar_prefetch=1, grid=(t // tb,),
        in_specs=[pl.BlockSpec((tb, d), row), pl.BlockSpec((tb, TOP_K), row),
                  pl.BlockSpec((1, d), fixed), pl.BlockSpec((1, d), fixed),
                  pl.BlockSpec(memory_space=pl.ANY)],
        out_specs=pl.BlockSpec((tb, d), row),
        scratch_shapes=[pltpu.VMEM((2, TOP_K, tb // SUBLANES, SUBLANES, d), F32), pltpu.SemaphoreType.DMA((2,))])
    return pl.pallas_call(
        functools.partial(_combine_ln_kernel, tb=tb, n_tok=t),
        out_shape=jax.ShapeDtypeStruct((t, d), F32),
        grid_spec=grid_spec,
        compiler_params=_cparams("arbitrary"),
        name="moe_combine_ln",
    )(pos_flat, x2d, wgt_cols, g.reshape(1, d), b.reshape(1, d), ys.reshape(-1, SUBLANES, d))


def _moe_ln(x2d, routing, w1, w3, w2, layer, g, b):
    t = x2d.shape[0]
    tm = min(MOE_TILE, t)
    idx, wgt, rank, counts = routing
    n_items = TOP_K * t // tm + N_EXPERTS - 1
    pos, tile_i, exp_i, lo, hi = _route_tables(idx, rank, counts, tm, n_items)
    pos_flat = pos.reshape(-1)
    xs = _dispatch(x2d, pos_flat, tm)
    ys = _experts(xs, tile_i, exp_i, lo, hi, w1, w3, w2, layer, tm)
    return _combine_ln(x2d, ys, pos_flat, wgt.T, g, b, tm)


def _gate_rows(col, batch, seq):
    tb = min(TIME_BLOCK, seq)
    return col.reshape(batch, seq, D_HEADS).transpose(0, 2, 1).reshape(batch, D_HEADS, seq // tb, tb // CHUNK, CHUNK)


def _mix_cd(x2d, w_in, rel_table, ckv_g, w_uk, w_uv, conv_w, gate_b, d_norm_g, batch, seq):
    t = x2d.shape[0]
    h = _project(x2d, *_pack_cd(w_in), min(PROJ_ROWS, t), PROJ_TILE)
    ckv, ckvt, kdup = _dsa_prep(h, ckv_g, t)
    mask = _indexer(h, kdup, batch, seq)
    oc = _dsa_attention(h, mask, ckv, ckvt, w_uk, w_uv, rel_table, batch, seq)
    tail = h[:, CD_TAIL:CD_TAIL + LANES]
    ig_rows = _gate_rows(tail[:, TAIL_DI:TAIL_DI + D_HEADS], batch, seq)
    fg_rows = _gate_rows(tail[:, TAIL_DF:TAIL_DF + D_HEADS], batch, seq)
    od = _mlstm(h, conv_w, ig_rows, fg_rows, gate_b, d_norm_g, batch, seq,
                q_off=CD_DQ // (D_HEADS * D_DK), k_off=CD_DK // (D_HEADS * D_DK),
                v_off=CD_DV // (D_HEADS * D_DV), g_off=CD_DOG // (D_HEADS * D_DV))
    return oc, od


def kernel(x, w_in_ab, w_out_ab, hgrn_lb_logits, a_norm_g, gla_wa2, gla_ba2, b_norm_g, w_in_cd, w_out_cd,
           ckv_norm_g, w_uk, w_uv, mlstm_conv_w, mlstm_gate_b, d_norm_g, rel_table, w_router, b_router,
           moe_w1, moe_w3, moe_w2, ln_g, ln_b):
    batch, seq, d = x.shape
    x2d = x.reshape(batch * seq, d)
    for layer in range(DEPTH):
        li = layer // 2
        if layer % 2 == 0:
            mix_a, mix_b = _mix_ab(x2d, w_in_ab[li], hgrn_lb_logits, li, a_norm_g[li], gla_wa2[li], gla_ba2[li],
                                   b_norm_g[li], batch, seq)
            w_out = w_out_ab[li]
        else:
            mix_a, mix_b = _mix_cd(x2d, w_in_cd[li], rel_table, ckv_norm_g[li], w_uk[li], w_uv[li],
                                   mlstm_conv_w[li], mlstm_gate_b[li], d_norm_g[li], batch, seq)
            w_out = w_out_cd[li]
        ka = mix_a.shape[1]
        x2d, *routing = _outproj_ln(x2d, mix_a, mix_b, w_out[:ka].astype(BF16), w_out[ka:].astype(BF16),
                                    ln_g[layer, 0], ln_b[layer, 0], w_router, b_router)
        x2d = _moe_ln(x2d, routing, moe_w1, moe_w3, moe_w2, layer, ln_g[layer, 1], ln_b[layer, 1])
    return x2d.reshape(batch, seq, d)
```

```python
import functools
import math

import numpy as np
import jax
import jax.numpy as jnp
from jax import lax
from jax.experimental import pallas as pl
from jax.experimental.pallas import tpu as pltpu

F32 = jnp.float32
BF16 = jnp.bfloat16
I32 = jnp.int32

D_MODEL = 2048
DEPTH = 2
A_HEADS, A_DK, A_DV = 8, 128, 128
B_HEADS, B_DK, B_DV = 4, 128, 256
B_GATE_RANK, B_GATE_TAU = 16, 16.0
C_HEADS, C_DH, C_DLAT = 8, 128, 256
IDX_HEADS, IDX_DIM = 16, 64
TOPK_MAX = 256
D_HEADS, D_DK, D_DV = 4, 128, 256
CONV_K = 4
REL_BUCKETS, REL_MAX_DIST = 32, 128
N_EXPERTS, N_GROUPS, TOP_K, D_EXPERT = 16, 4, 2, 512
EXPERTS_PER_GROUP = N_EXPERTS // N_GROUPS
DN_ALPHA = (2 * DEPTH) ** 0.25
EPS = 1e-5

LANES = 128
SUBLANES = 8
VMEM_LIMIT = 56 * 1024 * 1024

CHUNK = 128
N_LEVELS = 7
LOG2E = math.log2(math.e)
TIME_BLOCK = 512
QB = 128
NEG = -1e30
INT_MIN = -2 ** 31
INT_MAX = 2 ** 31 - 1


def _cparams(*sem):
    return pltpu.CompilerParams(dimension_semantics=sem, vmem_limit_bytes=VMEM_LIMIT)


def _dot(a, b):
    return jnp.dot(a, b, preferred_element_type=F32)


def _dot_nt(a, b):
    return lax.dot_general(a, b, (((1,), (1,)), ((), ())), preferred_element_type=F32)


def _dot_tn(a, b):
    return lax.dot_general(a, b, (((0,), (0,)), ((), ())), preferred_element_type=F32)


def _split3(a):
    hi = a.astype(BF16)
    r1 = a - hi.astype(F32)
    mid = r1.astype(BF16)
    lo = (r1 - mid.astype(F32)).astype(BF16)
    return hi, mid, lo


def _dot01(m01, a):
    hi, mid, lo = _split3(a)
    return _dot(m01, hi) + _dot(m01, mid) + _dot(m01, lo)


def _sigmoid(x):
    return 1.0 / (1.0 + jnp.exp(-x))


def _silu(x):
    return x * _sigmoid(x)


def _log_sigmoid(x):
    return jnp.minimum(x, 0.0) - jnp.log(1.0 + jnp.exp(-jnp.abs(x)))


def _proj_kernel(x_ref, w_ref, wt_ref, o_ref, xb_ref):
    j = pl.program_id(1)

    @pl.when(j == 0)
    def _():
        xb_ref[...] = x_ref[...].astype(BF16)

    @pl.when(j < pl.num_programs(1) - 1)
    def _():
        o_ref[...] = _dot(xb_ref[...], w_ref[...])

    @pl.when(j == pl.num_programs(1) - 1)
    def _():
        o_ref[...] = _dot(xb_ref[...], wt_ref[...])


def _cast_kernel(w_ref, o_ref):
    o_ref[...] = w_ref[...].astype(o_ref.dtype)


def _cast_columns(w, n_cols, tn):
    k = w.shape[0]
    return pl.pallas_call(
        _cast_kernel,
        out_shape=jax.ShapeDtypeStruct((k, n_cols), BF16),
        grid=(n_cols // tn,),
        in_specs=[pl.BlockSpec((k, tn), lambda j: (0, j))],
        out_specs=pl.BlockSpec((k, tn), lambda j: (0, j)),
        compiler_params=_cparams("parallel"),
        name="weight_cast",
    )(w)


def _project(x, w, w_tail, tm, tn):
    m, k = x.shape
    n_main = w.shape[1] // tn
    return pl.pallas_call(
        _proj_kernel,
        out_shape=jax.ShapeDtypeStruct((m, (n_main + 1) * tn), F32),
        grid=(m // tm, n_main + 1),
        in_specs=[pl.BlockSpec((tm, k), lambda i, j: (i, 0), pipeline_mode=pl.Buffered(1)),
                  pl.BlockSpec((k, tn), lambda i, j: (0, jnp.minimum(j, n_main - 1))),
                  pl.BlockSpec((k, tn), lambda i, j: (0, 0))],
        out_specs=pl.BlockSpec((tm, tn), lambda i, j: (i, j)),
        scratch_shapes=[pltpu.VMEM((tm, k), BF16)],
        compiler_params=_cparams("parallel", "arbitrary"),
        name="in_proj",
    )(x, w, w_tail)


def _layer_norm_rows(z, g, b):
    mu = jnp.mean(z, axis=-1, keepdims=True)
    zc = z - mu
    var = jnp.mean(zc * zc, axis=-1, keepdims=True)
    return zc * lax.rsqrt(var + EPS) * g + b


def _outproj_ln_kernel(x_ref, ma_ref, mb_ref, wa_ref, wb_ref, g_ref, b_ref, wh_ref, wl_ref, rb_ref, upper_ref,
                       o_ref, idx_ref, wgt_ref, rank_ref, cnt_ref):
    mixed = _dot(ma_ref[...].astype(BF16), wa_ref[...]) + _dot(mb_ref[...].astype(BF16), wb_ref[...])
    z = _layer_norm_rows(DN_ALPHA * x_ref[...] + mixed, g_ref[...], b_ref[...])
    o_ref[...] = z
    _route_block(z, wh_ref, wl_ref, rb_ref, upper_ref, idx_ref, wgt_ref, rank_ref, cnt_ref)


def _outproj_ln(x, mix_a, mix_b, w_a, w_b, g, b, w_router, b_router):
    m, d = x.shape
    tm = min(ROUTER_BLOCK, m)
    ka, kb = mix_a.shape[1], mix_b.shape[1]
    wt = w_router.T.astype(F32)
    wh = wt.astype(BF16)
    wl = (wt - wh.astype(F32)).astype(BF16)
    r = np.arange(tm)
    upper = jnp.asarray(r[:, None] < r[None, :], BF16)
    row = lambda i: (i, 0)
    fixed = lambda i: (0, 0)
    pair_out = pl.BlockSpec((TOP_K, tm), lambda i: (0, i))
    return pl.pallas_call(
        _outproj_ln_kernel,
        out_shape=(jax.ShapeDtypeStruct((m, d), F32),
                   jax.ShapeDtypeStruct((TOP_K, m), I32), jax.ShapeDtypeStruct((TOP_K, m), F32),
                   jax.ShapeDtypeStruct((TOP_K, m), I32), jax.ShapeDtypeStruct((N_EXPERTS, 1), F32)),
        grid=(m // tm,),
        in_specs=[pl.BlockSpec((tm, d), row), pl.BlockSpec((tm, ka), row), pl.BlockSpec((tm, kb), row),
                  pl.BlockSpec((ka, d), fixed), pl.BlockSpec((kb, d), fixed),
                  pl.BlockSpec((1, d), fixed), pl.BlockSpec((1, d), fixed),
                  pl.BlockSpec((N_EXPERTS, d), fixed), pl.BlockSpec((N_EXPERTS, d), fixed),
                  pl.BlockSpec((N_EXPERTS, 1), fixed), pl.BlockSpec((tm, tm), fixed)],
        out_specs=(pl.BlockSpec((tm, d), row), pair_out, pair_out, pair_out, pl.BlockSpec((N_EXPERTS, 1), fixed)),
        compiler_params=_cparams("arbitrary"),
        name="out_proj_ln",
    )(x, mix_a, mix_b, w_a, w_b, g.reshape(1, d), b.reshape(1, d), wh, wl, b_router.reshape(-1, 1).astype(F32), upper)


def _chunk_constants():
    t = np.arange(CHUNK)
    tri = (t[:, None] >= t[None, :]).astype(np.float32)
    pair, odd = [], []
    for lev in range(1, N_LEVELS + 1):
        c = CHUNK >> lev
        pair.append((t[:, None] // (2 * c) == t[None, :] // (2 * c)).astype(np.float32))
        odd.append(np.broadcast_to((((t // c) & 1) == 1).astype(np.float32)[:, None], (CHUNK, LANES)))
    pair.append(np.eye(CHUNK, dtype=np.float32))
    return jnp.asarray(tri, BF16), jnp.asarray(np.stack(pair), F32), jnp.asarray(np.stack(odd), F32)


def _level_log_decay(la, bcum, lev):
    c = CHUNK >> lev
    if 2 * c >= SUBLANES:
        mids = [jnp.broadcast_to(bcum[g * 2 * c + c - 1:g * 2 * c + c], (2 * c, bcum.shape[1]))
                for g in range(CHUNK // (2 * c))]
        return -jnp.abs(bcum - (mids[0] if len(mids) == 1 else jnp.concatenate(mids, axis=0)))
    r = lax.broadcasted_iota(I32, la.shape, 0) & (2 * c - 1)
    if c == 2:
        nxt = pltpu.roll(la, CHUNK - 1, axis=0)
        prv = pltpu.roll(la, 1, axis=0)
        return jnp.where(r == 0, nxt, jnp.where(r == 1, 0.0, jnp.where(r == 2, la, la + prv)))
    return jnp.where(r == 1, la, 0.0)


def _glr_chunk(q, k, v, la, st_ref, cum_ref, pair_ref, odd_ref):
    la = la * LOG2E
    bcum = _dot01(cum_ref[...], la)
    attn = pair_ref[N_LEVELS] * _dot_nt(q.astype(BF16), k.astype(BF16))
    for lev in range(1, N_LEVELS + 1):
        e = jnp.exp2(_level_log_decay(la, bcum, lev))
        c = CHUNK >> lev
        if c >= SUBLANES:
            zero = jnp.zeros((c, q.shape[1]), F32)
            blocks = [slice(b * c, (b + 1) * c) for b in range(CHUNK // c)]
            ql = jnp.concatenate([q[s] * e[s] if b % 2 else zero for b, s in enumerate(blocks)], axis=0)
            kl = jnp.concatenate([zero if b % 2 else k[s] * e[s] for b, s in enumerate(blocks)], axis=0)
        else:
            eq = e * odd_ref[lev - 1]
            ql = q * eq
            kl = k * (e - eq)
        attn = attn + pair_ref[lev - 1] * _dot_nt(ql.astype(BF16), kl.astype(BF16))
    st = st_ref[...]
    o = _dot_nt((q * jnp.exp2(bcum)).astype(BF16), st.astype(BF16)) + _dot(attn.astype(BF16), v.astype(BF16))
    b_last = bcum[CHUNK - 1:CHUNK]
    kdec = (k * jnp.exp2(b_last - bcum)).astype(BF16)
    st_ref[...] = st * jnp.exp2(b_last) + _dot_tn(v.astype(BF16), kdec)
    return o


def _rms_gate(o, g, gate):
    ms = jnp.mean(o * o, axis=-1, keepdims=True)
    return o * lax.rsqrt(ms + EPS) * g * gate


HEAD_GROUP = 4


def _hgrn2_kernel(q_ref, f_ref, i_ref, g_ref, lb_ref, ng_ref, cum_ref, pair_ref, odd_ref, o_ref, st_ref):
    @pl.when(pl.program_id(2) == 0)
    def _():
        st_ref[...] = jnp.zeros_like(st_ref)

    def body(c, carry):
        rows = pl.ds(pl.multiple_of(c * CHUNK, CHUNK), CHUNK)
        for hh in range(HEAD_GROUP):
            ck = slice(A_DK * hh, A_DK * (hh + 1))
            cv = slice(A_DV * hh, A_DV * (hh + 1))
            lb = lb_ref[:, ck]
            f = lb + (1.0 - lb) * _sigmoid(f_ref[rows, ck])
            o = _glr_chunk(_silu(q_ref[rows, ck]), 1.0 - f, i_ref[rows, cv], jnp.log(f),
                           st_ref.at[hh], cum_ref, pair_ref, odd_ref)
            o_ref[rows, cv] = _rms_gate(o, ng_ref[...], _silu(g_ref[rows, cv]))
        return carry

    lax.fori_loop(0, q_ref.shape[0] // CHUNK, body, 0)


def _gla_kernel(q_ref, k_ref, v_ref, g_ref, r_ref, wa_ref, ba_ref, ng_ref, cum_ref, pair_ref, odd_ref,
                o_ref, st_ref):
    @pl.when(pl.program_id(2) == 0)
    def _():
        st_ref[...] = jnp.zeros_like(st_ref)

    def body(c, carry):
        rows = pl.ds(pl.multiple_of(c * CHUNK, CHUNK), CHUNK)
        pre = _dot(r_ref[rows, :].astype(BF16), wa_ref[...]) + ba_ref[...]
        la = _log_sigmoid(pre) * (1.0 / B_GATE_TAU)
        for hh in range(HEAD_GROUP):
            ck = slice(B_DK * hh, B_DK * (hh + 1))
            cv = slice(B_DV * hh, B_DV * (hh + 1))
            o = _glr_chunk(q_ref[rows, ck] * (B_DK ** -0.5), k_ref[rows, ck], v_ref[rows, cv], la[:, ck],
                           st_ref.at[hh], cum_ref, pair_ref, odd_ref)
            o_ref[rows, cv] = _rms_gate(o, ng_ref[...], _silu(g_ref[rows, cv]))
        return carry

    lax.fori_loop(0, q_ref.shape[0] // CHUNK, body, 0)


def _const_spec(arr):
    nd = arr.ndim
    return pl.BlockSpec(arr.shape, lambda *_: (0,) * nd)


def _hgrn2(h, lb, norm_g, batch, seq):
    tb = min(TIME_BLOCK, seq)
    nt = seq // tb
    ng = A_HEADS // HEAD_GROUP
    wk, wv = HEAD_GROUP * A_DK, HEAD_GROUP * A_DV
    consts = _chunk_constants()
    col = lambda seg, w: pl.BlockSpec((tb, w), lambda b, g, t, seg=seg: (b * nt + t, seg * ng + g))
    return pl.pallas_call(
        _hgrn2_kernel,
        out_shape=jax.ShapeDtypeStruct((batch * seq, A_HEADS * A_DV), F32),
        grid=(batch, ng, nt),
        in_specs=[col(0, wk), col(1, wk), col(2, wv), col(3, wv),
                  pl.BlockSpec((1, wk), lambda b, g, t: (0, g)),
                  pl.BlockSpec((1, A_DV), lambda b, g, t: (0, 0))] + [_const_spec(c) for c in consts],
        out_specs=pl.BlockSpec((tb, wv), lambda b, g, t: (b * nt + t, g)),
        scratch_shapes=[pltpu.VMEM((HEAD_GROUP, A_DV, A_DK), F32)],
        compiler_params=_cparams("parallel", "parallel", "arbitrary"),
        name="hgrn2",
    )(h, h, h, h, lb.reshape(1, -1), norm_g.reshape(1, -1), *consts)


def _gla(h, wa2p, ba2, norm_g, batch, seq, q_off, k_off, v_off, g_off, r_off):
    tb = min(TIME_BLOCK, seq)
    nt = seq // tb
    wk, wv = B_HEADS * B_DK, B_HEADS * B_DV
    consts = _chunk_constants()
    col = lambda off, w: pl.BlockSpec((tb, w), lambda b, g, t, off=off: (b * nt + t, off))
    fixed = lambda b, g, t: (0, 0)
    return pl.pallas_call(
        _gla_kernel,
        out_shape=jax.ShapeDtypeStruct((batch * seq, wv), F32),
        grid=(batch, 1, nt),
        in_specs=[col(q_off, wk), col(k_off, wk), col(v_off, wv), col(g_off, wv), col(r_off, LANES),
                  pl.BlockSpec((LANES, wk), fixed), pl.BlockSpec((1, wk), fixed),
                  pl.BlockSpec((1, B_DV), fixed)] + [_const_spec(c) for c in consts],
        out_specs=pl.BlockSpec((tb, wv), lambda b, g, t: (b * nt + t, 0)),
        scratch_shapes=[pltpu.VMEM((B_HEADS, B_DV, B_DK), F32)],
        compiler_params=_cparams("parallel", "parallel", "arbitrary"),
        name="gla",
    )(h, h, h, h, h, wa2p, ba2.reshape(1, -1), norm_g.reshape(1, -1), *consts)


PROJ_TILE = 512
PROJ_ROWS = 2048
AB_MAIN = 4 * A_HEADS * A_DK + 2 * B_HEADS * B_DK + 2 * B_HEADS * B_DV
AB_PAD = AB_MAIN + PROJ_TILE


def _mix_ab(x2d, w_in, lb_logits, li, a_norm_g, wa2, ba2, b_norm_g, batch, seq):
    d = x2d.shape[1]
    w_tail = jnp.pad(w_in[:, AB_MAIN:], ((0, 0), (0, AB_PAD - w_in.shape[1]))).astype(BF16)
    h = _project(x2d, _cast_columns(w_in, AB_MAIN, PROJ_TILE), w_tail, min(PROJ_ROWS, x2d.shape[0]), PROJ_TILE)
    lb = jnp.cumsum(jax.nn.softmax(lb_logits.astype(F32), axis=0), axis=0)[li]
    oa = _hgrn2(h, lb, a_norm_g, batch, seq)
    wa2p = jnp.concatenate([wa2, jnp.zeros((LANES - B_GATE_RANK, wa2.shape[1]), F32)], axis=0).astype(BF16)
    a_cols = 4 * A_HEADS * A_DK
    wk, wv = B_HEADS * B_DK, B_HEADS * B_DV
    ob = _gla(h, wa2p, ba2, b_norm_g, batch, seq, q_off=a_cols // wk, k_off=a_cols // wk + 1,
              v_off=(a_cols + 2 * wk) // wv, g_off=(a_cols + 2 * wk) // wv + 1, r_off=AB_MAIN // LANES)
    return oa, ob


CONV_HALO = 8


def _causal_conv(x_ref, w_ref, buf_ref, tail_ref):
    tb = x_ref.shape[0]
    x = x_ref[...]
    buf_ref[0:CONV_HALO, :] = tail_ref[...]
    buf_ref[CONV_HALO:CONV_HALO + tb, :] = x
    tail_ref[...] = x[tb - CONV_HALO:tb]
    y = w_ref[CONV_K - 1:CONV_K, :] * x
    for j in range(CONV_K - 1):
        y = y + w_ref[j:j + 1, :] * buf_ref[pl.ds(CONV_HALO - (CONV_K - 1) + j, tb), :]
    return y


def _row_to_col(row, eye):
    return jnp.sum(jnp.where(eye, row, 0.0), axis=1, keepdims=True)


def _mlstm_kernel(q_ref, k_ref, v_ref, og_ref, wq_ref, wk_ref, ig_ref, fg_ref, gb_ref, ng_ref, tri_ref,
                  o_ref, ct_ref, n_ref, m_ref, qt_ref, kt_ref, qs_ref, ks_ref, buf_ref):
    @pl.when(pl.program_id(2) == 0)
    def _():
        ct_ref[...] = jnp.zeros_like(ct_ref)
        n_ref[...] = jnp.zeros_like(n_ref)
        m_ref[...] = jnp.zeros_like(m_ref)
        qt_ref[...] = jnp.zeros_like(qt_ref)
        kt_ref[...] = jnp.zeros_like(kt_ref)

    qs_ref[...] = _silu(_causal_conv(q_ref, wq_ref, buf_ref, qt_ref))
    ks_ref[...] = _silu(_causal_conv(k_ref, wk_ref, buf_ref, kt_ref)) * (D_DK ** -0.5)

    r_i = lax.broadcasted_iota(I32, (CHUNK, CHUNK), 0)
    c_i = lax.broadcasted_iota(I32, (CHUNK, CHUNK), 1)
    eye = r_i == c_i
    causal = r_i >= c_i

    def body(c, carry):
        rows = pl.ds(pl.multiple_of(c * CHUNK, CHUNK), CHUNK)
        tri = tri_ref[...]
        for hh in range(D_HEADS):
            ck = slice(D_DK * hh, D_DK * (hh + 1))
            cv = slice(D_DV * hh, D_DV * (hh + 1))
            q = qs_ref[rows, ck]
            k = ks_ref[rows, ck]
            v = v_ref[rows, cv].astype(BF16)
            qb = q.astype(BF16)
            ig_row = ig_ref[hh, pl.ds(c, 1), :] + gb_ref[0, hh]
            lf_row = _log_sigmoid(fg_ref[hh, pl.ds(c, 1), :] + gb_ref[1, hh])
            hi, mid, lo = _split3(lf_row)
            bcum_row = _dot(hi, tri) + _dot(mid, tri) + _dot(lo, tri)
            bcum_col = _row_to_col(bcum_row, eye)
            ig_col = _row_to_col(ig_row, eye)
            m_prev = m_ref[hh, :, 0:1]
            log_w = jnp.where(causal, bcum_col - bcum_row + ig_row, NEG)
            log_inter = bcum_col + m_prev
            m_t = jnp.maximum(jnp.max(log_w, axis=1, keepdims=True), log_inter)
            s = _dot_nt(qb, k.astype(BF16)) * jnp.exp(log_w - m_t)
            w_inter = jnp.exp(log_inter - m_t)
            num = _dot(s.astype(BF16), v) + w_inter * _dot_nt(qb, ct_ref[hh].astype(BF16))
            qn = jnp.sum(s, axis=1, keepdims=True) + w_inter * jnp.sum(q * n_ref[hh], axis=1, keepdims=True)
            h = num / jnp.maximum(jnp.abs(qn), jnp.exp(-m_t))
            o_ref[rows, cv] = _rms_gate(h, ng_ref[...], _sigmoid(og_ref[rows, cv]))
            b_last = bcum_row[:, CHUNK - 1:CHUNK]
            log_u = b_last - bcum_col + ig_col
            m_new = jnp.maximum(b_last + m_prev, jnp.max(log_u, axis=0, keepdims=True))
            decay = jnp.exp(b_last + m_prev - m_new)
            ku = k * jnp.exp(log_u - m_new)
            ct_ref[hh] = decay * ct_ref[hh] + _dot_tn(v, ku.astype(BF16))
            n_ref[hh] = decay * n_ref[hh] + jnp.sum(ku, axis=0, keepdims=True)
            m_ref[hh] = jnp.broadcast_to(m_new, (1, LANES))
        return carry

    lax.fori_loop(0, q_ref.shape[0] // CHUNK, body, 0)


def _mlstm(h, conv_w, ig_rows, fg_rows, gate_b, norm_g, batch, seq, q_off, k_off, v_off, g_off):
    tb = min(TIME_BLOCK, seq)
    nt = seq // tb
    nc = tb // CHUNK
    wk, wv = D_HEADS * D_DK, D_HEADS * D_DV
    t = np.arange(CHUNK)
    tri = jnp.asarray(t[:, None] <= t[None, :], BF16)
    gb = jnp.broadcast_to(gate_b.reshape(2, D_HEADS, 1, 1), (2, D_HEADS, 1, CHUNK)).astype(F32)
    col = lambda off, w: pl.BlockSpec((tb, w), lambda b, g, t, off=off: (b * nt + t, off))
    gate = pl.BlockSpec((None, D_HEADS, None, nc, CHUNK), lambda b, g, t: (b, 0, t, 0, 0))
    fixed = lambda b, g, t: (0, 0)
    return pl.pallas_call(
        _mlstm_kernel,
        out_shape=jax.ShapeDtypeStruct((batch * seq, wv), F32),
        grid=(batch, 1, nt),
        in_specs=[col(q_off, wk), col(k_off, wk), col(v_off, wv), col(g_off, wv),
                  pl.BlockSpec((CONV_K, wk), lambda b, g, t: (0, 0)),
                  pl.BlockSpec((CONV_K, wk), lambda b, g, t: (0, 1)),
                  gate, gate,
                  pl.BlockSpec((2, D_HEADS, 1, CHUNK), lambda b, g, t: (0, 0, 0, 0)),
                  pl.BlockSpec((1, D_DV), fixed), pl.BlockSpec((CHUNK, CHUNK), fixed)],
        out_specs=pl.BlockSpec((tb, wv), lambda b, g, t: (b * nt + t, 0)),
        scratch_shapes=[pltpu.VMEM((D_HEADS, D_DV, D_DK), F32), pltpu.VMEM((D_HEADS, 1, D_DK), F32),
                        pltpu.VMEM((D_HEADS, 1, LANES), F32),
                        pltpu.VMEM((CONV_HALO, wk), F32), pltpu.VMEM((CONV_HALO, wk), F32),
                        pltpu.VMEM((tb, wk), F32), pltpu.VMEM((tb, wk), F32),
                        pltpu.VMEM((tb + CONV_HALO, wk), F32)],
        compiler_params=_cparams("parallel", "parallel", "arbitrary"),
        name="mlstm",
    )(h, h, h, h, conv_w, conv_w, ig_rows, fg_rows, gb, norm_g.reshape(1, -1), tri)


CD_CQ, CD_IQ, CD_DQ, CD_DK, CD_DV, CD_DOG, CD_CKV, CD_TAIL = 0, 1024, 2048, 2560, 3072, 4096, 5120, 5376
CD_PAD = 5632
TAIL_IK, TAIL_IW, TAIL_DI, TAIL_DF = 0, 64, 80, 84


def _pack_cd(w_in):
    cq, ckv, iq, ik, iw, dq, dk, dv, di, df, dog = jnp.split(
        w_in, [int(i) for i in np.cumsum(
            (C_HEADS * C_DH, C_DLAT, IDX_HEADS * IDX_DIM, IDX_DIM, IDX_HEADS, D_HEADS * D_DK, D_HEADS * D_DK,
             D_HEADS * D_DV, D_HEADS, D_HEADS))], axis=1)
    used = CD_TAIL + IDX_DIM + IDX_HEADS + 2 * D_HEADS
    pad = jnp.zeros((w_in.shape[0], CD_PAD - used), F32)
    main = jnp.concatenate([cq, iq, dq, dk, dv, dog], axis=1).astype(BF16)
    tail = jnp.concatenate([ckv, ik, iw, di, df, pad], axis=1).astype(BF16)
    return main, tail


KB = 2 * QB
IDX_GROUP = 4
ATT_GROUP = 4
SUM_ROWS = 16


def _dsa_prep_kernel(ckv_ref, tail_ref, g_ref, ckv_o, ckvt_o, kdup_o):
    c = ckv_ref[...]
    cn = c * lax.rsqrt(jnp.mean(c * c, axis=-1, keepdims=True) + EPS) * g_ref[...]
    ckv_o[...] = cn.astype(BF16)
    ckvt_o[0:C_DLAT, :] = cn.T.astype(BF16)
    ckvt_o[C_DLAT:C_DLAT + SUM_ROWS, :] = jnp.ones((SUM_ROWS, KB), BF16)
    tail = tail_ref[...]
    lane = lax.broadcasted_iota(I32, tail.shape, 1)
    kdup_o[...] = jnp.where(lane < IDX_DIM, tail, pltpu.roll(tail, IDX_DIM, axis=1)).astype(BF16)


def _dsa_prep(h, ckv_g, n_rows):
    nb = n_rows // KB
    return pl.pallas_call(
        _dsa_prep_kernel,
        out_shape=(jax.ShapeDtypeStruct((nb, KB, C_DLAT), BF16), jax.ShapeDtypeStruct((nb, C_DLAT + SUM_ROWS, KB), BF16),
                   jax.ShapeDtypeStruct((nb, KB, LANES), BF16)),
        grid=(nb,),
        in_specs=[pl.BlockSpec((KB, C_DLAT), lambda i: (i, CD_CKV // C_DLAT)),
                  pl.BlockSpec((KB, LANES), lambda i: (i, CD_TAIL // LANES)),
                  pl.BlockSpec((1, C_DLAT), lambda i: (0, 0))],
        out_specs=(pl.BlockSpec((None, KB, C_DLAT), lambda i: (i, 0, 0)),
                   pl.BlockSpec((None, C_DLAT + SUM_ROWS, KB), lambda i: (i, 0, 0)),
                   pl.BlockSpec((None, KB, LANES), lambda i: (i, 0, 0))),
        compiler_params=_cparams("parallel"),
        name="dsa_prep",
    )(h, h, ckv_g.reshape(1, -1))


def _sortable_key(x):
    b = lax.bitcast_convert_type(x, I32)
    key = b ^ ((b >> 31) & 0x7FFFFFFF)
    return jnp.where(key == -1, 0, key)


BISECT_STEPS = 4


def _indexer_kernel(iq_ref, tail_ref, kdup_ref, tri_ref, mask_ref, key_ref, wst_ref, gmax_ref, *, k_sel):
    j = pl.program_id(1)
    nk = key_ref.shape[0]
    n_live = (j * QB + QB + KB - 1) // KB
    w_t = tail_ref[...].T
    lane = lax.broadcasted_iota(I32, (QB, LANES), 1)
    for p in range(IDX_HEADS // 2):
        pair = iq_ref[:, LANES * p:LANES * (p + 1)]
        g, r = divmod(2 * p, IDX_GROUP)
        wst_ref[g, r * QB:(r + 1) * QB, :] = jnp.where(lane < IDX_DIM, pair, 0.0).astype(BF16)
        wst_ref[g, (r + 1) * QB:(r + 2) * QB, :] = jnp.where(lane >= IDX_DIM, pair, 0.0).astype(BF16)

    s_loc = lax.broadcasted_iota(I32, (KB, QB), 0)
    t_abs = j * QB + lax.broadcasted_iota(I32, (1, QB), 1)

    def score_chunk(kc, carry):
        kd = kdup_ref[kc]
        acc = jnp.zeros((KB, QB), F32)
        for g in range(IDX_HEADS // IDX_GROUP):
            dots = _dot_nt(kd, wst_ref[g])
            for r in range(IDX_GROUP):
                row = TAIL_IW + g * IDX_GROUP + r
                acc = acc + jnp.maximum(dots[:, r * QB:(r + 1) * QB], 0.0) * w_t[row:row + 1, :]
        key = jnp.where(kc * KB + s_loc > t_abs, INT_MIN, _sortable_key(acc))
        key_ref[kc] = key
        gmax_ref[...] = jnp.maximum(gmax_ref[...], key)
        key = key.reshape(KB // SUBLANES, SUBLANES, QB)
        k_max, k_min = carry
        k_max = jnp.maximum(k_max, jnp.max(key, axis=0))
        k_min = jnp.minimum(k_min, jnp.min(jnp.where(key == INT_MIN, INT_MAX, key), axis=0))
        return k_max, k_min

    def score_two(i, carry):
        return score_chunk(2 * i + 1, score_chunk(2 * i, carry))

    gmax_ref[...] = jnp.full(gmax_ref.shape, INT_MIN, I32)
    k_max, k_min = lax.fori_loop(0, (n_live + 1) // 2, score_two, (jnp.full((SUBLANES, QB), INT_MIN, I32),
                                                                   jnp.full((SUBLANES, QB), INT_MAX, I32)))
    g_low = jnp.min(gmax_ref[...].reshape(KB // SUBLANES, SUBLANES, QB), axis=0)
    for shift in (4, 2, 1):
        k_max = jnp.maximum(k_max, pltpu.roll(k_max, shift, axis=0))
        k_min = jnp.minimum(k_min, pltpu.roll(k_min, shift, axis=0))
        g_low = jnp.minimum(g_low, pltpu.roll(g_low, shift, axis=0))
    k_low = jnp.maximum(k_min, g_low)
    k_row = jnp.minimum(k_sel, j * QB + lax.broadcasted_iota(I32, (SUBLANES, QB), 1) + 1)

    def count(pred):
        def add(kc, acc):
            hit = jnp.where(pred(key_ref[kc].reshape(KB // SUBLANES, SUBLANES, QB)), 1, 0)
            return acc + jnp.sum(hit, axis=0)

        def add_two(i, acc):
            return add(2 * i + 1, add(2 * i, acc))
        acc = lax.fori_loop(0, (n_live + 1) // 2, add_two, jnp.zeros((SUBLANES, QB), I32))
        for shift in (4, 2, 1):
            acc = acc + pltpu.roll(acc, shift, axis=0)
        return acc

    def unfinished(state):
        lo, hi = state
        return jnp.max(jnp.where(lo < hi, 1, 0)) > 0

    def halve(state):
        lo, hi = state
        mid = (lo >> 1) + (hi >> 1) + (((lo & 1) + (hi & 1) + 1) >> 1)
        cnt = count(lambda k: k >= mid)
        enough = cnt >= k_row
        lo_n = jnp.where(enough, mid, lo)
        hi_n = jnp.where(cnt == k_row, mid, jnp.where(enough, hi, mid - 1))
        return lo_n, hi_n

    def halve_steps(state):
        for _ in range(BISECT_STEPS):
            state = halve(state)
        return state

    tau8, _ = lax.while_loop(unfinished, halve_steps, (k_low, k_max))
    n_ge = count(lambda k: k >= tau8)
    has_tie = jnp.max(jnp.where(n_ge != k_row, 1, 0)) > 0
    tau = tau8[0:1, :]

    @pl.when(jnp.logical_not(has_tie))
    def _():
        def put(kc, carry):
            mask_ref[kc] = jnp.where(key_ref[kc] >= tau, 1.0, 0.0).astype(BF16)
            return carry
        lax.fori_loop(0, n_live, put, 0)

    @pl.when(has_tie)
    def _():
        need = (k_row - count(lambda k: k > tau8)).astype(F32)[0:1, :]

        def put(kc, seen):
            k = key_ref[kc]
            eq = jnp.where(k == tau, 1.0, 0.0)
            before = _dot(tri_ref[...], eq.astype(BF16)) + seen
            take = jnp.where(k > tau, 1.0, jnp.where(before < need, eq, 0.0))
            mask_ref[kc] = take.astype(BF16)
            return seen + jnp.sum(eq, axis=0, keepdims=True)
        lax.fori_loop(0, n_live, put, jnp.zeros((1, QB), F32))

    def clear(kc, carry):
        mask_ref[kc] = jnp.zeros((KB, QB), BF16)
        return carry
    lax.fori_loop(n_live, nk, clear, 0)


def _indexer(h, kdup, batch, seq):
    nq = seq // QB
    nk = seq // KB
    k_sel = min(TOPK_MAX, seq // 4)
    r = np.arange(KB)
    tri = jnp.asarray(r[None, :] < r[:, None], BF16)
    return pl.pallas_call(
        functools.partial(_indexer_kernel, k_sel=k_sel),
        out_shape=jax.ShapeDtypeStruct((batch, nk, KB, seq), BF16),
        grid=(batch, nq),
        in_specs=[pl.BlockSpec((QB, IDX_HEADS * IDX_DIM), lambda b, j: (b * nq + j, CD_IQ // (IDX_HEADS * IDX_DIM))),
                  pl.BlockSpec((QB, LANES), lambda b, j: (b * nq + j, CD_TAIL // LANES)),
                  pl.BlockSpec((None, nk, KB, LANES), lambda b, j: (b, 0, 0, 0)),
                  pl.BlockSpec((KB, KB), lambda b, j: (0, 0))],
        out_specs=pl.BlockSpec((None, nk, KB, QB), lambda b, j: (b, 0, 0, j)),
        scratch_shapes=[pltpu.VMEM((nk, KB, QB), I32),
                        pltpu.VMEM((IDX_HEADS // IDX_GROUP, IDX_GROUP * QB, LANES), BF16),
                        pltpu.VMEM((KB, QB), I32)],
        compiler_params=_cparams("parallel", "parallel"),
        name="dsa_indexer",
    )(h, h, kdup.reshape(batch, nk, KB, LANES), tri)


def _dsa_attn_kernel(cq_ref, mask_ref, ckv_ref, ckvt_ref, wuk_ref, wuvt_ref, bias_ref, o_ref,
                     qt_ref, m_ref, acc_ref, ot_ref, p_ref, alpha_ref):
    j = pl.program_id(1)
    for hh in range(C_HEADS):
        q_h = cq_ref[:, C_DH * hh:C_DH * (hh + 1)].astype(BF16)
        part = slice(QB * (hh % ATT_GROUP), QB * (hh % ATT_GROUP + 1))
        qt_ref[hh // ATT_GROUP, :, part] = (_dot_nt(wuk_ref[hh], q_h) * (C_DH ** -0.5 * LOG2E)).astype(BF16)
    m_ref[...] = jnp.full(m_ref.shape, NEG, F32)
    acc_ref[...] = jnp.zeros_like(acc_ref)
    p_ref[...] = jnp.zeros_like(p_ref)
    alpha_ref[...] = jnp.ones_like(alpha_ref)

    n_live = (j * QB + QB + KB - 1) // KB
    odd = (j % 2) == 1

    def accumulate(kc_done, hp):
        acc_ref[hp] = alpha_ref[hp] * acc_ref[hp] + _dot(ckvt_ref[kc_done], p_ref[hp])

    def body(kc, carry, near):
        ck = ckv_ref[kc]
        kc_prev = jnp.maximum(kc - 1, 0)
        drop = jnp.where(mask_ref[kc].astype(F32) > 0.5, 0.0, NEG)
        drop = jnp.concatenate([drop] * ATT_GROUP, axis=1)
        back = n_live - 1 - kc
        which = jnp.where(back == 0, jnp.where(odd, 0, 1), jnp.where(jnp.logical_and(back == 1, jnp.logical_not(odd)), 2, 3))
        for hp in range(C_HEADS // ATT_GROUP):
            accumulate(kc_prev, hp)
            if near:
                tiles = [bias_ref[hp * ATT_GROUP + g, which] for g in range(ATT_GROUP)]
                logit = _dot(ck, qt_ref[hp]) + (jnp.concatenate(tiles, axis=1) + drop)
            else:
                logit = _dot(ck, qt_ref[hp]) + drop
            m_old = m_ref[hp]
            m_new = jnp.maximum(m_old, jnp.max(logit, axis=0, keepdims=True))
            alpha_ref[hp] = jnp.exp2(m_old - m_new)
            p_ref[hp] = jnp.exp2(logit - m_new).astype(BF16)
            m_ref[hp] = m_new
        return carry

    def body_two(i, carry, near):
        return body(2 * i + 1, body(2 * i, carry, near), near)

    def body_four(i, carry):
        return body_two(2 * i + 1, body_two(2 * i, carry, False), False)

    n_pairs = (n_live + 1) // 2
    n_far = jnp.maximum(n_pairs - 2, 0)
    lax.fori_loop(0, n_far // 2, body_four, 0)
    lax.fori_loop(2 * (n_far // 2), n_far, functools.partial(body_two, near=False), 0)
    lax.fori_loop(n_far, n_pairs, functools.partial(body_two, near=True), 0)
    for hp in range(C_HEADS // ATT_GROUP):
        accumulate(2 * n_pairs - 1, hp)
    for hh in range(C_HEADS):
        part = slice(QB * (hh % ATT_GROUP), QB * (hh % ATT_GROUP + 1))
        total = acc_ref[hh // ATT_GROUP, C_DLAT:C_DLAT + 1, part]
        o_lat = (acc_ref[hh // ATT_GROUP, 0:C_DLAT, part] * (1.0 / total)).astype(BF16)
        ot_ref[C_DH * hh:C_DH * (hh + 1), :] = _dot(wuvt_ref[hh], o_lat)
    o_ref[...] = ot_ref[...].T


def _rel_bias_tiles(rel_table):
    s = np.arange(QB)[:, None]
    t = np.arange(QB)[None, :]
    diag, prev, far = np.maximum(t - s, 0), QB + t - s, np.full((QB, QB), 2 * QB)
    kinds = [(prev, diag), (diag, far), (far, prev), (far, far)]
    n = jnp.asarray(np.stack([np.concatenate(k, axis=0) for k in kinds]).astype(np.int32))
    max_exact = REL_BUCKETS // 2
    large = max_exact + (jnp.log(jnp.maximum(n, 1).astype(F32) / max_exact)
                         / math.log(REL_MAX_DIST / max_exact) * (REL_BUCKETS - max_exact)).astype(I32)
    bucket = jnp.where(n < max_exact, n, jnp.minimum(large, REL_BUCKETS - 1))
    onehot = (bucket[..., None] == jnp.arange(REL_BUCKETS, dtype=I32)).astype(F32)
    bias = jnp.einsum("hb,kstb->hkst", rel_table.astype(F32).T * LOG2E, onehot, precision=lax.Precision.HIGHEST)
    return bias - bias[:, 3:4, 0:1, 0:1]


def _dsa_attention(h, mask, ckv, ckvt, w_uk, w_uv, rel_table, batch, seq):
    nq = seq // QB
    nk = seq // KB
    wuk = w_uk.transpose(1, 0, 2).astype(BF16)
    wuvt = w_uv.transpose(1, 2, 0).astype(BF16)
    bias = _rel_bias_tiles(rel_table)
    return pl.pallas_call(
        _dsa_attn_kernel,
        out_shape=jax.ShapeDtypeStruct((batch * seq, C_HEADS * C_DH), F32),
        grid=(batch, nq),
        in_specs=[pl.BlockSpec((QB, C_HEADS * C_DH), lambda b, j: (b * nq + j, CD_CQ // (C_HEADS * C_DH))),
                  pl.BlockSpec((None, nk, KB, QB), lambda b, j: (b, 0, 0, j)),
                  pl.BlockSpec((None, nk, KB, C_DLAT), lambda b, j: (b, 0, 0, 0)),
                  pl.BlockSpec((None, nk, C_DLAT + SUM_ROWS, KB), lambda b, j: (b, 0, 0, 0)),
                  _const_spec(wuk), _const_spec(wuvt), _const_spec(bias)],
        out_specs=pl.BlockSpec((QB, C_HEADS * C_DH), lambda b, j: (b * nq + j, 0)),
        scratch_shapes=[pltpu.VMEM((C_HEADS // ATT_GROUP, C_DLAT, ATT_GROUP * QB), BF16),
                        pltpu.VMEM((C_HEADS // ATT_GROUP, 1, ATT_GROUP * QB), F32),
                        pltpu.VMEM((C_HEADS // ATT_GROUP, C_DLAT + SUM_ROWS, ATT_GROUP * QB), F32),
                        pltpu.VMEM((C_HEADS * C_DH, QB), F32),
                        pltpu.VMEM((C_HEADS // ATT_GROUP, KB, ATT_GROUP * QB), BF16),
                        pltpu.VMEM((C_HEADS // ATT_GROUP, 1, ATT_GROUP * QB), F32)],
        compiler_params=_cparams("parallel", "parallel"),
        name="dsa_attention",
    )(h, mask, ckv.reshape(batch, nk, KB, C_DLAT), ckvt.reshape(batch, nk, C_DLAT + SUM_ROWS, KB), wuk, wuvt, bias)


ROUTER_BLOCK = 512
MOE_TILE = 512
COMBINE_ROWS = 64


def _route_block(x, wh_ref, wl_ref, b_ref, upper_ref, idx_ref, wgt_ref, rank_ref, cnt_ref):
    xh = x.astype(BF16)
    xl = (x - xh.astype(F32)).astype(BF16)
    logit = _dot_nt(wh_ref[...], xh) + _dot_nt(wl_ref[...], xh) + _dot_nt(wh_ref[...], xl)
    aff = _sigmoid(logit)
    sel = aff + b_ref[...]
    s_rows = [sel[e:e + 1] for e in range(N_EXPERTS)]
    a_rows = [aff[e:e + 1] for e in range(N_EXPERTS)]
    n = EXPERTS_PER_GROUP

    g_best = jnp.zeros(s_rows[0].shape, I32)
    best = None
    for g in range(N_GROUPS):
        v = s_rows[g * n:(g + 1) * n]
        top2 = None
        for a in range(n):
            for b in range(a + 1, n):
                pair = v[a] + v[b]
                top2 = pair if top2 is None else jnp.maximum(top2, pair)
        if best is None:
            best = top2
        else:
            upd = top2 > best
            g_best = jnp.where(upd, g, g_best)
            best = jnp.where(upd, top2, best)

    sv, av = [], []
    for i in range(n):
        s_i, a_i = s_rows[i], a_rows[i]
        for g in range(1, N_GROUPS):
            pick = g_best == g
            s_i = jnp.where(pick, s_rows[g * n + i], s_i)
            a_i = jnp.where(pick, a_rows[g * n + i], a_i)
        sv.append(s_i)
        av.append(a_i)

    i1, s1, a1 = jnp.zeros_like(g_best), sv[0], av[0]
    for i in range(1, n):
        upd = sv[i] > s1
        i1 = jnp.where(upd, i, i1)
        s1 = jnp.where(upd, sv[i], s1)
        a1 = jnp.where(upd, av[i], a1)
    i2 = jnp.zeros_like(g_best)
    s2 = jnp.full(s1.shape, -jnp.inf, F32)
    a2 = jnp.zeros_like(a1)
    for i in range(n):
        cand = jnp.where(i1 == i, -jnp.inf, sv[i])
        upd = cand > s2
        i2 = jnp.where(upd, i, i2)
        s2 = jnp.where(upd, cand, s2)
        a2 = jnp.where(upd, av[i], a2)

    tot = a1 + a2
    e1 = g_best * n + i1
    e2 = g_best * n + i2
    idx_ref[0:1, :] = e1
    idx_ref[1:2, :] = e2
    wgt_ref[0:1, :] = a1 / tot
    wgt_ref[1:2, :] = a2 / tot

    @pl.when(pl.program_id(0) == 0)
    def _():
        cnt_ref[...] = jnp.zeros_like(cnt_ref)

    e_iota = lax.broadcasted_iota(I32, sel.shape, 0)
    oh1 = jnp.where(e_iota == e1, 1.0, 0.0)
    oh2 = jnp.where(e_iota == e2, 1.0, 0.0)
    both = oh1 + oh2
    before = cnt_ref[...] + _dot(both.astype(BF16), upper_ref[...])
    rank_ref[0:1, :] = jnp.sum(oh1 * before, axis=0, keepdims=True).astype(I32)
    rank_ref[1:2, :] = jnp.sum(oh2 * before, axis=0, keepdims=True).astype(I32)
    cnt_ref[...] = cnt_ref[...] + jnp.sum(both, axis=1, keepdims=True)


def _route_tables(idx, rank, counts, tm, n_items):
    cnt = counts.reshape(-1).astype(I32)
    start = jnp.cumsum(cnt) - cnt
    experts = jnp.arange(N_EXPERTS, dtype=I32)
    pos = rank + jnp.sum(jnp.where(idx[..., None] == experts, start, 0), axis=-1)
    first_tile = start // tm
    n_e = jnp.where(cnt > 0, (start + cnt - 1) // tm - first_tile + 1, 0)
    item_end = jnp.cumsum(n_e)
    item = jnp.arange(n_items, dtype=I32)
    used = item < item_end[-1]
    e_i = jnp.minimum(jnp.sum((item[:, None] >= item_end[None, :]).astype(I32), axis=1), N_EXPERTS - 1)
    e_last = jnp.max(jnp.where(cnt > 0, experts, 0))
    e_i = jnp.where(used, e_i, e_last)
    tile_i = jnp.where(used, first_tile[e_i] + item - (item_end - n_e)[e_i], (TOP_K * idx.shape[1]) // tm - 1)
    lo = jnp.where(used, jnp.maximum(start[e_i], tile_i * tm) - tile_i * tm, 0)
    hi = jnp.where(used, jnp.minimum(start[e_i] + cnt[e_i], (tile_i + 1) * tm) - tile_i * tm, 0)
    return pos.astype(I32), tile_i.astype(I32), e_i, lo.astype(I32), hi.astype(I32)


def _row_of(ref, row):
    return ref.at[lax.shift_right_logical(row, SUBLANES.bit_length() - 1), pl.ds(row & (SUBLANES - 1), 1)]


def _dispatch_kernel(pos_ref, x_ref, xs_hbm, stage, sem, *, tb, n_tok):
    i = pl.program_id(0)
    slot = i % 2
    stage[slot] = x_ref[...].reshape(stage.shape[1:])

    def issue(grp, carry):
        dst = [[pos_ref[s * n_tok + i * tb + grp * SUBLANES + u] for s in range(TOP_K)] for u in range(SUBLANES)]
        for u in range(SUBLANES):
            for s in range(TOP_K):
                row = dst[u][s]
                pltpu.make_async_copy(stage.at[slot, grp, pl.ds(u, 1)], _row_of(xs_hbm, row),
                                      sem.at[slot]).start(priority=s)
        return carry

    lax.fori_loop(0, tb // SUBLANES, issue, 0)

    def wait_block(s):
        for _ in range(TOP_K):
            pltpu.make_async_copy(stage.at[s], xs_hbm.at[pl.ds(0, tb // SUBLANES)], sem.at[s]).wait()

    @pl.when(i >= 1)
    def _():
        wait_block(1 - slot)

    @pl.when(i == pl.num_programs(0) - 1)
    def _():
        wait_block(slot)


def _dispatch(x2d, pos_flat, tb):
    t, d = x2d.shape
    grid_spec = pltpu.PrefetchScalarGridSpec(
        num_scalar_prefetch=1, grid=(t // tb,),
        in_specs=[pl.BlockSpec((tb, d), lambda i, pos: (i, 0))],
        out_specs=pl.BlockSpec(memory_space=pl.ANY),
        scratch_shapes=[pltpu.VMEM((2, tb // SUBLANES, SUBLANES, d), F32), pltpu.SemaphoreType.DMA((2,))])
    return pl.pallas_call(
        functools.partial(_dispatch_kernel, tb=tb, n_tok=t),
        out_shape=jax.ShapeDtypeStruct((TOP_K * t // SUBLANES, SUBLANES, d), F32),
        grid_spec=grid_spec,
        compiler_params=_cparams("arbitrary"),
        name="moe_dispatch",
    )(pos_flat, x2d).reshape(TOP_K * t, d)


def _experts_kernel(tile_ref, exp_ref, lo_ref, hi_ref, x_ref, w1_ref, w3_ref, w2_ref, o_ref, w1b, w3b, w2b):
    i = pl.program_id(0)
    prev = jnp.maximum(i - 1, 0)

    @pl.when(jnp.logical_or(i == 0, exp_ref[i] != exp_ref[prev]))
    def _():
        w1b[...] = w1_ref[...].astype(BF16)
        w3b[...] = w3_ref[...].astype(BF16)
        w2b[...] = w2_ref[...].astype(BF16)

    lo = lo_ref[i]
    hi = hi_ref[i]
    first = jnp.logical_or(i == 0, tile_ref[i] != tile_ref[prev])

    @pl.when(hi > lo)
    def _():
        x = x_ref[...].astype(BF16)
        row = lax.broadcasted_iota(I32, (x.shape[0], 1), 0)
        mine = jnp.where(jnp.logical_and(row >= lo, row < hi), 1.0, 0.0)
        hid = _silu(_dot(x, w1b[...])) * _dot(x, w3b[...]) * mine
        y = _dot(hid.astype(BF16), w2b[...])

        @pl.when(first)
        def _():
            o_ref[...] = y

        @pl.when(jnp.logical_not(first))
        def _():
            o_ref[...] += y


def _experts(xs, tile_i, exp_i, lo, hi, w1, w3, w2, layer, tm):
    n, d = xs.shape
    f = w1.shape[-1]
    by_tile = lambda i, tile, exp, lo, hi: (tile[i], 0)
    by_exp = lambda i, tile, exp, lo, hi: (layer, exp[i], 0, 0)
    grid_spec = pltpu.PrefetchScalarGridSpec(
        num_scalar_prefetch=4, grid=(tile_i.shape[0],),
        in_specs=[pl.BlockSpec((tm, d), by_tile),
                  pl.BlockSpec((None, None, d, f), by_exp), pl.BlockSpec((None, None, d, f), by_exp),
                  pl.BlockSpec((None, None, f, d), by_exp)],
        out_specs=pl.BlockSpec((tm, d), by_tile),
        scratch_shapes=[pltpu.VMEM((d, f), BF16), pltpu.VMEM((d, f), BF16), pltpu.VMEM((f, d), BF16)])
    return pl.pallas_call(
        _experts_kernel,
        out_shape=jax.ShapeDtypeStruct((n, d), F32),
        grid_spec=grid_spec,
        compiler_params=_cparams("arbitrary"),
        name="moe_experts",
    )(tile_i, exp_i, lo, hi, xs, w1, w3, w2)


def _combine_ln_kernel(pos_ref, x_ref, w_ref, g_ref, b_ref, ys_hbm, o_ref, gbuf, sem, *, tb, n_tok):
    i = pl.program_id(0)
    slot = i % 2

    def issue(blk, s, grp):
        src = [[pos_ref[k * n_tok + blk * tb + grp * SUBLANES + u] for k in range(TOP_K)] for u in range(SUBLANES)]
        for u in range(SUBLANES):
            for k in range(TOP_K):
                pltpu.make_async_copy(_row_of(ys_hbm, src[u][k]), gbuf.at[s, k, grp, pl.ds(u, 1)],
                                      sem.at[s]).start(priority=k)

    def finish(part):
        tiles = COMBINE_ROWS // SUBLANES
        rows = pl.ds(pl.multiple_of(part * COMBINE_ROWS, COMBINE_ROWS), COMBINE_ROWS)
        w = w_ref[rows, :]
        g0 = gbuf[slot, 0, pl.ds(part * tiles, tiles)].reshape(COMBINE_ROWS, -1)
        g1 = gbuf[slot, 1, pl.ds(part * tiles, tiles)].reshape(COMBINE_ROWS, -1)
        y = g0 * w[:, 0:1] + g1 * w[:, 1:2]
        o_ref[rows, :] = _layer_norm_rows(DN_ALPHA * x_ref[rows, :] + y, g_ref[...], b_ref[...])

    def issue_part(blk, s, part):
        for q in range(COMBINE_ROWS // SUBLANES):
            issue(blk, s, part * (COMBINE_ROWS // SUBLANES) + q)

    @pl.when(i == 0)
    def _():
        lax.fori_loop(0, tb // SUBLANES, lambda grp, c: (issue(0, 0, grp), c)[1], 0)

    for k in range(TOP_K):
        pltpu.make_async_copy(ys_hbm.at[pl.ds(0, tb // SUBLANES)], gbuf.at[slot, k], sem.at[slot]).wait()

    @pl.when(i + 1 < pl.num_programs(0))
    def _():
        def both(part, carry):
            finish(part)
            issue_part(i + 1, 1 - slot, part)
            return carry
        lax.fori_loop(0, tb // COMBINE_ROWS, both, 0)

    @pl.when(i + 1 == pl.num_programs(0))
    def _():
        lax.fori_loop(0, tb // COMBINE_ROWS, lambda part, c: (finish(part), c)[1], 0)


def _combine_ln(x2d, ys, pos_flat, wgt_cols, g, b, tb):
    t, d = x2d.shape
    row = lambda i, pos: (i, 0)
    fixed = lambda i, pos: (0, 0)
    grid_spec = pltpu.PrefetchScalarGridSpec(
        num_scalar_prefetch=1, grid=(t // tb,),
        in_specs=[pl.BlockSpec((tb, d), row), pl.BlockSpec((tb, TOP_K), row),
                  pl.BlockSpec((1, d), fixed), pl.BlockSpec((1, d), fixed),
                  pl.BlockSpec(memory_space=pl.ANY)],
        out_specs=pl.BlockSpec((tb, d), row),
        scratch_shapes=[pltpu.VMEM((2, TOP_K, tb // SUBLANES, SUBLANES, d), F32), pltpu.SemaphoreType.DMA((2,))])
    return pl.pallas_call(
        functools.partial(_combine_ln_kernel, tb=tb, n_tok=t),
        out_shape=jax.ShapeDtypeStruct((t, d), F32),
        grid_spec=grid_spec,
        compiler_params=_cparams("arbitrary"),
        name="moe_combine_ln",
    )(pos_flat, x2d, wgt_cols, g.reshape(1, d), b.reshape(1, d), ys.reshape(-1, SUBLANES, d))


def _moe_ln(x2d, routing, w1, w3, w2, layer, g, b):
    t = x2d.shape[0]
    tm = min(MOE_TILE, t)
    idx, wgt, rank, counts = routing
    n_items = TOP_K * t // tm + N_EXPERTS - 1
    pos, tile_i, exp_i, lo, hi = _route_tables(idx, rank, counts, tm, n_items)
    pos_flat = pos.reshape(-1)
    xs = _dispatch(x2d, pos_flat, tm)
    ys = _experts(xs, tile_i, exp_i, lo, hi, w1, w3, w2, layer, tm)
    return _combine_ln(x2d, ys, pos_flat, wgt.T, g, b, tm)


def _gate_rows(col, batch, seq):
    tb = min(TIME_BLOCK, seq)
    return col.reshape(batch, seq, D_HEADS).transpose(0, 2, 1).reshape(batch, D_HEADS, seq // tb, tb // CHUNK, CHUNK)


def _mix_cd(x2d, w_in, rel_table, ckv_g, w_uk, w_uv, conv_w, gate_b, d_norm_g, batch, seq):
    t = x2d.shape[0]
    h = _project(x2d, *_pack_cd(w_in), min(PROJ_ROWS, t), PROJ_TILE)
    ckv, ckvt, kdup = _dsa_prep(h, ckv_g, t)
    mask = _indexer(h, kdup, batch, seq)
    oc = _dsa_attention(h, mask, ckv, ckvt, w_uk, w_uv, rel_table, batch, seq)
    tail = h[:, CD_TAIL:CD_TAIL + LANES]
    ig_rows = _gate_rows(tail[:, TAIL_DI:TAIL_DI + D_HEADS], batch, seq)
    fg_rows = _gate_rows(tail[:, TAIL_DF:TAIL_DF + D_HEADS], batch, seq)
    od = _mlstm(h, conv_w, ig_rows, fg_rows, gate_b, d_norm_g, batch, seq,
                q_off=CD_DQ // (D_HEADS * D_DK), k_off=CD_DK // (D_HEADS * D_DK),
                v_off=CD_DV // (D_HEADS * D_DV), g_off=CD_DOG // (D_HEADS * D_DV))
    return oc, od


def kernel(x, w_in_ab, w_out_ab, hgrn_lb_logits, a_norm_g, gla_wa2, gla_ba2, b_norm_g, w_in_cd, w_out_cd,
           ckv_norm_g, w_uk, w_uv, mlstm_conv_w, mlstm_gate_b, d_norm_g, rel_table, w_router, b_router,
           moe_w1, moe_w3, moe_w2, ln_g, ln_b):
    batch, seq, d = x.shape
    x2d = x.reshape(batch * seq, d)
    for layer in range(DEPTH):
        li = layer // 2
        if layer % 2 == 0:
            mix_a, mix_b = _mix_ab(x2d, w_in_ab[li], hgrn_lb_logits, li, a_norm_g[li], gla_wa2[li], gla_ba2[li],
                                   b_norm_g[li], batch, seq)
            w_out = w_out_ab[li]
        else:
            mix_a, mix_b = _mix_cd(x2d, w_in_cd[li], rel_table, ckv_norm_g[li], w_uk[li], w_uv[li],
                                   mlstm_conv_w[li], mlstm_gate_b[li], d_norm_g[li], batch, seq)
            w_out = w_out_cd[li]
        ka = mix_a.shape[1]
        x2d, *routing = _outproj_ln(x2d, mix_a, mix_b, w_out[:ka].astype(BF16), w_out[ka:].astype(BF16),
                                    ln_g[layer, 0], ln_b[layer, 0], w_router, b_router)
        x2d = _moe_ln(x2d, routing, moe_w1, moe_w3, moe_w2, layer, ln_g[layer, 1], ln_b[layer, 1])
    return x2d.reshape(batch, seq, d)
```
